```python
import jax, jax.numpy as jnp
from jax import lax
import numpy as np


D_MODEL = 2048
BATCH = 1
SEQ = 16384
DEPTH = 2

LN_EPS = 1e-5
DEEPNORM_ALPHA = (2 * DEPTH) ** 0.25
DEEPNORM_BETA = (8 * DEPTH) ** -0.25

SB_HEADS = 8
SB_HEAD_DIM = 128
SB_WIDTH = SB_HEADS * SB_HEAD_DIM
SB_BLOCK = 128

GLA_HEADS = 4
GLA_HEAD_K = 128
GLA_HEAD_V = 256
GLA_KEY_DIM = GLA_HEADS * GLA_HEAD_K
GLA_VALUE_DIM = GLA_HEADS * GLA_HEAD_V
GLA_GATE_RANK = 16
GLA_GATE_NORMALIZER = 16.0
GLA_CHUNK = 64
GLA_NORM_EPS = 1e-5

RWKV_HEADS = 16
RWKV_HEAD_DIM = 64
RWKV_WIDTH = RWKV_HEADS * RWKV_HEAD_DIM
RWKV_DECAY_RANK = 64
RWKV_AAA_RANK = 64
RWKV_VALUE_RANK = 32
RWKV_GATE_RANK = 128
RWKV_LNX_EPS = 64e-5

N_EXPERTS = 64
TOP_K = 6
EXPERT_DIM = 512
SHARED_DIM = 512
N_GROUPS = 8
TOPK_GROUPS = 4
ROUTED_SCALE = 2.5
MOE_BLOCK = 128

SB_COLS = 3 * SB_WIDTH
GLA_COLS = 2 * GLA_KEY_DIM + GLA_VALUE_DIM + GLA_GATE_RANK + GLA_VALUE_DIM
RWKV_COLS = 3 * RWKV_WIDTH + RWKV_DECAY_RANK + RWKV_AAA_RANK + RWKV_GATE_RANK
MERGE_COLS = 3 * D_MODEL
N_IN = SB_COLS + GLA_COLS + RWKV_COLS + MERGE_COLS

kernel_name = "hybrid_sb_gla_rwkv7_moe_deepnorm_adaln"


def _split(a, sizes):
    return jnp.split(a, np.cumsum(sizes)[:-1].tolist(), axis=-1)


def _layer_norm(x, g, b):
    xf = x.astype(jnp.float32)
    mu = jnp.mean(xf, axis=-1, keepdims=True)
    var = jnp.mean(jnp.square(xf - mu), axis=-1, keepdims=True)
    return ((xf - mu) * lax.rsqrt(var + LN_EPS) * g + b).astype(x.dtype)


def _group_norm(x, eps):
    xf = x.astype(jnp.float32)
    mu = jnp.mean(xf, axis=-1, keepdims=True)
    var = jnp.mean(jnp.square(xf - mu), axis=-1, keepdims=True)
    return ((xf - mu) * lax.rsqrt(var + eps)).astype(x.dtype)


def _rms_norm(x, g, eps):
    xf = x.astype(jnp.float32)
    return (xf * lax.rsqrt(jnp.mean(jnp.square(xf), axis=-1, keepdims=True) + eps) * g).astype(x.dtype)


def _token_shift(a, mu):
    prev = jnp.pad(a[:, :-1], ((0, 0), (1, 0), (0, 0)))
    return a + (prev - a) * mu


def _stick_breaking(q, k, v):
    B, T, H, Dh = q.shape
    qh, kh, vh = (t.transpose(0, 2, 1, 3) for t in (q, k, v))
    key_pos = jnp.arange(T)

    def block(i):
        qb = lax.dynamic_slice_in_dim(qh, i * SB_BLOCK, SB_BLOCK, axis=2)
        z = jnp.einsum('bhqd,bhkd->bhqk', qb, kh).astype(jnp.float32) * (Dh ** -0.5)
        q_pos = i * SB_BLOCK + jnp.arange(SB_BLOCK)
        causal = (key_pos[None, :] < q_pos[:, None])[None, None]
        log_keep = jnp.where(causal, jax.nn.log_sigmoid(-z), 0.0)
        log_later = lax.cumsum(log_keep, axis=3, reverse=True) - log_keep
        weights = jnp.where(causal, jnp.exp(jax.nn.log_sigmoid(z) + log_later), 0.0)
        return jnp.einsum('bhqk,bhkd->bhqd', weights.astype(vh.dtype), vh)

    out = lax.map(block, jnp.arange(T // SB_BLOCK))
    return out.transpose(1, 0, 3, 2, 4).reshape(B, T, H * Dh)


def _gla(q, k, v, log_a):
    B, T, H, dk = q.shape
    dv = v.shape[-1]
    C = GLA_CHUNK
    n = T // C
    f32 = jnp.float32

    def to_chunks(a):
        return a.astype(f32).reshape(B, n, C, H, a.shape[-1]).transpose(1, 0, 3, 2, 4)

    causal = jnp.tril(jnp.ones((C, C), bool))[:, :, None]

    def step(S, inp):
        qc, kc, vc, gc = inp
        bcum = lax.cumsum(gc, axis=2)
        o_inter = jnp.einsum('bhcd,bhde->bhce', qc * jnp.exp(bcum), S)
        diff = jnp.where(causal, bcum[:, :, :, None, :] - bcum[:, :, None, :, :], -jnp.inf)
        scores = jnp.sum(qc[:, :, :, None, :] * kc[:, :, None, :, :] * jnp.exp(diff), axis=-1)
        o = o_inter + jnp.einsum('bhij,bhje->bhie', scores, vc)
        b_last = bcum[:, :, -1:, :]
        S = jnp.exp(b_last[:, :, 0, :])[..., None] * S + jnp.einsum(
            'bhcd,bhce->bhde', kc * jnp.exp(b_last - bcum), vc)
        return S, o

    S0 = jnp.zeros((B, H, dk, dv), f32)
    _, o = lax.scan(step, S0, (to_chunks(q * GLA_HEAD_K ** -0.5), to_chunks(k), to_chunks(v), to_chunks(log_a)))
    return o.transpose(1, 0, 3, 2, 4).reshape(B, T, H, dv).astype(v.dtype)


def _rwkv7_recurrence(r, w, k, v, a, b):
    B, T, H, N = r.shape

    def step(S, inp):
        rt, wt, kt, vt, at, bt = inp
        sa = jnp.einsum('bhij,bhj->bhi', S, at)
        S = S * wt[:, :, None, :] + sa[..., None] * bt[:, :, None, :] + vt[..., None] * kt[:, :, None, :]
        return S, jnp.einsum('bhij,bhj->bhi', S, rt)

    xs = tuple(t.astype(jnp.float32).transpose(1, 0, 2, 3) for t in (r, w, k, v, a, b))
    _, y = lax.scan(step, jnp.zeros((B, H, N, N), jnp.float32), xs)
    return y.transpose(1, 0, 2, 3).astype(v.dtype)


def _mixer_sublayer(u, v_first, v_res, w_in, gla_gk_up, gla_gk_bias, gla_norm_g, rw_mu, rw_w0, rw_w2,
                    rw_a0, rw_a2, rw_g2, rw_k_k, rw_k_a, rw_r_k, rw_lnx_g, rw_lnx_b,
                    w_br_sb, w_br_gla, w_br_rw, w_o):
    B, T, _ = u.shape
    proj = u @ w_in
    p_sb, p_gla, p_rw, p_merge = _split(proj, (SB_COLS, GLA_COLS, RWKV_COLS, MERGE_COLS))

    q, k, v = (t.reshape(B, T, SB_HEADS, SB_HEAD_DIM) for t in jnp.split(p_sb, 3, axis=-1))
    o_sb = _stick_breaking(q, k, v)

    gq, gk, gv, g_lr, g_out = _split(p_gla, (GLA_KEY_DIM, GLA_KEY_DIM, GLA_VALUE_DIM, GLA_GATE_RANK, GLA_VALUE_DIM))
    log_a = jax.nn.log_sigmoid(g_lr @ gla_gk_up + gla_gk_bias) / GLA_GATE_NORMALIZER
    hk = lambda t: t.reshape(B, T, GLA_HEADS, GLA_HEAD_K)
    hv = lambda t: t.reshape(B, T, GLA_HEADS, GLA_HEAD_V)
    o = _gla(hk(gq), hk(gk), hv(gv), hk(log_a))
    o_gla = (_rms_norm(o, gla_norm_g, GLA_NORM_EPS) * jax.nn.silu(hv(g_out))).reshape(B, T, GLA_VALUE_DIM)

    p_rw = _token_shift(p_rw, rw_mu)
    r, kr, vr, w_lr, a_lr, gr_lr = _split(
        p_rw, (RWKV_WIDTH, RWKV_WIDTH, RWKV_WIDTH, RWKV_DECAY_RANK, RWKV_AAA_RANK, RWKV_GATE_RANK))
    w_log = -jax.nn.softplus(-(rw_w0 + jnp.tanh(w_lr) @ rw_w2)) - 0.5
    decay = jnp.exp(-jnp.exp(w_log.astype(jnp.float32)))
    if v_res is None:
        v_first = vr
    else:
        v0, v1, v2 = v_res
        vr = vr + (v_first - vr) * jax.nn.sigmoid(v0 + (vr @ v1) @ v2)
    a = jax.nn.sigmoid(rw_a0 + a_lr @ rw_a2)
    g = jax.nn.sigmoid(gr_lr) @ rw_g2
    hr = lambda t: t.reshape(B, T, RWKV_HEADS, RWKV_HEAD_DIM)
    kk = hr(kr * rw_k_k)
    kk = kk / jnp.maximum(jnp.sqrt(jnp.sum(kk * kk, axis=-1, keepdims=True)), 1e-12)
    kr = kr * (1.0 + (a - 1.0) * rw_k_a)
    y = _rwkv7_recurrence(hr(r), hr(decay), hr(kr), hr(vr), -kk, kk * hr(a))
    y = _group_norm(y, RWKV_LNX_EPS).reshape(B, T, RWKV_WIDTH) * rw_lnx_g + rw_lnx_b
    bonus = jnp.sum(hr(r) * hr(kr) * rw_r_k, axis=-1, keepdims=True) * hr(vr)
    o_rw = (y + bonus.reshape(B, T, RWKV_WIDTH)) * g

    gate_sb, gate_gla, gate_rw = jnp.split(jax.nn.sigmoid(p_merge), 3, axis=-1)
    merged = gate_sb * (o_sb @ w_br_sb) + gate_gla * (o_gla @ w_br_gla) + gate_rw * (o_rw @ w_br_rw)
    return merged @ w_o, v_first


def _routed_experts(h, top_e, gate, w_gate, w_up, w_down):
    T, D = h.shape
    n_assign = T * TOP_K
    flat_e = top_e.reshape(-1)
    order = jnp.argsort(flat_e)
    e_sorted = flat_e[order]
    tok_sorted = (order // TOP_K).astype(jnp.int32)
    w_sorted = gate.reshape(-1)[order]
    counts = jnp.bincount(flat_e, length=N_EXPERTS)
    padded = (counts + MOE_BLOCK - 1) // MOE_BLOCK * MOE_BLOCK
    start = jnp.cumsum(counts) - counts
    padded_end = jnp.cumsum(padded)
    padded_start = padded_end - padded
    dest = padded_start[e_sorted] + jnp.arange(n_assign) - start[e_sorted]
    n_blocks = -(-n_assign // MOE_BLOCK) + N_EXPERTS
    slots = n_blocks * MOE_BLOCK
    slot_tok = jnp.zeros((slots,), jnp.int32).at[dest].set(tok_sorted)
    slot_w = jnp.zeros((slots,), h.dtype).at[dest].set(w_sorted.astype(h.dtype))
    block_expert = jnp.minimum(
        jnp.searchsorted(padded_end, jnp.arange(n_blocks) * MOE_BLOCK, side='right'), N_EXPERTS - 1)

    def step(out, blk):
        tok, wt, e = blk
        xb = h[tok]
        hb = jax.nn.silu(xb @ w_gate[e]) * (xb @ w_up[e])
        return out.at[tok].add((hb @ w_down[e]) * wt[:, None]), None

    out, _ = lax.scan(step, jnp.zeros_like(h), (slot_tok.reshape(n_blocks, MOE_BLOCK),
                                                 slot_w.reshape(n_blocks, MOE_BLOCK), block_expert))
    return out


def _moe(h, w_router, router_bias, w_gate, w_up, w_down, w_sh_gate, w_sh_up, w_sh_down):
    T, D = h.shape
    scores = jax.nn.sigmoid((h @ w_router).astype(jnp.float32))
    biased = scores + router_bias
    per_group = N_EXPERTS // N_GROUPS
    group_score = jnp.sum(lax.top_k(biased.reshape(T, N_GROUPS, per_group), 2)[0], axis=-1)
    _, top_groups = lax.top_k(group_score, TOPK_GROUPS)
    group_ok = jnp.any(top_groups[:, :, None] == jnp.arange(N_GROUPS)[None, None, :], axis=1)
    expert_ok = jnp.repeat(group_ok, per_group, axis=1)
    _, top_e = lax.top_k(jnp.where(expert_ok, biased, -jnp.inf), TOP_K)
    gate = jnp.take_along_axis(scores, top_e, axis=1)
    gate = gate / jnp.sum(gate, axis=-1, keepdims=True) * ROUTED_SCALE
    routed = _routed_experts(h, top_e, gate, w_gate, w_up, w_down)
    shared = (jax.nn.silu(h @ w_sh_gate) * (h @ w_sh_up)) @ w_sh_down
    return routed + shared


def setup_inputs(seed: int = 0) -> dict:
    key = jax.random.key(seed)
    ks = iter(jax.random.split(key, 40))

    def nrm(shape, scale):
        return scale * jax.random.normal(next(ks), shape, jnp.float32)

    D, L = D_MODEL, DEPTH
    return {
        "x": nrm((BATCH, SEQ, D), 1.0),
        "c": nrm((BATCH, D), 1.0),
        "w_ada": nrm((L, D, 6 * D), 0.1 * D ** -0.5),
        "b_ada": nrm((L, 6 * D), 0.02),
        "w_in": nrm((L, D, N_IN), D ** -0.5),
        "gla_gk_up": nrm((L, GLA_GATE_RANK, GLA_KEY_DIM), GLA_GATE_RANK ** -0.5),
        "gla_gk_bias": nrm((L, GLA_KEY_DIM), 0.5),
        "gla_norm_g": 1.0 + nrm((L, GLA_HEAD_V), 0.02),
        "rw_mu": jax.random.uniform(next(ks), (L, RWKV_COLS), jnp.float32),
        "rw_w0": nrm((L, RWKV_WIDTH), 0.5),
        "rw_w2": nrm((L, RWKV_DECAY_RANK, RWKV_WIDTH), 0.1),
        "rw_a0": nrm((L, RWKV_WIDTH), 0.5),
        "rw_a2": nrm((L, RWKV_AAA_RANK, RWKV_WIDTH), 0.1),
        "rw_g2": nrm((L, RWKV_GATE_RANK, RWKV_WIDTH), RWKV_GATE_RANK ** -0.5),
        "rw_k_k": 1.0 + nrm((L, RWKV_WIDTH), 0.1),
        "rw_k_a": 1.0 + nrm((L, RWKV_WIDTH), 0.1),
        "rw_r_k": nrm((L, RWKV_HEADS, RWKV_HEAD_DIM), 0.1),
        "rw_lnx_g": 1.0 + nrm((L, RWKV_WIDTH), 0.02),
        "rw_lnx_b": nrm((L, RWKV_WIDTH), 0.02),
        "rw_v0": nrm((L - 1, RWKV_WIDTH), 0.5),
        "rw_v1": nrm((L - 1, RWKV_WIDTH, RWKV_VALUE_RANK), RWKV_WIDTH ** -0.5),
        "rw_v2": nrm((L - 1, RWKV_VALUE_RANK, RWKV_WIDTH), 0.1),
        "w_br_sb": nrm((L, SB_WIDTH, D), SB_WIDTH ** -0.5),
        "w_br_gla": nrm((L, GLA_VALUE_DIM, D), GLA_VALUE_DIM ** -0.5),
        "w_br_rw": nrm((L, RWKV_WIDTH, D), RWKV_WIDTH ** -0.5),
        "w_o": nrm((L, D, D), DEEPNORM_BETA * D ** -0.5),
        "ln_g": 1.0 + nrm((L, 2, D), 0.02),
        "ln_b": nrm((L, 2, D), 0.02),
        "w_router": nrm((L, D, N_EXPERTS), D ** -0.5),
        "router_bias": nrm((L, N_EXPERTS), 0.01),
        "w_exp_gate": nrm((L, N_EXPERTS, D, EXPERT_DIM), D ** -0.5),
        "w_exp_up": nrm((L, N_EXPERTS, D, EXPERT_DIM), D ** -0.5),
        "w_exp_down": nrm((L, N_EXPERTS, EXPERT_DIM, D), DEEPNORM_BETA * EXPERT_DIM ** -0.5),
        "w_sh_gate": nrm((L, D, SHARED_DIM), D ** -0.5),
        "w_sh_up": nrm((L, D, SHARED_DIM), D ** -0.5),
        "w_sh_down": nrm((L, SHARED_DIM, D), DEEPNORM_BETA * SHARED_DIM ** -0.5),
    }


def reference(x, c, w_ada, b_ada, w_in, gla_gk_up, gla_gk_bias, gla_norm_g, rw_mu, rw_w0, rw_w2, rw_a0,
              rw_a2, rw_g2, rw_k_k, rw_k_a, rw_r_k, rw_lnx_g, rw_lnx_b, rw_v0, rw_v1, rw_v2,
              w_br_sb, w_br_gla, w_br_rw, w_o, ln_g, ln_b, w_router, router_bias,
              w_exp_gate, w_exp_up, w_exp_down, w_sh_gate, w_sh_up, w_sh_down):
    B, T, D = x.shape
    v_first = None
    for l in range(DEPTH):
        mod = jax.nn.silu(c) @ w_ada[l] + b_ada[l]
        sh1, sc1, g1, sh2, sc2, g2 = jnp.split(mod[:, None, :], 6, axis=-1)

        u = x * (1.0 + sc1) + sh1
        v_res = None if l == 0 else (rw_v0[l - 1], rw_v1[l - 1], rw_v2[l - 1])
        mix, v_first = _mixer_sublayer(u, v_first, v_res, w_in[l], gla_gk_up[l], gla_gk_bias[l], gla_norm_g[l],
                                       rw_mu[l], rw_w0[l], rw_w2[l], rw_a0[l], rw_a2[l], rw_g2[l], rw_k_k[l],
                                       rw_k_a[l], rw_r_k[l], rw_lnx_g[l], rw_lnx_b[l],
                                       w_br_sb[l], w_br_gla[l], w_br_rw[l], w_o[l])
        x = _layer_norm(DEEPNORM_ALPHA * x + (1.0 + g1) * mix, ln_g[l, 0], ln_b[l, 0])

        u = x * (1.0 + sc2) + sh2
        ffn = _moe(u.reshape(B * T, D), w_router[l], router_bias[l], w_exp_gate[l], w_exp_up[l], w_exp_down[l],
                   w_sh_gate[l], w_sh_up[l], w_sh_down[l]).reshape(B, T, D)
        x = _layer_norm(DEEPNORM_ALPHA * x + (1.0 + g2) * ffn, ln_g[l, 1], ln_b[l, 1])
    return x
```

```python
import functools

import numpy as np
import jax
import jax.numpy as jnp
from jax import lax
from jax.experimental import pallas as pl
from jax.experimental.pallas import tpu as pltpu

F32 = jnp.float32
BF16 = jnp.bfloat16
I32 = jnp.int32

D_MODEL = 2048
DEPTH = 2
LN_EPS = 1e-5
DEEPNORM_ALPHA = (2 * DEPTH) ** 0.25

SB_HEADS = 8
SB_HEAD_DIM = 128
SB_WIDTH = SB_HEADS * SB_HEAD_DIM

GLA_HEADS = 4
GLA_HEAD_K = 128
GLA_HEAD_V = 256
GLA_KEY_DIM = GLA_HEADS * GLA_HEAD_K
GLA_VALUE_DIM = GLA_HEADS * GLA_HEAD_V
GLA_GATE_RANK = 16
GLA_GATE_NORMALIZER = 16.0
GLA_CHUNK = 64
GLA_NORM_EPS = 1e-5

RWKV_HEADS = 16
RWKV_HEAD_DIM = 64
RWKV_WIDTH = RWKV_HEADS * RWKV_HEAD_DIM
RWKV_DECAY_RANK = 64
RWKV_AAA_RANK = 64
RWKV_VALUE_RANK = 32
RWKV_GATE_RANK = 128
RWKV_LNX_EPS = 64e-5

N_EXPERTS = 64
TOP_K = 6
EXPERT_DIM = 512
SHARED_DIM = 512
N_GROUPS = 8
TOPK_GROUPS = 4
ROUTED_SCALE = 2.5

SB_COLS = 3 * SB_WIDTH
GLA_COLS = 2 * GLA_KEY_DIM + GLA_VALUE_DIM + GLA_GATE_RANK + GLA_VALUE_DIM
RWKV_COLS = 3 * RWKV_WIDTH + RWKV_DECAY_RANK + RWKV_AAA_RANK + RWKV_GATE_RANK
MERGE_COLS = 3 * D_MODEL

LANES = 128
SUBLANES = 8
VMEM_LIMIT = 56 * 1024 * 1024
MOE_BM = 256
SB_EXIT = -104.0

NN = (((1,), (0,)), ((), ()))
NT = (((1,), (1,)), ((), ()))
TN = (((0,), (0,)), ((), ()))


def _cp(*sem):
    return pltpu.CompilerParams(dimension_semantics=sem, vmem_limit_bytes=VMEM_LIMIT)


def _dot(a, b, dims=NN):
    return lax.dot_general(a, b, dims, preferred_element_type=F32)


def _split(a, n):
    out = []
    r = a
    for i in range(n):
        p = r.astype(BF16)
        out.append(p)
        if i + 1 < n:
            r = r - p.astype(F32)
    return out


def _dot_fx(a, e, n, dims=NN):
    acc = None
    for p in _split(a, n):
        t = _dot(p, e, dims)
        acc = t if acc is None else acc + t
    return acc


def _dot_xf(e, b, n, dims=NN):
    acc = None
    for p in _split(b, n):
        t = _dot(e, p, dims)
        acc = t if acc is None else acc + t
    return acc


def _dot_ff(a, b, dims=NN):
    a1, a2, a3 = _split(a, 3)
    b1, b2, b3 = _split(b, 3)
    acc = _dot(a1, b1, dims)
    for p, q in ((a1, b2), (a2, b1), (a2, b2), (a1, b3), (a3, b1)):
        acc = acc + _dot(p, q, dims)
    return acc


def _sigmoid(x):
    return 1.0 / (1.0 + jnp.exp(-x))


def _softplus(x):
    return jnp.maximum(x, 0.0) + jnp.log1p(jnp.exp(-jnp.abs(x)))


def _silu(x):
    return x * _sigmoid(x)


def _ada_kernel(c_ref, w_ref, b_ref, o_ref):
    c = c_ref[...]
    o_ref[0] = _dot_ff(_silu(c), w_ref[0]) + b_ref[0]


def _ada(c, w_ada, b_ada):
    L, D, N = w_ada.shape
    tn = 512
    c8 = jnp.broadcast_to(c, (SUBLANES, D))
    out = pl.pallas_call(
        _ada_kernel,
        grid=(L, N // tn),
        in_specs=[pl.BlockSpec((SUBLANES, D), lambda l, j: (0, 0)),
                  pl.BlockSpec((1, D, tn), lambda l, j: (l, 0, j)),
                  pl.BlockSpec((1, 1, tn), lambda l, j: (l, 0, j))],
        out_specs=pl.BlockSpec((1, SUBLANES, tn), lambda l, j: (l, 0, j)),
        out_shape=jax.ShapeDtypeStruct((L, SUBLANES, N), F32),
        compiler_params=_cp("parallel", "parallel"),
        name="ada",
    )(c8, w_ada, b_ada.reshape(L, 1, N))
    return out[:, 0, :]


def _modmm_kernel(x_ref, sc_ref, sh_ref, w_ref, o_ref, u_ref):
    @pl.when(pl.program_id(1) == 0)
    def _():
        u_ref[...] = (x_ref[...] * (1.0 + sc_ref[...]) + sh_ref[...]).astype(BF16)

    o_ref[...] = _dot(u_ref[...], w_ref[...]).astype(o_ref.dtype)


def _modmm(x, sc, sh, w, out_dtype, tn, tm=512, name="modmm"):
    T, D = x.shape
    N = w.shape[1]
    return pl.pallas_call(
        _modmm_kernel,
        grid=(T // tm, N // tn),
        in_specs=[pl.BlockSpec((tm, D), lambda i, j: (i, 0)),
                  pl.BlockSpec((1, D), lambda i, j: (0, 0)),
                  pl.BlockSpec((1, D), lambda i, j: (0, 0)),
                  pl.BlockSpec((D, tn), lambda i, j: (0, j))],
        out_specs=pl.BlockSpec((tm, tn), lambda i, j: (i, j)),
        out_shape=jax.ShapeDtypeStruct((T, N), out_dtype),
        scratch_shapes=[pltpu.VMEM((tm, D), BF16)],
        compiler_params=_cp("parallel", "arbitrary"),
        name=name,
    )(x, sc, sh, w)


def _sb_kernel(q_ref, k_ref, v_ref, o_ref, acc_ref, car_ref, *, bq, bk, scale):
    i = pl.program_id(1)
    acc_ref[...] = jnp.zeros_like(acc_ref)
    car_ref[...] = jnp.zeros_like(car_ref)
    q = q_ref[...]
    row = i * bq + lax.broadcasted_iota(I32, (bq, bk), 0)
    later = (lax.broadcasted_iota(I32, (bk, bk), 0) > lax.broadcasted_iota(I32, (bk, bk), 1)).astype(BF16)

    def body(c):
        j, _ = c
        off = pl.multiple_of(j * bk, bk)
        kb = k_ref[pl.ds(off, bk), :]
        vb = v_ref[pl.ds(off, bk), :]
        z = _dot(q, kb, NT) * scale
        mask = (j * bk + lax.broadcasted_iota(I32, (bq, bk), 1)) < row
        sp = _softplus(z)
        log_keep = jnp.where(mask, -sp, 0.0)
        carry = car_ref[...]
        log_later = _dot_fx(log_keep, later, 2) + carry
        w = jnp.where(mask, jnp.exp((z - sp) + log_later), 0.0)
        acc_ref[...] += _dot(w.astype(BF16), vb)
        carry = carry + jnp.sum(log_keep, axis=-1, keepdims=True)
        car_ref[...] = carry
        done = jnp.max(carry) < SB_EXIT
        return j - 1, done.astype(I32)

    lax.while_loop(lambda c: jnp.logical_and(c[0] >= 0, c[1] == 0), body,
                   ((i + 1) * (bq // bk) - 1, jnp.int32(0)))
    o_ref[...] = acc_ref[...].astype(o_ref.dtype)


def _sb_attention(qkv, bq=128, bk=128):
    T = qkv.shape[0]
    H, Dh = SB_HEADS, SB_HEAD_DIM
    kern = functools.partial(_sb_kernel, bq=bq, bk=bk, scale=Dh ** -0.5)
    return pl.pallas_call(
        kern,
        grid=(H, T // bq),
        in_specs=[pl.BlockSpec((bq, Dh), lambda h, i: (i, h)),
                  pl.BlockSpec((T, Dh), lambda h, i: (0, H + h)),
                  pl.BlockSpec((T, Dh), lambda h, i: (0, 2 * H + h))],
        out_specs=pl.BlockSpec((bq, Dh), lambda h, i: (i, h)),
        out_shape=jax.ShapeDtypeStruct((T, H * Dh), BF16),
        scratch_shapes=[pltpu.VMEM((bq, Dh), F32), pltpu.VMEM((bq, 1), F32)],
        compiler_params=_cp("parallel", "parallel"),
        name="sb_attention",
    )(qkv, qkv, qkv)


_GLA_LEVELS = (32, 16, 8, 4, 2, 1)


def _gla_constants():
    C = GLA_CHUNK
    i = np.arange(C)[:, None]
    s = np.arange(C)[None, :]
    sel = []
    masks = []
    for m in _GLA_LEVELS:
        same = (i // m) == (s // m)
        sel.append(same & (s <= i) & ((i // m) % 2 == 1))
        sel.append(same & (s > i) & ((i // m) % 2 == 0))
        masks.append(((i // (2 * m)) == (s // (2 * m))) & ((i // m) % 2 == 1) & ((s // m) % 2 == 0))
    sel.append(s <= i)
    sel.append(s > i)
    masks.append(i == s)
    return (np.concatenate(sel, 0).astype(np.float32), np.stack(masks, 0).astype(np.float32))


def _gla_kernel(q_ref, k_ref, v_ref, go_ref, glr_ref, up_ref, bias_ref, ng_ref, sel_ref, msk_ref,
                o_ref, st_ref, *, n_chunks):
    C = GLA_CHUNK
    nl = len(_GLA_LEVELS)

    @pl.when(pl.program_id(1) == 0)
    def _():
        st_ref[...] = jnp.zeros_like(st_ref)

    sel = sel_ref[...]
    up = up_ref[...]
    bias = bias_ref[...]
    ng = ng_ref[...]
    for c in range(n_chunks):
        rows = slice(c * C, (c + 1) * C)
        q = q_ref[rows, :] * (GLA_HEAD_K ** -0.5)
        k = k_ref[rows, :]
        v = v_ref[rows, :]
        x = _dot(glr_ref[rows, :].astype(BF16), up) + bias
        g = -_softplus(-x) * (1.0 / GLA_GATE_NORMALIZER)
        f = jnp.exp(_dot_xf(sel, g, 2))
        scores = None
        for l in range(nl):
            ql = (q * f[(2 * l) * C:(2 * l + 1) * C]).astype(BF16)
            kl = (k * f[(2 * l + 1) * C:(2 * l + 2) * C]).astype(BF16)
            t = _dot(ql, kl, NT) * msk_ref[l]
            scores = t if scores is None else scores + t
        scores = scores + _dot(q.astype(BF16), k.astype(BF16), NT) * msk_ref[nl]
        f_start = f[(2 * nl) * C:(2 * nl + 1) * C]
        f_end = f[(2 * nl + 1) * C:(2 * nl + 2) * C]
        st = st_ref[...]
        o = _dot((q * f_start).astype(BF16), st.astype(BF16), NT)
        o = o + _dot(scores.astype(BF16), v.astype(BF16))
        st_ref[...] = st * f_start[C - 1:C, :] + _dot(v.astype(BF16), (k * f_end).astype(BF16), TN)
        rms = lax.rsqrt(jnp.mean(o * o, axis=-1, keepdims=True) + GLA_NORM_EPS)
        o_ref[rows, :] = (o * rms * ng * _silu(go_ref[rows, :])).astype(o_ref.dtype)


def _gla(p_gla, gk_up, gk_bias, norm_g, tt=256):
    T = p_gla.shape[0]
    H, dk, dv = GLA_HEADS, GLA_HEAD_K, GLA_HEAD_V
    sel, msk = _gla_constants()
    up = jnp.zeros((LANES, GLA_KEY_DIM), F32).at[:GLA_GATE_RANK].set(gk_up).astype(BF16)
    kern = functools.partial(_gla_kernel, n_chunks=tt // GLA_CHUNK)
    return pl.pallas_call(
        kern,
        grid=(H, T // tt),
        in_specs=[pl.BlockSpec((tt, dk), lambda h, i: (i, h)),
                  pl.BlockSpec((tt, dk), lambda h, i: (i, H + h)),
                  pl.BlockSpec((tt, dv), lambda h, i: (i, H + h)),
                  pl.BlockSpec((tt, dv), lambda h, i: (i, 2 * H + h)),
                  pl.BlockSpec((tt, LANES), lambda h, i: (i, 3 * GLA_VALUE_DIM // LANES)),
                  pl.BlockSpec((LANES, dk), lambda h, i: (0, h)),
                  pl.BlockSpec((1, dk), lambda h, i: (0, h)),
                  pl.BlockSpec((1, dv), lambda h, i: (0, 0)),
                  pl.BlockSpec(sel.shape, lambda h, i: (0, 0)),
                  pl.BlockSpec(msk.shape, lambda h, i: (0, 0, 0))],
        out_specs=pl.BlockSpec((tt, dv), lambda h, i: (i, h)),
        out_shape=jax.ShapeDtypeStruct((T, H * dv), BF16),
        scratch_shapes=[pltpu.VMEM((dv, dk), F32)],
        compiler_params=_cp("parallel", "arbitrary"),
        name="gla",
    )(p_gla, p_gla, p_gla, p_gla, p_gla, up, gk_bias.reshape(1, -1), norm_g.reshape(1, -1),
      jnp.asarray(sel, BF16), jnp.asarray(msk, F32))


def _head_ones():
    l = np.arange(LANES)
    return (l[:, None] // RWKV_HEAD_DIM == l[None, :] // RWKV_HEAD_DIM).astype(np.float32)


def _head_diag():
    i = np.arange(RWKV_HEAD_DIM)[:, None]
    l = np.arange(LANES)[None, :]
    return (l % RWKV_HEAD_DIM == i).astype(np.float32)


def _rwprep_kernel(p_ref, pp_ref, mu_ref, w0_ref, w2_ref, a0_ref, a2_ref, g2_ref, kk_ref, ka_ref,
                   v0_ref, v1_ref, v2_ref, vf_ref, bo_ref,
                   r_ref, w_ref, k_ref, v_ref, a_ref, b_ref, g_ref, *, first_layer):
    W = RWKV_WIDTH
    i = pl.program_id(0)
    p = p_ref[...]
    tm = p.shape[0]
    prev_row = jnp.where(i == 0, 0.0, pp_ref[SUBLANES - 1:SUBLANES, :])
    rid = lax.broadcasted_iota(I32, p.shape, 0)
    prev = jnp.where(rid == 0, prev_row, pltpu.roll(p, 1, axis=0))
    ps = p + (prev - p) * mu_ref[...]
    r = ps[:, 0:W]
    kr = ps[:, W:2 * W]
    vr = ps[:, 2 * W:3 * W]
    lr = ps[:, 3 * W:3 * W + LANES]
    gl = ps[:, 3 * W + LANES:3 * W + 2 * LANES]
    w_log = -_softplus(-(w0_ref[...] + _dot(jnp.tanh(lr).astype(BF16), w2_ref[...]))) - 0.5
    w_ref[...] = jnp.exp(-jnp.exp(w_log))
    if not first_layer:
        lo = _dot(vr.astype(BF16), v1_ref[...])
        gate = _sigmoid(v0_ref[...] + _dot(lo.astype(BF16), v2_ref[...]))
        vr = vr + (vf_ref[...] - vr) * gate
    v_ref[...] = vr
    a = _sigmoid(a0_ref[...] + _dot(lr.astype(BF16), a2_ref[...]))
    g_ref[...] = _dot(_sigmoid(gl).astype(BF16), g2_ref[...])
    kk = kr * kk_ref[...]
    bo = bo_ref[...]
    for hp in range(W // LANES):
        sl = slice(hp * LANES, (hp + 1) * LANES)
        kh = kk[:, sl]
        ss = _dot_fx(kh * kh, bo, 3)
        kn = kh / jnp.maximum(jnp.sqrt(ss), 1e-12)
        a_ref[:, sl] = -kn
        b_ref[:, sl] = kn * a[:, sl]
    r_ref[...] = r
    k_ref[...] = kr * (1.0 + (a - 1.0) * ka_ref[...])


def _rwkv_prep(p_rw, v_first, mu, w0, w2, a0, a2, g2, k_k, k_a, v0, v1, v2, tm=256):
    T, NC = p_rw.shape
    W = RWKV_WIDTH
    first = v_first is None
    w2p = jnp.zeros((LANES, W), F32).at[:RWKV_DECAY_RANK].set(w2).astype(BF16)
    a2p = jnp.zeros((LANES, W), F32).at[RWKV_DECAY_RANK:].set(a2).astype(BF16)
    if first:
        v_first = jnp.zeros((SUBLANES, W), F32)
        v0 = jnp.zeros((W,), F32)
        v1p = jnp.zeros((W, LANES), BF16)
        v2p = jnp.zeros((LANES, W), BF16)
        vf_spec = pl.BlockSpec((SUBLANES, W), lambda i: (0, 0))
    else:
        v1p = jnp.zeros((W, LANES), F32).at[:, :RWKV_VALUE_RANK].set(v1).astype(BF16)
        v2p = jnp.zeros((LANES, W), F32).at[:RWKV_VALUE_RANK].set(v2).astype(BF16)
        vf_spec = pl.BlockSpec((tm, W), lambda i: (i, 0))
    row = lambda a: a.reshape(1, -1)
    full = lambda shape: pl.BlockSpec(shape, lambda i: (0,) * len(shape))
    out = jax.ShapeDtypeStruct((T, W), F32)
    ospec = pl.BlockSpec((tm, W), lambda i: (i, 0))
    kern = functools.partial(_rwprep_kernel, first_layer=first)
    return pl.pallas_call(
        kern,
        grid=(T // tm,),
        in_specs=[pl.BlockSpec((tm, NC), lambda i: (i, 0)),
                  pl.BlockSpec((SUBLANES, NC), lambda i: (jnp.maximum(i * (tm // SUBLANES) - 1, 0), 0)),
                  full((1, NC)), full((1, W)), full((LANES, W)), full((1, W)), full((LANES, W)),
                  full((LANES, W)), full((1, W)), full((1, W)),
                  full((1, W)), full((W, LANES)), full((LANES, W)), vf_spec, full((LANES, LANES))],
        out_specs=[ospec] * 7,
        out_shape=[out] * 7,
        compiler_params=_cp("parallel"),
        name="rwkv_prep",
    )(p_rw, p_rw, row(mu), row(w0), w2p, row(a0), a2p, g2.astype(BF16), row(k_k), row(k_a),
      row(v0), v1p, v2p, v_first, jnp.asarray(_head_ones(), BF16))


def _rwrec_kernel(r_ref, w_ref, k_ref, v_ref, a_ref, b_ref, bo_ref, dg_ref, y_ref, s_ref, *, tt):
    N = RWKV_HEAD_DIM

    @pl.when(pl.program_id(0) == 0)
    def _():
        s_ref[...] = jnp.zeros_like(s_ref)

    bo = bo_ref[...]
    diag = dg_ref[...] > 0.5
    sub = lax.broadcasted_iota(I32, (SUBLANES, LANES), 0)

    def group(gi, carry):
        off = pl.multiple_of(gi * SUBLANES, SUBLANES)
        R = r_ref[pl.ds(off, SUBLANES), :]
        Wd = w_ref[pl.ds(off, SUBLANES), :]
        K = k_ref[pl.ds(off, SUBLANES), :]
        V = v_ref[pl.ds(off, SUBLANES), :]
        A = a_ref[pl.ds(off, SUBLANES), :]
        B = b_ref[pl.ds(off, SUBLANES), :]
        for hp in range(RWKV_WIDTH // LANES):
            sl = slice(hp * LANES, (hp + 1) * LANES)
            S = s_ref[hp]
            yacc = jnp.zeros((SUBLANES, LANES), F32)
            for t in range(SUBLANES):
                bc = lambda X: jnp.broadcast_to(X[t:t + 1, sl], (N, LANES))
                sa = _dot_fx(S * bc(A), bo, 2)
                vcol = _dot_fx(jnp.where(diag, bc(V), 0.0), bo, 3)
                S = S * bc(Wd) + sa * bc(B) + vcol * bc(K)
                yb = _dot_fx(S * bc(R), bo, 2)
                yrow = jnp.sum(jnp.where(diag, yb, 0.0), axis=0, keepdims=True)
                yacc = jnp.where(sub == t, yrow, yacc)
            s_ref[hp] = S
            y_ref[pl.ds(off, SUBLANES), sl] = yacc
        return carry

    lax.fori_loop(0, tt // SUBLANES, group, 0)


def _rwkv_recurrence(r, w, k, v, a, b, tt=256):
    T, W = r.shape
    spec = pl.BlockSpec((tt, W), lambda i: (i, 0))
    kern = functools.partial(_rwrec_kernel, tt=tt)
    return pl.pallas_call(
        kern,
        grid=(T // tt,),
        in_specs=[spec] * 6 + [pl.BlockSpec((LANES, LANES), lambda i: (0, 0)),
                               pl.BlockSpec((RWKV_HEAD_DIM, LANES), lambda i: (0, 0))],
        out_specs=spec,
        out_shape=jax.ShapeDtypeStruct((T, W), F32),
        scratch_shapes=[pltpu.VMEM((W // LANES, RWKV_HEAD_DIM, LANES), F32)],
        compiler_params=_cp("arbitrary"),
        name="rwkv_recurrence",
    )(r, w, k, v, a, b, jnp.asarray(_head_ones(), BF16), jnp.asarray(_head_diag(), F32))


def _rwpost_kernel(y_ref, r_ref, k_ref, v_ref, g_ref, lg_ref, lb_ref, rk_ref, bo_ref, o_ref):
    bo = bo_ref[...]
    inv = 1.0 / RWKV_HEAD_DIM
    for hp in range(RWKV_WIDTH // LANES):
        sl = slice(hp * LANES, (hp + 1) * LANES)
        y = y_ref[:, sl]
        mu = _dot_fx(y, bo, 3) * inv
        d = y - mu
        var = _dot_fx(d * d, bo, 3) * inv
        yn = d * lax.rsqrt(var + RWKV_LNX_EPS) * lg_ref[:, sl] + lb_ref[:, sl]
        bonus = _dot_fx(r_ref[:, sl] * k_ref[:, sl] * rk_ref[:, sl], bo, 3) * v_ref[:, sl]
        o_ref[:, sl] = ((yn + bonus) * g_ref[:, sl]).astype(o_ref.dtype)


def _rwkv_post(y, r, k, v, g, lnx_g, lnx_b, r_k, tm=512):
    T, W = y.shape
    spec = pl.BlockSpec((tm, W), lambda i: (i, 0))
    rowspec = pl.BlockSpec((1, W), lambda i: (0, 0))
    return pl.pallas_call(
        _rwpost_kernel,
        grid=(T // tm,),
        in_specs=[spec] * 5 + [rowspec] * 3 + [pl.BlockSpec((LANES, LANES), lambda i: (0, 0))],
        out_specs=spec,
        out_shape=jax.ShapeDtypeStruct((T, W), BF16),
        compiler_params=_cp("parallel"),
        name="rwkv_post",
    )(y, r, k, v, g, lnx_g.reshape(1, W), lnx_b.reshape(1, W), r_k.reshape(1, W),
      jnp.asarray(_head_ones(), BF16))


def _merge_kernel(osb_ref, ogl_ref, orw_ref, g1_ref, g2_ref, g3_ref, w1_ref, w2_ref, w3_ref, o_ref):
    acc = _sigmoid(g1_ref[...].astype(F32)) * _dot(osb_ref[...], w1_ref[...])
    acc = acc + _sigmoid(g2_ref[...].astype(F32)) * _dot(ogl_ref[...], w2_ref[...])
    acc = acc + _sigmoid(g3_ref[...].astype(F32)) * _dot(orw_ref[...], w3_ref[...])
    o_ref[...] = acc.astype(o_ref.dtype)


def _merge(o_sb, o_gla, o_rw, p_merge, w_sb, w_gla, w_rw, tm=512, tn=1024):
    T = o_sb.shape[0]
    D = D_MODEL
    nb = D // tn
    aspec = pl.BlockSpec((tm, 1024), lambda j, i: (i, 0))
    wspec = pl.BlockSpec((1024, tn), lambda j, i: (0, j))
    gspec = lambda b: pl.BlockSpec((tm, tn), lambda j, i: (i, b * nb + j))
    return pl.pallas_call(
        _merge_kernel,
        grid=(nb, T // tm),
        in_specs=[aspec, aspec, aspec, gspec(0), gspec(1), gspec(2), wspec, wspec, wspec],
        out_specs=pl.BlockSpec((tm, tn), lambda j, i: (i, j)),
        out_shape=jax.ShapeDtypeStruct((T, D), BF16),
        compiler_params=_cp("parallel", "parallel"),
        name="merge",
    )(o_sb, o_gla, o_rw, p_merge, p_merge, p_merge, w_sb, w_gla, w_rw)


def _layer_norm(h, g, b):
    mu = jnp.mean(h, axis=-1, keepdims=True)
    d = h - mu
    var = jnp.mean(d * d, axis=-1, keepdims=True)
    return d * lax.rsqrt(var + LN_EPS) * g + b


def _oproj_kernel(m_ref, w_ref, x_ref, gate_ref, lg_ref, lb_ref, o_ref):
    mix = _dot(m_ref[...], w_ref[...])
    h = DEEPNORM_ALPHA * x_ref[...] + (1.0 + gate_ref[...]) * mix
    o_ref[...] = _layer_norm(h, lg_ref[...], lb_ref[...])


def _oproj(merged, w_o, x, gate, ln_g, ln_b, tm=512):
    T, D = x.shape
    spec = pl.BlockSpec((tm, D), lambda i: (i, 0))
    rowspec = pl.BlockSpec((1, D), lambda i: (0, 0))
    return pl.pallas_call(
        _oproj_kernel,
        grid=(T // tm,),
        in_specs=[spec, pl.BlockSpec((D, D), lambda i: (0, 0)), spec, rowspec, rowspec, rowspec],
        out_specs=spec,
        out_shape=jax.ShapeDtypeStruct((T, D), F32),
        compiler_params=_cp("parallel"),
        name="oproj_norm",
    )(merged, w_o, x, gate, ln_g.reshape(1, D), ln_b.reshape(1, D))


def _first_max(x, idx, axis, big):
    m = jnp.max(x, axis=axis, keepdims=True)
    first = jnp.min(jnp.where(x == m, idx, big), axis=axis, keepdims=True)
    return m, idx == first


def _router_kernel(x_ref, sc_ref, sh_ref, wr_ref, rb_ref, u_ref, e_ref, g_ref, p_ref, cnt_ref, car_ref, *, tm):
    E, G = N_EXPERTS, N_GROUPS
    per = E // G
    neg = -jnp.inf

    @pl.when(pl.program_id(0) == 0)
    def _():
        car_ref[...] = jnp.zeros_like(car_ref)

    u = x_ref[...] * (1.0 + sc_ref[...]) + sh_ref[...]
    u_ref[...] = u
    scores = _sigmoid(_dot_ff(wr_ref[...], u, NT))
    biased = scores + rb_ref[...]
    b3 = biased.reshape(G, per, tm)
    i3 = lax.broadcasted_iota(I32, (G, per, tm), 1)
    m1, hit = _first_max(b3, i3, 1, per)
    m2 = jnp.max(jnp.where(hit, neg, b3), axis=1, keepdims=True)
    gs = (m1 + m2).reshape(G, tm)
    gi = lax.broadcasted_iota(I32, (G, tm), 0)
    ok = jnp.zeros((G, tm), jnp.bool_)
    for _ in range(TOPK_GROUPS):
        _, hit = _first_max(gs, gi, 0, G)
        ok = jnp.logical_or(ok, hit)
        gs = jnp.where(hit, neg, gs)
    ok_e = jnp.broadcast_to(ok.reshape(G, 1, tm), (G, per, tm)).reshape(E, tm)
    cand = jnp.where(ok_e, biased, neg)
    ei = lax.broadcasted_iota(I32, (E, tm), 0)
    sels, gates = [], []
    for _ in range(TOP_K):
        _, hit = _first_max(cand, ei, 0, E)
        sels.append(hit)
        gates.append(jnp.sum(jnp.where(hit, scores, 0.0), axis=0, keepdims=True))
        cand = jnp.where(hit, neg, cand)
    denom = gates[0]
    for gk in gates[1:]:
        denom = denom + gk
    chosen = sels[0]
    for s in sels[1:]:
        chosen = jnp.logical_or(chosen, s)
    cnt = jnp.where(chosen, 1.0, 0.0)
    earlier = (lax.broadcasted_iota(I32, (tm, tm), 0) < lax.broadcasted_iota(I32, (tm, tm), 1)).astype(BF16)
    rank = _dot(cnt.astype(BF16), earlier) + car_ref[...][:, 0:1]
    car_ref[...] = car_ref[...] + jnp.sum(cnt, axis=1, keepdims=True)
    cnt_ref[...] = car_ref[...]
    zi = jnp.zeros((1, tm), I32)
    zf = jnp.zeros((1, tm), F32)
    e_rows = [jnp.sum(jnp.where(s, ei, 0), axis=0, keepdims=True) for s in sels]
    p_rows = [jnp.sum(jnp.where(s, rank, 0.0), axis=0, keepdims=True).astype(I32) for s in sels]
    g_rows = [gk / denom * ROUTED_SCALE for gk in gates]
    e_ref[...] = jnp.concatenate(e_rows + [zi, zi], axis=0)
    p_ref[...] = jnp.concatenate(p_rows + [zi, zi], axis=0)
    g_ref[...] = jnp.concatenate(g_rows + [zf, zf], axis=0)


def _router(x, sc, sh, w_router, router_bias, tm=256):
    T, D = x.shape
    E = N_EXPERTS
    kern = functools.partial(_router_kernel, tm=tm)
    rowspec = pl.BlockSpec((1, D), lambda i: (0, 0))
    kspec = pl.BlockSpec((SUBLANES, tm), lambda i: (0, i))
    return pl.pallas_call(
        kern,
        grid=(T // tm,),
        in_specs=[pl.BlockSpec((tm, D), lambda i: (i, 0)), rowspec, rowspec,
                  pl.BlockSpec((E, D), lambda i: (0, 0)), pl.BlockSpec((E, 1), lambda i: (0, 0))],
        out_specs=[pl.BlockSpec((tm, D), lambda i: (i, 0)), kspec, kspec, kspec,
                   pl.BlockSpec((E, LANES), lambda i: (0, 0))],
        out_shape=[jax.ShapeDtypeStruct((T, D), F32),
                   jax.ShapeDtypeStruct((SUBLANES, T), I32),
                   jax.ShapeDtypeStruct((SUBLANES, T), F32),
                   jax.ShapeDtypeStruct((SUBLANES, T), I32),
                   jax.ShapeDtypeStruct((E, LANES), F32)],
        scratch_shapes=[pltpu.VMEM((E, LANES), F32)],
        compiler_params=_cp("arbitrary"),
        name="router",
    )(x, sc, sh, w_router.T, router_bias.reshape(E, 1))


def _dispatch_kernel(dest_ref, fill_ref, u_ref, z_ref, xs_ref, sem, zsem, *, tm, bm):
    i = pl.program_id(0)

    @pl.when(i == 0)
    def _():
        def zcopy(e):
            return pltpu.make_async_copy(z_ref, xs_ref.at[pl.ds(pl.multiple_of(fill_ref[e], bm), bm)], zsem)

        def zstart(e, c):
            zcopy(e).start()
            return c

        def zwait(e, c):
            zcopy(e).wait()
            return c

        lax.fori_loop(0, N_EXPERTS, zstart, 0)
        lax.fori_loop(0, N_EXPERTS, zwait, 0)

    def copy(n):
        t = n // TOP_K
        k = n - t * TOP_K
        return pltpu.make_async_copy(u_ref.at[pl.ds(i * tm + t, 1)], xs_ref.at[pl.ds(dest_ref[k, t], 1)], sem)

    def start(n, c):
        copy(n).start()
        return c

    def wait(n, c):
        copy(n).wait()
        return c

    lax.fori_loop(0, tm * TOP_K, start, 0)
    lax.fori_loop(0, tm * TOP_K, wait, 0)


def _dispatch(u, dest, fill, n_slots, tm=512, bm=MOE_BM):
    T, D = u.shape
    kern = functools.partial(_dispatch_kernel, tm=tm, bm=bm)
    zeros = jnp.zeros((bm, D), F32)
    return pl.pallas_call(
        kern,
        grid=(T // tm,),
        in_specs=[pl.BlockSpec((SUBLANES, tm), lambda i: (0, i), memory_space=pltpu.SMEM),
                  pl.BlockSpec(memory_space=pltpu.SMEM),
                  pl.BlockSpec(memory_space=pl.ANY),
                  pl.BlockSpec(memory_space=pl.ANY)],
        out_specs=pl.BlockSpec(memory_space=pl.ANY),
        out_shape=jax.ShapeDtypeStruct((n_slots, D), F32),
        scratch_shapes=[pltpu.SemaphoreType.DMA(()), pltpu.SemaphoreType.DMA(())],
        compiler_params=_cp("arbitrary"),
        name="moe_dispatch",
    )(dest, fill, u, zeros)


def _expert_kernel(be_ref, nv_ref, x_ref, wg_ref, wu_ref, wd_ref, y_ref):
    @pl.when(pl.program_id(0) < nv_ref[0])
    def _():
        x = x_ref[...].astype(BF16)
        h = _silu(_dot(x, wg_ref[0])) * _dot(x, wu_ref[0])
        y_ref[...] = _dot(h.astype(BF16), wd_ref[0])


def _experts(xs, block_expert, n_valid, w_gate, w_up, w_down, bm=MOE_BM):
    n_slots, D = xs.shape
    nb = n_slots // bm
    F = w_gate.shape[-1]
    blk = lambda b, be, nv: (jnp.minimum(b, nv[0] - 1), 0)
    wsel = lambda b, be, nv: (be[jnp.minimum(b, nv[0] - 1)], 0, 0)
    gs = pltpu.PrefetchScalarGridSpec(
        num_scalar_prefetch=2,
        grid=(nb,),
        in_specs=[pl.BlockSpec((bm, D), blk),
                  pl.BlockSpec((1, D, F), wsel),
                  pl.BlockSpec((1, D, F), wsel),
                  pl.BlockSpec((1, F, D), wsel)],
        out_specs=pl.BlockSpec((bm, D), blk),
    )
    return pl.pallas_call(
        _expert_kernel,
        grid_spec=gs,
        out_shape=jax.ShapeDtypeStruct((n_slots, D), F32),
        compiler_params=_cp("arbitrary"),
        name="moe_experts",
    )(block_expert, n_valid, xs, w_gate, w_up, w_down)


def _combine_kernel(dest_ref, ys_ref, u_ref, x_ref, gt_ref, wg_ref, wu_ref, wd_ref, gate_ref, lg_ref, lb_ref,
                    o_ref, buf_ref, sem, *, tm):
    def copy(n):
        t = n // TOP_K
        k = n - t * TOP_K
        return pltpu.make_async_copy(ys_ref.at[pl.ds(dest_ref[k, t], 1)], buf_ref.at[k, pl.ds(t, 1)], sem)

    def start(n, c):
        copy(n).start()
        return c

    def wait(n, c):
        copy(n).wait()
        return c

    lax.fori_loop(0, tm * TOP_K, start, 0)
    u = u_ref[...].astype(BF16)
    h = _silu(_dot(u, wg_ref[...])) * _dot(u, wu_ref[...])
    ffn = _dot(h.astype(BF16), wd_ref[...])
    lax.fori_loop(0, tm * TOP_K, wait, 0)
    gt = gt_ref[...]
    for k in range(TOP_K):
        ffn = ffn + gt[:, k:k + 1] * buf_ref[k]
    hres = DEEPNORM_ALPHA * x_ref[...] + (1.0 + gate_ref[...]) * ffn
    o_ref[...] = _layer_norm(hres, lg_ref[...], lb_ref[...])


def _combine(ys, dest, u, x, gates_t, w_sg, w_su, w_sd, gate, ln_g, ln_b, tm=128):
    T, D = x.shape
    F = w_sg.shape[-1]
    kern = functools.partial(_combine_kernel, tm=tm)
    spec = pl.BlockSpec((tm, D), lambda i: (i, 0))
    rowspec = pl.BlockSpec((1, D), lambda i: (0, 0))
    return pl.pallas_call(
        kern,
        grid=(T // tm,),
        in_specs=[pl.BlockSpec((SUBLANES, tm), lambda i: (0, i), memory_space=pltpu.SMEM),
                  pl.BlockSpec(memory_space=pl.ANY),
                  spec, spec,
                  pl.BlockSpec((tm, SUBLANES), lambda i: (i, 0)),
                  pl.BlockSpec((D, F), lambda i: (0, 0)),
                  pl.BlockSpec((D, F), lambda i: (0, 0)),
                  pl.BlockSpec((F, D), lambda i: (0, 0)),
                  rowspec, rowspec, rowspec],
        out_specs=spec,
        out_shape=jax.ShapeDtypeStruct((T, D), F32),
        scratch_shapes=[pltpu.VMEM((TOP_K, tm, D), F32), pltpu.SemaphoreType.DMA(())],
        compiler_params=_cp("arbitrary"),
        name="moe_combine_norm",
    )(dest, ys, u, x, gates_t, w_sg, w_su, w_sd, gate, ln_g.reshape(1, D), ln_b.reshape(1, D))


def _moe_plan(top_e, rank, counts, bm, n_blocks):
    counts = counts.astype(I32)
    padded = (counts + bm - 1) // bm * bm
    padded_end = jnp.cumsum(padded)
    padded_start = padded_end - padded
    dest = padded_start[top_e] + rank
    fill = jnp.maximum(padded_end - bm, padded_start)
    fill = jnp.minimum(fill, (n_blocks - 1) * bm)
    block_expert = jnp.minimum(
        jnp.searchsorted(padded_end, jnp.arange(n_blocks, dtype=I32) * bm, side="right"), N_EXPERTS - 1).astype(I32)
    n_valid = jnp.maximum(padded_end[-1:] // bm, 1).astype(I32)
    return dest.astype(I32), fill.astype(I32), block_expert, n_valid


def _in_proj_weights(w_in):
    o = SB_COLS
    gq = w_in[:, o:o + 2 * GLA_KEY_DIM + GLA_VALUE_DIM]
    o += 2 * GLA_KEY_DIM + GLA_VALUE_DIM
    glr = w_in[:, o:o + GLA_GATE_RANK]
    o += GLA_GATE_RANK
    gout = w_in[:, o:o + GLA_VALUE_DIM]
    o += GLA_VALUE_DIM
    pad = jnp.zeros((w_in.shape[0], LANES - GLA_GATE_RANK), w_in.dtype)
    w_gla = jnp.concatenate([gq, gout, glr, pad], axis=1)
    w_rw = w_in[:, o:o + RWKV_COLS]
    o += RWKV_COLS
    w_mg = w_in[:, o:o + MERGE_COLS]
    return (w_in[:, :SB_COLS].astype(BF16), w_gla.astype(BF16), w_rw.astype(BF16), w_mg.astype(BF16))


def kernel(x, c, w_ada, b_ada, w_in, gla_gk_up, gla_gk_bias, gla_norm_g, rw_mu, rw_w0, rw_w2, rw_a0,
           rw_a2, rw_g2, rw_k_k, rw_k_a, rw_r_k, rw_lnx_g, rw_lnx_b, rw_v0, rw_v1, rw_v2,
           w_br_sb, w_br_gla, w_br_rw, w_o, ln_g, ln_b, w_router, router_bias,
           w_exp_gate, w_exp_up, w_exp_down, w_sh_gate, w_sh_up, w_sh_down):
    B, T, D = x.shape
    assert B == 1 and D == D_MODEL
    n_blocks = -(-(T * TOP_K) // MOE_BM) + N_EXPERTS
    mod = _ada(c, w_ada, b_ada)
    xs = x.reshape(T, D)
    v_first = None
    for l in range(DEPTH):
        sh1, sc1, g1, sh2, sc2, g2 = (mod[l, j * D:(j + 1) * D].reshape(1, D) for j in range(6))
        w_sb, w_gla, w_rw, w_mg = _in_proj_weights(w_in[l])
        p_sb = _modmm(xs, sc1, sh1, w_sb, BF16, 512, name="inproj_sb")
        p_gla = _modmm(xs, sc1, sh1, w_gla, F32, 640, name="inproj_gla")
        p_rw = _modmm(xs, sc1, sh1, w_rw, F32, 1664, name="inproj_rwkv")
        p_mg = _modmm(xs, sc1, sh1, w_mg, BF16, 1024, name="inproj_merge")

        o_sb = _sb_attention(p_sb)
        o_gla = _gla(p_gla, gla_gk_up[l], gla_gk_bias[l], gla_norm_g[l])
        if l == 0:
            r, w, k, v, a, b, g = _rwkv_prep(p_rw, None, rw_mu[l], rw_w0[l], rw_w2[l], rw_a0[l], rw_a2[l],
                                             rw_g2[l], rw_k_k[l], rw_k_a[l], None, None, None)
            v_first = v
        else:
            r, w, k, v, a, b, g = _rwkv_prep(p_rw, v_first, rw_mu[l], rw_w0[l], rw_w2[l], rw_a0[l], rw_a2[l],
                                             rw_g2[l], rw_k_k[l], rw_k_a[l], rw_v0[l - 1], rw_v1[l - 1],
                                             rw_v2[l - 1])
        y = _rwkv_recurrence(r, w, k, v, a, b)
        o_rw = _rwkv_post(y, r, k, v, g, rw_lnx_g[l], rw_lnx_b[l], rw_r_k[l])

        merged = _merge(o_sb, o_gla, o_rw, p_mg, w_br_sb[l].astype(BF16), w_br_gla[l].astype(BF16),
                        w_br_rw[l].astype(BF16))
        xs = _oproj(merged, w_o[l].astype(BF16), xs, g1, ln_g[l, 0], ln_b[l, 0])

        u2, top_e, gates, rank, counts = _router(xs, sc2, sh2, w_router[l], router_bias[l])
        dest, fill, block_expert, n_valid = _moe_plan(top_e, rank, counts[:, 0], MOE_BM, n_blocks)
        x_sorted = _dispatch(u2, dest, fill, n_blocks * MOE_BM)
        y_sorted = _experts(x_sorted, block_expert, n_valid, w_exp_gate[l].astype(BF16),
                            w_exp_up[l].astype(BF16), w_exp_down[l].astype(BF16))
        xs = _combine(y_sorted, dest, u2, xs, gates.T, w_sh_gate[l].astype(BF16), w_sh_up[l].astype(BF16),
                      w_sh_down[l].astype(BF16), g2, ln_g[l, 1], ln_b[l, 1])
    return xs.reshape(B, T, D)
```

```python
import functools

import numpy as np
import jax
import jax.numpy as jnp
from jax import lax
from jax.experimental import pallas as pl
from jax.experimental.pallas import tpu as pltpu

F32 = jnp.float32
BF16 = jnp.bfloat16
I32 = jnp.int32

D_MODEL = 2048
DEPTH = 2
LN_EPS = 1e-5
DEEPNORM_ALPHA = (2 * DEPTH) ** 0.25

SB_HEADS = 8
SB_HEAD_DIM = 128
SB_WIDTH = SB_HEADS * SB_HEAD_DIM

GLA_HEADS = 4
GLA_HEAD_K = 128
GLA_HEAD_V = 256
GLA_KEY_DIM = GLA_HEADS * GLA_HEAD_K
GLA_VALUE_DIM = GLA_HEADS * GLA_HEAD_V
GLA_GATE_RANK = 16
GLA_GATE_NORMALIZER = 16.0
GLA_CHUNK = 64
GLA_NORM_EPS = 1e-5

RWKV_HEADS = 16
RWKV_HEAD_DIM = 64
RWKV_WIDTH = RWKV_HEADS * RWKV_HEAD_DIM
RWKV_DECAY_RANK = 64
RWKV_AAA_RANK = 64
RWKV_VALUE_RANK = 32
RWKV_GATE_RANK = 128
RWKV_LNX_EPS = 64e-5

N_EXPERTS = 64
TOP_K = 6
EXPERT_DIM = 512
SHARED_DIM = 512
N_GROUPS = 8
TOPK_GROUPS = 4
ROUTED_SCALE = 2.5

SB_COLS = 3 * SB_WIDTH
GLA_COLS = 2 * GLA_KEY_DIM + GLA_VALUE_DIM + GLA_GATE_RANK + GLA_VALUE_DIM
RWKV_COLS = 3 * RWKV_WIDTH + RWKV_DECAY_RANK + RWKV_AAA_RANK + RWKV_GATE_RANK
MERGE_COLS = 3 * D_MODEL

LANES = 128
SUBLANES = 8
VMEM_LIMIT = 56 * 1024 * 1024
MOE_BM = 256
SB_EXIT = -104.0

NN = (((1,), (0,)), ((), ()))
NT = (((1,), (1,)), ((), ()))
TN = (((0,), (0,)), ((), ()))


def _cp(*sem):
    return pltpu.CompilerParams(dimension_semantics=sem, vmem_limit_bytes=VMEM_LIMIT)


def _dot(a, b, dims=NN):
    return lax.dot_general(a, b, dims, preferred_element_type=F32)


def _split(a, n):
    out = []
    r = a
    for i in range(n):
        p = r.astype(BF16)
        out.append(p)
        if i + 1 < n:
            r = r - p.astype(F32)
    return out


def _dot_fx(a, e, n, dims=NN):
    acc = None
    for p in _split(a, n):
        t = _dot(p, e, dims)
        acc = t if acc is None else acc + t
    return acc


def _dot_xf(e, b, n, dims=NN):
    acc = None
    for p in _split(b, n):
        t = _dot(e, p, dims)
        acc = t if acc is None else acc + t
    return acc


def _dot_ff(a, b, dims=NN):
    a1, a2, a3 = _split(a, 3)
    b1, b2, b3 = _split(b, 3)
    acc = _dot(a1, b1, dims)
    for p, q in ((a1, b2), (a2, b1), (a2, b2), (a1, b3), (a3, b1)):
        acc = acc + _dot(p, q, dims)
    return acc


def _sigmoid(x):
    return 1.0 / (1.0 + jnp.exp(-x))


def _softplus(x):
    return jnp.maximum(x, 0.0) + jnp.log1p(jnp.exp(-jnp.abs(x)))


def _silu(x):
    return x * _sigmoid(x)


def _ada_kernel(c_ref, w_ref, b_ref, o_ref):
    c = c_ref[...]
    o_ref[0] = _dot_ff(_silu(c), w_ref[0]) + b_ref[0]


def _ada(c, w_ada, b_ada):
    L, D, N = w_ada.shape
    tn = 512
    c8 = jnp.broadcast_to(c, (SUBLANES, D))
    out = pl.pallas_call(
        _ada_kernel,
        grid=(L, N // tn),
        in_specs=[pl.BlockSpec((SUBLANES, D), lambda l, j: (0, 0)),
                  pl.BlockSpec((1, D, tn), lambda l, j: (l, 0, j)),
                  pl.BlockSpec((1, 1, tn), lambda l, j: (l, 0, j))],
        out_specs=pl.BlockSpec((1, SUBLANES, tn), lambda l, j: (l, 0, j)),
        out_shape=jax.ShapeDtypeStruct((L, SUBLANES, N), F32),
        compiler_params=_cp("parallel", "parallel"),
        name="ada",
    )(c8, w_ada, b_ada.reshape(L, 1, N))
    return out[:, 0, :]


def _modmm_kernel(x_ref, sc_ref, sh_ref, w_ref, o_ref, u_ref):
    @pl.when(pl.program_id(1) == 0)
    def _():
        u_ref[...] = (x_ref[...] * (1.0 + sc_ref[...]) + sh_ref[...]).astype(BF16)

    o_ref[...] = _dot(u_ref[...], w_ref[...]).astype(o_ref.dtype)


def _modmm(x, sc, sh, w, out_dtype, tn, tm=512, name="modmm"):
    T, D = x.shape
    N = w.shape[1]
    return pl.pallas_call(
        _modmm_kernel,
        grid=(T // tm, N // tn),
        in_specs=[pl.BlockSpec((tm, D), lambda i, j: (i, 0)),
                  pl.BlockSpec((1, D), lambda i, j: (0, 0)),
                  pl.BlockSpec((1, D), lambda i, j: (0, 0)),
                  pl.BlockSpec((D, tn), lambda i, j: (0, j))],
        out_specs=pl.BlockSpec((tm, tn), lambda i, j: (i, j)),
        out_shape=jax.ShapeDtypeStruct((T, N), out_dtype),
        scratch_shapes=[pltpu.VMEM((tm, D), BF16)],
        compiler_params=_cp("parallel", "arbitrary"),
        name=name,
    )(x, sc, sh, w)


def _sb_kernel(q_ref, k_ref, v_ref, o_ref, acc_ref, car_ref, *, bq, bk, scale):
    i = pl.program_id(1)
    acc_ref[...] = jnp.zeros_like(acc_ref)
    car_ref[...] = jnp.zeros_like(car_ref)
    q = q_ref[...]
    row = i * bq + lax.broadcasted_iota(I32, (bq, bk), 0)
    later = (lax.broadcasted_iota(I32, (bk, bk), 0) > lax.broadcasted_iota(I32, (bk, bk), 1)).astype(BF16)

    def body(c):
        j, _ = c
        off = pl.multiple_of(j * bk, bk)
        kb = k_ref[pl.ds(off, bk), :]
        vb = v_ref[pl.ds(off, bk), :]
        z = _dot(q, kb, NT) * scale
        mask = (j * bk + lax.broadcasted_iota(I32, (bq, bk), 1)) < row
        sp = _softplus(z)
        log_keep = jnp.where(mask, -sp, 0.0)
        carry = car_ref[...]
        log_later = _dot_fx(log_keep, later, 2) + carry
        w = jnp.where(mask, jnp.exp((z - sp) + log_later), 0.0)
        acc_ref[...] += _dot(w.astype(BF16), vb)
        carry = carry + jnp.sum(log_keep, axis=-1, keepdims=True)
        car_ref[...] = carry
        done = jnp.max(carry) < SB_EXIT
        return j - 1, done.astype(I32)

    lax.while_loop(lambda c: jnp.logical_and(c[0] >= 0, c[1] == 0), body,
                   ((i + 1) * (bq // bk) - 1, jnp.int32(0)))
    o_ref[...] = acc_ref[...].astype(o_ref.dtype)


def _sb_attention(qkv, bq=128, bk=128):
    T = qkv.shape[0]
    H, Dh = SB_HEADS, SB_HEAD_DIM
    kern = functools.partial(_sb_kernel, bq=bq, bk=bk, scale=Dh ** -0.5)
    return pl.pallas_call(
        kern,
        grid=(H, T // bq),
        in_specs=[pl.BlockSpec((bq, Dh), lambda h, i: (i, h)),
                  pl.BlockSpec((T, Dh), lambda h, i: (0, H + h)),
                  pl.BlockSpec((T, Dh), lambda h, i: (0, 2 * H + h))],
        out_specs=pl.BlockSpec((bq, Dh), lambda h, i: (i, h)),
        out_shape=jax.ShapeDtypeStruct((T, H * Dh), BF16),
        scratch_shapes=[pltpu.VMEM((bq, Dh), F32), pltpu.VMEM((bq, 1), F32)],
        compiler_params=_cp("parallel", "parallel"),
        name="sb_attention",
    )(qkv, qkv, qkv)


_GLA_LEVELS = (32, 16, 8, 4, 2, 1)


def _gla_constants():
    C = GLA_CHUNK
    i = np.arange(C)[:, None]
    s = np.arange(C)[None, :]
    sel = []
    masks = []
    for m in _GLA_LEVELS:
        same = (i // m) == (s // m)
        sel.append(same & (s <= i) & ((i // m) % 2 == 1))
        sel.append(same & (s > i) & ((i // m) % 2 == 0))
        masks.append(((i // (2 * m)) == (s // (2 * m))) & ((i // m) % 2 == 1) & ((s // m) % 2 == 0))
    sel.append(s <= i)
    sel.append(s > i)
    masks.append(i == s)
    return (np.concatenate(sel, 0).astype(np.float32), np.stack(masks, 0).astype(np.float32))


def _gla_kernel(q_ref, k_ref, v_ref, go_ref, glr_ref, up_ref, bias_ref, ng_ref, sel_ref, msk_ref,
                o_ref, st_ref, *, n_chunks):
    C = GLA_CHUNK
    nl = len(_GLA_LEVELS)

    @pl.when(pl.program_id(1) == 0)
    def _():
        st_ref[...] = jnp.zeros_like(st_ref)

    sel = sel_ref[...]
    up = up_ref[...]
    bias = bias_ref[...]
    ng = ng_ref[...]
    for c in range(n_chunks):
        rows = slice(c * C, (c + 1) * C)
        q = q_ref[rows, :] * (GLA_HEAD_K ** -0.5)
        k = k_ref[rows, :]
        v = v_ref[rows, :]
        x = _dot(glr_ref[rows, :].astype(BF16), up) + bias
        g = -_softplus(-x) * (1.0 / GLA_GATE_NORMALIZER)
        f = jnp.exp(_dot_xf(sel, g, 2))
        scores = None
        for l in range(nl):
            ql = (q * f[(2 * l) * C:(2 * l + 1) * C]).astype(BF16)
            kl = (k * f[(2 * l + 1) * C:(2 * l + 2) * C]).astype(BF16)
            t = _dot(ql, kl, NT) * msk_ref[l]
            scores = t if scores is None else scores + t
        scores = scores + _dot(q.astype(BF16), k.astype(BF16), NT) * msk_ref[nl]
        f_start = f[(2 * nl) * C:(2 * nl + 1) * C]
        f_end = f[(2 * nl + 1) * C:(2 * nl + 2) * C]
        st = st_ref[...]
        o = _dot((q * f_start).astype(BF16), st.astype(BF16), NT)
        o = o + _dot(scores.astype(BF16), v.astype(BF16))
        st_ref[...] = st * f_start[C - 1:C, :] + _dot(v.astype(BF16), (k * f_end).astype(BF16), TN)
        rms = lax.rsqrt(jnp.mean(o * o, axis=-1, keepdims=True) + GLA_NORM_EPS)
        o_ref[rows, :] = (o * rms * ng * _silu(go_ref[rows, :])).astype(o_ref.dtype)


def _gla(p_gla, gk_up, gk_bias, norm_g, tt=256):
    T = p_gla.shape[0]
    H, dk, dv = GLA_HEADS, GLA_HEAD_K, GLA_HEAD_V
    sel, msk = _gla_constants()
    up = jnp.zeros((LANES, GLA_KEY_DIM), F32).at[:GLA_GATE_RANK].set(gk_up).astype(BF16)
    kern = functools.partial(_gla_kernel, n_chunks=tt // GLA_CHUNK)
    return pl.pallas_call(
        kern,
        grid=(H, T // tt),
        in_specs=[pl.BlockSpec((tt, dk), lambda h, i: (i, h)),
                  pl.BlockSpec((tt, dk), lambda h, i: (i, H + h)),
                  pl.BlockSpec((tt, dv), lambda h, i: (i, H + h)),
                  pl.BlockSpec((tt, dv), lambda h, i: (i, 2 * H + h)),
                  pl.BlockSpec((tt, LANES), lambda h, i: (i, 3 * GLA_VALUE_DIM // LANES)),
                  pl.BlockSpec((LANES, dk), lambda h, i: (0, h)),
                  pl.BlockSpec((1, dk), lambda h, i: (0, h)),
                  pl.BlockSpec((1, dv), lambda h, i: (0, 0)),
                  pl.BlockSpec(sel.shape, lambda h, i: (0, 0)),
                  pl.BlockSpec(msk.shape, lambda h, i: (0, 0, 0))],
        out_specs=pl.BlockSpec((tt, dv), lambda h, i: (i, h)),
        out_shape=jax.ShapeDtypeStruct((T, H * dv), BF16),
        scratch_shapes=[pltpu.VMEM((dv, dk), F32)],
        compiler_params=_cp("parallel", "arbitrary"),
        name="gla",
    )(p_gla, p_gla, p_gla, p_gla, p_gla, up, gk_bias.reshape(1, -1), norm_g.reshape(1, -1),
      jnp.asarray(sel, BF16), jnp.asarray(msk, F32))


def _head_ones():
    l = np.arange(LANES)
    return (l[:, None] // RWKV_HEAD_DIM == l[None, :] // RWKV_HEAD_DIM).astype(np.float32)


def _head_diag():
    i = np.arange(RWKV_HEAD_DIM)[:, None]
    l = np.arange(LANES)[None, :]
    return (l % RWKV_HEAD_DIM == i).astype(np.float32)


def _rwprep_kernel(p_ref, pp_ref, mu_ref, w0_ref, w2_ref, a0_ref, a2_ref, g2_ref, kk_ref, ka_ref,
                   v0_ref, v1_ref, v2_ref, vf_ref, bo_ref,
                   r_ref, w_ref, k_ref, v_ref, a_ref, b_ref, g_ref, *, first_layer):
    W = RWKV_WIDTH
    i = pl.program_id(0)
    p = p_ref[...]
    tm = p.shape[0]
    prev_row = jnp.where(i == 0, 0.0, pp_ref[SUBLANES - 1:SUBLANES, :])
    rid = lax.broadcasted_iota(I32, p.shape, 0)
    prev = jnp.where(rid == 0, prev_row, pltpu.roll(p, 1, axis=0))
    ps = p + (prev - p) * mu_ref[...]
    r = ps[:, 0:W]
    kr = ps[:, W:2 * W]
    vr = ps[:, 2 * W:3 * W]
    lr = ps[:, 3 * W:3 * W + LANES]
    gl = ps[:, 3 * W + LANES:3 * W + 2 * LANES]
    w_log = -_softplus(-(w0_ref[...] + _dot(jnp.tanh(lr).astype(BF16), w2_ref[...]))) - 0.5
    w_ref[...] = -jnp.exp(w_log)
    if not first_layer:
        lo = _dot(vr.astype(BF16), v1_ref[...])
        gate = _sigmoid(v0_ref[...] + _dot(lo.astype(BF16), v2_ref[...]))
        vr = vr + (vf_ref[...] - vr) * gate
    v_ref[...] = vr
    a = _sigmoid(a0_ref[...] + _dot(lr.astype(BF16), a2_ref[...]))
    g_ref[...] = _dot(_sigmoid(gl).astype(BF16), g2_ref[...])
    kk = kr * kk_ref[...]
    bo = bo_ref[...]
    for hp in range(W // LANES):
        sl = slice(hp * LANES, (hp + 1) * LANES)
        kh = kk[:, sl]
        ss = _dot_fx(kh * kh, bo, 3)
        kn = kh / jnp.maximum(jnp.sqrt(ss), 1e-12)
        a_ref[:, sl] = -kn
        b_ref[:, sl] = kn * a[:, sl]
    r_ref[...] = r
    k_ref[...] = kr * (1.0 + (a - 1.0) * ka_ref[...])


def _rwkv_prep(p_rw, v_first, mu, w0, w2, a0, a2, g2, k_k, k_a, v0, v1, v2, tm=256):
    T, NC = p_rw.shape
    W = RWKV_WIDTH
    first = v_first is None
    w2p = jnp.zeros((LANES, W), F32).at[:RWKV_DECAY_RANK].set(w2).astype(BF16)
    a2p = jnp.zeros((LANES, W), F32).at[RWKV_DECAY_RANK:].set(a2).astype(BF16)
    if first:
        v_first = jnp.zeros((SUBLANES, W), F32)
        v0 = jnp.zeros((W,), F32)
        v1p = jnp.zeros((W, LANES), BF16)
        v2p = jnp.zeros((LANES, W), BF16)
        vf_spec = pl.BlockSpec((SUBLANES, W), lambda i: (0, 0))
    else:
        v1p = jnp.zeros((W, LANES), F32).at[:, :RWKV_VALUE_RANK].set(v1).astype(BF16)
        v2p = jnp.zeros((LANES, W), F32).at[:RWKV_VALUE_RANK].set(v2).astype(BF16)
        vf_spec = pl.BlockSpec((tm, W), lambda i: (i, 0))
    row = lambda a: a.reshape(1, -1)
    full = lambda shape: pl.BlockSpec(shape, lambda i: (0,) * len(shape))
    out = jax.ShapeDtypeStruct((T, W), F32)
    ospec = pl.BlockSpec((tm, W), lambda i: (i, 0))
    kern = functools.partial(_rwprep_kernel, first_layer=first)
    return pl.pallas_call(
        kern,
        grid=(T // tm,),
        in_specs=[pl.BlockSpec((tm, NC), lambda i: (i, 0)),
                  pl.BlockSpec((SUBLANES, NC), lambda i: (jnp.maximum(i * (tm // SUBLANES) - 1, 0), 0)),
                  full((1, NC)), full((1, W)), full((LANES, W)), full((1, W)), full((LANES, W)),
                  full((LANES, W)), full((1, W)), full((1, W)),
                  full((1, W)), full((W, LANES)), full((LANES, W)), vf_spec, full((LANES, LANES))],
        out_specs=[ospec] * 7,
        out_shape=[out] * 7,
        compiler_params=_cp("parallel"),
        name="rwkv_prep",
    )(p_rw, p_rw, row(mu), row(w0), w2p, row(a0), a2p, g2.astype(BF16), row(k_k), row(k_a),
      row(v0), v1p, v2p, v_first, jnp.asarray(_head_ones(), BF16))


RW_CHUNK = 64


def _rw_constants():
    C = RW_CHUNK
    i = np.arange(C)[:, None]
    s = np.arange(C)[None, :]
    i2 = np.arange(2 * C)[:, None]
    s2 = np.arange(2 * C)[None, :]
    same = (i2 // C) == (s2 // C)
    cm = np.stack([same & (s2 < i2), same & (s2 <= i2), (i2 // 16) == (s2 // 16), (i2 // 32) == (s2 // 32),
                   same], 0).astype(np.float32)
    return (s <= i).astype(np.float32), cm


def _mm(a, b, dims=NN):
    a1, a2 = _split(a, 2)
    b1, b2 = _split(b, 2)
    return _dot(a1, b1, dims) + (_dot(a1, b2, dims) + _dot(a2, b1, dims))


def _unit_lower_inverse(nms, eye, bd16, bd32):
    nd = [n * bd16 for n in nms]
    x = [eye + n for n in nd]
    p = [_mm(n, n) for n in nd]
    for step in range(3):
        x = [a + _mm(a, q) for a, q in zip(x, p)]
        if step < 2:
            p = [_mm(q, q) for q in p]
    for lo_mask in (bd32 - bd16, 1.0 - bd32):
        t = [_mm(n * lo_mask, a) for n, a in zip(nms, x)]
        x = [a + _mm(a, q) for a, q in zip(x, t)]
    return x


def _rwrec_kernel(r_ref, lw_ref, k_ref, v_ref, a_ref, b_ref, lt_ref, cm_ref, y_ref, mt_ref, *, n_chunks):
    C = RW_CHUNK
    C2 = 2 * C

    @pl.when(pl.program_id(1) == 0)
    def _():
        mt_ref[...] = jnp.zeros_like(mt_ref)

    lt = lt_ref[...]
    sl, li, bd16, bd32, bdh = cm_ref[0], cm_ref[1], cm_ref[2], cm_ref[3], cm_ref[4]
    eye = li - sl
    lane = lax.broadcasted_iota(I32, (1, LANES), 1)
    m0 = jnp.where(lane < RWKV_HEAD_DIM, 1.0, 0.0)
    m1 = 1.0 - m0
    stack = lambda x: jnp.concatenate([x * m0, x * m1], axis=0)
    fold = lambda x: x[:C] + x[C:]
    rows = [slice(c * C, (c + 1) * C) for c in range(n_chunks)]

    lw = [lw_ref[s, :] for s in rows]
    cw = [_dot_xf(lt, x, 3) for x in lw]
    en = [jnp.exp(-x) for x in cw]
    at = [a_ref[s, :] * jnp.exp(c - l) for s, c, l in zip(rows, cw, lw)]
    rt = [r_ref[s, :] * jnp.exp(c) for s, c in zip(rows, cw)]
    bh = [b_ref[s, :] * e for s, e in zip(rows, en)]
    kh = [k_ref[s, :] * e for s, e in zip(rows, en)]
    v = [v_ref[s, :] for s in rows]
    a2 = [stack(x) for x in at]
    ar2 = [jnp.concatenate([x, stack(y)], axis=0) for x, y in zip(a2, rt)]
    gb = [_mm(x, stack(y), NT) for x, y in zip(ar2, bh)]
    gk = [_mm(x, stack(y), NT) for x, y in zip(ar2, kh)]
    tinv = _unit_lower_inverse([g[:C2] * sl for g in gb], eye, bd16, bd32)
    akv = [_mm(jnp.concatenate([g[:C2] * sl, g[C2:] * li], axis=0), stack(x)) for g, x in zip(gk, v)]
    tav = [_mm(t, jnp.concatenate([x, y[:C2]], axis=1)) for t, x, y in zip(tinv, a2, akv)]
    rav = [_mm(g[C2:] * li, x) for g, x in zip(gb, tav)]
    ta = [fold(x[:, :LANES]) for x in tav]
    u0 = [fold(x[:, LANES:]) for x in tav]
    qe = [x + fold(y[:, :LANES]) for x, y in zip(rt, rav)]
    y0 = [fold(x[:, LANES:] + y[C2:]) for x, y in zip(rav, akv)]
    dl = [jnp.exp(x[C - 1:C, :]) for x in cw]
    gmt = [(eye + _mm(x, y, TN) * bdh) * d for x, y, d in zip(ta, bh, dl)]
    hmt = [(_mm(x, y, TN) + _mm(z, w, TN)) * bdh * d for x, y, z, w, d in zip(u0, bh, v, kh, dl)]

    mt = mt_ref[...]
    for c in range(n_chunks):
        y_ref[rows[c], :] = _mm(qe[c], mt, NT) + y0[c]
        mt = _mm(mt, gmt[c]) + hmt[c]
    mt_ref[...] = mt


def _rwkv_recurrence(r, lw, k, v, a, b, tt=256):
    T, W = r.shape
    lt, cm = _rw_constants()
    spec = pl.BlockSpec((tt, LANES), lambda p, i: (i, p))
    kern = functools.partial(_rwrec_kernel, n_chunks=tt // RW_CHUNK)
    return pl.pallas_call(
        kern,
        grid=(W // LANES, T // tt),
        in_specs=[spec] * 6 + [pl.BlockSpec(lt.shape, lambda p, i: (0, 0)),
                               pl.BlockSpec(cm.shape, lambda p, i: (0, 0, 0))],
        out_specs=spec,
        out_shape=jax.ShapeDtypeStruct((T, W), F32),
        scratch_shapes=[pltpu.VMEM((LANES, LANES), F32)],
        compiler_params=_cp("parallel", "arbitrary"),
        name="rwkv_recurrence",
    )(r, lw, k, v, a, b, jnp.asarray(lt, BF16), jnp.asarray(cm, F32))


def _rwpost_kernel(y_ref, r_ref, k_ref, v_ref, g_ref, lg_ref, lb_ref, rk_ref, bo_ref, o_ref):
    bo = bo_ref[...]
    inv = 1.0 / RWKV_HEAD_DIM
    for hp in range(RWKV_WIDTH // LANES):
        sl = slice(hp * LANES, (hp + 1) * LANES)
        y = y_ref[:, sl]
        mu = _dot_fx(y, bo, 3) * inv
        d = y - mu
        var = _dot_fx(d * d, bo, 3) * inv
        yn = d * lax.rsqrt(var + RWKV_LNX_EPS) * lg_ref[:, sl] + lb_ref[:, sl]
        bonus = _dot_fx(r_ref[:, sl] * k_ref[:, sl] * rk_ref[:, sl], bo, 3) * v_ref[:, sl]
        o_ref[:, sl] = ((yn + bonus) * g_ref[:, sl]).astype(o_ref.dtype)


def _rwkv_post(y, r, k, v, g, lnx_g, lnx_b, r_k, tm=512):
    T, W = y.shape
    spec = pl.BlockSpec((tm, W), lambda i: (i, 0))
    rowspec = pl.BlockSpec((1, W), lambda i: (0, 0))
    return pl.pallas_call(
        _rwpost_kernel,
        grid=(T // tm,),
        in_specs=[spec] * 5 + [rowspec] * 3 + [pl.BlockSpec((LANES, LANES), lambda i: (0, 0))],
        out_specs=spec,
        out_shape=jax.ShapeDtypeStruct((T, W), BF16),
        compiler_params=_cp("parallel"),
        name="rwkv_post",
    )(y, r, k, v, g, lnx_g.reshape(1, W), lnx_b.reshape(1, W), r_k.reshape(1, W),
      jnp.asarray(_head_ones(), BF16))


def _merge_kernel(osb_ref, ogl_ref, orw_ref, g1_ref, g2_ref, g3_ref, w1_ref, w2_ref, w3_ref, o_ref):
    acc = _sigmoid(g1_ref[...].astype(F32)) * _dot(osb_ref[...], w1_ref[...])
    acc = acc + _sigmoid(g2_ref[...].astype(F32)) * _dot(ogl_ref[...], w2_ref[...])
    acc = acc + _sigmoid(g3_ref[...].astype(F32)) * _dot(orw_ref[...], w3_ref[...])
    o_ref[...] = acc.astype(o_ref.dtype)


def _merge(o_sb, o_gla, o_rw, p_merge, w_sb, w_gla, w_rw, tm=512, tn=1024):
    T = o_sb.shape[0]
    D = D_MODEL
    nb = D // tn
    aspec = pl.BlockSpec((tm, 1024), lambda j, i: (i, 0))
    wspec = pl.BlockSpec((1024, tn), lambda j, i: (0, j))
    gspec = lambda b: pl.BlockSpec((tm, tn), lambda j, i: (i, b * nb + j))
    return pl.pallas_call(
        _merge_kernel,
        grid=(nb, T // tm),
        in_specs=[aspec, aspec, aspec, gspec(0), gspec(1), gspec(2), wspec, wspec, wspec],
        out_specs=pl.BlockSpec((tm, tn), lambda j, i: (i, j)),
        out_shape=jax.ShapeDtypeStruct((T, D), BF16),
        compiler_params=_cp("parallel", "parallel"),
        name="merge",
    )(o_sb, o_gla, o_rw, p_merge, p_merge, p_merge, w_sb, w_gla, w_rw)


def _layer_norm(h, g, b):
    mu = jnp.mean(h, axis=-1, keepdims=True)
    d = h - mu
    var = jnp.mean(d * d, axis=-1, keepdims=True)
    return d * lax.rsqrt(var + LN_EPS) * g + b


def _oproj_kernel(m_ref, w_ref, x_ref, gate_ref, lg_ref, lb_ref, o_ref):
    mix = _dot(m_ref[...], w_ref[...])
    h = DEEPNORM_ALPHA * x_ref[...] + (1.0 + gate_ref[...]) * mix
    o_ref[...] = _layer_norm(h, lg_ref[...], lb_ref[...])


def _oproj(merged, w_o, x, gate, ln_g, ln_b, tm=512):
    T, D = x.shape
    spec = pl.BlockSpec((tm, D), lambda i: (i, 0))
    rowspec = pl.BlockSpec((1, D), lambda i: (0, 0))
    return pl.pallas_call(
        _oproj_kernel,
        grid=(T // tm,),
        in_specs=[spec, pl.BlockSpec((D, D), lambda i: (0, 0)), spec, rowspec, rowspec, rowspec],
        out_specs=spec,
        out_shape=jax.ShapeDtypeStruct((T, D), F32),
        compiler_params=_cp("parallel"),
        name="oproj_norm",
    )(merged, w_o, x, gate, ln_g.reshape(1, D), ln_b.reshape(1, D))


def _first_max(x, idx, axis, big):
    m = jnp.max(x, axis=axis, keepdims=True)
    first = jnp.min(jnp.where(x == m, idx, big), axis=axis, keepdims=True)
    return m, idx == first


def _router_kernel(x_ref, sc_ref, sh_ref, wr_ref, rb_ref, u_ref, e_ref, g_ref, p_ref, cnt_ref, car_ref, *, tm):
    E, G = N_EXPERTS, N_GROUPS
    per = E // G
    neg = -jnp.inf

    @pl.when(pl.program_id(0) == 0)
    def _():
        car_ref[...] = jnp.zeros_like(car_ref)

    u = x_ref[...] * (1.0 + sc_ref[...]) + sh_ref[...]
    u_ref[...] = u
    scores = _sigmoid(_dot_ff(wr_ref[...], u, NT))
    biased = scores + rb_ref[...]
    b3 = biased.reshape(G, per, tm)
    i3 = lax.broadcasted_iota(I32, (G, per, tm), 1)
    m1, hit = _first_max(b3, i3, 1, per)
    m2 = jnp.max(jnp.where(hit, neg, b3), axis=1, keepdims=True)
    gs = (m1 + m2).reshape(G, tm)
    gi = lax.broadcasted_iota(I32, (G, tm), 0)
    ok = jnp.zeros((G, tm), jnp.bool_)
    for _ in range(TOPK_GROUPS):
        _, hit = _first_max(gs, gi, 0, G)
        ok = jnp.logical_or(ok, hit)
        gs = jnp.where(hit, neg, gs)
    ok_e = jnp.broadcast_to(ok.reshape(G, 1, tm), (G, per, tm)).reshape(E, tm)
    cand = jnp.where(ok_e, biased, neg)
    ei = lax.broadcasted_iota(I32, (E, tm), 0)
    sels, gates = [], []
    for _ in range(TOP_K):
        _, hit = _first_max(cand, ei, 0, E)
        sels.append(hit)
        gates.append(jnp.sum(jnp.where(hit, scores, 0.0), axis=0, keepdims=True))
        cand = jnp.where(hit, neg, cand)
    denom = gates[0]
    for gk in gates[1:]:
        denom = denom + gk
    chosen = sels[0]
    for s in sels[1:]:
        chosen = jnp.logical_or(chosen, s)
    cnt = jnp.where(chosen, 1.0, 0.0)
    earlier = (lax.broadcasted_iota(I32, (tm, tm), 0) < lax.broadcasted_iota(I32, (tm, tm), 1)).astype(BF16)
    rank = _dot(cnt.astype(BF16), earlier) + car_ref[...][:, 0:1]
    car_ref[...] = car_ref[...] + jnp.sum(cnt, axis=1, keepdims=True)
    cnt_ref[...] = car_ref[...]
    zi = jnp.zeros((1, tm), I32)
    zf = jnp.zeros((1, tm), F32)
    e_rows = [jnp.sum(jnp.where(s, ei, 0), axis=0, keepdims=True) for s in sels]
    p_rows = [jnp.sum(jnp.where(s, rank, 0.0), axis=0, keepdims=True).astype(I32) for s in sels]
    g_rows = [gk / denom * ROUTED_SCALE for gk in gates]
    e_ref[...] = jnp.concatenate(e_rows + [zi, zi], axis=0)
    p_ref[...] = jnp.concatenate(p_rows + [zi, zi], axis=0)
    g_ref[...] = jnp.concatenate(g_rows + [zf, zf], axis=0)


def _router(x, sc, sh, w_router, router_bias, tm=256):
    T, D = x.shape
    E = N_EXPERTS
    kern = functools.partial(_router_kernel, tm=tm)
    rowspec = pl.BlockSpec((1, D), lambda i: (0, 0))
    kspec = pl.BlockSpec((SUBLANES, tm), lambda i: (0, i))
    return pl.pallas_call(
        kern,
        grid=(T // tm,),
        in_specs=[pl.BlockSpec((tm, D), lambda i: (i, 0)), rowspec, rowspec,
                  pl.BlockSpec((E, D), lambda i: (0, 0)), pl.BlockSpec((E, 1), lambda i: (0, 0))],
        out_specs=[pl.BlockSpec((tm, D), lambda i: (i, 0)), kspec, kspec, kspec,
                   pl.BlockSpec((E, LANES), lambda i: (0, 0))],
        out_shape=[jax.ShapeDtypeStruct((T, D), F32),
                   jax.ShapeDtypeStruct((SUBLANES, T), I32),
                   jax.ShapeDtypeStruct((SUBLANES, T), F32),
                   jax.ShapeDtypeStruct((SUBLANES, T), I32),
                   jax.ShapeDtypeStruct((E, LANES), F32)],
        scratch_shapes=[pltpu.VMEM((E, LANES), F32)],
        compiler_params=_cp("arbitrary"),
        name="router",
    )(x, sc, sh, w_router.T, router_bias.reshape(E, 1))


def _dispatch_kernel(dest_ref, fill_ref, u_ref, z_ref, xs_ref, sem, zsem, *, tm, bm):
    i = pl.program_id(0)

    @pl.when(i == 0)
    def _():
        def zcopy(e):
            return pltpu.make_async_copy(z_ref, xs_ref.at[pl.ds(pl.multiple_of(fill_ref[e], bm), bm)], zsem)

        def zstart(e, c):
            zcopy(e).start()
            return c

        def zwait(e, c):
            zcopy(e).wait()
            return c

        lax.fori_loop(0, N_EXPERTS, zstart, 0)
        lax.fori_loop(0, N_EXPERTS, zwait, 0)

    def copy(t, k):
        return pltpu.make_async_copy(u_ref.at[pl.ds(t, 1)], xs_ref.at[pl.ds(dest_ref[k, t], 1)], sem)

    def start(t, c):
        for k in range(TOP_K):
            copy(t, k).start()
        return c

    def wait(t, c):
        for k in range(TOP_K):
            copy(t, k).wait()
        return c

    lax.fori_loop(0, tm, start, 0)
    lax.fori_loop(0, tm, wait, 0)


def _dispatch(u, dest, fill, n_slots, tm=256, bm=MOE_BM):
    T, D = u.shape
    kern = functools.partial(_dispatch_kernel, tm=tm, bm=bm)
    zeros = jnp.zeros((bm, D), F32)
    return pl.pallas_call(
        kern,
        grid=(T // tm,),
        in_specs=[pl.BlockSpec((SUBLANES, tm), lambda i: (0, i), memory_space=pltpu.SMEM),
                  pl.BlockSpec(memory_space=pltpu.SMEM),
                  pl.BlockSpec((tm, D), lambda i: (i, 0)),
                  pl.BlockSpec(memory_space=pl.ANY)],
        out_specs=pl.BlockSpec(memory_space=pl.ANY),
        out_shape=jax.ShapeDtypeStruct((n_slots, D), F32),
        scratch_shapes=[pltpu.SemaphoreType.DMA(()), pltpu.SemaphoreType.DMA(())],
        compiler_params=_cp("arbitrary"),
        name="moe_dispatch",
    )(dest, fill, u, zeros)


def _expert_kernel(be_ref, nv_ref, x_ref, wg_ref, wu_ref, wd_ref, y_ref):
    @pl.when(pl.program_id(0) < nv_ref[0])
    def _():
        x = x_ref[...].astype(BF16)
        h = _silu(_dot(x, wg_ref[0])) * _dot(x, wu_ref[0])
        y_ref[...] = _dot(h.astype(BF16), wd_ref[0])


def _experts(xs, block_expert, n_valid, w_gate, w_up, w_down, bm=MOE_BM):
    n_slots, D = xs.shape
    nb = n_slots // bm
    F = w_gate.shape[-1]
    blk = lambda b, be, nv: (jnp.minimum(b, nv[0] - 1), 0)
    wsel = lambda b, be, nv: (be[jnp.minimum(b, nv[0] - 1)], 0, 0)
    gs = pltpu.PrefetchScalarGridSpec(
        num_scalar_prefetch=2,
        grid=(nb,),
        in_specs=[pl.BlockSpec((bm, D), blk),
                  pl.BlockSpec((1, D, F), wsel),
                  pl.BlockSpec((1, D, F), wsel),
                  pl.BlockSpec((1, F, D), wsel)],
        out_specs=pl.BlockSpec((bm, D), blk),
    )
    return pl.pallas_call(
        _expert_kernel,
        grid_spec=gs,
        out_shape=jax.ShapeDtypeStruct((n_slots, D), F32),
        compiler_params=_cp("arbitrary"),
        name="moe_experts",
    )(block_expert, n_valid, xs, w_gate, w_up, w_down)


def _combine_kernel(dest_ref, ys_ref, u_ref, x_ref, gt_ref, wg_ref, wu_ref, wd_ref, gate_ref, lg_ref, lb_ref,
                    o_ref, buf_ref, sem, *, tm):
    def copy(t, k):
        return pltpu.make_async_copy(ys_ref.at[pl.ds(dest_ref[k, t], 1)], buf_ref.at[k, pl.ds(t, 1)], sem)

    def start(t, c):
        for k in range(TOP_K):
            copy(t, k).start()
        return c

    def wait(t, c):
        for k in range(TOP_K):
            copy(t, k).wait()
        return c

    lax.fori_loop(0, tm, start, 0)
    u = u_ref[...].astype(BF16)
    h = _silu(_dot(u, wg_ref[...])) * _dot(u, wu_ref[...])
    ffn = _dot(h.astype(BF16), wd_ref[...])
    lax.fori_loop(0, tm, wait, 0)
    gt = gt_ref[...]
    for k in range(TOP_K):
        ffn = ffn + gt[:, k:k + 1] * buf_ref[k]
    hres = DEEPNORM_ALPHA * x_ref[...] + (1.0 + gate_ref[...]) * ffn
    o_ref[...] = _layer_norm(hres, lg_ref[...], lb_ref[...])


def _combine(ys, dest, u, x, gates_t, w_sg, w_su, w_sd, gate, ln_g, ln_b, tm=128):
    T, D = x.shape
    F = w_sg.shape[-1]
    kern = functools.partial(_combine_kernel, tm=tm)
    spec = pl.BlockSpec((tm, D), lambda i: (i, 0))
    rowspec = pl.BlockSpec((1, D), lambda i: (0, 0))
    return pl.pallas_call(
        kern,
        grid=(T // tm,),
        in_specs=[pl.BlockSpec((SUBLANES, tm), lambda i: (0, i), memory_space=pltpu.SMEM),
                  pl.BlockSpec(memory_space=pl.ANY),
                  spec, spec,
                  pl.BlockSpec((tm, SUBLANES), lambda i: (i, 0)),
                  pl.BlockSpec((D, F), lambda i: (0, 0)),
                  pl.BlockSpec((D, F), lambda i: (0, 0)),
                  pl.BlockSpec((F, D), lambda i: (0, 0)),
                  rowspec, rowspec, rowspec],
        out_specs=spec,
        out_shape=jax.ShapeDtypeStruct((T, D), F32),
        scratch_shapes=[pltpu.VMEM((TOP_K, tm, D), F32), pltpu.SemaphoreType.DMA(())],
        compiler_params=_cp("arbitrary"),
        name="moe_combine_norm",
    )(dest, ys, u, x, gates_t, w_sg, w_su, w_sd, gate, ln_g.reshape(1, D), ln_b.reshape(1, D))


def _moe_plan(top_e, rank, counts, bm, n_blocks):
    counts = counts.astype(I32)
    padded = (counts + bm - 1) // bm * bm
    padded_end = jnp.cumsum(padded)
    padded_start = padded_end - padded
    onehot = top_e[:, :, None] == jnp.arange(N_EXPERTS, dtype=I32)
    dest = jnp.sum(jnp.where(onehot, padded_start, 0), axis=-1) + rank
    fill = jnp.maximum(padded_end - bm, padded_start)
    fill = jnp.minimum(fill, (n_blocks - 1) * bm)
    block_expert = jnp.minimum(
        jnp.searchsorted(padded_end, jnp.arange(n_blocks, dtype=I32) * bm, side="right"), N_EXPERTS - 1).astype(I32)
    n_valid = jnp.maximum(padded_end[-1:] // bm, 1).astype(I32)
    return dest.astype(I32), fill.astype(I32), block_expert, n_valid


def _in_proj_weights(w_in):
    o = SB_COLS
    gq = w_in[:, o:o + 2 * GLA_KEY_DIM + GLA_VALUE_DIM]
    o += 2 * GLA_KEY_DIM + GLA_VALUE_DIM
    glr = w_in[:, o:o + GLA_GATE_RANK]
    o += GLA_GATE_RANK
    gout = w_in[:, o:o + GLA_VALUE_DIM]
    o += GLA_VALUE_DIM
    pad = jnp.zeros((w_in.shape[0], LANES - GLA_GATE_RANK), w_in.dtype)
    w_gla = jnp.concatenate([gq, gout, glr, pad], axis=1)
    w_rw = w_in[:, o:o + RWKV_COLS]
    o += RWKV_COLS
    w_mg = w_in[:, o:o + MERGE_COLS]
    return (w_in[:, :SB_COLS].astype(BF16), w_gla.astype(BF16), w_rw.astype(BF16), w_mg.astype(BF16))


def kernel(x, c, w_ada, b_ada, w_in, gla_gk_up, gla_gk_bias, gla_norm_g, rw_mu, rw_w0, rw_w2, rw_a0,
           rw_a2, rw_g2, rw_k_k, rw_k_a, rw_r_k, rw_lnx_g, rw_lnx_b, rw_v0, rw_v1, rw_v2,
           w_br_sb, w_br_gla, w_br_rw, w_o, ln_g, ln_b, w_router, router_bias,
           w_exp_gate, w_exp_up, w_exp_down, w_sh_gate, w_sh_up, w_sh_down):
    B, T, D = x.shape
    assert B == 1 and D == D_MODEL
    n_blocks = -(-(T * TOP_K) // MOE_BM) + N_EXPERTS
    mod = _ada(c, w_ada, b_ada)
    xs = x.reshape(T, D)
    v_first = None
    for l in range(DEPTH):
        sh1, sc1, g1, sh2, sc2, g2 = (mod[l, j * D:(j + 1) * D].reshape(1, D) for j in range(6))
        w_sb, w_gla, w_rw, w_mg = _in_proj_weights(w_in[l])
        p_sb = _modmm(xs, sc1, sh1, w_sb, BF16, 512, name="inproj_sb")
        p_gla = _modmm(xs, sc1, sh1, w_gla, F32, 640, name="inproj_gla")
        p_rw = _modmm(xs, sc1, sh1, w_rw, F32, 1664, name="inproj_rwkv")
        p_mg = _modmm(xs, sc1, sh1, w_mg, BF16, 1024, name="inproj_merge")

        o_sb = _sb_attention(p_sb)
        o_gla = _gla(p_gla, gla_gk_up[l], gla_gk_bias[l], gla_norm_g[l])
        if l == 0:
            r, w, k, v, a, b, g = _rwkv_prep(p_rw, None, rw_mu[l], rw_w0[l], rw_w2[l], rw_a0[l], rw_a2[l],
                                             rw_g2[l], rw_k_k[l], rw_k_a[l], None, None, None)
            v_first = v
        else:
            r, w, k, v, a, b, g = _rwkv_prep(p_rw, v_first, rw_mu[l], rw_w0[l], rw_w2[l], rw_a0[l], rw_a2[l],
                                             rw_g2[l], rw_k_k[l], rw_k_a[l], rw_v0[l - 1], rw_v1[l - 1],
                                             rw_v2[l - 1])
        y = _rwkv_recurrence(r, w, k, v, a, b)
        o_rw = _rwkv_post(y, r, k, v, g, rw_lnx_g[l], rw_lnx_b[l], rw_r_k[l])

        merged = _merge(o_sb, o_gla, o_rw, p_mg, w_br_sb[l].astype(BF16), w_br_gla[l].astype(BF16),
                        w_br_rw[l].astype(BF16))
        xs = _oproj(merged, w_o[l].astype(BF16), xs, g1, ln_g[l, 0], ln_b[l, 0])

        u2, top_e, gates, rank, counts = _router(xs, sc2, sh2, w_router[l], router_bias[l])
        dest, fill, block_expert, n_valid = _moe_plan(top_e, rank, counts[:, 0], MOE_BM, n_blocks)
        x_sorted = _dispatch(u2, dest, fill, n_blocks * MOE_BM)
        y_sorted = _experts(x_sorted, block_expert, n_valid, w_exp_gate[l].astype(BF16),
                            w_exp_up[l].astype(BF16), w_exp_down[l].astype(BF16))
        xs = _combine(y_sorted, dest, u2, xs, gates.T, w_sh_gate[l].astype(BF16), w_sh_up[l].astype(BF16),
                      w_sh_down[l].astype(BF16), g2, ln_g[l, 1], ln_b[l, 1])
    return xs.reshape(B, T, D)
```

```python
import functools

import numpy as np
import jax
import jax.numpy as jnp
from jax import lax
from jax.experimental import pallas as pl
from jax.experimental.pallas import tpu as pltpu

F32 = jnp.float32
BF16 = jnp.bfloat16
I32 = jnp.int32

D_MODEL = 2048
DEPTH = 2
LN_EPS = 1e-5
DEEPNORM_ALPHA = (2 * DEPTH) ** 0.25

SB_HEADS = 8
SB_HEAD_DIM = 128
SB_WIDTH = SB_HEADS * SB_HEAD_DIM

GLA_HEADS = 4
GLA_HEAD_K = 128
GLA_HEAD_V = 256
GLA_KEY_DIM = GLA_HEADS * GLA_HEAD_K
GLA_VALUE_DIM = GLA_HEADS * GLA_HEAD_V
GLA_GATE_RANK = 16
GLA_GATE_NORMALIZER = 16.0
GLA_CHUNK = 64
GLA_NORM_EPS = 1e-5

RWKV_HEADS = 16
RWKV_HEAD_DIM = 64
RWKV_WIDTH = RWKV_HEADS * RWKV_HEAD_DIM
RWKV_DECAY_RANK = 64
RWKV_AAA_RANK = 64
RWKV_VALUE_RANK = 32
RWKV_GATE_RANK = 128
RWKV_LNX_EPS = 64e-5

N_EXPERTS = 64
TOP_K = 6
EXPERT_DIM = 512
SHARED_DIM = 512
N_GROUPS = 8
TOPK_GROUPS = 4
ROUTED_SCALE = 2.5

SB_COLS = 3 * SB_WIDTH
GLA_COLS = 2 * GLA_KEY_DIM + GLA_VALUE_DIM + GLA_GATE_RANK + GLA_VALUE_DIM
RWKV_COLS = 3 * RWKV_WIDTH + RWKV_DECAY_RANK + RWKV_AAA_RANK + RWKV_GATE_RANK
MERGE_COLS = 3 * D_MODEL

LANES = 128
SUBLANES = 8
VMEM_LIMIT = 56 * 1024 * 1024
MOE_BM = 256
SB_EXIT = -104.0
TOKEN_ROWS = D_MODEL // LANES

NN = (((1,), (0,)), ((), ()))
NT = (((1,), (1,)), ((), ()))
TN = (((0,), (0,)), ((), ()))


def _cp(*sem):
    return pltpu.CompilerParams(dimension_semantics=sem, vmem_limit_bytes=VMEM_LIMIT)


def _dot(a, b, dims=NN):
    return lax.dot_general(a, b, dims, preferred_element_type=F32)


def _split(a, n):
    out = []
    r = a
    for i in range(n):
        p = r.astype(BF16)
        out.append(p)
        if i + 1 < n:
            r = r - p.astype(F32)
    return out


def _dot_fx(a, e, n, dims=NN):
    acc = None
    for p in _split(a, n):
        t = _dot(p, e, dims)
        acc = t if acc is None else acc + t
    return acc


def _dot_xf(e, b, n, dims=NN):
    acc = None
    for p in _split(b, n):
        t = _dot(e, p, dims)
        acc = t if acc is None else acc + t
    return acc


def _dot_ff(a, b, dims=NN):
    a1, a2, a3 = _split(a, 3)
    b1, b2, b3 = _split(b, 3)
    acc = _dot(a1, b1, dims)
    for p, q in ((a1, b2), (a2, b1), (a2, b2), (a1, b3), (a3, b1)):
        acc = acc + _dot(p, q, dims)
    return acc


def _sigmoid(x):
    return 1.0 / (1.0 + jnp.exp(-x))


def _softplus(x):
    return jnp.maximum(x, 0.0) + jnp.log1p(jnp.exp(-jnp.abs(x)))


def _silu(x):
    return x * _sigmoid(x)


def _load_tokens(ref, n, lead=()):
    return jnp.concatenate([ref[lead + (pl.ds(s, n, stride=TOKEN_ROWS), slice(None))] for s in range(TOKEN_ROWS)],
                           axis=1)


def _store_tokens(ref, val, n):
    for s in range(TOKEN_ROWS):
        ref[pl.ds(s, n, stride=TOKEN_ROWS), :] = val[:, s * LANES:(s + 1) * LANES]


def _token_rows(ref, t):
    return ref.at[pl.ds(pl.multiple_of(t * TOKEN_ROWS, TOKEN_ROWS), TOKEN_ROWS)]


def _ada_kernel(c_ref, w_ref, b_ref, o_ref):
    c = c_ref[...]
    o_ref[0] = _dot_ff(_silu(c), w_ref[0]) + b_ref[0]


def _ada(c, w_ada, b_ada):
    L, D, N = w_ada.shape
    tn = 512
    c8 = jnp.broadcast_to(c, (SUBLANES, D))
    out = pl.pallas_call(
        _ada_kernel,
        grid=(L, N // tn),
        in_specs=[pl.BlockSpec((SUBLANES, D), lambda l, j: (0, 0)),
                  pl.BlockSpec((1, D, tn), lambda l, j: (l, 0, j)),
                  pl.BlockSpec((1, 1, tn), lambda l, j: (l, 0, j))],
        out_specs=pl.BlockSpec((1, SUBLANES, tn), lambda l, j: (l, 0, j)),
        out_shape=jax.ShapeDtypeStruct((L, SUBLANES, N), F32),
        compiler_params=_cp("parallel", "parallel"),
        name="ada",
    )(c8, w_ada, b_ada.reshape(L, 1, N))
    return out[:, 0, :]


def _modulate_kernel(x_ref, sc_ref, sh_ref, o_ref):
    o_ref[...] = (x_ref[...] * (1.0 + sc_ref[...]) + sh_ref[...]).astype(o_ref.dtype)


def _modulate(x, sc, sh, tm=1024):
    T, D = x.shape
    spec = pl.BlockSpec((tm, D), lambda i: (i, 0))
    rowspec = pl.BlockSpec((1, D), lambda i: (0, 0))
    return pl.pallas_call(
        _modulate_kernel,
        grid=(T // tm,),
        in_specs=[spec, rowspec, rowspec],
        out_specs=spec,
        out_shape=jax.ShapeDtypeStruct((T, D), BF16),
        compiler_params=_cp("parallel"),
        name="modulate",
    )(x, sc, sh)


def _wmm_kernel(u_ref, w_ref, o_ref, wb_ref):
    @pl.when(pl.program_id(1) == 0)
    def _():
        wb_ref[...] = w_ref[...].astype(BF16)

    o_ref[...] = _dot(u_ref[...], wb_ref[...]).astype(o_ref.dtype)


def _wmm(u, w, out_dtype, tn, tm=512, name="wmm"):
    T, D = u.shape
    N = w.shape[1]
    return pl.pallas_call(
        _wmm_kernel,
        grid=(N // tn, T // tm),
        in_specs=[pl.BlockSpec((tm, D), lambda j, i: (i, 0)),
                  pl.BlockSpec((D, tn), lambda j, i: (0, j))],
        out_specs=pl.BlockSpec((tm, tn), lambda j, i: (i, j)),
        out_shape=jax.ShapeDtypeStruct((T, N), out_dtype),
        scratch_shapes=[pltpu.VMEM((D, tn), BF16)],
        compiler_params=_cp("parallel", "arbitrary"),
        name=name,
    )(u, w)


def _sb_kernel(q_ref, k_ref, v_ref, o_ref, acc_ref, car_ref, *, bq, bk, scale):
    i = pl.program_id(1)
    acc_ref[...] = jnp.zeros_like(acc_ref)
    car_ref[...] = jnp.zeros_like(car_ref)
    q = q_ref[...]
    row = i * bq + lax.broadcasted_iota(I32, (bq, bk), 0)
    later = (lax.broadcasted_iota(I32, (bk, bk), 0) > lax.broadcasted_iota(I32, (bk, bk), 1)).astype(BF16)

    def body(c):
        j, _ = c
        off = pl.multiple_of(j * bk, bk)
        kb = k_ref[pl.ds(off, bk), :]
        vb = v_ref[pl.ds(off, bk), :]
        z = _dot(q, kb, NT) * scale
        mask = (j * bk + lax.broadcasted_iota(I32, (bq, bk), 1)) < row
        sp = _softplus(z)
        log_keep = jnp.where(mask, -sp, 0.0)
        carry = car_ref[...]
        log_later = _dot_fx(log_keep, later, 2) + carry
        w = jnp.where(mask, jnp.exp((z - sp) + log_later), 0.0)
        acc_ref[...] += _dot(w.astype(BF16), vb)
        carry = carry + jnp.sum(log_keep, axis=-1, keepdims=True)
        car_ref[...] = carry
        done = jnp.max(carry) < SB_EXIT
        return j - 1, done.astype(I32)

    lax.while_loop(lambda c: jnp.logical_and(c[0] >= 0, c[1] == 0), body,
                   ((i + 1) * (bq // bk) - 1, jnp.int32(0)))
    o_ref[...] = acc_ref[...].astype(o_ref.dtype)


def _sb_attention(qkv, bq=256, bk=256):
    T = qkv.shape[0]
    H, Dh = SB_HEADS, SB_HEAD_DIM
    kern = functools.partial(_sb_kernel, bq=bq, bk=bk, scale=Dh ** -0.5)
    return pl.pallas_call(
        kern,
        grid=(H, T // bq),
        in_specs=[pl.BlockSpec((bq, Dh), lambda h, i: (i, h)),
                  pl.BlockSpec((T, Dh), lambda h, i: (0, H + h)),
                  pl.BlockSpec((T, Dh), lambda h, i: (0, 2 * H + h))],
        out_specs=pl.BlockSpec((bq, Dh), lambda h, i: (i, h)),
        out_shape=jax.ShapeDtypeStruct((T, H * Dh), BF16),
        scratch_shapes=[pltpu.VMEM((bq, Dh), F32), pltpu.VMEM((bq, 1), F32)],
        compiler_params=_cp("parallel", "parallel"),
        name="sb_attention",
    )(qkv, qkv, qkv)


_GLA_LEVELS = (32, 16, 8, 4, 2, 1)


def _gla_constants():
    C = GLA_CHUNK
    i = np.arange(C)[:, None]
    s = np.arange(C)[None, :]
    sel = []
    masks = []
    for m in _GLA_LEVELS:
        same = (i // m) == (s // m)
        sel.append(same & (s <= i) & ((i // m) % 2 == 1))
        sel.append(same & (s > i) & ((i // m) % 2 == 0))
        masks.append(((i // (2 * m)) == (s // (2 * m))) & ((i // m) % 2 == 1) & ((s // m) % 2 == 0))
    sel.append(s <= i)
    sel.append(s > i)
    masks.append(i == s)
    return (np.concatenate(sel, 0).astype(np.float32), np.stack(masks, 0).astype(np.float32))


def _gla_kernel(q_ref, k_ref, v_ref, go_ref, glr_ref, up_ref, bias_ref, ng_ref, sel_ref, msk_ref,
                o_ref, st_ref, *, n_chunks):
    C = GLA_CHUNK
    nl = len(_GLA_LEVELS)

    @pl.when(pl.program_id(1) == 0)
    def _():
        st_ref[...] = jnp.zeros_like(st_ref)

    sel = sel_ref[...]
    up = up_ref[...]
    bias = bias_ref[...]
    ng = ng_ref[...]
    rows = [slice(c * C, (c + 1) * C) for c in range(n_chunks)]
    blk = lambda f, j: f[j * C:(j + 1) * C]
    q = [q_ref[s, :] * (GLA_HEAD_K ** -0.5) for s in rows]
    k = [k_ref[s, :] for s in rows]
    v = [v_ref[s, :].astype(BF16) for s in rows]
    x = [_dot(glr_ref[s, :].astype(BF16), up) + bias for s in rows]
    g = [-_softplus(-t) * (1.0 / GLA_GATE_NORMALIZER) for t in x]
    f = [jnp.exp(_dot_xf(sel, t, 2)) for t in g]
    scores = [_dot(a.astype(BF16), b.astype(BF16), NT) * msk_ref[nl] for a, b in zip(q, k)]
    for l in range(nl):
        ql = [(a * blk(t, 2 * l)).astype(BF16) for a, t in zip(q, f)]
        kl = [(b * blk(t, 2 * l + 1)).astype(BF16) for b, t in zip(k, f)]
        scores = [s + _dot(a, b, NT) * msk_ref[l] for s, a, b in zip(scores, ql, kl)]
    o_intra = [_dot(s.astype(BF16), b) for s, b in zip(scores, v)]
    qe = [(a * blk(t, 2 * nl)).astype(BF16) for a, t in zip(q, f)]
    upd = [_dot(b, (a * blk(t, 2 * nl + 1)).astype(BF16), TN) for b, a, t in zip(v, k, f)]
    dec = [blk(t, 2 * nl)[C - 1:C, :] for t in f]

    st = st_ref[...]
    for c in range(n_chunks):
        o = _dot(qe[c], st.astype(BF16), NT) + o_intra[c]
        st = st * dec[c] + upd[c]
        rms = lax.rsqrt(jnp.mean(o * o, axis=-1, keepdims=True) + GLA_NORM_EPS)
        o_ref[rows[c], :] = (o * rms * ng * _silu(go_ref[rows[c], :])).astype(o_ref.dtype)
    st_ref[...] = st


def _gla(p_gla, gk_up, gk_bias, norm_g, tt=256):
    T = p_gla.shape[0]
    H, dk, dv = GLA_HEADS, GLA_HEAD_K, GLA_HEAD_V
    sel, msk = _gla_constants()
    up = jnp.zeros((LANES, GLA_KEY_DIM), F32).at[:GLA_GATE_RANK].set(gk_up).astype(BF16)
    kern = functools.partial(_gla_kernel, n_chunks=tt // GLA_CHUNK)
    return pl.pallas_call(
        kern,
        grid=(H, T // tt),
        in_specs=[pl.BlockSpec((tt, dk), lambda h, i: (i, h)),
                  pl.BlockSpec((tt, dk), lambda h, i: (i, H + h)),
                  pl.BlockSpec((tt, dv), lambda h, i: (i, H + h)),
                  pl.BlockSpec((tt, dv), lambda h, i: (i, 2 * H + h)),
                  pl.BlockSpec((tt, LANES), lambda h, i: (i, 3 * GLA_VALUE_DIM // LANES)),
                  pl.BlockSpec((LANES, dk), lambda h, i: (0, h)),
                  pl.BlockSpec((1, dk), lambda h, i: (0, h)),
                  pl.BlockSpec((1, dv), lambda h, i: (0, 0)),
                  pl.BlockSpec(sel.shape, lambda h, i: (0, 0)),
                  pl.BlockSpec(msk.shape, lambda h, i: (0, 0, 0))],
        out_specs=pl.BlockSpec((tt, dv), lambda h, i: (i, h)),
        out_shape=jax.ShapeDtypeStruct((T, H * dv), BF16),
        scratch_shapes=[pltpu.VMEM((dv, dk), F32)],
        compiler_params=_cp("parallel", "arbitrary"),
        name="gla",
    )(p_gla, p_gla, p_gla, p_gla, p_gla, up, gk_bias.reshape(1, -1), norm_g.reshape(1, -1),
      jnp.asarray(sel, BF16), jnp.asarray(msk, F32))


def _head_ones():
    l = np.arange(LANES)
    return (l[:, None] // RWKV_HEAD_DIM == l[None, :] // RWKV_HEAD_DIM).astype(np.float32)


def _head_diag():
    i = np.arange(RWKV_HEAD_DIM)[:, None]
    l = np.arange(LANES)[None, :]
    return (l % RWKV_HEAD_DIM == i).astype(np.float32)


def _rwprep_kernel(p_ref, pp_ref, mu_ref, w0_ref, w2_ref, a0_ref, a2_ref, g2_ref, kk_ref, ka_ref,
                   v0_ref, v1_ref, v2_ref, vf_ref, bo_ref,
                   r_ref, w_ref, k_ref, v_ref, a_ref, b_ref, g_ref, *, first_layer):
    W = RWKV_WIDTH
    i = pl.program_id(0)
    p = p_ref[...]
    tm = p.shape[0]
    prev_row = jnp.where(i == 0, 0.0, pp_ref[SUBLANES - 1:SUBLANES, :])
    rid = lax.broadcasted_iota(I32, p.shape, 0)
    prev = jnp.where(rid == 0, prev_row, pltpu.roll(p, 1, axis=0))
    ps = p + (prev - p) * mu_ref[...]
    r = ps[:, 0:W]
    kr = ps[:, W:2 * W]
    vr = ps[:, 2 * W:3 * W]
    lr = ps[:, 3 * W:3 * W + LANES]
    gl = ps[:, 3 * W + LANES:3 * W + 2 * LANES]
    w_log = -_softplus(-(w0_ref[...] + _dot(jnp.tanh(lr).astype(BF16), w2_ref[...]))) - 0.5
    w_ref[...] = -jnp.exp(w_log)
    if not first_layer:
        lo = _dot(vr.astype(BF16), v1_ref[...])
        gate = _sigmoid(v0_ref[...] + _dot(lo.astype(BF16), v2_ref[...]))
        vr = vr + (vf_ref[...] - vr) * gate
    v_ref[...] = vr
    a = _sigmoid(a0_ref[...] + _dot(lr.astype(BF16), a2_ref[...]))
    g_ref[...] = _dot(_sigmoid(gl).astype(BF16), g2_ref[...])
    kk = kr * kk_ref[...]
    bo = bo_ref[...]
    for hp in range(W // LANES):
        sl = slice(hp * LANES, (hp + 1) * LANES)
        kh = kk[:, sl]
        ss = _dot_fx(kh * kh, bo, 3)
        kn = kh / jnp.maximum(jnp.sqrt(ss), 1e-12)
        a_ref[:, sl] = -kn
        b_ref[:, sl] = kn * a[:, sl]
    r_ref[...] = r
    k_ref[...] = kr * (1.0 + (a - 1.0) * ka_ref[...])


def _rwkv_prep(p_rw, v_first, mu, w0, w2, a0, a2, g2, k_k, k_a, v0, v1, v2, tm=256):
    T, NC = p_rw.shape
    W = RWKV_WIDTH
    first = v_first is None
    w2p = jnp.zeros((LANES, W), F32).at[:RWKV_DECAY_RANK].set(w2).astype(BF16)
    a2p = jnp.zeros((LANES, W), F32).at[RWKV_DECAY_RANK:].set(a2).astype(BF16)
    if first:
        v_first = jnp.zeros((SUBLANES, W), F32)
        v0 = jnp.zeros((W,), F32)
        v1p = jnp.zeros((W, LANES), BF16)
        v2p = jnp.zeros((LANES, W), BF16)
        vf_spec = pl.BlockSpec((SUBLANES, W), lambda i: (0, 0))
    else:
        v1p = jnp.zeros((W, LANES), F32).at[:, :RWKV_VALUE_RANK].set(v1).astype(BF16)
        v2p = jnp.zeros((LANES, W), F32).at[:RWKV_VALUE_RANK].set(v2).astype(BF16)
        vf_spec = pl.BlockSpec((tm, W), lambda i: (i, 0))
    row = lambda a: a.reshape(1, -1)
    full = lambda shape: pl.BlockSpec(shape, lambda i: (0,) * len(shape))
    out = jax.ShapeDtypeStruct((T, W), F32)
    ospec = pl.BlockSpec((tm, W), lambda i: (i, 0))
    kern = functools.partial(_rwprep_kernel, first_layer=first)
    return pl.pallas_call(
        kern,
        grid=(T // tm,),
        in_specs=[pl.BlockSpec((tm, NC), lambda i: (i, 0)),
                  pl.BlockSpec((SUBLANES, NC), lambda i: (jnp.maximum(i * (tm // SUBLANES) - 1, 0), 0)),
                  full((1, NC)), full((1, W)), full((LANES, W)), full((1, W)), full((LANES, W)),
                  full((LANES, W)), full((1, W)), full((1, W)),
                  full((1, W)), full((W, LANES)), full((LANES, W)), vf_spec, full((LANES, LANES))],
        out_specs=[ospec] * 7,
        out_shape=[out] * 7,
        compiler_params=_cp("parallel"),
        name="rwkv_prep",
    )(p_rw, p_rw, row(mu), row(w0), w2p, row(a0), a2p, g2.astype(BF16), row(k_k), row(k_a),
      row(v0), v1p, v2p, v_first, jnp.asarray(_head_ones(), BF16))


RW_CHUNK = 64


def _rw_constants():
    C = RW_CHUNK
    i = np.arange(C)[:, None]
    s = np.arange(C)[None, :]
    i2 = np.arange(2 * C)[:, None]
    s2 = np.arange(2 * C)[None, :]
    same = (i2 // C) == (s2 // C)
    cm = np.stack([same & (s2 < i2), same & (s2 <= i2), (i2 // 16) == (s2 // 16), (i2 // 32) == (s2 // 32),
                   same], 0).astype(np.float32)
    return (s <= i).astype(np.float32), cm


def _mm(a, b, dims=NN):
    a1, a2 = _split(a, 2)
    b1, b2 = _split(b, 2)
    return _dot(a1, b1, dims) + (_dot(a1, b2, dims) + _dot(a2, b1, dims))


def _unit_lower_inverse(nms, eye, bd16, bd32):
    nd = [n * bd16 for n in nms]
    x = [eye + n for n in nd]
    p = [_mm(n, n) for n in nd]
    for step in range(3):
        x = [a + _mm(a, q) for a, q in zip(x, p)]
        if step < 2:
            p = [_mm(q, q) for q in p]
    for lo_mask in (bd32 - bd16, 1.0 - bd32):
        t = [_mm(n * lo_mask, a) for n, a in zip(nms, x)]
        x = [a + _mm(a, q) for a, q in zip(x, t)]
    return x


def _rwrec_kernel(r_ref, lw_ref, k_ref, v_ref, a_ref, b_ref, lt_ref, cm_ref, y_ref, mt_ref, *, n_chunks):
    C = RW_CHUNK
    C2 = 2 * C

    @pl.when(pl.program_id(1) == 0)
    def _():
        mt_ref[...] = jnp.zeros_like(mt_ref)

    lt = lt_ref[...]
    sl, li, bd16, bd32, bdh = cm_ref[0], cm_ref[1], cm_ref[2], cm_ref[3], cm_ref[4]
    eye = li - sl
    lane = lax.broadcasted_iota(I32, (1, LANES), 1)
    m0 = jnp.where(lane < RWKV_HEAD_DIM, 1.0, 0.0)
    m1 = 1.0 - m0
    stack = lambda x: jnp.concatenate([x * m0, x * m1], axis=0)
    fold = lambda x: x[:C] + x[C:]
    rows = [slice(c * C, (c + 1) * C) for c in range(n_chunks)]

    lw = [lw_ref[s, :] for s in rows]
    cw = [_dot_xf(lt, x, 3) for x in lw]
    en = [jnp.exp(-x) for x in cw]
    at = [a_ref[s, :] * jnp.exp(c - l) for s, c, l in zip(rows, cw, lw)]
    rt = [r_ref[s, :] * jnp.exp(c) for s, c in zip(rows, cw)]
    bh = [b_ref[s, :] * e for s, e in zip(rows, en)]
    kh = [k_ref[s, :] * e for s, e in zip(rows, en)]
    v = [v_ref[s, :] for s in rows]
    a2 = [stack(x) for x in at]
    ar2 = [jnp.concatenate([x, stack(y)], axis=0) for x, y in zip(a2, rt)]
    gb = [_mm(x, stack(y), NT) for x, y in zip(ar2, bh)]
    gk = [_mm(x, stack(y), NT) for x, y in zip(ar2, kh)]
    tinv = _unit_lower_inverse([g[:C2] * sl for g in gb], eye, bd16, bd32)
    akv = [_mm(jnp.concatenate([g[:C2] * sl, g[C2:] * li], axis=0), stack(x)) for g, x in zip(gk, v)]
    tav = [_mm(t, jnp.concatenate([x, y[:C2]], axis=1)) for t, x, y in zip(tinv, a2, akv)]
    rav = [_mm(g[C2:] * li, x) for g, x in zip(gb, tav)]
    ta = [fold(x[:, :LANES]) for x in tav]
    u0 = [fold(x[:, LANES:]) for x in tav]
    qe = [x + fold(y[:, :LANES]) for x, y in zip(rt, rav)]
    y0 = [fold(x[:, LANES:] + y[C2:]) for x, y in zip(rav, akv)]
    dl = [jnp.exp(x[C - 1:C, :]) for x in cw]
    gmt = [(eye + _mm(x, y, TN) * bdh) * d for x, y, d in zip(ta, bh, dl)]
    hmt = [(_mm(x, y, TN) + _mm(z, w, TN)) * bdh * d for x, y, z, w, d in zip(u0, bh, v, kh, dl)]

    mt = mt_ref[...]
    for c in range(n_chunks):
        y_ref[rows[c], :] = _mm(qe[c], mt, NT) + y0[c]
        mt = _mm(mt, gmt[c]) + hmt[c]
    mt_ref[...] = mt


def _rwkv_recurrence(r, lw, k, v, a, b, tt=256):
    T, W = r.shape
    lt, cm = _rw_constants()
    spec = pl.BlockSpec((tt, LANES), lambda p, i: (i, p))
    kern = functools.partial(_rwrec_kernel, n_chunks=tt // RW_CHUNK)
    return pl.pallas_call(
        kern,
        grid=(W // LANES, T // tt),
        in_specs=[spec] * 6 + [pl.BlockSpec(lt.shape, lambda p, i: (0, 0)),
                               pl.BlockSpec(cm.shape, lambda p, i: (0, 0, 0))],
        out_specs=spec,
        out_shape=jax.ShapeDtypeStruct((T, W), F32),
        scratch_shapes=[pltpu.VMEM((LANES, LANES), F32)],
        compiler_params=_cp("parallel", "arbitrary"),
        name="rwkv_recurrence",
    )(r, lw, k, v, a, b, jnp.asarray(lt, BF16), jnp.asarray(cm, F32))


def _rwpost_kernel(y_ref, r_ref, k_ref, v_ref, g_ref, lg_ref, lb_ref, rk_ref, bo_ref, o_ref):
    bo = bo_ref[...]
    inv = 1.0 / RWKV_HEAD_DIM
    for hp in range(RWKV_WIDTH // LANES):
        sl = slice(hp * LANES, (hp + 1) * LANES)
        y = y_ref[:, sl]
        mu = _dot_fx(y, bo, 3) * inv
        d = y - mu
        var = _dot_fx(d * d, bo, 3) * inv
        yn = d * lax.rsqrt(var + RWKV_LNX_EPS) * lg_ref[:, sl] + lb_ref[:, sl]
        bonus = _dot_fx(r_ref[:, sl] * k_ref[:, sl] * rk_ref[:, sl], bo, 3) * v_ref[:, sl]
        o_ref[:, sl] = ((yn + bonus) * g_ref[:, sl]).astype(o_ref.dtype)


def _rwkv_post(y, r, k, v, g, lnx_g, lnx_b, r_k, tm=512):
    T, W = y.shape
    spec = pl.BlockSpec((tm, W), lambda i: (i, 0))
    rowspec = pl.BlockSpec((1, W), lambda i: (0, 0))
    return pl.pallas_call(
        _rwpost_kernel,
        grid=(T // tm,),
        in_specs=[spec] * 5 + [rowspec] * 3 + [pl.BlockSpec((LANES, LANES), lambda i: (0, 0))],
        out_specs=spec,
        out_shape=jax.ShapeDtypeStruct((T, W), BF16),
        compiler_params=_cp("parallel"),
        name="rwkv_post",
    )(y, r, k, v, g, lnx_g.reshape(1, W), lnx_b.reshape(1, W), r_k.reshape(1, W),
      jnp.asarray(_head_ones(), BF16))


def _merge_kernel(osb_ref, ogl_ref, orw_ref, g1_ref, g2_ref, g3_ref, w1_ref, w2_ref, w3_ref, o_ref):
    acc = _sigmoid(g1_ref[...].astype(F32)) * _dot(osb_ref[...], w1_ref[...])
    acc = acc + _sigmoid(g2_ref[...].astype(F32)) * _dot(ogl_ref[...], w2_ref[...])
    acc = acc + _sigmoid(g3_ref[...].astype(F32)) * _dot(orw_ref[...], w3_ref[...])
    o_ref[...] = acc.astype(o_ref.dtype)


def _merge(o_sb, o_gla, o_rw, p_merge, w_sb, w_gla, w_rw, tm=512, tn=1024):
    T = o_sb.shape[0]
    D = D_MODEL
    nb = D // tn
    aspec = pl.BlockSpec((tm, 1024), lambda j, i: (i, 0))
    wspec = pl.BlockSpec((1024, tn), lambda j, i: (0, j))
    gspec = lambda b: pl.BlockSpec((tm, tn), lambda j, i: (i, b * nb + j))
    return pl.pallas_call(
        _merge_kernel,
        grid=(nb, T // tm),
        in_specs=[aspec, aspec, aspec, gspec(0), gspec(1), gspec(2), wspec, wspec, wspec],
        out_specs=pl.BlockSpec((tm, tn), lambda j, i: (i, j)),
        out_shape=jax.ShapeDtypeStruct((T, D), BF16),
        compiler_params=_cp("parallel", "parallel"),
        name="merge",
    )(o_sb, o_gla, o_rw, p_merge, p_merge, p_merge, w_sb, w_gla, w_rw)


def _layer_norm(h, g, b):
    mu = jnp.mean(h, axis=-1, keepdims=True)
    d = h - mu
    var = jnp.mean(d * d, axis=-1, keepdims=True)
    return d * lax.rsqrt(var + LN_EPS) * g + b


def _oproj_kernel(m_ref, w_ref, x_ref, gate_ref, lg_ref, lb_ref, o_ref):
    mix = _dot(m_ref[...], w_ref[...])
    h = DEEPNORM_ALPHA * x_ref[...] + (1.0 + gate_ref[...]) * mix
    o_ref[...] = _layer_norm(h, lg_ref[...], lb_ref[...])


def _oproj(merged, w_o, x, gate, ln_g, ln_b, tm=512):
    T, D = x.shape
    spec = pl.BlockSpec((tm, D), lambda i: (i, 0))
    rowspec = pl.BlockSpec((1, D), lambda i: (0, 0))
    return pl.pallas_call(
        _oproj_kernel,
        grid=(T // tm,),
        in_specs=[spec, pl.BlockSpec((D, D), lambda i: (0, 0)), spec, rowspec, rowspec, rowspec],
        out_specs=spec,
        out_shape=jax.ShapeDtypeStruct((T, D), F32),
        compiler_params=_cp("parallel"),
        name="oproj_norm",
    )(merged, w_o, x, gate, ln_g.reshape(1, D), ln_b.reshape(1, D))


def _first_max(x, idx, axis, big):
    m = jnp.max(x, axis=axis, keepdims=True)
    first = jnp.min(jnp.where(x == m, idx, big), axis=axis, keepdims=True)
    return m, idx == first


def _router_kernel(x_ref, sc_ref, sh_ref, wr_ref, rb_ref, u_ref, e_ref, g_ref, p_ref, cnt_ref, car_ref, *, tm):
    E, G = N_EXPERTS, N_GROUPS
    per = E // G
    neg = -jnp.inf

    @pl.when(pl.program_id(0) == 0)
    def _():
        car_ref[...] = jnp.zeros_like(car_ref)

    u = x_ref[...] * (1.0 + sc_ref[...]) + sh_ref[...]
    _store_tokens(u_ref, u, tm)
    scores =_sigmoid(_dot_ff(wr_ref[...], u, NT))
    biased = scores + rb_ref[...]
    b3 = biased.reshape(G, per, tm)
    i3 = lax.broadcasted_iota(I32, (G, per, tm), 1)
    m1, hit = _first_max(b3, i3, 1, per)
    m2 = jnp.max(jnp.where(hit, neg, b3), axis=1, keepdims=True)
    gs = (m1 + m2).reshape(G, tm)
    gi = lax.broadcasted_iota(I32, (G, tm), 0)
    ok = jnp.zeros((G, tm), jnp.bool_)
    for _ in range(TOPK_GROUPS):
        _, hit = _first_max(gs, gi, 0, G)
        ok = jnp.logical_or(ok, hit)
        gs = jnp.where(hit, neg, gs)
    ok_e = jnp.broadcast_to(ok.reshape(G, 1, tm), (G, per, tm)).reshape(E, tm)
    cand = jnp.where(ok_e, biased, neg)
    ei = lax.broadcasted_iota(I32, (E, tm), 0)
    sels, gates = [], []
    for _ in range(TOP_K):
        _, hit = _first_max(cand, ei, 0, E)
        sels.append(hit)
        gates.append(jnp.sum(jnp.where(hit, scores, 0.0), axis=0, keepdims=True))
        cand = jnp.where(hit, neg, cand)
    denom = gates[0]
    for gk in gates[1:]:
        denom = denom + gk
    chosen = sels[0]
    for s in sels[1:]:
        chosen = jnp.logical_or(chosen, s)
    cnt = jnp.where(chosen, 1.0, 0.0)
    earlier = (lax.broadcasted_iota(I32, (tm, tm), 0) < lax.broadcasted_iota(I32, (tm, tm), 1)).astype(BF16)
    rank = _dot(cnt.astype(BF16), earlier) + car_ref[...][:, 0:1]
    car_ref[...] = car_ref[...] + jnp.sum(cnt, axis=1, keepdims=True)
    cnt_ref[...] = car_ref[...]
    zi = jnp.zeros((1, tm), I32)
    zf = jnp.zeros((1, tm), F32)
    e_rows = [jnp.sum(jnp.where(s, ei, 0), axis=0, keepdims=True) for s in sels]
    p_rows = [jnp.sum(jnp.where(s, rank, 0.0), axis=0, keepdims=True).astype(I32) for s in sels]
    g_rows = [gk / denom * ROUTED_SCALE for gk in gates]
    e_ref[...] = jnp.concatenate(e_rows + [zi, zi], axis=0)
    p_ref[...] = jnp.concatenate(p_rows + [zi, zi], axis=0)
    g_ref[...] = jnp.concatenate(g_rows + [zf, zf], axis=0)


def _router(x, sc, sh, w_router, router_bias, tm=256):
    T, D = x.shape
    E = N_EXPERTS
    kern = functools.partial(_router_kernel, tm=tm)
    rowspec = pl.BlockSpec((1, D), lambda i: (0, 0))
    kspec = pl.BlockSpec((SUBLANES, tm), lambda i: (0, i))
    return pl.pallas_call(
        kern,
        grid=(T // tm,),
        in_specs=[pl.BlockSpec((tm, D), lambda i: (i, 0)), rowspec, rowspec,
                  pl.BlockSpec((E, D), lambda i: (0, 0)), pl.BlockSpec((E, 1), lambda i: (0, 0))],
        out_specs=[pl.BlockSpec((tm * TOKEN_ROWS, LANES), lambda i: (i, 0)), kspec, kspec, kspec,
                   pl.BlockSpec((E, LANES), lambda i: (0, 0))],
        out_shape=[jax.ShapeDtypeStruct((T * TOKEN_ROWS, LANES), F32),
                   jax.ShapeDtypeStruct((SUBLANES, T), I32),
                   jax.ShapeDtypeStruct((SUBLANES, T), F32),
                   jax.ShapeDtypeStruct((SUBLANES, T), I32),
                   jax.ShapeDtypeStruct((E, LANES), F32)],
        scratch_shapes=[pltpu.VMEM((E, LANES), F32)],
        compiler_params=_cp("arbitrary"),
        name="router",
    )(x, sc, sh, w_router.T, router_bias.reshape(E, 1))


def _dispatch_kernel(dest_ref, fill_ref, u_ref, z_ref, xs_ref, sem, zsem, *, tm, bm):
    i = pl.program_id(0)

    @pl.when(i == 0)
    def _():
        def zcopy(e):
            start_row = pl.multiple_of(fill_ref[e] * TOKEN_ROWS, bm * TOKEN_ROWS)
            return pltpu.make_async_copy(z_ref, xs_ref.at[pl.ds(start_row, bm * TOKEN_ROWS)], zsem)

        def zstart(e, c):
            zcopy(e).start()
            return c

        def zwait(e, c):
            zcopy(e).wait()
            return c

        lax.fori_loop(0, N_EXPERTS, zstart, 0)
        lax.fori_loop(0, N_EXPERTS, zwait, 0)

    def copy(t, k):
        return pltpu.make_async_copy(_token_rows(u_ref, t), _token_rows(xs_ref, dest_ref[k, t]), sem)

    def start(t, c):
        for k in range(TOP_K):
            copy(t, k).start()
        return c

    def wait(t, c):
        for k in range(TOP_K):
            copy(t, k).wait()
        return c

    lax.fori_loop(0, tm, start, 0)
    lax.fori_loop(0, tm, wait, 0)


def _dispatch(u, dest, fill, n_slots, tm=256, bm=MOE_BM):
    T = u.shape[0] // TOKEN_ROWS
    kern = functools.partial(_dispatch_kernel, tm=tm, bm=bm)
    zeros = jnp.zeros((bm * TOKEN_ROWS, LANES), F32)
    return pl.pallas_call(
        kern,
        grid=(T // tm,),
        in_specs=[pl.BlockSpec((SUBLANES, tm), lambda i: (0, i), memory_space=pltpu.SMEM),
                  pl.BlockSpec(memory_space=pltpu.SMEM),
                  pl.BlockSpec((tm * TOKEN_ROWS, LANES), lambda i: (i, 0)),
                  pl.BlockSpec(memory_space=pl.ANY)],
        out_specs=pl.BlockSpec(memory_space=pl.ANY),
        out_shape=jax.ShapeDtypeStruct((n_slots * TOKEN_ROWS, LANES), F32),
        scratch_shapes=[pltpu.SemaphoreType.DMA(()), pltpu.SemaphoreType.DMA(())],
        compiler_params=_cp("arbitrary"),
        name="moe_dispatch",
    )(dest, fill, u, zeros)


def _expert_kernel(be_ref, nv_ref, x_ref, wg_ref, wu_ref, wd_ref, y_ref, wgb_ref, wub_ref, wdb_ref, *, bm):
    b = pl.program_id(0)
    valid = b < nv_ref[0]
    new_expert = jnp.logical_or(b == 0, be_ref[b] != be_ref[jnp.maximum(b - 1, 0)])

    @pl.when(jnp.logical_and(valid, new_expert))
    def _():
        wgb_ref[...] = wg_ref[0].astype(BF16)
        wub_ref[...] = wu_ref[0].astype(BF16)
        wdb_ref[...] = wd_ref[0].astype(BF16)

    @pl.when(valid)
    def _():
        x = _load_tokens(x_ref, bm).astype(BF16)
        h = _silu(_dot(x, wgb_ref[...])) * _dot(x, wub_ref[...])
        _store_tokens(y_ref, _dot(h.astype(BF16), wdb_ref[...]), bm)


def _experts(xs, block_expert, n_valid, w_gate, w_up, w_down, layer, bm=MOE_BM):
    n_slots = xs.shape[0] // TOKEN_ROWS
    D = D_MODEL
    nb = n_slots // bm
    F = w_gate.shape[-1]
    blk = lambda b, be, nv: (jnp.minimum(b, nv[0] - 1), 0)
    wsel = lambda b, be, nv: (layer, be[jnp.minimum(b, nv[0] - 1)], 0, 0)
    gs = pltpu.PrefetchScalarGridSpec(
        num_scalar_prefetch=2,
        grid=(nb,),
        in_specs=[pl.BlockSpec((bm * TOKEN_ROWS, LANES), blk),
                  pl.BlockSpec((None, 1, D, F), wsel),
                  pl.BlockSpec((None, 1, D, F), wsel),
                  pl.BlockSpec((None, 1, F, D), wsel)],
        out_specs=pl.BlockSpec((bm * TOKEN_ROWS, LANES), blk),
        scratch_shapes=[pltpu.VMEM((D, F), BF16), pltpu.VMEM((D, F), BF16), pltpu.VMEM((F, D), BF16)],
    )
    return pl.pallas_call(
        functools.partial(_expert_kernel, bm=bm),
        grid_spec=gs,
        out_shape=jax.ShapeDtypeStruct((n_slots * TOKEN_ROWS, LANES), F32),
        compiler_params=_cp("arbitrary"),
        name="moe_experts",
    )(block_expert, n_valid, xs, w_gate, w_up, w_down)


def _combine_kernel(dest_ref, ys_ref, u_ref, x_ref, gt_ref, wg_ref, wu_ref, wd_ref, gate_ref, lg_ref, lb_ref,
                    o_ref, buf_ref, sem, *, tm):
    def copy(t, k):
        return pltpu.make_async_copy(_token_rows(ys_ref, dest_ref[k, t]), _token_rows(buf_ref.at[k], t), sem)

    def start(t, c):
        for k in range(TOP_K):
            copy(t, k).start()
        return c

    def wait(t, c):
        for k in range(TOP_K):
            copy(t, k).wait()
        return c

    lax.fori_loop(0, tm, start, 0)
    u = _load_tokens(u_ref, tm).astype(BF16)
    h = _silu(_dot(u, wg_ref[...])) * _dot(u, wu_ref[...])
    ffn = _dot(h.astype(BF16), wd_ref[...])
    lax.fori_loop(0, tm, wait, 0)
    gt = gt_ref[...]
    for k in range(TOP_K):
        ffn = ffn + gt[:, k:k + 1] * _load_tokens(buf_ref, tm, lead=(k,))
    hres = DEEPNORM_ALPHA * x_ref[...] + (1.0 + gate_ref[...]) * ffn
    o_ref[...] = _layer_norm(hres, lg_ref[...], lb_ref[...])


def _combine(ys, dest, u, x, gates_t, w_sg, w_su, w_sd, gate, ln_g, ln_b, tm=128):
    T, D = x.shape
    F = w_sg.shape[-1]
    kern = functools.partial(_combine_kernel, tm=tm)
    spec = pl.BlockSpec((tm, D), lambda i: (i, 0))
    rowspec = pl.BlockSpec((1, D), lambda i: (0, 0))
    return pl.pallas_call(
        kern,
        grid=(T // tm,),
        in_specs=[pl.BlockSpec((SUBLANES, tm), lambda i: (0, i), memory_space=pltpu.SMEM),
                  pl.BlockSpec(memory_space=pl.ANY),
                  pl.BlockSpec((tm * TOKEN_ROWS, LANES), lambda i: (i, 0)), spec,
                  pl.BlockSpec((tm, SUBLANES), lambda i: (i, 0)),
                  pl.BlockSpec((D, F), lambda i: (0, 0)),
                  pl.BlockSpec((D, F), lambda i: (0, 0)),
                  pl.BlockSpec((F, D), lambda i: (0, 0)),
                  rowspec, rowspec, rowspec],
        out_specs=spec,
        out_shape=jax.ShapeDtypeStruct((T, D), F32),
        scratch_shapes=[pltpu.VMEM((TOP_K, tm * TOKEN_ROWS, LANES), F32), pltpu.SemaphoreType.DMA(())],
        compiler_params=_cp("arbitrary"),
        name="moe_combine_norm",
    )(dest, ys, u, x, gates_t, w_sg, w_su, w_sd, gate, ln_g.reshape(1, D), ln_b.reshape(1, D))


def _moe_plan(top_e, rank, counts, bm, n_blocks):
    counts = counts.astype(I32)
    padded = (counts + bm - 1) // bm * bm
    padded_end = jnp.cumsum(padded)
    padded_start = padded_end - padded
    onehot = top_e[:, :, None] == jnp.arange(N_EXPERTS, dtype=I32)
    dest = jnp.sum(jnp.where(onehot, padded_start, 0), axis=-1) + rank
    fill = jnp.maximum(padded_end - bm, padded_start)
    fill = jnp.minimum(fill, (n_blocks - 1) * bm)
    block_start = jnp.arange(n_blocks, dtype=I32) * bm
    block_expert = jnp.minimum(jnp.sum((padded_end[None, :] <= block_start[:, None]).astype(I32), axis=1),
                               N_EXPERTS - 1)
    n_valid = jnp.maximum(padded_end[-1:] // bm, 1).astype(I32)
    return dest.astype(I32), fill.astype(I32), block_expert, n_valid


def _in_proj_weights(w_in):
    o = SB_COLS
    gq = w_in[:, o:o + 2 * GLA_KEY_DIM + GLA_VALUE_DIM]
    o += 2 * GLA_KEY_DIM + GLA_VALUE_DIM
    glr = w_in[:, o:o + GLA_GATE_RANK]
    o += GLA_GATE_RANK
    gout = w_in[:, o:o + GLA_VALUE_DIM]
    o += GLA_VALUE_DIM
    pad = jnp.zeros((w_in.shape[0], LANES - GLA_GATE_RANK), w_in.dtype)
    w_gla = jnp.concatenate([gq, gout, glr, pad], axis=1)
    w_rw = w_in[:, o:o + RWKV_COLS]
    o += RWKV_COLS
    w_mg = w_in[:, o:o + MERGE_COLS]
    return w_in[:, :SB_COLS], w_gla, w_rw, w_mg


def kernel(x, c, w_ada, b_ada, w_in, gla_gk_up, gla_gk_bias, gla_norm_g, rw_mu, rw_w0, rw_w2, rw_a0,
           rw_a2, rw_g2, rw_k_k, rw_k_a, rw_r_k, rw_lnx_g, rw_lnx_b, rw_v0, rw_v1, rw_v2,
           w_br_sb, w_br_gla, w_br_rw, w_o, ln_g, ln_b, w_router, router_bias,
           w_exp_gate, w_exp_up, w_exp_down, w_sh_gate, w_sh_up, w_sh_down):
    B, T, D = x.shape
    assert B == 1 and D == D_MODEL
    n_blocks = -(-(T * TOP_K) // MOE_BM) + N_EXPERTS
    mod = _ada(c, w_ada, b_ada)
    xs = x.reshape(T, D)
    v_first = None
    for l in range(DEPTH):
        sh1, sc1, g1, sh2, sc2, g2 = (mod[l, j * D:(j + 1) * D].reshape(1, D) for j in range(6))
        w_sb, w_gla, w_rw, w_mg = _in_proj_weights(w_in[l])
        u1 = _modulate(xs, sc1, sh1)
        p_sb = _wmm(u1, w_sb, BF16, 1024, name="inproj_sb")
        p_gla = _wmm(u1, w_gla, F32, 640, name="inproj_gla")
        p_rw = _wmm(u1, w_rw, F32, 832 * 2, name="inproj_rwkv")
        p_mg = _wmm(u1, w_mg, BF16, 1024, name="inproj_merge")

        o_sb = _sb_attention(p_sb)
        o_gla = _gla(p_gla, gla_gk_up[l], gla_gk_bias[l], gla_norm_g[l])
        if l == 0:
            r, w, k, v, a, b, g = _rwkv_prep(p_rw, None, rw_mu[l], rw_w0[l], rw_w2[l], rw_a0[l], rw_a2[l],
                                             rw_g2[l], rw_k_k[l], rw_k_a[l], None, None, None)
            v_first = v
        else:
            r, w, k, v, a, b, g = _rwkv_prep(p_rw, v_first, rw_mu[l], rw_w0[l], rw_w2[l], rw_a0[l], rw_a2[l],
                                             rw_g2[l], rw_k_k[l], rw_k_a[l], rw_v0[l - 1], rw_v1[l - 1],
                                             rw_v2[l - 1])
        y = _rwkv_recurrence(r, w, k, v, a, b)
        o_rw = _rwkv_post(y, r, k, v, g, rw_lnx_g[l], rw_lnx_b[l], rw_r_k[l])

        merged = _merge(o_sb, o_gla, o_rw, p_mg, w_br_sb[l].astype(BF16), w_br_gla[l].astype(BF16),
                        w_br_rw[l].astype(BF16))
        xs = _oproj(merged, w_o[l].astype(BF16), xs, g1, ln_g[l, 0], ln_b[l, 0])

        u2, top_e, gates, rank, counts = _router(xs, sc2, sh2, w_router[l], router_bias[l])
        dest, fill, block_expert, n_valid = _moe_plan(top_e, rank, counts[:, 0], MOE_BM, n_blocks)
        x_sorted = _dispatch(u2, dest, fill, n_blocks * MOE_BM)
        y_sorted = _experts(x_sorted, block_expert, n_valid, w_exp_gate, w_exp_up, w_exp_down, l)
        xs = _combine(y_sorted, dest, u2, xs, gates.T, w_sh_gate[l].astype(BF16), w_sh_up[l].astype(BF16),
                      w_sh_down[l].astype(BF16), g2, ln_g[l, 1], ln_b[l, 1])
    return xs.reshape(B, T, D)
```

```python
import functools

import numpy as np
import jax
import jax.numpy as jnp
from jax import lax
from jax.experimental import pallas as pl
from jax.experimental.pallas import tpu as pltpu

F32 = jnp.float32
BF16 = jnp.bfloat16
I32 = jnp.int32

D_MODEL = 2048
DEPTH = 2
LN_EPS = 1e-5
DEEPNORM_ALPHA = (2 * DEPTH) ** 0.25

SB_HEADS = 8
SB_HEAD_DIM = 128
SB_WIDTH = SB_HEADS * SB_HEAD_DIM

GLA_HEADS = 4
GLA_HEAD_K = 128
GLA_HEAD_V = 256
GLA_KEY_DIM = GLA_HEADS * GLA_HEAD_K
GLA_VALUE_DIM = GLA_HEADS * GLA_HEAD_V
GLA_GATE_RANK = 16
GLA_GATE_NORMALIZER = 16.0
GLA_CHUNK = 64
GLA_NORM_EPS = 1e-5

RWKV_HEADS = 16
RWKV_HEAD_DIM = 64
RWKV_WIDTH = RWKV_HEADS * RWKV_HEAD_DIM
RWKV_DECAY_RANK = 64
RWKV_AAA_RANK = 64
RWKV_VALUE_RANK = 32
RWKV_GATE_RANK = 128
RWKV_LNX_EPS = 64e-5

N_EXPERTS = 64
TOP_K = 6
EXPERT_DIM = 512
SHARED_DIM = 512
N_GROUPS = 8
TOPK_GROUPS = 4
ROUTED_SCALE = 2.5

SB_COLS = 3 * SB_WIDTH
GLA_COLS = 2 * GLA_KEY_DIM + GLA_VALUE_DIM + GLA_GATE_RANK + GLA_VALUE_DIM
RWKV_COLS = 3 * RWKV_WIDTH + RWKV_DECAY_RANK + RWKV_AAA_RANK + RWKV_GATE_RANK
MERGE_COLS = 3 * D_MODEL

LANES = 128
SUBLANES = 8
VMEM_LIMIT = 56 * 1024 * 1024
MOE_BM = 256
SB_EXIT = -104.0

NN = (((1,), (0,)), ((), ()))
NT = (((1,), (1,)), ((), ()))
TN = (((0,), (0,)), ((), ()))


def _cp(*sem):
    return pltpu.CompilerParams(dimension_semantics=sem, vmem_limit_bytes=VMEM_LIMIT)


def _dot(a, b, dims=NN):
    return lax.dot_general(a, b, dims, preferred_element_type=F32)


def _split(a, n):
    out = []
    r = a
    for i in range(n):
        p = r.astype(BF16)
        out.append(p)
        if i + 1 < n:
            r = r - p.astype(F32)
    return out


def _dot_fx(a, e, n, dims=NN):
    acc = None
    for p in _split(a, n):
        t = _dot(p, e, dims)
        acc = t if acc is None else acc + t
    return acc


def _dot_xf(e, b, n, dims=NN):
    acc = None
    for p in _split(b, n):
        t = _dot(e, p, dims)
        acc = t if acc is None else acc + t
    return acc


def _dot_ff(a, b, dims=NN):
    a1, a2, a3 = _split(a, 3)
    b1, b2, b3 = _split(b, 3)
    acc = _dot(a1, b1, dims)
    for p, q in ((a1, b2), (a2, b1), (a2, b2), (a1, b3), (a3, b1)):
        acc = acc + _dot(p, q, dims)
    return acc


def _sigmoid(x):
    return 1.0 / (1.0 + jnp.exp(-x))


def _softplus(x):
    return jnp.maximum(x, 0.0) + jnp.log1p(jnp.exp(-jnp.abs(x)))


def _silu(x):
    return x * _sigmoid(x)


def _row(ref, t):
    return ref.at[pl.ds(t, 1)]


def _ada_kernel(c_ref, w_ref, b_ref, o_ref):
    c = c_ref[...]
    o_ref[0] = _dot_ff(_silu(c), w_ref[0]) + b_ref[0]


def _ada(c, w_ada, b_ada):
    L, D, N = w_ada.shape
    tn = 512
    c8 = jnp.broadcast_to(c, (SUBLANES, D))
    out = pl.pallas_call(
        _ada_kernel,
        grid=(L, N // tn),
        in_specs=[pl.BlockSpec((SUBLANES, D), lambda l, j: (0, 0)),
                  pl.BlockSpec((1, D, tn), lambda l, j: (l, 0, j)),
                  pl.BlockSpec((1, 1, tn), lambda l, j: (l, 0, j))],
        out_specs=pl.BlockSpec((1, SUBLANES, tn), lambda l, j: (l, 0, j)),
        out_shape=jax.ShapeDtypeStruct((L, SUBLANES, N), F32),
        compiler_params=_cp("parallel", "parallel"),
        name="ada",
    )(c8, w_ada, b_ada.reshape(L, 1, N))
    return out[:, 0, :]


def _modulate_kernel(x_ref, sc_ref, sh_ref, o_ref):
    o_ref[...] = (x_ref[...] * (1.0 + sc_ref[...]) + sh_ref[...]).astype(o_ref.dtype)


def _modulate(x, sc, sh, tm=1024):
    T, D = x.shape
    spec = pl.BlockSpec((tm, D), lambda i: (i, 0))
    rowspec = pl.BlockSpec((1, D), lambda i: (0, 0))
    return pl.pallas_call(
        _modulate_kernel,
        grid=(T // tm,),
        in_specs=[spec, rowspec, rowspec],
        out_specs=spec,
        out_shape=jax.ShapeDtypeStruct((T, D), BF16),
        compiler_params=_cp("parallel"),
        name="modulate",
    )(x, sc, sh)


def _wmm_kernel(u_ref, w_ref, o_ref, wb_ref):
    @pl.when(pl.program_id(1) == 0)
    def _():
        wb_ref[...] = w_ref[...].astype(BF16)

    o_ref[...] = _dot(u_ref[...], wb_ref[...]).astype(o_ref.dtype)


def _wmm(u, w, out_dtype, tn, tm=512, name="wmm"):
    T, D = u.shape
    N = w.shape[1]
    return pl.pallas_call(
        _wmm_kernel,
        grid=(N // tn, T // tm),
        in_specs=[pl.BlockSpec((tm, D), lambda j, i: (i, 0)),
                  pl.BlockSpec((D, tn), lambda j, i: (0, j))],
        out_specs=pl.BlockSpec((tm, tn), lambda j, i: (i, j)),
        out_shape=jax.ShapeDtypeStruct((T, N), out_dtype),
        scratch_shapes=[pltpu.VMEM((D, tn), BF16)],
        compiler_params=_cp("parallel", "arbitrary"),
        name=name,
    )(u, w)


def _sb_kernel(q_ref, k_ref, v_ref, o_ref, acc_ref, car_ref, *, bq, bk, scale):
    i = pl.program_id(1)
    acc_ref[...] = jnp.zeros_like(acc_ref)
    car_ref[...] = jnp.zeros_like(car_ref)
    q = q_ref[...]
    row = i * bq + lax.broadcasted_iota(I32, (bq, bk), 0)
    later = (lax.broadcasted_iota(I32, (bk, bk), 0) > lax.broadcasted_iota(I32, (bk, bk), 1)).astype(BF16)

    def body(c):
        j, _ = c
        off = pl.multiple_of(j * bk, bk)
        kb = k_ref[pl.ds(off, bk), :]
        vb = v_ref[pl.ds(off, bk), :]
        z = _dot(q, kb, NT) * scale
        mask = (j * bk + lax.broadcasted_iota(I32, (bq, bk), 1)) < row
        sp = _softplus(z)
        log_keep = jnp.where(mask, -sp, 0.0)
        carry = car_ref[...]
        log_later = _dot_fx(log_keep, later, 2) + carry
        w = jnp.where(mask, jnp.exp((z - sp) + log_later), 0.0)
        acc_ref[...] += _dot(w.astype(BF16), vb)
        carry = carry + jnp.sum(log_keep, axis=-1, keepdims=True)
        car_ref[...] = carry
        done = jnp.max(carry) < SB_EXIT
        return j - 1, done.astype(I32)

    lax.while_loop(lambda c: jnp.logical_and(c[0] >= 0, c[1] == 0), body,
                   ((i + 1) * (bq // bk) - 1, jnp.int32(0)))
    o_ref[...] = acc_ref[...].astype(o_ref.dtype)


def _sb_attention(qkv, bq=256, bk=256):
    T = qkv.shape[0]
    H, Dh = SB_HEADS, SB_HEAD_DIM
    kern = functools.partial(_sb_kernel, bq=bq, bk=bk, scale=Dh ** -0.5)
    return pl.pallas_call(
        kern,
        grid=(H, T // bq),
        in_specs=[pl.BlockSpec((bq, Dh), lambda h, i: (i, h)),
                  pl.BlockSpec((T, Dh), lambda h, i: (0, H + h)),
                  pl.BlockSpec((T, Dh), lambda h, i: (0, 2 * H + h))],
        out_specs=pl.BlockSpec((bq, Dh), lambda h, i: (i, h)),
        out_shape=jax.ShapeDtypeStruct((T, H * Dh), BF16),
        scratch_shapes=[pltpu.VMEM((bq, Dh), F32), pltpu.VMEM((bq, 1), F32)],
        compiler_params=_cp("parallel", "parallel"),
        name="sb_attention",
    )(qkv, qkv, qkv)


_GLA_LEVELS = (32, 16, 8, 4, 2, 1)


def _gla_constants():
    C = GLA_CHUNK
    i = np.arange(C)[:, None]
    s = np.arange(C)[None, :]
    sel = []
    masks = []
    for m in _GLA_LEVELS:
        same = (i // m) == (s // m)
        sel.append(same & (s <= i) & ((i // m) % 2 == 1))
        sel.append(same & (s > i) & ((i // m) % 2 == 0))
        masks.append(((i // (2 * m)) == (s // (2 * m))) & ((i // m) % 2 == 1) & ((s // m) % 2 == 0))
    sel.append(s <= i)
    sel.append(s > i)
    masks.append(i == s)
    return (np.concatenate(sel, 0).astype(np.float32), np.stack(masks, 0).astype(np.float32))


def _gla_kernel(q_ref, k_ref, v_ref, go_ref, glr_ref, up_ref, bias_ref, ng_ref, sel_ref, msk_ref,
                o_ref, st_ref, *, n_chunks):
    C = GLA_CHUNK
    nl = len(_GLA_LEVELS)

    @pl.when(pl.program_id(1) == 0)
    def _():
        st_ref[...] = jnp.zeros_like(st_ref)

    sel = sel_ref[...]
    up = up_ref[...]
    bias = bias_ref[...]
    ng = ng_ref[...]
    rows = [slice(c * C, (c + 1) * C) for c in range(n_chunks)]
    blk = lambda f, j: f[j * C:(j + 1) * C]
    q = [q_ref[s, :] * (GLA_HEAD_K ** -0.5) for s in rows]
    k = [k_ref[s, :] for s in rows]
    v = [v_ref[s, :].astype(BF16) for s in rows]
    x = [_dot(glr_ref[s, :].astype(BF16), up) + bias for s in rows]
    g = [-_softplus(-t) * (1.0 / GLA_GATE_NORMALIZER) for t in x]
    f = [jnp.exp(_dot_xf(sel, t, 2)) for t in g]
    scores = [_dot(a.astype(BF16), b.astype(BF16), NT) * msk_ref[nl] for a, b in zip(q, k)]
    for l in range(nl):
        ql = [(a * blk(t, 2 * l)).astype(BF16) for a, t in zip(q, f)]
        kl = [(b * blk(t, 2 * l + 1)).astype(BF16) for b, t in zip(k, f)]
        scores = [s + _dot(a, b, NT) * msk_ref[l] for s, a, b in zip(scores, ql, kl)]
    o_intra = [_dot(s.astype(BF16), b) for s, b in zip(scores, v)]
    qe = [(a * blk(t, 2 * nl)).astype(BF16) for a, t in zip(q, f)]
    upd = [_dot(b, (a * blk(t, 2 * nl + 1)).astype(BF16), TN) for b, a, t in zip(v, k, f)]
    dec = [blk(t, 2 * nl)[C - 1:C, :] for t in f]

    st = st_ref[...]
    for c in range(n_chunks):
        o = _dot(qe[c], st.astype(BF16), NT) + o_intra[c]
        st = st * dec[c] + upd[c]
        rms = lax.rsqrt(jnp.mean(o * o, axis=-1, keepdims=True) + GLA_NORM_EPS)
        o_ref[rows[c], :] = (o * rms * ng * _silu(go_ref[rows[c], :])).astype(o_ref.dtype)
    st_ref[...] = st


def _gla(p_gla, gk_up, gk_bias, norm_g, tt=256):
    T = p_gla.shape[0]
    H, dk, dv = GLA_HEADS, GLA_HEAD_K, GLA_HEAD_V
    sel, msk = _gla_constants()
    up = jnp.zeros((LANES, GLA_KEY_DIM), F32).at[:GLA_GATE_RANK].set(gk_up).astype(BF16)
    kern = functools.partial(_gla_kernel, n_chunks=tt // GLA_CHUNK)
    return pl.pallas_call(
        kern,
        grid=(H, T // tt),
        in_specs=[pl.BlockSpec((tt, dk), lambda h, i: (i, h)),
                  pl.BlockSpec((tt, dk), lambda h, i: (i, H + h)),
                  pl.BlockSpec((tt, dv), lambda h, i: (i, H + h)),
                  pl.BlockSpec((tt, dv), lambda h, i: (i, 2 * H + h)),
                  pl.BlockSpec((tt, LANES), lambda h, i: (i, 3 * GLA_VALUE_DIM // LANES)),
                  pl.BlockSpec((LANES, dk), lambda h, i: (0, h)),
                  pl.BlockSpec((1, dk), lambda h, i: (0, h)),
                  pl.BlockSpec((1, dv), lambda h, i: (0, 0)),
                  pl.BlockSpec(sel.shape, lambda h, i: (0, 0)),
                  pl.BlockSpec(msk.shape, lambda h, i: (0, 0, 0))],
        out_specs=pl.BlockSpec((tt, dv), lambda h, i: (i, h)),
        out_shape=jax.ShapeDtypeStruct((T, H * dv), BF16),
        scratch_shapes=[pltpu.VMEM((dv, dk), F32)],
        compiler_params=_cp("parallel", "arbitrary"),
        name="gla",
    )(p_gla, p_gla, p_gla, p_gla, p_gla, up, gk_bias.reshape(1, -1), norm_g.reshape(1, -1),
      jnp.asarray(sel, BF16), jnp.asarray(msk, F32))


def _head_ones():
    l = np.arange(LANES)
    return (l[:, None] // RWKV_HEAD_DIM == l[None, :] // RWKV_HEAD_DIM).astype(np.float32)


def _head_diag():
    i = np.arange(RWKV_HEAD_DIM)[:, None]
    l = np.arange(LANES)[None, :]
    return (l % RWKV_HEAD_DIM == i).astype(np.float32)


def _rwprep_kernel(p_ref, pp_ref, mu_ref, w0_ref, w2_ref, a0_ref, a2_ref, g2_ref, kk_ref, ka_ref,
                   v0_ref, v1_ref, v2_ref, vf_ref, bo_ref,
                   r_ref, w_ref, k_ref, v_ref, a_ref, b_ref, g_ref, *, first_layer):
    W = RWKV_WIDTH
    i = pl.program_id(0)
    p = p_ref[...]
    tm = p.shape[0]
    prev_row = jnp.where(i == 0, 0.0, pp_ref[SUBLANES - 1:SUBLANES, :])
    rid = lax.broadcasted_iota(I32, p.shape, 0)
    prev = jnp.where(rid == 0, prev_row, pltpu.roll(p, 1, axis=0))
    ps = p + (prev - p) * mu_ref[...]
    r = ps[:, 0:W]
    kr = ps[:, W:2 * W]
    vr = ps[:, 2 * W:3 * W]
    lr = ps[:, 3 * W:3 * W + LANES]
    gl = ps[:, 3 * W + LANES:3 * W + 2 * LANES]
    w_log = -_softplus(-(w0_ref[...] + _dot(jnp.tanh(lr).astype(BF16), w2_ref[...]))) - 0.5
    w_ref[...] = -jnp.exp(w_log)
    if not first_layer:
        lo = _dot(vr.astype(BF16), v1_ref[...])
        gate = _sigmoid(v0_ref[...] + _dot(lo.astype(BF16), v2_ref[...]))
        vr = vr + (vf_ref[...] - vr) * gate
    v_ref[...] = vr
    a = _sigmoid(a0_ref[...] + _dot(lr.astype(BF16), a2_ref[...]))
    g_ref[...] = _dot(_sigmoid(gl).astype(BF16), g2_ref[...])
    kk = kr * kk_ref[...]
    bo = bo_ref[...]
    for hp in range(W // LANES):
        sl = slice(hp * LANES, (hp + 1) * LANES)
        kh = kk[:, sl]
        ss = _dot_fx(kh * kh, bo, 3)
        kn = kh / jnp.maximum(jnp.sqrt(ss), 1e-12)
        a_ref[:, sl] = -kn
        b_ref[:, sl] = kn * a[:, sl]
    r_ref[...] = r
    k_ref[...] = kr * (1.0 + (a - 1.0) * ka_ref[...])


def _rwkv_prep(p_rw, v_first, mu, w0, w2, a0, a2, g2, k_k, k_a, v0, v1, v2, tm=256):
    T, NC = p_rw.shape
    W = RWKV_WIDTH
    first = v_first is None
    w2p = jnp.zeros((LANES, W), F32).at[:RWKV_DECAY_RANK].set(w2).astype(BF16)
    a2p = jnp.zeros((LANES, W), F32).at[RWKV_DECAY_RANK:].set(a2).astype(BF16)
    if first:
        v_first = jnp.zeros((SUBLANES, W), F32)
        v0 = jnp.zeros((W,), F32)
        v1p = jnp.zeros((W, LANES), BF16)
        v2p = jnp.zeros((LANES, W), BF16)
        vf_spec = pl.BlockSpec((SUBLANES, W), lambda i: (0, 0))
    else:
        v1p = jnp.zeros((W, LANES), F32).at[:, :RWKV_VALUE_RANK].set(v1).astype(BF16)
        v2p = jnp.zeros((LANES, W), F32).at[:RWKV_VALUE_RANK].set(v2).astype(BF16)
        vf_spec = pl.BlockSpec((tm, W), lambda i: (i, 0))
    row = lambda a: a.reshape(1, -1)
    full = lambda shape: pl.BlockSpec(shape, lambda i: (0,) * len(shape))
    out = jax.ShapeDtypeStruct((T, W), F32)
    ospec = pl.BlockSpec((tm, W), lambda i: (i, 0))
    kern = functools.partial(_rwprep_kernel, first_layer=first)
    return pl.pallas_call(
        kern,
        grid=(T // tm,),
        in_specs=[pl.BlockSpec((tm, NC), lambda i: (i, 0)),
                  pl.BlockSpec((SUBLANES, NC), lambda i: (jnp.maximum(i * (tm // SUBLANES) - 1, 0), 0)),
                  full((1, NC)), full((1, W)), full((LANES, W)), full((1, W)), full((LANES, W)),
                  full((LANES, W)), full((1, W)), full((1, W)),
                  full((1, W)), full((W, LANES)), full((LANES, W)), vf_spec, full((LANES, LANES))],
        out_specs=[ospec] * 7,
        out_shape=[out] * 7,
        compiler_params=_cp("parallel"),
        name="rwkv_prep",
    )(p_rw, p_rw, row(mu), row(w0), w2p, row(a0), a2p, g2.astype(BF16), row(k_k), row(k_a),
      row(v0), v1p, v2p, v_first, jnp.asarray(_head_ones(), BF16))


RW_CHUNK = 64


def _rw_constants():
    C = RW_CHUNK
    i = np.arange(C)[:, None]
    s = np.arange(C)[None, :]
    i2 = np.arange(2 * C)[:, None]
    s2 = np.arange(2 * C)[None, :]
    same = (i2 // C) == (s2 // C)
    cm = np.stack([same & (s2 < i2), same & (s2 <= i2), (i2 // 16) == (s2 // 16), (i2 // 32) == (s2 // 32),
                   same], 0).astype(np.float32)
    return (s <= i).astype(np.float32), cm


def _mm(a, b, dims=NN):
    a1, a2 = _split(a, 2)
    b1, b2 = _split(b, 2)
    return _dot(a1, b1, dims) + (_dot(a1, b2, dims) + _dot(a2, b1, dims))


def _mm1(a, b, dims=NN):
    return _dot(a.astype(BF16), b.astype(BF16), dims)


def _unit_lower_inverse(nms, eye, bd16, bd32):
    nd = [n * bd16 for n in nms]
    x = [eye + n for n in nd]
    p = [_mm(n, n) for n in nd]
    for step in range(3):
        x = [a + _mm(a, q) for a, q in zip(x, p)]
        if step < 2:
            p = [_mm(q, q) for q in p]
    for lo_mask in (bd32 - bd16, 1.0 - bd32):
        t = [_mm(n * lo_mask, a) for n, a in zip(nms, x)]
        x = [a + _mm(a, q) for a, q in zip(x, t)]
    return x


def _rwrec_kernel(r_ref, lw_ref, k_ref, v_ref, a_ref, b_ref, lt_ref, cm_ref, y_ref, mt_ref, *, n_chunks):
    C = RW_CHUNK
    C2 = 2 * C

    @pl.when(pl.program_id(1) == 0)
    def _():
        mt_ref[...] = jnp.zeros_like(mt_ref)

    lt = lt_ref[...]
    sl, li, bd16, bd32, bdh = cm_ref[0], cm_ref[1], cm_ref[2], cm_ref[3], cm_ref[4]
    eye = li - sl
    lane = lax.broadcasted_iota(I32, (1, LANES), 1)
    m0 = jnp.where(lane < RWKV_HEAD_DIM, 1.0, 0.0)
    m1 = 1.0 - m0
    stack = lambda x: jnp.concatenate([x * m0, x * m1], axis=0)
    fold = lambda x: x[:C] + x[C:]
    rows = [slice(c * C, (c + 1) * C) for c in range(n_chunks)]

    lw = [lw_ref[s, :] for s in rows]
    cw = [_dot_xf(lt, x, 3) for x in lw]
    en = [jnp.exp(-x) for x in cw]
    at = [a_ref[s, :] * jnp.exp(c - l) for s, c, l in zip(rows, cw, lw)]
    rt = [r_ref[s, :] * jnp.exp(c) for s, c in zip(rows, cw)]
    bh = [b_ref[s, :] * e for s, e in zip(rows, en)]
    kh = [k_ref[s, :] * e for s, e in zip(rows, en)]
    v = [v_ref[s, :] for s in rows]
    a2 = [stack(x) for x in at]
    ar2 = [jnp.concatenate([x, stack(y)], axis=0) for x, y in zip(a2, rt)]
    gb = [_mm(x, stack(y), NT) for x, y in zip(ar2, bh)]
    gk = [_mm(x, stack(y), NT) for x, y in zip(ar2, kh)]
    tinv = _unit_lower_inverse([g[:C2] * sl for g in gb], eye, bd16, bd32)
    akv = [_mm(jnp.concatenate([g[:C2] * sl, g[C2:] * li], axis=0), stack(x)) for g, x in zip(gk, v)]
    tav = [_mm(t, jnp.concatenate([x, y[:C2]], axis=1)) for t, x, y in zip(tinv, a2, akv)]
    rav = [_mm1(g[C2:] * li, x) for g, x in zip(gb, tav)]
    ta = [fold(x[:, :LANES]) for x in tav]
    u0 = [fold(x[:, LANES:]) for x in tav]
    qe = [x + fold(y[:, :LANES]) for x, y in zip(rt, rav)]
    y0 = [fold(x[:, LANES:] + y[C2:]) for x, y in zip(rav, akv)]
    dl = [jnp.exp(x[C - 1:C, :]) for x in cw]
    gmt = [(eye + _mm(x, y, TN) * bdh) * d for x, y, d in zip(ta, bh, dl)]
    hmt = [(_mm(x, y, TN) + _mm(z, w, TN)) * bdh * d for x, y, z, w, d in zip(u0, bh, v, kh, dl)]

    mt = mt_ref[...]
    for c in range(n_chunks):
        y_ref[rows[c], :] = _mm1(qe[c], mt, NT) + y0[c]
        mt = _mm(mt, gmt[c]) + hmt[c]
    mt_ref[...] = mt


def _rwkv_recurrence(r, lw, k, v, a, b, tt=256):
    T, W = r.shape
    lt, cm = _rw_constants()
    spec = pl.BlockSpec((tt, LANES), lambda p, i: (i, p))
    kern = functools.partial(_rwrec_kernel, n_chunks=tt // RW_CHUNK)
    return pl.pallas_call(
        kern,
        grid=(W // LANES, T // tt),
        in_specs=[spec] * 6 + [pl.BlockSpec(lt.shape, lambda p, i: (0, 0)),
                               pl.BlockSpec(cm.shape, lambda p, i: (0, 0, 0))],
        out_specs=spec,
        out_shape=jax.ShapeDtypeStruct((T, W), F32),
        scratch_shapes=[pltpu.VMEM((LANES, LANES), F32)],
        compiler_params=_cp("parallel", "arbitrary"),
        name="rwkv_recurrence",
    )(r, lw, k, v, a, b, jnp.asarray(lt, BF16), jnp.asarray(cm, F32))


def _rwpost_kernel(y_ref, r_ref, k_ref, v_ref, g_ref, lg_ref, lb_ref, rk_ref, bo_ref, o_ref):
    bo = bo_ref[...]
    inv = 1.0 / RWKV_HEAD_DIM
    for hp in range(RWKV_WIDTH // LANES):
        sl = slice(hp * LANES, (hp + 1) * LANES)
        y = y_ref[:, sl]
        mu = _dot_fx(y, bo, 3) * inv
        d = y - mu
        var = _dot_fx(d * d, bo, 3) * inv
        yn = d * lax.rsqrt(var + RWKV_LNX_EPS) * lg_ref[:, sl] + lb_ref[:, sl]
        bonus = _dot_fx(r_ref[:, sl] * k_ref[:, sl] * rk_ref[:, sl], bo, 3) * v_ref[:, sl]
        o_ref[:, sl] = ((yn + bonus) * g_ref[:, sl]).astype(o_ref.dtype)


def _rwkv_post(y, r, k, v, g, lnx_g, lnx_b, r_k, tm=512):
    T, W = y.shape
    spec = pl.BlockSpec((tm, W), lambda i: (i, 0))
    rowspec = pl.BlockSpec((1, W), lambda i: (0, 0))
    return pl.pallas_call(
        _rwpost_kernel,
        grid=(T // tm,),
        in_specs=[spec] * 5 + [rowspec] * 3 + [pl.BlockSpec((LANES, LANES), lambda i: (0, 0))],
        out_specs=spec,
        out_shape=jax.ShapeDtypeStruct((T, W), BF16),
        compiler_params=_cp("parallel"),
        name="rwkv_post",
    )(y, r, k, v, g, lnx_g.reshape(1, W), lnx_b.reshape(1, W), r_k.reshape(1, W),
      jnp.asarray(_head_ones(), BF16))


def _merge_kernel(osb_ref, ogl_ref, orw_ref, g1_ref, g2_ref, g3_ref, w1_ref, w2_ref, w3_ref, o_ref):
    acc = _sigmoid(g1_ref[...].astype(F32)) * _dot(osb_ref[...], w1_ref[...])
    acc = acc + _sigmoid(g2_ref[...].astype(F32)) * _dot(ogl_ref[...], w2_ref[...])
    acc = acc + _sigmoid(g3_ref[...].astype(F32)) * _dot(orw_ref[...], w3_ref[...])
    o_ref[...] = acc.astype(o_ref.dtype)


def _merge(o_sb, o_gla, o_rw, p_merge, w_sb, w_gla, w_rw, tm=512, tn=1024):
    T = o_sb.shape[0]
    D = D_MODEL
    nb = D // tn
    aspec = pl.BlockSpec((tm, 1024), lambda j, i: (i, 0))
    wspec = pl.BlockSpec((1024, tn), lambda j, i: (0, j))
    gspec = lambda b: pl.BlockSpec((tm, tn), lambda j, i: (i, b * nb + j))
    return pl.pallas_call(
        _merge_kernel,
        grid=(nb, T // tm),
        in_specs=[aspec, aspec, aspec, gspec(0), gspec(1), gspec(2), wspec, wspec, wspec],
        out_specs=pl.BlockSpec((tm, tn), lambda j, i: (i, j)),
        out_shape=jax.ShapeDtypeStruct((T, D), BF16),
        compiler_params=_cp("parallel", "parallel"),
        name="merge",
    )(o_sb, o_gla, o_rw, p_merge, p_merge, p_merge, w_sb, w_gla, w_rw)


def _layer_norm(h, g, b):
    mu = jnp.mean(h, axis=-1, keepdims=True)
    d = h - mu
    var = jnp.mean(d * d, axis=-1, keepdims=True)
    return d * lax.rsqrt(var + LN_EPS) * g + b


def _oproj_kernel(m_ref, w_ref, x_ref, gate_ref, lg_ref, lb_ref, o_ref):
    mix = _dot(m_ref[...], w_ref[...])
    h = DEEPNORM_ALPHA * x_ref[...] + (1.0 + gate_ref[...]) * mix
    o_ref[...] = _layer_norm(h, lg_ref[...], lb_ref[...])


def _oproj(merged, w_o, x, gate, ln_g, ln_b, tm=512):
    T, D = x.shape
    spec = pl.BlockSpec((tm, D), lambda i: (i, 0))
    rowspec = pl.BlockSpec((1, D), lambda i: (0, 0))
    return pl.pallas_call(
        _oproj_kernel,
        grid=(T // tm,),
        in_specs=[spec, pl.BlockSpec((D, D), lambda i: (0, 0)), spec, rowspec, rowspec, rowspec],
        out_specs=spec,
        out_shape=jax.ShapeDtypeStruct((T, D), F32),
        compiler_params=_cp("parallel"),
        name="oproj_norm",
    )(merged, w_o, x, gate, ln_g.reshape(1, D), ln_b.reshape(1, D))


def _first_max(x, idx, axis, big):
    m = jnp.max(x, axis=axis, keepdims=True)
    first = jnp.min(jnp.where(x == m, idx, big), axis=axis, keepdims=True)
    return m, idx == first


def _router_kernel(x_ref, sc_ref, sh_ref, wr_ref, rb_ref, u_ref, e_ref, g_ref, p_ref, cnt_ref, car_ref, *, tm):
    E, G = N_EXPERTS, N_GROUPS
    per = E // G
    neg = -jnp.inf

    @pl.when(pl.program_id(0) == 0)
    def _():
        car_ref[...] = jnp.zeros_like(car_ref)

    u = x_ref[...] * (1.0 + sc_ref[...]) + sh_ref[...]
    u_ref[...] = u
    scores =_sigmoid(_dot_ff(wr_ref[...], u, NT))
    biased = scores + rb_ref[...]
    b3 = biased.reshape(G, per, tm)
    i3 = lax.broadcasted_iota(I32, (G, per, tm), 1)
    m1, hit = _first_max(b3, i3, 1, per)
    m2 = jnp.max(jnp.where(hit, neg, b3), axis=1, keepdims=True)
    gs = (m1 + m2).reshape(G, tm)
    gi = lax.broadcasted_iota(I32, (G, tm), 0)
    ok = jnp.zeros((G, tm), jnp.bool_)
    for _ in range(TOPK_GROUPS):
        _, hit = _first_max(gs, gi, 0, G)
        ok = jnp.logical_or(ok, hit)
        gs = jnp.where(hit, neg, gs)
    ok_e = jnp.broadcast_to(ok.reshape(G, 1, tm), (G, per, tm)).reshape(E, tm)
    cand = jnp.where(ok_e, biased, neg)
    ei = lax.broadcasted_iota(I32, (E, tm), 0)
    sels, gates = [], []
    for _ in range(TOP_K):
        _, hit = _first_max(cand, ei, 0, E)
        sels.append(hit)
        gates.append(jnp.sum(jnp.where(hit, scores, 0.0), axis=0, keepdims=True))
        cand = jnp.where(hit, neg, cand)
    denom = gates[0]
    for gk in gates[1:]:
        denom = denom + gk
    chosen = sels[0]
    for s in sels[1:]:
        chosen = jnp.logical_or(chosen, s)
    cnt = jnp.where(chosen, 1.0, 0.0)
    earlier = (lax.broadcasted_iota(I32, (tm, tm), 0) < lax.broadcasted_iota(I32, (tm, tm), 1)).astype(BF16)
    rank = _dot(cnt.astype(BF16), earlier) + car_ref[...][:, 0:1]
    car_ref[...] = car_ref[...] + jnp.sum(cnt, axis=1, keepdims=True)
    cnt_ref[...] = car_ref[...]
    zi = jnp.zeros((1, tm), I32)
    zf = jnp.zeros((1, tm), F32)
    e_rows = [jnp.sum(jnp.where(s, ei, 0), axis=0, keepdims=True) for s in sels]
    p_rows = [jnp.sum(jnp.where(s, rank, 0.0), axis=0, keepdims=True).astype(I32) for s in sels]
    g_rows = [gk / denom * ROUTED_SCALE for gk in gates]
    e_ref[...] = jnp.concatenate(e_rows + [zi, zi], axis=0)
    p_ref[...] = jnp.concatenate(p_rows + [zi, zi], axis=0)
    g_ref[...] = jnp.concatenate(g_rows + [zf, zf], axis=0)


def _router(x, sc, sh, w_router, router_bias, tm=256):
    T, D = x.shape
    E = N_EXPERTS
    kern = functools.partial(_router_kernel, tm=tm)
    rowspec = pl.BlockSpec((1, D), lambda i: (0, 0))
    kspec = pl.BlockSpec((SUBLANES, tm), lambda i: (0, i))
    return pl.pallas_call(
        kern,
        grid=(T // tm,),
        in_specs=[pl.BlockSpec((tm, D), lambda i: (i, 0)), rowspec, rowspec,
                  pl.BlockSpec((E, D), lambda i: (0, 0)), pl.BlockSpec((E, 1), lambda i: (0, 0))],
        out_specs=[pl.BlockSpec((tm, D), lambda i: (i, 0)), kspec, kspec, kspec,
                   pl.BlockSpec((E, LANES), lambda i: (0, 0))],
        out_shape=[jax.ShapeDtypeStruct((T, D), F32),
                   jax.ShapeDtypeStruct((SUBLANES, T), I32),
                   jax.ShapeDtypeStruct((SUBLANES, T), F32),
                   jax.ShapeDtypeStruct((SUBLANES, T), I32),
                   jax.ShapeDtypeStruct((E, LANES), F32)],
        scratch_shapes=[pltpu.VMEM((E, LANES), F32)],
        compiler_params=_cp("arbitrary"),
        name="router",
    )(x, sc, sh, w_router.T, router_bias.reshape(E, 1))


def _dispatch_kernel(dest_ref, fill_ref, u_ref, xs_ref, z_ref, sem, zsem, *, tm, bm):
    i = pl.program_id(0)

    @pl.when(i == 0)
    def _():
        z_ref[...] = jnp.zeros_like(z_ref)

        def zcopy(e):
            return pltpu.make_async_copy(z_ref, xs_ref.at[pl.ds(pl.multiple_of(fill_ref[e], bm), bm)], zsem)

        def zstart(e, c):
            zcopy(e).start()
            return c

        def zwait(e, c):
            zcopy(e).wait()
            return c

        lax.fori_loop(0, N_EXPERTS, zstart, 0)
        lax.fori_loop(0, N_EXPERTS, zwait, 0)

    def copy(t, k):
        return pltpu.make_async_copy(_row(u_ref, t), _row(xs_ref, dest_ref[k, t]), sem)

    def start(t, c):
        for k in range(TOP_K):
            copy(t, k).start()
        return c

    def wait(t, c):
        for k in range(TOP_K):
            copy(t, k).wait()
        return c

    lax.fori_loop(0, tm, start, 0)
    lax.fori_loop(0, tm, wait, 0)


def _dispatch(u, dest, fill, n_slots, tm=256, bm=MOE_BM):
    T, D = u.shape
    kern = functools.partial(_dispatch_kernel, tm=tm, bm=bm)
    return pl.pallas_call(
        kern,
        grid=(T // tm,),
        in_specs=[pl.BlockSpec((SUBLANES, tm), lambda i: (0, i), memory_space=pltpu.SMEM),
                  pl.BlockSpec(memory_space=pltpu.SMEM),
                  pl.BlockSpec((tm, D), lambda i: (i, 0))],
        out_specs=pl.BlockSpec(memory_space=pl.ANY),
        out_shape=jax.ShapeDtypeStruct((n_slots, D), F32),
        scratch_shapes=[pltpu.VMEM((bm, D), F32),
                        pltpu.SemaphoreType.DMA(()), pltpu.SemaphoreType.DMA(())],
        compiler_params=_cp("arbitrary"),
        name="moe_dispatch",
    )(dest, fill, u)


def _expert_kernel(be_ref, nv_ref, x_ref, wg_ref, wu_ref, wd_ref, y_ref, wgb_ref, wub_ref, wdb_ref, *, bm):
    b = pl.program_id(0)
    valid = b < nv_ref[0]
    new_expert = jnp.logical_or(b == 0, be_ref[b] != be_ref[jnp.maximum(b - 1, 0)])

    @pl.when(jnp.logical_and(valid, new_expert))
    def _():
        wgb_ref[...] = wg_ref[0].astype(BF16)
        wub_ref[...] = wu_ref[0].astype(BF16)
        wdb_ref[...] = wd_ref[0].astype(BF16)

    @pl.when(valid)
    def _():
        x = x_ref[...].astype(BF16)
        h = _silu(_dot(x, wgb_ref[...])) * _dot(x, wub_ref[...])
        y_ref[...] = _dot(h.astype(BF16), wdb_ref[...])


def _experts(xs, block_expert, n_valid, w_gate, w_up, w_down, layer, bm=MOE_BM):
    n_slots, D = xs.shape
    nb = n_slots // bm
    F = w_gate.shape[-1]
    blk = lambda b, be, nv: (jnp.minimum(b, nv[0] - 1), 0)
    wsel = lambda b, be, nv: (layer, be[jnp.minimum(b, nv[0] - 1)], 0, 0)
    gs = pltpu.PrefetchScalarGridSpec(
        num_scalar_prefetch=2,
        grid=(nb,),
        in_specs=[pl.BlockSpec((bm, D), blk),
                  pl.BlockSpec((None, 1, D, F), wsel),
                  pl.BlockSpec((None, 1, D, F), wsel),
                  pl.BlockSpec((None, 1, F, D), wsel)],
        out_specs=pl.BlockSpec((bm, D), blk),
        scratch_shapes=[pltpu.VMEM((D, F), BF16), pltpu.VMEM((D, F), BF16), pltpu.VMEM((F, D), BF16)],
    )
    return pl.pallas_call(
        functools.partial(_expert_kernel, bm=bm),
        grid_spec=gs,
        out_shape=jax.ShapeDtypeStruct((n_slots, D), F32),
        compiler_params=_cp("arbitrary"),
        name="moe_experts",
    )(block_expert, n_valid, xs, w_gate, w_up, w_down)


def _combine_kernel(dest_ref, ys_ref, u_ref, x_ref, gt_ref, wg_ref, wu_ref, wd_ref, gate_ref, lg_ref, lb_ref,
                    o_ref, buf_ref, sem, *, tm):
    def copy(t, k):
        return pltpu.make_async_copy(_row(ys_ref, dest_ref[k, t]), _row(buf_ref.at[k], t), sem)

    def start(t, c):
        for k in range(TOP_K):
            copy(t, k).start()
        return c

    def wait(t, c):
        for k in range(TOP_K):
            copy(t, k).wait()
        return c

    lax.fori_loop(0, tm, start, 0)
    u = u_ref[...].astype(BF16)
    h = _silu(_dot(u, wg_ref[...])) * _dot(u, wu_ref[...])
    ffn = _dot(h.astype(BF16), wd_ref[...])
    lax.fori_loop(0, tm, wait, 0)
    gt = gt_ref[...]
    for k in range(TOP_K):
        ffn = ffn + gt[:, k:k + 1] * buf_ref[k]
    hres = DEEPNORM_ALPHA * x_ref[...] + (1.0 + gate_ref[...]) * ffn
    o_ref[...] = _layer_norm(hres, lg_ref[...], lb_ref[...])


def _combine(ys, dest, u, x, gates_t, w_sg, w_su, w_sd, gate, ln_g, ln_b, tm=128):
    T, D = x.shape
    F = w_sg.shape[-1]
    kern = functools.partial(_combine_kernel, tm=tm)
    spec = pl.BlockSpec((tm, D), lambda i: (i, 0))
    rowspec = pl.BlockSpec((1, D), lambda i: (0, 0))
    return pl.pallas_call(
        kern,
        grid=(T // tm,),
        in_specs=[pl.BlockSpec((SUBLANES, tm), lambda i: (0, i), memory_space=pltpu.SMEM),
                  pl.BlockSpec(memory_space=pl.ANY),
                  spec, spec,
                  pl.BlockSpec((tm, SUBLANES), lambda i: (i, 0)),
                  pl.BlockSpec((D, F), lambda i: (0, 0)),
                  pl.BlockSpec((D, F), lambda i: (0, 0)),
                  pl.BlockSpec((F, D), lambda i: (0, 0)),
                  rowspec, rowspec, rowspec],
        out_specs=spec,
        out_shape=jax.ShapeDtypeStruct((T, D), F32),
        scratch_shapes=[pltpu.VMEM((TOP_K, tm, D), F32), pltpu.SemaphoreType.DMA(())],
        compiler_params=_cp("arbitrary"),
        name="moe_combine_norm",
    )(dest, ys, u, x, gates_t, w_sg, w_su, w_sd, gate, ln_g.reshape(1, D), ln_b.reshape(1, D))


def _moe_plan(top_e, rank, counts, bm, n_blocks):
    counts = counts.astype(I32)
    padded = (counts + bm - 1) // bm * bm
    padded_end = jnp.cumsum(padded)
    padded_start = padded_end - padded
    onehot = top_e[:, :, None] == jnp.arange(N_EXPERTS, dtype=I32)
    dest = jnp.sum(jnp.where(onehot, padded_start, 0), axis=-1) + rank
    fill = jnp.maximum(padded_end - bm, padded_start)
    fill = jnp.minimum(fill, (n_blocks - 1) * bm)
    block_start = jnp.arange(n_blocks, dtype=I32) * bm
    block_expert = jnp.minimum(jnp.sum((padded_end[None, :] <= block_start[:, None]).astype(I32), axis=1),
                               N_EXPERTS - 1)
    n_valid = jnp.maximum(padded_end[-1:] // bm, 1).astype(I32)
    return dest.astype(I32), fill.astype(I32), block_expert, n_valid


def _in_proj_weights(w_in):
    o = SB_COLS
    gq = w_in[:, o:o + 2 * GLA_KEY_DIM + GLA_VALUE_DIM]
    o += 2 * GLA_KEY_DIM + GLA_VALUE_DIM
    glr = w_in[:, o:o + GLA_GATE_RANK]
    o += GLA_GATE_RANK
    gout = w_in[:, o:o + GLA_VALUE_DIM]
    o += GLA_VALUE_DIM
    pad = jnp.zeros((w_in.shape[0], LANES - GLA_GATE_RANK), w_in.dtype)
    w_gla = jnp.concatenate([gq, gout, glr, pad], axis=1)
    w_rw = w_in[:, o:o + RWKV_COLS]
    o += RWKV_COLS
    w_mg = w_in[:, o:o + MERGE_COLS]
    return w_in[:, :SB_COLS], w_gla, w_rw, w_mg


def kernel(x, c, w_ada, b_ada, w_in, gla_gk_up, gla_gk_bias, gla_norm_g, rw_mu, rw_w0, rw_w2, rw_a0,
           rw_a2, rw_g2, rw_k_k, rw_k_a, rw_r_k, rw_lnx_g, rw_lnx_b, rw_v0, rw_v1, rw_v2,
           w_br_sb, w_br_gla, w_br_rw, w_o, ln_g, ln_b, w_router, router_bias,
           w_exp_gate, w_exp_up, w_exp_down, w_sh_gate, w_sh_up, w_sh_down):
    B, T, D = x.shape
    assert B == 1 and D == D_MODEL
    n_blocks = -(-(T * TOP_K) // MOE_BM) + N_EXPERTS
    mod = _ada(c, w_ada, b_ada)
    xs = x.reshape(T, D)
    v_first = None
    for l in range(DEPTH):
        sh1, sc1, g1, sh2, sc2, g2 = (mod[l, j * D:(j + 1) * D].reshape(1, D) for j in range(6))
        w_sb, w_gla, w_rw, w_mg = _in_proj_weights(w_in[l])
        u1 = _modulate(xs, sc1, sh1)
        p_sb = _wmm(u1, w_sb, BF16, 1024, name="inproj_sb")
        p_gla = _wmm(u1, w_gla, F32, 640, name="inproj_gla")
        p_rw = _wmm(u1, w_rw, F32, 832 * 2, name="inproj_rwkv")
        p_mg = _wmm(u1, w_mg, BF16, 1024, name="inproj_merge")

        o_sb = _sb_attention(p_sb)
        o_gla = _gla(p_gla, gla_gk_up[l], gla_gk_bias[l], gla_norm_g[l])
        if l == 0:
            r, w, k, v, a, b, g = _rwkv_prep(p_rw, None, rw_mu[l], rw_w0[l], rw_w2[l], rw_a0[l], rw_a2[l],
                                             rw_g2[l], rw_k_k[l], rw_k_a[l], None, None, None)
            v_first = v
        else:
            r, w, k, v, a, b, g = _rwkv_prep(p_rw, v_first, rw_mu[l], rw_w0[l], rw_w2[l], rw_a0[l], rw_a2[l],
                                             rw_g2[l], rw_k_k[l], rw_k_a[l], rw_v0[l - 1], rw_v1[l - 1],
                                             rw_v2[l - 1])
        y = _rwkv_recurrence(r, w, k, v, a, b)
        o_rw = _rwkv_post(y, r, k, v, g, rw_lnx_g[l], rw_lnx_b[l], rw_r_k[l])

        merged = _merge(o_sb, o_gla, o_rw, p_mg, w_br_sb[l].astype(BF16), w_br_gla[l].astype(BF16),
                        w_br_rw[l].astype(BF16))
        xs = _oproj(merged, w_o[l].astype(BF16), xs, g1, ln_g[l, 0], ln_b[l, 0])

        u2, top_e, gates, rank, counts = _router(xs, sc2, sh2, w_router[l], router_bias[l])
        dest, fill, block_expert, n_valid = _moe_plan(top_e, rank, counts[:, 0], MOE_BM, n_blocks)
        x_sorted = _dispatch(u2, dest, fill, n_blocks * MOE_BM)
        y_sorted = _experts(x_sorted, block_expert, n_valid, w_exp_gate, w_exp_up, w_exp_down, l)
        xs = _combine(y_sorted, dest, u2, xs, gates.T, w_sh_gate[l].astype(BF16), w_sh_up[l].astype(BF16),
                      w_sh_down[l].astype(BF16), g2, ln_g[l, 1], ln_b[l, 1])
    return xs.reshape(B, T, D)
```

```python
import functools

import numpy as np
import jax
import jax.numpy as jnp
from jax import lax
from jax.experimental import pallas as pl
from jax.experimental.pallas import tpu as pltpu

F32 = jnp.float32
BF16 = jnp.bfloat16
I32 = jnp.int32
U32 = jnp.uint32

D_MODEL = 2048
DEPTH = 2
LN_EPS = 1e-5
DEEPNORM_ALPHA = (2 * DEPTH) ** 0.25

SB_HEADS = 8
SB_HEAD_DIM = 128
SB_WIDTH = SB_HEADS * SB_HEAD_DIM

GLA_HEADS = 4
GLA_HEAD_K = 128
GLA_HEAD_V = 256
GLA_KEY_DIM = GLA_HEADS * GLA_HEAD_K
GLA_VALUE_DIM = GLA_HEADS * GLA_HEAD_V
GLA_GATE_RANK = 16
GLA_GATE_NORMALIZER = 16.0
GLA_CHUNK = 64
GLA_NORM_EPS = 1e-5

RWKV_HEADS = 16
RWKV_HEAD_DIM = 64
RWKV_WIDTH = RWKV_HEADS * RWKV_HEAD_DIM
RWKV_DECAY_RANK = 64
RWKV_AAA_RANK = 64
RWKV_VALUE_RANK = 32
RWKV_GATE_RANK = 128
RWKV_LNX_EPS = 64e-5

N_EXPERTS = 64
TOP_K = 6
EXPERT_DIM = 512
SHARED_DIM = 512
N_GROUPS = 8
TOPK_GROUPS = 4
ROUTED_SCALE = 2.5

SB_COLS = 3 * SB_WIDTH
GLA_COLS = 2 * GLA_KEY_DIM + GLA_VALUE_DIM + GLA_GATE_RANK + GLA_VALUE_DIM
RWKV_COLS = 3 * RWKV_WIDTH + RWKV_DECAY_RANK + RWKV_AAA_RANK + RWKV_GATE_RANK
MERGE_COLS = 3 * D_MODEL

LANES = 128
SUBLANES = 8
VMEM_LIMIT = 56 * 1024 * 1024
MOE_BM = 512
SB_EXIT = -104.0
SB_GROUPS = 2

NN = (((1,), (0,)), ((), ()))
NT = (((1,), (1,)), ((), ()))
TN = (((0,), (0,)), ((), ()))


def _cp(*sem):
    return pltpu.CompilerParams(dimension_semantics=sem, vmem_limit_bytes=VMEM_LIMIT)


def _dot(a, b, dims=NN):
    return lax.dot_general(a, b, dims, preferred_element_type=F32)


def _split(a, n):
    out = []
    r = a
    for i in range(n):
        p = r.astype(BF16)
        out.append(p)
        if i + 1 < n:
            r = r - p.astype(F32)
    return out


def _dot_fx(a, e, n, dims=NN):
    acc = None
    for p in _split(a, n):
        t = _dot(p, e, dims)
        acc = t if acc is None else acc + t
    return acc


def _dot_xf(e, b, n, dims=NN):
    acc = None
    for p in _split(b, n):
        t = _dot(e, p, dims)
        acc = t if acc is None else acc + t
    return acc


def _dot_ff(a, b, dims=NN):
    a1, a2, a3 = _split(a, 3)
    b1, b2, b3 = _split(b, 3)
    acc = _dot(a1, b1, dims)
    for p, q in ((a1, b2), (a2, b1), (a2, b2), (a1, b3), (a3, b1)):
        acc = acc + _dot(p, q, dims)
    return acc


def _sigmoid(x):
    return 1.0 / (1.0 + jnp.exp(-x))


def _softplus(x):
    return jnp.maximum(x, 0.0) + jnp.log1p(jnp.exp(-jnp.abs(x)))


def _silu(x):
    return x * _sigmoid(x)


def _row(ref, t):
    return ref.at[pl.ds(t, 1)]


def _pack_pairs(x):
    w = x.shape[1] // 2
    bits = lax.bitcast_convert_type(x.astype(BF16).astype(F32), U32)
    return (bits[:, :w] >> 16) | bits[:, w:]


def _unpack_pairs(p):
    lo = lax.bitcast_convert_type(p << 16, F32)
    hi = lax.bitcast_convert_type(p & jnp.uint32(0xFFFF0000), F32)
    return jnp.concatenate([lo, hi], axis=1)


def _ada_kernel(c_ref, w_ref, b_ref, o_ref):
    c = c_ref[...]
    o_ref[0] = _dot_ff(_silu(c), w_ref[0]) + b_ref[0]


def _ada(c, w_ada, b_ada):
    L, D, N = w_ada.shape
    tn = 512
    c8 = jnp.broadcast_to(c, (SUBLANES, D))
    out = pl.pallas_call(
        _ada_kernel,
        grid=(L, N // tn),
        in_specs=[pl.BlockSpec((SUBLANES, D), lambda l, j: (0, 0)),
                  pl.BlockSpec((1, D, tn), lambda l, j: (l, 0, j)),
                  pl.BlockSpec((1, 1, tn), lambda l, j: (l, 0, j))],
        out_specs=pl.BlockSpec((1, SUBLANES, tn), lambda l, j: (l, 0, j)),
        out_shape=jax.ShapeDtypeStruct((L, SUBLANES, N), F32),
        compiler_params=_cp("parallel", "parallel"),
        name="ada",
    )(c8, w_ada, b_ada.reshape(L, 1, N))
    return out[:, 0, :]


def _modulate_kernel(x_ref, sc_ref, sh_ref, o_ref):
    o_ref[...] = (x_ref[...] * (1.0 + sc_ref[...]) + sh_ref[...]).astype(o_ref.dtype)


def _modulate(x, sc, sh, tm=1024):
    T, D = x.shape
    spec = pl.BlockSpec((tm, D), lambda i: (i, 0))
    rowspec = pl.BlockSpec((1, D), lambda i: (0, 0))
    return pl.pallas_call(
        _modulate_kernel,
        grid=(T // tm,),
        in_specs=[spec, rowspec, rowspec],
        out_specs=spec,
        out_shape=jax.ShapeDtypeStruct((T, D), BF16),
        compiler_params=_cp("parallel"),
        name="modulate",
    )(x, sc, sh)


def _wmm_kernel(u_ref, w_ref, o_ref, wb_ref):
    @pl.when(pl.program_id(1) == 0)
    def _():
        wb_ref[...] = w_ref[...].astype(BF16)

    o_ref[...] = _dot(u_ref[...], wb_ref[...]).astype(o_ref.dtype)


def _wmm(u, w, out_dtype, tn, tm=512, name="wmm"):
    T, D = u.shape
    N = w.shape[1]
    return pl.pallas_call(
        _wmm_kernel,
        grid=(N // tn, T // tm),
        in_specs=[pl.BlockSpec((tm, D), lambda j, i: (i, 0)),
                  pl.BlockSpec((D, tn), lambda j, i: (0, j))],
        out_specs=pl.BlockSpec((tm, tn), lambda j, i: (i, j)),
        out_shape=jax.ShapeDtypeStruct((T, N), out_dtype),
        scratch_shapes=[pltpu.VMEM((D, tn), BF16)],
        compiler_params=_cp("parallel", "arbitrary"),
        name=name,
    )(u, w)


def _sb_kernel(q_ref, k_ref, v_ref, o_ref, acc_ref, car_ref, *, bq, scale):
    i = pl.program_id(1)
    q = q_ref[...]
    r_id = lax.broadcasted_iota(I32, (bq, bq), 0)
    c_id = lax.broadcasted_iota(I32, (bq, bq), 1)
    later = (r_id > c_id).astype(BF16)
    causal = c_id < r_id

    def block(j, carry, diagonal):
        off = pl.multiple_of(j * bq, bq)
        kb = k_ref[pl.ds(off, bq), :]
        vb = v_ref[pl.ds(off, bq), :]
        grp = [slice(g * (bq // SB_GROUPS), (g + 1) * (bq // SB_GROUPS)) for g in range(SB_GROUPS)]
        z = [_dot(q[s], kb, NT) * scale for s in grp]
        drop = [jnp.maximum(t, 0.0) + jnp.log(1.0 + jnp.exp(-jnp.abs(t))) for t in z]
        if diagonal:
            drop = [jnp.where(causal[s], d, 0.0) for s, d in zip(grp, drop)]
        after = [_dot_fx(d, later, 2) + carry[s] for s, d in zip(grp, drop)]
        w = [jnp.exp(t - d - a) for t, d, a in zip(z, drop, after)]
        if diagonal:
            w = [jnp.where(causal[s], x, 0.0) for s, x in zip(grp, w)]
        pv = jnp.concatenate([_dot(x.astype(BF16), vb) for x in w], axis=0)
        new_carry = jnp.concatenate([a[:, 0:1] + d[:, 0:1] for a, d in zip(after, drop)], axis=0)
        return pv, new_carry

    pv, carry = block(i, jnp.zeros((bq, 1), F32), True)
    acc_ref[...] = pv
    car_ref[...] = carry

    def body(c):
        j, _ = c
        pv, carry = block(j, car_ref[...], False)
        acc_ref[...] += pv
        car_ref[...] = carry
        return j - 1, (jnp.min(carry) > -SB_EXIT).astype(I32)

    lax.while_loop(lambda c: jnp.logical_and(c[0] >= 0, c[1] == 0), body, (i - 1, jnp.int32(0)))
    o_ref[...] = acc_ref[...].astype(o_ref.dtype)


def _sb_attention(qkv, bq=256):
    T = qkv.shape[0]
    H, Dh = SB_HEADS, SB_HEAD_DIM
    kern = functools.partial(_sb_kernel, bq=bq, scale=Dh ** -0.5)
    return pl.pallas_call(
        kern,
        grid=(H, T // bq),
        in_specs=[pl.BlockSpec((bq, Dh), lambda h, i: (i, h)),
                  pl.BlockSpec((T, Dh), lambda h, i: (0, H + h)),
                  pl.BlockSpec((T, Dh), lambda h, i: (0, 2 * H + h))],
        out_specs=pl.BlockSpec((bq, Dh), lambda h, i: (i, h)),
        out_shape=jax.ShapeDtypeStruct((T, H * Dh), BF16),
        scratch_shapes=[pltpu.VMEM((bq, Dh), F32), pltpu.VMEM((bq, 1), F32)],
        compiler_params=_cp("parallel", "parallel"),
        name="sb_attention",
    )(qkv, qkv, qkv)


_GLA_LEVELS = (32, 16, 8, 4, 2, 1)


def _gla_constants():
    C = GLA_CHUNK
    i = np.arange(C)[:, None]
    s = np.arange(C)[None, :]
    sel = []
    masks = []
    for m in _GLA_LEVELS:
        same = (i // m) == (s // m)
        sel.append(same & (s <= i) & ((i // m) % 2 == 1))
        sel.append(same & (s > i) & ((i // m) % 2 == 0))
        masks.append(((i // (2 * m)) == (s // (2 * m))) & ((i // m) % 2 == 1) & ((s // m) % 2 == 0))
    sel.append(s <= i)
    sel.append(s > i)
    masks.append(i == s)
    return (np.concatenate(sel, 0).astype(np.float32), np.stack(masks, 0).astype(np.float32))


def _gla_kernel(q_ref, k_ref, v_ref, go_ref, glr_ref, up_ref, bias_ref, ng_ref, sel_ref, msk_ref,
                o_ref, st_ref, *, n_chunks):
    C = GLA_CHUNK
    nl = len(_GLA_LEVELS)

    @pl.when(pl.program_id(1) == 0)
    def _():
        st_ref[...] = jnp.zeros_like(st_ref)

    sel = sel_ref[...]
    up = up_ref[...]
    bias = bias_ref[...]
    ng = ng_ref[...]
    rows = [slice(c * C, (c + 1) * C) for c in range(n_chunks)]
    blk = lambda f, j: f[j * C:(j + 1) * C]
    q = [q_ref[s, :] * (GLA_HEAD_K ** -0.5) for s in rows]
    k = [k_ref[s, :] for s in rows]
    v = [v_ref[s, :].astype(BF16) for s in rows]
    x = [_dot(glr_ref[s, :].astype(BF16), up) + bias for s in rows]
    g = [-_softplus(-t) * (1.0 / GLA_GATE_NORMALIZER) for t in x]
    f = [jnp.exp(_dot_xf(sel, t, 2)) for t in g]
    scores = [_dot(a.astype(BF16), b.astype(BF16), NT) * msk_ref[nl] for a, b in zip(q, k)]
    for l in range(nl):
        ql = [(a * blk(t, 2 * l)).astype(BF16) for a, t in zip(q, f)]
        kl = [(b * blk(t, 2 * l + 1)).astype(BF16) for b, t in zip(k, f)]
        scores = [s + _dot(a, b, NT) * msk_ref[l] for s, a, b in zip(scores, ql, kl)]
    o_intra = [_dot(s.astype(BF16), b) for s, b in zip(scores, v)]
    qe = [(a * blk(t, 2 * nl)).astype(BF16) for a, t in zip(q, f)]
    upd = [_dot(b, (a * blk(t, 2 * nl + 1)).astype(BF16), TN) for b, a, t in zip(v, k, f)]
    dec = [blk(t, 2 * nl)[C - 1:C, :] for t in f]

    st = st_ref[...]
    for c in range(n_chunks):
        o = _dot(qe[c], st.astype(BF16), NT) + o_intra[c]
        st = st * dec[c] + upd[c]
        rms = lax.rsqrt(jnp.mean(o * o, axis=-1, keepdims=True) + GLA_NORM_EPS)
        o_ref[rows[c], :] = (o * rms * ng * _silu(go_ref[rows[c], :])).astype(o_ref.dtype)
    st_ref[...] = st


def _gla(p_gla, gk_up, gk_bias, norm_g, tt=256):
    T = p_gla.shape[0]
    H, dk, dv = GLA_HEADS, GLA_HEAD_K, GLA_HEAD_V
    sel, msk = _gla_constants()
    up = jnp.zeros((LANES, GLA_KEY_DIM), F32).at[:GLA_GATE_RANK].set(gk_up).astype(BF16)
    kern = functools.partial(_gla_kernel, n_chunks=tt // GLA_CHUNK)
    return pl.pallas_call(
        kern,
        grid=(H, T // tt),
        in_specs=[pl.BlockSpec((tt, dk), lambda h, i: (i, h)),
                  pl.BlockSpec((tt, dk), lambda h, i: (i, H + h)),
                  pl.BlockSpec((tt, dv), lambda h, i: (i, H + h)),
                  pl.BlockSpec((tt, dv), lambda h, i: (i, 2 * H + h)),
                  pl.BlockSpec((tt, LANES), lambda h, i: (i, 3 * GLA_VALUE_DIM // LANES)),
                  pl.BlockSpec((LANES, dk), lambda h, i: (0, h)),
                  pl.BlockSpec((1, dk), lambda h, i: (0, h)),
                  pl.BlockSpec((1, dv), lambda h, i: (0, 0)),
                  pl.BlockSpec(sel.shape, lambda h, i: (0, 0)),
                  pl.BlockSpec(msk.shape, lambda h, i: (0, 0, 0))],
        out_specs=pl.BlockSpec((tt, dv), lambda h, i: (i, h)),
        out_shape=jax.ShapeDtypeStruct((T, H * dv), BF16),
        scratch_shapes=[pltpu.VMEM((dv, dk), F32)],
        compiler_params=_cp("parallel", "arbitrary"),
        name="gla",
    )(p_gla, p_gla, p_gla, p_gla, p_gla, up, gk_bias.reshape(1, -1), norm_g.reshape(1, -1),
      jnp.asarray(sel, BF16), jnp.asarray(msk, F32))


def _head_ones():
    l = np.arange(LANES)
    return (l[:, None] // RWKV_HEAD_DIM == l[None, :] // RWKV_HEAD_DIM).astype(np.float32)


def _head_diag():
    i = np.arange(RWKV_HEAD_DIM)[:, None]
    l = np.arange(LANES)[None, :]
    return (l % RWKV_HEAD_DIM == i).astype(np.float32)


def _rwprep_kernel(p_ref, pp_ref, mu_ref, w0_ref, w2_ref, a0_ref, a2_ref, g2_ref, kk_ref, ka_ref,
                   v0_ref, v1_ref, v2_ref, vf_ref, bo_ref,
                   r_ref, w_ref, k_ref, v_ref, a_ref, b_ref, g_ref, *, first_layer):
    W = RWKV_WIDTH
    i = pl.program_id(0)
    p = p_ref[...]
    tm = p.shape[0]
    prev_row = jnp.where(i == 0, 0.0, pp_ref[SUBLANES - 1:SUBLANES, :])
    rid = lax.broadcasted_iota(I32, p.shape, 0)
    prev = jnp.where(rid == 0, prev_row, pltpu.roll(p, 1, axis=0))
    ps = p + (prev - p) * mu_ref[...]
    r = ps[:, 0:W]
    kr = ps[:, W:2 * W]
    vr = ps[:, 2 * W:3 * W]
    lr = ps[:, 3 * W:3 * W + LANES]
    gl = ps[:, 3 * W + LANES:3 * W + 2 * LANES]
    w_log = -_softplus(-(w0_ref[...] + _dot(jnp.tanh(lr).astype(BF16), w2_ref[...]))) - 0.5
    w_ref[...] = -jnp.exp(w_log)
    if not first_layer:
        lo = _dot(vr.astype(BF16), v1_ref[...])
        gate = _sigmoid(v0_ref[...] + _dot(lo.astype(BF16), v2_ref[...]))
        vr = vr + (vf_ref[...] - vr) * gate
    v_ref[...] = vr
    a = _sigmoid(a0_ref[...] + _dot(lr.astype(BF16), a2_ref[...]))
    g_ref[...] = _dot(_sigmoid(gl).astype(BF16), g2_ref[...])
    kk = kr * kk_ref[...]
    bo = bo_ref[...]
    for hp in range(W // LANES):
        sl = slice(hp * LANES, (hp + 1) * LANES)
        kh = kk[:, sl]
        ss = _dot_fx(kh * kh, bo, 3)
        kn = kh / jnp.maximum(jnp.sqrt(ss), 1e-12)
        a_ref[:, sl] = -kn
        b_ref[:, sl] = kn * a[:, sl]
    r_ref[...] = r
    k_ref[...] = kr * (1.0 + (a - 1.0) * ka_ref[...])


def _rwkv_prep(p_rw, v_first, mu, w0, w2, a0, a2, g2, k_k, k_a, v0, v1, v2, tm=256):
    T, NC = p_rw.shape
    W = RWKV_WIDTH
    first = v_first is None
    w2p = jnp.zeros((LANES, W), F32).at[:RWKV_DECAY_RANK].set(w2).astype(BF16)
    a2p = jnp.zeros((LANES, W), F32).at[RWKV_DECAY_RANK:].set(a2).astype(BF16)
    if first:
        v_first = jnp.zeros((SUBLANES, W), F32)
        v0 = jnp.zeros((W,), F32)
        v1p = jnp.zeros((W, LANES), BF16)
        v2p = jnp.zeros((LANES, W), BF16)
        vf_spec = pl.BlockSpec((SUBLANES, W), lambda i: (0, 0))
    else:
        v1p = jnp.zeros((W, LANES), F32).at[:, :RWKV_VALUE_RANK].set(v1).astype(BF16)
        v2p = jnp.zeros((LANES, W), F32).at[:RWKV_VALUE_RANK].set(v2).astype(BF16)
        vf_spec = pl.BlockSpec((tm, W), lambda i: (i, 0))
    row = lambda a: a.reshape(1, -1)
    full = lambda shape: pl.BlockSpec(shape, lambda i: (0,) * len(shape))
    out = jax.ShapeDtypeStruct((T, W), F32)
    ospec = pl.BlockSpec((tm, W), lambda i: (i, 0))
    kern = functools.partial(_rwprep_kernel, first_layer=first)
    return pl.pallas_call(
        kern,
        grid=(T // tm,),
        in_specs=[pl.BlockSpec((tm, NC), lambda i: (i, 0)),
                  pl.BlockSpec((SUBLANES, NC), lambda i: (jnp.maximum(i * (tm // SUBLANES) - 1, 0), 0)),
                  full((1, NC)), full((1, W)), full((LANES, W)), full((1, W)), full((LANES, W)),
                  full((LANES, W)), full((1, W)), full((1, W)),
                  full((1, W)), full((W, LANES)), full((LANES, W)), vf_spec, full((LANES, LANES))],
        out_specs=[ospec] * 7,
        out_shape=[out] * 7,
        compiler_params=_cp("parallel"),
        name="rwkv_prep",
    )(p_rw, p_rw, row(mu), row(w0), w2p, row(a0), a2p, g2.astype(BF16), row(k_k), row(k_a),
      row(v0), v1p, v2p, v_first, jnp.asarray(_head_ones(), BF16))


RW_CHUNK = 64


def _rw_constants():
    C = RW_CHUNK
    i = np.arange(C)[:, None]
    s = np.arange(C)[None, :]
    i2 = np.arange(2 * C)[:, None]
    s2 = np.arange(2 * C)[None, :]
    same = (i2 // C) == (s2 // C)
    cm = np.stack([same & (s2 < i2), same & (s2 <= i2), (i2 // 16) == (s2 // 16), (i2 // 32) == (s2 // 32),
                   same], 0).astype(np.float32)
    return (s <= i).astype(np.float32), cm


def _mm(a, b, dims=NN):
    a1, a2 = _split(a, 2)
    b1, b2 = _split(b, 2)
    return _dot(a1, b1, dims) + (_dot(a1, b2, dims) + _dot(a2, b1, dims))


def _mm1(a, b, dims=NN):
    return _dot(a.astype(BF16), b.astype(BF16), dims)


def _unit_lower_inverse(nms, eye, bd16, bd32):
    nd = [n * bd16 for n in nms]
    x = [eye + n for n in nd]
    p = [_mm(n, n) for n in nd]
    for step in range(3):
        x = [a + _mm(a, q) for a, q in zip(x, p)]
        if step < 2:
            p = [_mm(q, q) for q in p]
    for lo_mask in (bd32 - bd16, 1.0 - bd32):
        t = [_mm(n * lo_mask, a) for n, a in zip(nms, x)]
        x = [a + _mm(a, q) for a, q in zip(x, t)]
    return x


def _rwrec_kernel(r_ref, lw_ref, k_ref, v_ref, a_ref, b_ref, lt_ref, cm_ref, y_ref, mt_ref, *, n_chunks):
    C = RW_CHUNK
    C2 = 2 * C

    @pl.when(pl.program_id(1) == 0)
    def _():
        mt_ref[...] = jnp.zeros_like(mt_ref)

    lt = lt_ref[...]
    sl, li, bd16, bd32, bdh = cm_ref[0], cm_ref[1], cm_ref[2], cm_ref[3], cm_ref[4]
    eye = li - sl
    lane = lax.broadcasted_iota(I32, (1, LANES), 1)
    m0 = jnp.where(lane < RWKV_HEAD_DIM, 1.0, 0.0)
    m1 = 1.0 - m0
    stack = lambda x: jnp.concatenate([x * m0, x * m1], axis=0)
    fold = lambda x: x[:C] + x[C:]
    rows = [slice(c * C, (c + 1) * C) for c in range(n_chunks)]

    lw = [lw_ref[s, :] for s in rows]
    cw = [_dot_xf(lt, x, 3) for x in lw]
    en = [jnp.exp(-x) for x in cw]
    at = [a_ref[s, :] * jnp.exp(c - l) for s, c, l in zip(rows, cw, lw)]
    rt = [r_ref[s, :] * jnp.exp(c) for s, c in zip(rows, cw)]
    bh = [b_ref[s, :] * e for s, e in zip(rows, en)]
    kh = [k_ref[s, :] * e for s, e in zip(rows, en)]
    v = [v_ref[s, :] for s in rows]
    a2 = [stack(x) for x in at]
    ar2 = [jnp.concatenate([x, stack(y)], axis=0) for x, y in zip(a2, rt)]
    gb = [_mm(x, stack(y), NT) for x, y in zip(ar2, bh)]
    gk = [_mm(x, stack(y), NT) for x, y in zip(ar2, kh)]
    tinv = _unit_lower_inverse([g[:C2] * sl for g in gb], eye, bd16, bd32)
    akv = [_mm(jnp.concatenate([g[:C2] * sl, g[C2:] * li], axis=0), stack(x)) for g, x in zip(gk, v)]
    tav = [_mm(t, jnp.concatenate([x, y[:C2]], axis=1)) for t, x, y in zip(tinv, a2, akv)]
    rav = [_mm1(g[C2:] * li, x) for g, x in zip(gb, tav)]
    ta = [fold(x[:, :LANES]) for x in tav]
    u0 = [fold(x[:, LANES:]) for x in tav]
    qe = [x + fold(y[:, :LANES]) for x, y in zip(rt, rav)]
    y0 = [fold(x[:, LANES:] + y[C2:]) for x, y in zip(rav, akv)]
    dl = [jnp.exp(x[C - 1:C, :]) for x in cw]
    gmt = [(eye + _mm(x, y, TN) * bdh) * d for x, y, d in zip(ta, bh, dl)]
    hmt = [(_mm(x, y, TN) + _mm(z, w, TN)) * bdh * d for x, y, z, w, d in zip(u0, bh, v, kh, dl)]

    mt = mt_ref[...]
    for c in range(n_chunks):
        y_ref[rows[c], :] = _mm1(qe[c], mt, NT) + y0[c]
        mt = _mm(mt, gmt[c]) + hmt[c]
    mt_ref[...] = mt


def _rwkv_recurrence(r, lw, k, v, a, b, tt=256):
    T, W = r.shape
    lt, cm = _rw_constants()
    spec = pl.BlockSpec((tt, LANES), lambda p, i: (i, p))
    kern = functools.partial(_rwrec_kernel, n_chunks=tt // RW_CHUNK)
    return pl.pallas_call(
        kern,
        grid=(W // LANES, T // tt),
        in_specs=[spec] * 6 + [pl.BlockSpec(lt.shape, lambda p, i: (0, 0)),
                               pl.BlockSpec(cm.shape, lambda p, i: (0, 0, 0))],
        out_specs=spec,
        out_shape=jax.ShapeDtypeStruct((T, W), F32),
        scratch_shapes=[pltpu.VMEM((LANES, LANES), F32)],
        compiler_params=_cp("parallel", "arbitrary"),
        name="rwkv_recurrence",
    )(r, lw, k, v, a, b, jnp.asarray(lt, BF16), jnp.asarray(cm, F32))


def _rwpost_kernel(y_ref, r_ref, k_ref, v_ref, g_ref, lg_ref, lb_ref, rk_ref, bo_ref, o_ref):
    bo = bo_ref[...]
    inv = 1.0 / RWKV_HEAD_DIM
    for hp in range(RWKV_WIDTH // LANES):
        sl = slice(hp * LANES, (hp + 1) * LANES)
        y = y_ref[:, sl]
        mu = _dot_fx(y, bo, 3) * inv
        d = y - mu
        var = _dot_fx(d * d, bo, 3) * inv
        yn = d * lax.rsqrt(var + RWKV_LNX_EPS) * lg_ref[:, sl] + lb_ref[:, sl]
        bonus = _dot_fx(r_ref[:, sl] * k_ref[:, sl] * rk_ref[:, sl], bo, 3) * v_ref[:, sl]
        o_ref[:, sl] = ((yn + bonus) * g_ref[:, sl]).astype(o_ref.dtype)


def _rwkv_post(y, r, k, v, g, lnx_g, lnx_b, r_k, tm=512):
    T, W = y.shape
    spec = pl.BlockSpec((tm, W), lambda i: (i, 0))
    rowspec = pl.BlockSpec((1, W), lambda i: (0, 0))
    return pl.pallas_call(
        _rwpost_kernel,
        grid=(T // tm,),
        in_specs=[spec] * 5 + [rowspec] * 3 + [pl.BlockSpec((LANES, LANES), lambda i: (0, 0))],
        out_specs=spec,
        out_shape=jax.ShapeDtypeStruct((T, W), BF16),
        compiler_params=_cp("parallel"),
        name="rwkv_post",
    )(y, r, k, v, g, lnx_g.reshape(1, W), lnx_b.reshape(1, W), r_k.reshape(1, W),
      jnp.asarray(_head_ones(), BF16))


def _merge_kernel(osb_ref, ogl_ref, orw_ref, g1_ref, g2_ref, g3_ref, w1_ref, w2_ref, w3_ref, o_ref):
    acc = _sigmoid(g1_ref[...].astype(F32)) * _dot(osb_ref[...], w1_ref[...])
    acc = acc + _sigmoid(g2_ref[...].astype(F32)) * _dot(ogl_ref[...], w2_ref[...])
    acc = acc + _sigmoid(g3_ref[...].astype(F32)) * _dot(orw_ref[...], w3_ref[...])
    o_ref[...] = acc.astype(o_ref.dtype)


def _merge(o_sb, o_gla, o_rw, p_merge, w_sb, w_gla, w_rw, tm=512, tn=1024):
    T = o_sb.shape[0]
    D = D_MODEL
    nb = D // tn
    aspec = pl.BlockSpec((tm, 1024), lambda j, i: (i, 0))
    wspec = pl.BlockSpec((1024, tn), lambda j, i: (0, j))
    gspec = lambda b: pl.BlockSpec((tm, tn), lambda j, i: (i, b * nb + j))
    return pl.pallas_call(
        _merge_kernel,
        grid=(nb, T // tm),
        in_specs=[aspec, aspec, aspec, gspec(0), gspec(1), gspec(2), wspec, wspec, wspec],
        out_specs=pl.BlockSpec((tm, tn), lambda j, i: (i, j)),
        out_shape=jax.ShapeDtypeStruct((T, D), BF16),
        compiler_params=_cp("parallel", "parallel"),
        name="merge",
    )(o_sb, o_gla, o_rw, p_merge, p_merge, p_merge, w_sb, w_gla, w_rw)


def _layer_norm(h, g, b):
    mu = jnp.mean(h, axis=-1, keepdims=True)
    d = h - mu
    var = jnp.mean(d * d, axis=-1, keepdims=True)
    return d * lax.rsqrt(var + LN_EPS) * g + b


def _oproj_kernel(m_ref, w_ref, x_ref, gate_ref, lg_ref, lb_ref, o_ref):
    mix = _dot(m_ref[...], w_ref[...])
    h = DEEPNORM_ALPHA * x_ref[...] + (1.0 + gate_ref[...]) * mix
    o_ref[...] = _layer_norm(h, lg_ref[...], lb_ref[...])


def _oproj(merged, w_o, x, gate, ln_g, ln_b, tm=512):
    T, D = x.shape
    spec = pl.BlockSpec((tm, D), lambda i: (i, 0))
    rowspec = pl.BlockSpec((1, D), lambda i: (0, 0))
    return pl.pallas_call(
        _oproj_kernel,
        grid=(T // tm,),
        in_specs=[spec, pl.BlockSpec((D, D), lambda i: (0, 0)), spec, rowspec, rowspec, rowspec],
        out_specs=spec,
        out_shape=jax.ShapeDtypeStruct((T, D), F32),
        compiler_params=_cp("parallel"),
        name="oproj_norm",
    )(merged, w_o, x, gate, ln_g.reshape(1, D), ln_b.reshape(1, D))


def _first_max(x, idx, axis, big):
    m = jnp.max(x, axis=axis, keepdims=True)
    first = jnp.min(jnp.where(x == m, idx, big), axis=axis, keepdims=True)
    return m, idx == first


def _router_kernel(x_ref, sc_ref, sh_ref, wr_ref, rb_ref, u_ref, e_ref, g_ref, p_ref, cnt_ref, car_ref, *, tm):
    E, G = N_EXPERTS, N_GROUPS
    per = E // G
    neg = -jnp.inf

    @pl.when(pl.program_id(0) == 0)
    def _():
        car_ref[...] = jnp.zeros_like(car_ref)

    u = x_ref[...] * (1.0 + sc_ref[...]) + sh_ref[...]
    u_ref[...] = _pack_pairs(u)
    scores =_sigmoid(_dot_ff(wr_ref[...], u, NT))
    biased = scores + rb_ref[...]
    b3 = biased.reshape(G, per, tm)
    i3 = lax.broadcasted_iota(I32, (G, per, tm), 1)
    m1, hit = _first_max(b3, i3, 1, per)
    m2 = jnp.max(jnp.where(hit, neg, b3), axis=1, keepdims=True)
    gs = (m1 + m2).reshape(G, tm)
    gi = lax.broadcasted_iota(I32, (G, tm), 0)
    ok = jnp.zeros((G, tm), jnp.bool_)
    for _ in range(TOPK_GROUPS):
        _, hit = _first_max(gs, gi, 0, G)
        ok = jnp.logical_or(ok, hit)
        gs = jnp.where(hit, neg, gs)
    ok_e = jnp.broadcast_to(ok.reshape(G, 1, tm), (G, per, tm)).reshape(E, tm)
    cand = jnp.where(ok_e, biased, neg)
    ei = lax.broadcasted_iota(I32, (E, tm), 0)
    sels, gates = [], []
    for _ in range(TOP_K):
        _, hit = _first_max(cand, ei, 0, E)
        sels.append(hit)
        gates.append(jnp.sum(jnp.where(hit, scores, 0.0), axis=0, keepdims=True))
        cand = jnp.where(hit, neg, cand)
    denom = gates[0]
    for gk in gates[1:]:
        denom = denom + gk
    chosen = sels[0]
    for s in sels[1:]:
        chosen = jnp.logical_or(chosen, s)
    cnt = jnp.where(chosen, 1.0, 0.0)
    earlier = (lax.broadcasted_iota(I32, (tm, tm), 0) < lax.broadcasted_iota(I32, (tm, tm), 1)).astype(BF16)
    rank = _dot(cnt.astype(BF16), earlier) + car_ref[...][:, 0:1]
    car_ref[...] = car_ref[...] + jnp.sum(cnt, axis=1, keepdims=True)
    cnt_ref[...] = car_ref[...]
    zi = jnp.zeros((1, tm), I32)
    zf = jnp.zeros((1, tm), F32)
    e_rows = [jnp.sum(jnp.where(s, ei, 0), axis=0, keepdims=True) for s in sels]
    p_rows = [jnp.sum(jnp.where(s, rank, 0.0), axis=0, keepdims=True).astype(I32) for s in sels]
    g_rows = [gk / denom * ROUTED_SCALE for gk in gates]
    e_ref[...] = jnp.concatenate(e_rows + [zi, zi], axis=0)
    p_ref[...] = jnp.concatenate(p_rows + [zi, zi], axis=0)
    g_ref[...] = jnp.concatenate(g_rows + [zf, zf], axis=0)


def _router(x, sc, sh, w_router, router_bias, tm=256):
    T, D = x.shape
    E = N_EXPERTS
    kern = functools.partial(_router_kernel, tm=tm)
    rowspec = pl.BlockSpec((1, D), lambda i: (0, 0))
    kspec = pl.BlockSpec((SUBLANES, tm), lambda i: (0, i))
    return pl.pallas_call(
        kern,
        grid=(T // tm,),
        in_specs=[pl.BlockSpec((tm, D), lambda i: (i, 0)), rowspec, rowspec,
                  pl.BlockSpec((E, D), lambda i: (0, 0)), pl.BlockSpec((E, 1), lambda i: (0, 0))],
        out_specs=[pl.BlockSpec((tm, D // 2), lambda i: (i, 0)), kspec, kspec, kspec,
                   pl.BlockSpec((E, LANES), lambda i: (0, 0))],
        out_shape=[jax.ShapeDtypeStruct((T, D // 2), U32),
                   jax.ShapeDtypeStruct((SUBLANES, T), I32),
                   jax.ShapeDtypeStruct((SUBLANES, T), F32),
                   jax.ShapeDtypeStruct((SUBLANES, T), I32),
                   jax.ShapeDtypeStruct((E, LANES), F32)],
        scratch_shapes=[pltpu.VMEM((E, LANES), F32)],
        compiler_params=_cp("arbitrary"),
        name="router",
    )(x, sc, sh, w_router.T, router_bias.reshape(E, 1))


def _dispatch_kernel(dest_ref, fill_ref, u_ref, xs_ref, z_ref, sem, zsem, *, tm, bm):
    i = pl.program_id(0)

    @pl.when(i == 0)
    def _():
        z_ref[...] = jnp.zeros_like(z_ref)

        def zcopy(e):
            return pltpu.make_async_copy(z_ref, xs_ref.at[pl.ds(pl.multiple_of(fill_ref[e], bm), bm)], zsem)

        def zstart(e, c):
            zcopy(e).start()
            return c

        def zwait(e, c):
            zcopy(e).wait()
            return c

        lax.fori_loop(0, N_EXPERTS, zstart, 0)
        lax.fori_loop(0, N_EXPERTS, zwait, 0)

    def copy(t, k):
        return pltpu.make_async_copy(_row(u_ref, t), _row(xs_ref, dest_ref[k, t]), sem)

    def start(t, c):
        for k in range(TOP_K):
            copy(t, k).start()
        return c

    def wait(t, c):
        for k in range(TOP_K):
            copy(t, k).wait()
        return c

    lax.fori_loop(0, tm, start, 0)
    lax.fori_loop(0, tm, wait, 0)


def _dispatch(u, dest, fill, n_slots, tm=256, bm=MOE_BM):
    T, D = u.shape
    kern = functools.partial(_dispatch_kernel, tm=tm, bm=bm)
    return pl.pallas_call(
        kern,
        grid=(T // tm,),
        in_specs=[pl.BlockSpec((SUBLANES, tm), lambda i: (0, i), memory_space=pltpu.SMEM),
                  pl.BlockSpec(memory_space=pltpu.SMEM),
                  pl.BlockSpec((tm, D), lambda i: (i, 0))],
        out_specs=pl.BlockSpec(memory_space=pl.ANY),
        out_shape=jax.ShapeDtypeStruct((n_slots, D), u.dtype),
        scratch_shapes=[pltpu.VMEM((bm, D), u.dtype),
                        pltpu.SemaphoreType.DMA(()), pltpu.SemaphoreType.DMA(())],
        compiler_params=_cp("arbitrary"),
        name="moe_dispatch",
    )(dest, fill, u)


def _expert_kernel(be_ref, nv_ref, x_ref, wg_ref, wu_ref, wd_ref, y_ref, wgb_ref, wub_ref, wdb_ref, *, bm):
    b = pl.program_id(0)
    valid = b < nv_ref[0]
    new_expert = jnp.logical_or(b == 0, be_ref[b] != be_ref[jnp.maximum(b - 1, 0)])

    @pl.when(jnp.logical_and(valid, new_expert))
    def _():
        wgb_ref[...] = wg_ref[0].astype(BF16)
        wub_ref[...] = wu_ref[0].astype(BF16)
        wdb_ref[...] = wd_ref[0].astype(BF16)

    @pl.when(valid)
    def _():
        x = _unpack_pairs(x_ref[...]).astype(BF16)
        h = _silu(_dot(x, wgb_ref[...])) * _dot(x, wub_ref[...])
        y_ref[...] = _pack_pairs(_dot(h.astype(BF16), wdb_ref[...]))


def _experts(xs, block_expert, n_valid, w_gate, w_up, w_down, layer, bm=MOE_BM):
    n_slots, W = xs.shape
    D = 2 * W
    nb = n_slots // bm
    F = w_gate.shape[-1]
    blk = lambda b, be, nv: (jnp.minimum(b, nv[0] - 1), 0)
    wsel = lambda b, be, nv: (layer, be[jnp.minimum(b, nv[0] - 1)], 0, 0)
    gs = pltpu.PrefetchScalarGridSpec(
        num_scalar_prefetch=2,
        grid=(nb,),
        in_specs=[pl.BlockSpec((bm, W), blk),
                  pl.BlockSpec((None, 1, D, F), wsel),
                  pl.BlockSpec((None, 1, D, F), wsel),
                  pl.BlockSpec((None, 1, F, D), wsel)],
        out_specs=pl.BlockSpec((bm, W), blk),
        scratch_shapes=[pltpu.VMEM((D, F), BF16), pltpu.VMEM((D, F), BF16), pltpu.VMEM((F, D), BF16)],
    )
    return pl.pallas_call(
        functools.partial(_expert_kernel, bm=bm),
        grid_spec=gs,
        out_shape=jax.ShapeDtypeStruct((n_slots, W), U32),
        compiler_params=_cp("arbitrary"),
        name="moe_experts",
    )(block_expert, n_valid, xs, w_gate, w_up, w_down)


def _combine_kernel(dest_ref, dnext_ref, ys_ref, u_ref, x_ref, gt_ref, wg_ref, wu_ref, wd_ref, gate_ref, lg_ref,
                    lb_ref, o_ref, buf_ref, sem, *, tm):
    i = pl.program_id(0)
    slot = lax.rem(i, 2)

    def copy(d_ref, s, t, k):
        return pltpu.make_async_copy(_row(ys_ref, d_ref[k, t]), _row(buf_ref.at[s, k], t), sem.at[s])

    def gather(d_ref, s):
        def start(t, c):
            for k in range(TOP_K):
                copy(d_ref, s, t, k).start()
            return c

        lax.fori_loop(0, tm, start, 0)

    @pl.when(i == 0)
    def _():
        gather(dest_ref, slot)

    @pl.when(i + 1 < pl.num_programs(0))
    def _():
        gather(dnext_ref, 1 - slot)

    u = _unpack_pairs(u_ref[...]).astype(BF16)
    h = _silu(_dot(u, wg_ref[...])) * _dot(u, wu_ref[...])
    ffn = _dot(h.astype(BF16), wd_ref[...])

    def wait(t, c):
        for k in range(TOP_K):
            copy(dest_ref, slot, t, k).wait()
        return c

    lax.fori_loop(0, tm, wait, 0)
    gt = gt_ref[...]
    for k in range(TOP_K):
        ffn = ffn + gt[:, k:k + 1] * _unpack_pairs(buf_ref[slot, k])
    hres = DEEPNORM_ALPHA * x_ref[...] + (1.0 + gate_ref[...]) * ffn
    o_ref[...] = _layer_norm(hres, lg_ref[...], lb_ref[...])


def _combine(ys, dest, u, x, gates_t, w_sg, w_su, w_sd, gate, ln_g, ln_b, tm=128):
    T, D = x.shape
    W = ys.shape[1]
    F = w_sg.shape[-1]
    n = T // tm
    kern = functools.partial(_combine_kernel, tm=tm)
    spec = pl.BlockSpec((tm, D), lambda i: (i, 0))
    rowspec = pl.BlockSpec((1, D), lambda i: (0, 0))
    return pl.pallas_call(
        kern,
        grid=(n,),
        in_specs=[pl.BlockSpec((SUBLANES, tm), lambda i: (0, i), memory_space=pltpu.SMEM),
                  pl.BlockSpec((SUBLANES, tm), lambda i: (0, jnp.minimum(i + 1, n - 1)), memory_space=pltpu.SMEM),
                  pl.BlockSpec(memory_space=pl.ANY),
                  pl.BlockSpec((tm, W), lambda i: (i, 0)), spec,
                  pl.BlockSpec((tm, SUBLANES), lambda i: (i, 0)),
                  pl.BlockSpec((D, F), lambda i: (0, 0)),
                  pl.BlockSpec((D, F), lambda i: (0, 0)),
                  pl.BlockSpec((F, D), lambda i: (0, 0)),
                  rowspec, rowspec, rowspec],
        out_specs=spec,
        out_shape=jax.ShapeDtypeStruct((T, D), F32),
        scratch_shapes=[pltpu.VMEM((2, TOP_K, tm, W), ys.dtype), pltpu.SemaphoreType.DMA((2,))],
        compiler_params=_cp("arbitrary"),
        name="moe_combine_norm",
    )(dest, dest, ys, u, x, gates_t, w_sg, w_su, w_sd, gate, ln_g.reshape(1, D), ln_b.reshape(1, D))


def _moe_plan(top_e, rank, counts, bm, n_blocks):
    counts = counts.astype(I32)
    padded = (counts + bm - 1) // bm * bm
    padded_end = jnp.cumsum(padded)
    padded_start = padded_end - padded
    onehot = top_e[:, :, None] == jnp.arange(N_EXPERTS, dtype=I32)
    dest = jnp.sum(jnp.where(onehot, padded_start, 0), axis=-1) + rank
    fill = jnp.maximum(padded_end - bm, padded_start)
    fill = jnp.minimum(fill, (n_blocks - 1) * bm)
    block_start = jnp.arange(n_blocks, dtype=I32) * bm
    block_expert = jnp.minimum(jnp.sum((padded_end[None, :] <= block_start[:, None]).astype(I32), axis=1),
                               N_EXPERTS - 1)
    n_valid = jnp.maximum(padded_end[-1:] // bm, 1).astype(I32)
    return dest.astype(I32), fill.astype(I32), block_expert, n_valid


def _in_proj_weights(w_in):
    o = SB_COLS
    gq = w_in[:, o:o + 2 * GLA_KEY_DIM + GLA_VALUE_DIM]
    o += 2 * GLA_KEY_DIM + GLA_VALUE_DIM
    glr = w_in[:, o:o + GLA_GATE_RANK]
    o += GLA_GATE_RANK
    gout = w_in[:, o:o + GLA_VALUE_DIM]
    o += GLA_VALUE_DIM
    pad = jnp.zeros((w_in.shape[0], LANES - GLA_GATE_RANK), w_in.dtype)
    w_gla = jnp.concatenate([gq, gout, glr, pad], axis=1)
    w_rw = w_in[:, o:o + RWKV_COLS]
    o += RWKV_COLS
    w_mg = w_in[:, o:o + MERGE_COLS]
    return w_in[:, :SB_COLS], w_gla, w_rw, w_mg


def kernel(x, c, w_ada, b_ada, w_in, gla_gk_up, gla_gk_bias, gla_norm_g, rw_mu, rw_w0, rw_w2, rw_a0,
           rw_a2, rw_g2, rw_k_k, rw_k_a, rw_r_k, rw_lnx_g, rw_lnx_b, rw_v0, rw_v1, rw_v2,
           w_br_sb, w_br_gla, w_br_rw, w_o, ln_g, ln_b, w_router, router_bias,
           w_exp_gate, w_exp_up, w_exp_down, w_sh_gate, w_sh_up, w_sh_down):
    B, T, D = x.shape
    assert B == 1 and D == D_MODEL
    n_blocks = -(-(T * TOP_K) // MOE_BM) + N_EXPERTS
    mod = _ada(c, w_ada, b_ada)
    xs = x.reshape(T, D)
    v_first = None
    for l in range(DEPTH):
        sh1, sc1, g1, sh2, sc2, g2 = (mod[l, j * D:(j + 1) * D].reshape(1, D) for j in range(6))
        w_sb, w_gla, w_rw, w_mg = _in_proj_weights(w_in[l])
        u1 = _modulate(xs, sc1, sh1)
        p_sb = _wmm(u1, w_sb, BF16, 1024, name="inproj_sb")
        p_gla = _wmm(u1, w_gla, F32, 640, name="inproj_gla")
        p_rw = _wmm(u1, w_rw, F32, 832 * 2, name="inproj_rwkv")
        p_mg = _wmm(u1, w_mg, BF16, 1024, name="inproj_merge")

        o_sb = _sb_attention(p_sb)
        o_gla = _gla(p_gla, gla_gk_up[l], gla_gk_bias[l], gla_norm_g[l])
        if l == 0:
            r, w, k, v, a, b, g = _rwkv_prep(p_rw, None, rw_mu[l], rw_w0[l], rw_w2[l], rw_a0[l], rw_a2[l],
                                             rw_g2[l], rw_k_k[l], rw_k_a[l], None, None, None)
            v_first = v
        else:
            r, w, k, v, a, b, g = _rwkv_prep(p_rw, v_first, rw_mu[l], rw_w0[l], rw_w2[l], rw_a0[l], rw_a2[l],
                                             rw_g2[l], rw_k_k[l], rw_k_a[l], rw_v0[l - 1], rw_v1[l - 1],
                                             rw_v2[l - 1])
        y = _rwkv_recurrence(r, w, k, v, a, b)
        o_rw = _rwkv_post(y, r, k, v, g, rw_lnx_g[l], rw_lnx_b[l], rw_r_k[l])

        merged = _merge(o_sb, o_gla, o_rw, p_mg, w_br_sb[l].astype(BF16), w_br_gla[l].astype(BF16),
                        w_br_rw[l].astype(BF16))
        xs = _oproj(merged, w_o[l].astype(BF16), xs, g1, ln_g[l, 0], ln_b[l, 0])

        u2, top_e, gates, rank, counts = _router(xs, sc2, sh2, w_router[l], router_bias[l])
        dest, fill, block_expert, n_valid = _moe_plan(top_e, rank, counts[:, 0], MOE_BM, n_blocks)
        x_sorted = _dispatch(u2, dest, fill, n_blocks * MOE_BM)
        y_sorted = _experts(x_sorted, block_expert, n_valid, w_exp_gate, w_exp_up, w_exp_down, l)
        xs = _combine(y_sorted, dest, u2, xs, gates.T, w_sh_gate[l].astype(BF16), w_sh_up[l].astype(BF16),
                      w_sh_down[l].astype(BF16), g2, ln_g[l, 1], ln_b[l, 1])
    return xs.reshape(B, T, D)
```

```python
import functools

import numpy as np
import jax
import jax.numpy as jnp
from jax import lax
from jax.experimental import pallas as pl
from jax.experimental.pallas import tpu as pltpu

F32 = jnp.float32
BF16 = jnp.bfloat16
I32 = jnp.int32
U32 = jnp.uint32

D_MODEL = 2048
DEPTH = 2
LN_EPS = 1e-5
DEEPNORM_ALPHA = (2 * DEPTH) ** 0.25

SB_HEADS = 8
SB_HEAD_DIM = 128
SB_WIDTH = SB_HEADS * SB_HEAD_DIM

GLA_HEADS = 4
GLA_HEAD_K = 128
GLA_HEAD_V = 256
GLA_KEY_DIM = GLA_HEADS * GLA_HEAD_K
GLA_VALUE_DIM = GLA_HEADS * GLA_HEAD_V
GLA_GATE_RANK = 16
GLA_GATE_NORMALIZER = 16.0
GLA_CHUNK = 64
GLA_NORM_EPS = 1e-5

RWKV_HEADS = 16
RWKV_HEAD_DIM = 64
RWKV_WIDTH = RWKV_HEADS * RWKV_HEAD_DIM
RWKV_DECAY_RANK = 64
RWKV_AAA_RANK = 64
RWKV_VALUE_RANK = 32
RWKV_GATE_RANK = 128
RWKV_LNX_EPS = 64e-5

N_EXPERTS = 64
TOP_K = 6
EXPERT_DIM = 512
SHARED_DIM = 512
N_GROUPS = 8
TOPK_GROUPS = 4
ROUTED_SCALE = 2.5

SB_COLS = 3 * SB_WIDTH
GLA_COLS = 2 * GLA_KEY_DIM + GLA_VALUE_DIM + GLA_GATE_RANK + GLA_VALUE_DIM
RWKV_COLS = 3 * RWKV_WIDTH + RWKV_DECAY_RANK + RWKV_AAA_RANK + RWKV_GATE_RANK
MERGE_COLS = 3 * D_MODEL

LANES = 128
SUBLANES = 8
VMEM_LIMIT = 56 * 1024 * 1024
MOE_BM = 512
SB_EXIT = -104.0
SB_GROUPS = 2
ROW_UNROLL = 4

NN = (((1,), (0,)), ((), ()))
NT = (((1,), (1,)), ((), ()))
TN = (((0,), (0,)), ((), ()))


def _cp(*sem):
    return pltpu.CompilerParams(dimension_semantics=sem, vmem_limit_bytes=VMEM_LIMIT)


def _dot(a, b, dims=NN):
    return lax.dot_general(a, b, dims, preferred_element_type=F32)


def _split(a, n):
    out = []
    r = a
    for i in range(n):
        p = r.astype(BF16)
        out.append(p)
        if i + 1 < n:
            r = r - p.astype(F32)
    return out


def _dot_fx(a, e, n, dims=NN):
    acc = None
    for p in _split(a, n):
        t = _dot(p, e, dims)
        acc = t if acc is None else acc + t
    return acc


def _dot_xf(e, b, n, dims=NN):
    acc = None
    for p in _split(b, n):
        t = _dot(e, p, dims)
        acc = t if acc is None else acc + t
    return acc


def _dot_ff(a, b, dims=NN):
    a1, a2, a3 = _split(a, 3)
    b1, b2, b3 = _split(b, 3)
    acc = _dot(a1, b1, dims)
    for p, q in ((a1, b2), (a2, b1), (a2, b2), (a1, b3), (a3, b1)):
        acc = acc + _dot(p, q, dims)
    return acc


def _sigmoid(x):
    return 1.0 / (1.0 + jnp.exp(-x))


def _softplus(x):
    return jnp.maximum(x, 0.0) + jnp.log1p(jnp.exp(-jnp.abs(x)))


def _silu(x):
    return x * _sigmoid(x)


def _row(ref, t):
    return ref.at[pl.ds(t, 1)]


def _pack_pairs(x):
    w = x.shape[1] // 2
    bits = lax.bitcast_convert_type(x.astype(BF16).astype(F32), U32)
    return (bits[:, :w] >> 16) | bits[:, w:]


def _unpack_pairs(p):
    lo = lax.bitcast_convert_type(p << 16, F32)
    hi = lax.bitcast_convert_type(p & jnp.uint32(0xFFFF0000), F32)
    return jnp.concatenate([lo, hi], axis=1)


def _ada_kernel(c_ref, w_ref, b_ref, o_ref):
    c = c_ref[...]
    o_ref[0] = _mm(_silu(c), w_ref[0]) + b_ref[0]


def _ada(c, w_ada, b_ada):
    L, D, N = w_ada.shape
    tn = 512
    c8 = jnp.broadcast_to(c, (SUBLANES, D))
    out = pl.pallas_call(
        _ada_kernel,
        grid=(L, N // tn),
        in_specs=[pl.BlockSpec((SUBLANES, D), lambda l, j: (0, 0)),
                  pl.BlockSpec((1, D, tn), lambda l, j: (l, 0, j)),
                  pl.BlockSpec((1, 1, tn), lambda l, j: (l, 0, j))],
        out_specs=pl.BlockSpec((1, SUBLANES, tn), lambda l, j: (l, 0, j)),
        out_shape=jax.ShapeDtypeStruct((L, SUBLANES, N), F32),
        compiler_params=_cp("parallel", "parallel"),
        name="ada",
    )(c8, w_ada, b_ada.reshape(L, 1, N))
    return out[:, 0, :]


def _modulate_kernel(x_ref, sc_ref, sh_ref, o_ref):
    o_ref[...] = (x_ref[...] * (1.0 + sc_ref[...]) + sh_ref[...]).astype(o_ref.dtype)


def _modulate(x, sc, sh, tm=1024):
    T, D = x.shape
    spec = pl.BlockSpec((tm, D), lambda i: (i, 0))
    rowspec = pl.BlockSpec((1, D), lambda i: (0, 0))
    return pl.pallas_call(
        _modulate_kernel,
        grid=(T // tm,),
        in_specs=[spec, rowspec, rowspec],
        out_specs=spec,
        out_shape=jax.ShapeDtypeStruct((T, D), BF16),
        compiler_params=_cp("parallel"),
        name="modulate",
    )(x, sc, sh)


def _wmm_kernel(u_ref, w_ref, o_ref, wb_ref):
    @pl.when(pl.program_id(1) == 0)
    def _():
        wb_ref[...] = w_ref[...].astype(BF16)

    o_ref[...] = _dot(u_ref[...], wb_ref[...]).astype(o_ref.dtype)


def _wmm(u, w, out_dtype, tn, tm=512, name="wmm"):
    T, D = u.shape
    N = w.shape[1]
    return pl.pallas_call(
        _wmm_kernel,
        grid=(N // tn, T // tm),
        in_specs=[pl.BlockSpec((tm, D), lambda j, i: (i, 0)),
                  pl.BlockSpec((D, tn), lambda j, i: (0, j))],
        out_specs=pl.BlockSpec((tm, tn), lambda j, i: (i, j)),
        out_shape=jax.ShapeDtypeStruct((T, N), out_dtype),
        scratch_shapes=[pltpu.VMEM((D, tn), BF16)],
        compiler_params=_cp("parallel", "arbitrary"),
        name=name,
    )(u, w)


def _sb_kernel(q_ref, k_ref, v_ref, o_ref, acc_ref, car_ref, *, bq, scale):
    i = pl.program_id(1)
    q = q_ref[...]
    r_id = lax.broadcasted_iota(I32, (bq, bq), 0)
    c_id = lax.broadcasted_iota(I32, (bq, bq), 1)
    later = (r_id > c_id).astype(BF16)
    causal = c_id < r_id

    def block(j, carry, diagonal):
        off = pl.multiple_of(j * bq, bq)
        kb = k_ref[pl.ds(off, bq), :]
        vb = v_ref[pl.ds(off, bq), :]
        grp = [slice(g * (bq // SB_GROUPS), (g + 1) * (bq // SB_GROUPS)) for g in range(SB_GROUPS)]
        z = [_dot(q[s], kb, NT) * scale for s in grp]
        drop = [jnp.maximum(t, 0.0) + jnp.log(1.0 + jnp.exp(-jnp.abs(t))) for t in z]
        if diagonal:
            drop = [jnp.where(causal[s], d, 0.0) for s, d in zip(grp, drop)]
        after = [_dot_fx(d, later, 2) + carry[s] for s, d in zip(grp, drop)]
        w = [jnp.exp(t - d - a) for t, d, a in zip(z, drop, after)]
        if diagonal:
            w = [jnp.where(causal[s], x, 0.0) for s, x in zip(grp, w)]
        pv = jnp.concatenate([_dot(x.astype(BF16), vb) for x in w], axis=0)
        new_carry = jnp.concatenate([a[:, 0:1] + d[:, 0:1] for a, d in zip(after, drop)], axis=0)
        return pv, new_carry

    pv, carry = block(i, jnp.zeros((bq, 1), F32), True)
    acc_ref[...] = pv
    car_ref[...] = carry

    def body(c):
        j, _ = c
        pv, carry = block(j, car_ref[...], False)
        acc_ref[...] += pv
        car_ref[...] = carry
        return j - 1, (jnp.min(carry) > -SB_EXIT).astype(I32)

    lax.while_loop(lambda c: jnp.logical_and(c[0] >= 0, c[1] == 0), body, (i - 1, jnp.int32(0)))
    o_ref[...] = acc_ref[...].astype(o_ref.dtype)


def _sb_attention(qkv, bq=256):
    T = qkv.shape[0]
    H, Dh = SB_HEADS, SB_HEAD_DIM
    kern = functools.partial(_sb_kernel, bq=bq, scale=Dh ** -0.5)
    return pl.pallas_call(
        kern,
        grid=(H, T // bq),
        in_specs=[pl.BlockSpec((bq, Dh), lambda h, i: (i, h)),
                  pl.BlockSpec((T, Dh), lambda h, i: (0, H + h)),
                  pl.BlockSpec((T, Dh), lambda h, i: (0, 2 * H + h))],
        out_specs=pl.BlockSpec((bq, Dh), lambda h, i: (i, h)),
        out_shape=jax.ShapeDtypeStruct((T, H * Dh), BF16),
        scratch_shapes=[pltpu.VMEM((bq, Dh), F32), pltpu.VMEM((bq, 1), F32)],
        compiler_params=_cp("parallel", "parallel"),
        name="sb_attention",
    )(qkv, qkv, qkv)


_GLA_LEVELS = (32, 16, 8, 4, 2, 1)


def _gla_constants():
    C = GLA_CHUNK
    i = np.arange(C)[:, None]
    s = np.arange(C)[None, :]
    sel = []
    masks = []
    for m in _GLA_LEVELS:
        same = (i // m) == (s // m)
        sel.append(same & (s <= i) & ((i // m) % 2 == 1))
        sel.append(same & (s > i) & ((i // m) % 2 == 0))
        masks.append(((i // (2 * m)) == (s // (2 * m))) & ((i // m) % 2 == 1) & ((s // m) % 2 == 0))
    sel.append(s <= i)
    sel.append(s > i)
    masks.append(i == s)
    return (np.concatenate(sel, 0).astype(np.float32), np.stack(masks, 0).astype(np.float32))


def _gla_kernel(q_ref, k_ref, v_ref, go_ref, glr_ref, up_ref, bias_ref, ng_ref, sel_ref, msk_ref,
                o_ref, st_ref, *, n_chunks):
    C = GLA_CHUNK
    nl = len(_GLA_LEVELS)

    @pl.when(pl.program_id(1) == 0)
    def _():
        st_ref[...] = jnp.zeros_like(st_ref)

    sel = sel_ref[...]
    up = up_ref[...]
    bias = bias_ref[...]
    ng = ng_ref[...]
    rows = [slice(c * C, (c + 1) * C) for c in range(n_chunks)]
    blk = lambda f, j: f[j * C:(j + 1) * C]
    q = [q_ref[s, :] * (GLA_HEAD_K ** -0.5) for s in rows]
    k = [k_ref[s, :] for s in rows]
    v = [v_ref[s, :].astype(BF16) for s in rows]
    x = [_dot(glr_ref[s, :].astype(BF16), up) + bias for s in rows]
    g = [-_softplus(-t) * (1.0 / GLA_GATE_NORMALIZER) for t in x]
    f = [jnp.exp(_dot_xf(sel, t, 2)) for t in g]
    scores = [_dot(a.astype(BF16), b.astype(BF16), NT) * msk_ref[nl] for a, b in zip(q, k)]
    for l in range(nl):
        ql = [(a * blk(t, 2 * l)).astype(BF16) for a, t in zip(q, f)]
        kl = [(b * blk(t, 2 * l + 1)).astype(BF16) for b, t in zip(k, f)]
        scores = [s + _dot(a, b, NT) * msk_ref[l] for s, a, b in zip(scores, ql, kl)]
    o_intra = [_dot(s.astype(BF16), b) for s, b in zip(scores, v)]
    qe = [(a * blk(t, 2 * nl)).astype(BF16) for a, t in zip(q, f)]
    upd = [_dot(b, (a * blk(t, 2 * nl + 1)).astype(BF16), TN) for b, a, t in zip(v, k, f)]
    dec = [blk(t, 2 * nl)[C - 1:C, :] for t in f]

    st = st_ref[...]
    for c in range(n_chunks):
        o = _dot(qe[c], st.astype(BF16), NT) + o_intra[c]
        st = st * dec[c] + upd[c]
        rms = lax.rsqrt(jnp.mean(o * o, axis=-1, keepdims=True) + GLA_NORM_EPS)
        o_ref[rows[c], :] = (o * rms * ng * _silu(go_ref[rows[c], :])).astype(o_ref.dtype)
    st_ref[...] = st


def _gla(p_gla, gk_up, gk_bias, norm_g, tt=256):
    T = p_gla.shape[0]
    H, dk, dv = GLA_HEADS, GLA_HEAD_K, GLA_HEAD_V
    sel, msk = _gla_constants()
    up = jnp.zeros((LANES, GLA_KEY_DIM), F32).at[:GLA_GATE_RANK].set(gk_up).astype(BF16)
    kern = functools.partial(_gla_kernel, n_chunks=tt // GLA_CHUNK)
    return pl.pallas_call(
        kern,
        grid=(H, T // tt),
        in_specs=[pl.BlockSpec((tt, dk), lambda h, i: (i, h)),
                  pl.BlockSpec((tt, dk), lambda h, i: (i, H + h)),
                  pl.BlockSpec((tt, dv), lambda h, i: (i, H + h)),
                  pl.BlockSpec((tt, dv), lambda h, i: (i, 2 * H + h)),
                  pl.BlockSpec((tt, LANES), lambda h, i: (i, 3 * GLA_VALUE_DIM // LANES)),
                  pl.BlockSpec((LANES, dk), lambda h, i: (0, h)),
                  pl.BlockSpec((1, dk), lambda h, i: (0, h)),
                  pl.BlockSpec((1, dv), lambda h, i: (0, 0)),
                  pl.BlockSpec(sel.shape, lambda h, i: (0, 0)),
                  pl.BlockSpec(msk.shape, lambda h, i: (0, 0, 0))],
        out_specs=pl.BlockSpec((tt, dv), lambda h, i: (i, h)),
        out_shape=jax.ShapeDtypeStruct((T, H * dv), BF16),
        scratch_shapes=[pltpu.VMEM((dv, dk), F32)],
        compiler_params=_cp("parallel", "arbitrary"),
        name="gla",
    )(p_gla, p_gla, p_gla, p_gla, p_gla, up, gk_bias.reshape(1, -1), norm_g.reshape(1, -1),
      jnp.asarray(sel, BF16), jnp.asarray(msk, F32))


def _head_ones():
    l = np.arange(LANES)
    return (l[:, None] // RWKV_HEAD_DIM == l[None, :] // RWKV_HEAD_DIM).astype(np.float32)


def _head_diag():
    i = np.arange(RWKV_HEAD_DIM)[:, None]
    l = np.arange(LANES)[None, :]
    return (l % RWKV_HEAD_DIM == i).astype(np.float32)


def _rwprep_kernel(p_ref, pp_ref, mu_ref, w0_ref, w2_ref, a0_ref, a2_ref, g2_ref, kk_ref, ka_ref,
                   v0_ref, v1_ref, v2_ref, vf_ref, bo_ref,
                   r_ref, w_ref, k_ref, v_ref, a_ref, b_ref, g_ref, *, first_layer):
    W = RWKV_WIDTH
    i = pl.program_id(0)
    p = p_ref[...]
    tm = p.shape[0]
    prev_row = jnp.where(i == 0, 0.0, pp_ref[SUBLANES - 1:SUBLANES, :])
    rid = lax.broadcasted_iota(I32, p.shape, 0)
    prev = jnp.where(rid == 0, prev_row, pltpu.roll(p, 1, axis=0))
    ps = p + (prev - p) * mu_ref[...]
    r = ps[:, 0:W]
    kr = ps[:, W:2 * W]
    vr = ps[:, 2 * W:3 * W]
    lr = ps[:, 3 * W:3 * W + LANES]
    gl = ps[:, 3 * W + LANES:3 * W + 2 * LANES]
    w_log = -_softplus(-(w0_ref[...] + _dot(jnp.tanh(lr).astype(BF16), w2_ref[...]))) - 0.5
    w_ref[...] = -jnp.exp(w_log)
    if not first_layer:
        lo = _dot(vr.astype(BF16), v1_ref[...])
        gate = _sigmoid(v0_ref[...] + _dot(lo.astype(BF16), v2_ref[...]))
        vr = vr + (vf_ref[...] - vr) * gate
    v_ref[...] = vr
    a = _sigmoid(a0_ref[...] + _dot(lr.astype(BF16), a2_ref[...]))
    g_ref[...] = _dot(_sigmoid(gl).astype(BF16), g2_ref[...])
    kk = kr * kk_ref[...]
    bo = bo_ref[...]
    for hp in range(W // LANES):
        sl = slice(hp * LANES, (hp + 1) * LANES)
        kh = kk[:, sl]
        ss = _dot_fx(kh * kh, bo, 3)
        kn = kh / jnp.maximum(jnp.sqrt(ss), 1e-12)
        a_ref[:, sl] = -kn
        b_ref[:, sl] = kn * a[:, sl]
    r_ref[...] = r
    k_ref[...] = kr * (1.0 + (a - 1.0) * ka_ref[...])


def _rwkv_prep(p_rw, v_first, mu, w0, w2, a0, a2, g2, k_k, k_a, v0, v1, v2, tm=256):
    T, NC = p_rw.shape
    W = RWKV_WIDTH
    first = v_first is None
    w2p = jnp.zeros((LANES, W), F32).at[:RWKV_DECAY_RANK].set(w2).astype(BF16)
    a2p = jnp.zeros((LANES, W), F32).at[RWKV_DECAY_RANK:].set(a2).astype(BF16)
    if first:
        v_first = jnp.zeros((SUBLANES, W), F32)
        v0 = jnp.zeros((W,), F32)
        v1p = jnp.zeros((W, LANES), BF16)
        v2p = jnp.zeros((LANES, W), BF16)
        vf_spec = pl.BlockSpec((SUBLANES, W), lambda i: (0, 0))
    else:
        v1p = jnp.zeros((W, LANES), F32).at[:, :RWKV_VALUE_RANK].set(v1).astype(BF16)
        v2p = jnp.zeros((LANES, W), F32).at[:RWKV_VALUE_RANK].set(v2).astype(BF16)
        vf_spec = pl.BlockSpec((tm, W), lambda i: (i, 0))
    row = lambda a: a.reshape(1, -1)
    full = lambda shape: pl.BlockSpec(shape, lambda i: (0,) * len(shape))
    out = jax.ShapeDtypeStruct((T, W), F32)
    ospec = pl.BlockSpec((tm, W), lambda i: (i, 0))
    kern = functools.partial(_rwprep_kernel, first_layer=first)
    return pl.pallas_call(
        kern,
        grid=(T // tm,),
        in_specs=[pl.BlockSpec((tm, NC), lambda i: (i, 0)),
                  pl.BlockSpec((SUBLANES, NC), lambda i: (jnp.maximum(i * (tm // SUBLANES) - 1, 0), 0)),
                  full((1, NC)), full((1, W)), full((LANES, W)), full((1, W)), full((LANES, W)),
                  full((LANES, W)), full((1, W)), full((1, W)),
                  full((1, W)), full((W, LANES)), full((LANES, W)), vf_spec, full((LANES, LANES))],
        out_specs=[ospec] * 7,
        out_shape=[out] * 7,
        compiler_params=_cp("parallel"),
        name="rwkv_prep",
    )(p_rw, p_rw, row(mu), row(w0), w2p, row(a0), a2p, g2.astype(BF16), row(k_k), row(k_a),
      row(v0), v1p, v2p, v_first, jnp.asarray(_head_ones(), BF16))


RW_CHUNK = 64


def _rw_constants():
    C = RW_CHUNK
    i = np.arange(C)[:, None]
    s = np.arange(C)[None, :]
    i2 = np.arange(2 * C)[:, None]
    s2 = np.arange(2 * C)[None, :]
    same = (i2 // C) == (s2 // C)
    cm = np.stack([same & (s2 < i2), same & (s2 <= i2), (i2 // 16) == (s2 // 16), (i2 // 32) == (s2 // 32),
                   same], 0).astype(np.float32)
    return (s <= i).astype(np.float32), cm


def _mm(a, b, dims=NN):
    a1, a2 = _split(a, 2)
    b1, b2 = _split(b, 2)
    return _dot(a1, b1, dims) + (_dot(a1, b2, dims) + _dot(a2, b1, dims))


def _mm1(a, b, dims=NN):
    return _dot(a.astype(BF16), b.astype(BF16), dims)


def _unit_lower_inverse(nms, eye, bd16, bd32):
    nd = [n * bd16 for n in nms]
    x = [eye + n for n in nd]
    p = [_mm(n, n) for n in nd]
    for step in range(3):
        x = [a + _mm(a, q) for a, q in zip(x, p)]
        if step < 2:
            p = [_mm(q, q) for q in p]
    for lo_mask in (bd32 - bd16, 1.0 - bd32):
        t = [_mm(n * lo_mask, a) for n, a in zip(nms, x)]
        x = [a + _mm(a, q) for a, q in zip(x, t)]
    return x


def _rwrec_kernel(r_ref, lw_ref, k_ref, v_ref, a_ref, b_ref, lt_ref, cm_ref, y_ref, mt_ref, *, n_chunks):
    C = RW_CHUNK
    C2 = 2 * C

    @pl.when(pl.program_id(1) == 0)
    def _():
        mt_ref[...] = jnp.zeros_like(mt_ref)

    lt = lt_ref[...]
    sl, li, bd16, bd32, bdh = cm_ref[0], cm_ref[1], cm_ref[2], cm_ref[3], cm_ref[4]
    eye = li - sl
    lane = lax.broadcasted_iota(I32, (1, LANES), 1)
    m0 = jnp.where(lane < RWKV_HEAD_DIM, 1.0, 0.0)
    m1 = 1.0 - m0
    stack = lambda x: jnp.concatenate([x * m0, x * m1], axis=0)
    fold = lambda x: x[:C] + x[C:]
    rows = [slice(c * C, (c + 1) * C) for c in range(n_chunks)]

    lw = [lw_ref[s, :] for s in rows]
    cw = [_dot_xf(lt, x, 3) for x in lw]
    en = [jnp.exp(-x) for x in cw]
    at = [a_ref[s, :] * jnp.exp(c - l) for s, c, l in zip(rows, cw, lw)]
    rt = [r_ref[s, :] * jnp.exp(c) for s, c in zip(rows, cw)]
    bh = [b_ref[s, :] * e for s, e in zip(rows, en)]
    kh = [k_ref[s, :] * e for s, e in zip(rows, en)]
    v = [v_ref[s, :] for s in rows]
    a2 = [stack(x) for x in at]
    ar2 = [jnp.concatenate([x, stack(y)], axis=0) for x, y in zip(a2, rt)]
    gb = [_mm(x, stack(y), NT) for x, y in zip(ar2, bh)]
    gk = [_mm(x, stack(y), NT) for x, y in zip(ar2, kh)]
    tinv = _unit_lower_inverse([g[:C2] * sl for g in gb], eye, bd16, bd32)
    akv = [_mm(jnp.concatenate([g[:C2] * sl, g[C2:] * li], axis=0), stack(x)) for g, x in zip(gk, v)]
    tav = [_mm(t, jnp.concatenate([x, y[:C2]], axis=1)) for t, x, y in zip(tinv, a2, akv)]
    rav = [_mm1(g[C2:] * li, x) for g, x in zip(gb, tav)]
    ta = [fold(x[:, :LANES]) for x in tav]
    u0 = [fold(x[:, LANES:]) for x in tav]
    qe = [x + fold(y[:, :LANES]) for x, y in zip(rt, rav)]
    y0 = [fold(x[:, LANES:] + y[C2:]) for x, y in zip(rav, akv)]
    dl = [jnp.exp(x[C - 1:C, :]) for x in cw]
    gmt = [(eye + _mm(x, y, TN) * bdh) * d for x, y, d in zip(ta, bh, dl)]
    hmt = [(_mm(x, y, TN) + _mm(z, w, TN)) * bdh * d for x, y, z, w, d in zip(u0, bh, v, kh, dl)]

    mt = mt_ref[...]
    for c in range(n_chunks):
        y_ref[rows[c], :] = _mm1(qe[c], mt, NT) + y0[c]
        mt = _mm(mt, gmt[c]) + hmt[c]
    mt_ref[...] = mt


def _rwkv_recurrence(r, lw, k, v, a, b, tt=256):
    T, W = r.shape
    lt, cm = _rw_constants()
    spec = pl.BlockSpec((tt, LANES), lambda p, i: (i, p))
    kern = functools.partial(_rwrec_kernel, n_chunks=tt // RW_CHUNK)
    return pl.pallas_call(
        kern,
        grid=(W // LANES, T // tt),
        in_specs=[spec] * 6 + [pl.BlockSpec(lt.shape, lambda p, i: (0, 0)),
                               pl.BlockSpec(cm.shape, lambda p, i: (0, 0, 0))],
        out_specs=spec,
        out_shape=jax.ShapeDtypeStruct((T, W), F32),
        scratch_shapes=[pltpu.VMEM((LANES, LANES), F32)],
        compiler_params=_cp("parallel", "arbitrary"),
        name="rwkv_recurrence",
    )(r, lw, k, v, a, b, jnp.asarray(lt, BF16), jnp.asarray(cm, F32))


def _rwpost_kernel(y_ref, r_ref, k_ref, v_ref, g_ref, lg_ref, lb_ref, rk_ref, bo_ref, o_ref):
    bo = bo_ref[...]
    inv = 1.0 / RWKV_HEAD_DIM
    for hp in range(RWKV_WIDTH // LANES):
        sl = slice(hp * LANES, (hp + 1) * LANES)
        y = y_ref[:, sl]
        mu = _dot_fx(y, bo, 3) * inv
        d = y - mu
        var = _dot_fx(d * d, bo, 3) * inv
        yn = d * lax.rsqrt(var + RWKV_LNX_EPS) * lg_ref[:, sl] + lb_ref[:, sl]
        bonus = _dot_fx(r_ref[:, sl] * k_ref[:, sl] * rk_ref[:, sl], bo, 3) * v_ref[:, sl]
        o_ref[:, sl] = ((yn + bonus) * g_ref[:, sl]).astype(o_ref.dtype)


def _rwkv_post(y, r, k, v, g, lnx_g, lnx_b, r_k, tm=512):
    T, W = y.shape
    spec = pl.BlockSpec((tm, W), lambda i: (i, 0))
    rowspec = pl.BlockSpec((1, W), lambda i: (0, 0))
    return pl.pallas_call(
        _rwpost_kernel,
        grid=(T // tm,),
        in_specs=[spec] * 5 + [rowspec] * 3 + [pl.BlockSpec((LANES, LANES), lambda i: (0, 0))],
        out_specs=spec,
        out_shape=jax.ShapeDtypeStruct((T, W), BF16),
        compiler_params=_cp("parallel"),
        name="rwkv_post",
    )(y, r, k, v, g, lnx_g.reshape(1, W), lnx_b.reshape(1, W), r_k.reshape(1, W),
      jnp.asarray(_head_ones(), BF16))


def _merge_kernel(osb_ref, ogl_ref, orw_ref, g1_ref, g2_ref, g3_ref, w1_ref, w2_ref, w3_ref, o_ref):
    acc = _sigmoid(g1_ref[...].astype(F32)) * _dot(osb_ref[...], w1_ref[...])
    acc = acc + _sigmoid(g2_ref[...].astype(F32)) * _dot(ogl_ref[...], w2_ref[...])
    acc = acc + _sigmoid(g3_ref[...].astype(F32)) * _dot(orw_ref[...], w3_ref[...])
    o_ref[...] = acc.astype(o_ref.dtype)


def _merge(o_sb, o_gla, o_rw, p_merge, w_sb, w_gla, w_rw, tm=512, tn=1024):
    T = o_sb.shape[0]
    D = D_MODEL
    nb = D // tn
    aspec = pl.BlockSpec((tm, 1024), lambda j, i: (i, 0))
    wspec = pl.BlockSpec((1024, tn), lambda j, i: (0, j))
    gspec = lambda b: pl.BlockSpec((tm, tn), lambda j, i: (i, b * nb + j))
    return pl.pallas_call(
        _merge_kernel,
        grid=(nb, T // tm),
        in_specs=[aspec, aspec, aspec, gspec(0), gspec(1), gspec(2), wspec, wspec, wspec],
        out_specs=pl.BlockSpec((tm, tn), lambda j, i: (i, j)),
        out_shape=jax.ShapeDtypeStruct((T, D), BF16),
        compiler_params=_cp("parallel", "parallel"),
        name="merge",
    )(o_sb, o_gla, o_rw, p_merge, p_merge, p_merge, w_sb, w_gla, w_rw)


def _layer_norm(h, g, b):
    mu = jnp.mean(h, axis=-1, keepdims=True)
    d = h - mu
    var = jnp.mean(d * d, axis=-1, keepdims=True)
    return d * lax.rsqrt(var + LN_EPS) * g + b


def _oproj_kernel(m_ref, w_ref, x_ref, gate_ref, lg_ref, lb_ref, o_ref):
    mix = _dot(m_ref[...], w_ref[...])
    h = DEEPNORM_ALPHA * x_ref[...] + (1.0 + gate_ref[...]) * mix
    o_ref[...] = _layer_norm(h, lg_ref[...], lb_ref[...])


def _oproj(merged, w_o, x, gate, ln_g, ln_b, tm=512):
    T, D = x.shape
    spec = pl.BlockSpec((tm, D), lambda i: (i, 0))
    rowspec = pl.BlockSpec((1, D), lambda i: (0, 0))
    return pl.pallas_call(
        _oproj_kernel,
        grid=(T // tm,),
        in_specs=[spec, pl.BlockSpec((D, D), lambda i: (0, 0)), spec, rowspec, rowspec, rowspec],
        out_specs=spec,
        out_shape=jax.ShapeDtypeStruct((T, D), F32),
        compiler_params=_cp("parallel"),
        name="oproj_norm",
    )(merged, w_o, x, gate, ln_g.reshape(1, D), ln_b.reshape(1, D))


def _first_max(x, idx, axis, big):
    m = jnp.max(x, axis=axis, keepdims=True)
    first = jnp.min(jnp.where(x == m, idx, big), axis=axis, keepdims=True)
    return m, idx == first


def _router_kernel(x_ref, sc_ref, sh_ref, wr_ref, rb_ref, u_ref, e_ref, g_ref, p_ref, cnt_ref, car_ref, *, tm):
    E, G = N_EXPERTS, N_GROUPS
    per = E // G
    neg = -jnp.inf

    @pl.when(pl.program_id(0) == 0)
    def _():
        car_ref[...] = jnp.zeros_like(car_ref)

    u = x_ref[...] * (1.0 + sc_ref[...]) + sh_ref[...]
    u_ref[...] = _pack_pairs(u)
    scores =_sigmoid(_dot_ff(wr_ref[...], u, NT))
    biased = scores + rb_ref[...]
    b3 = biased.reshape(G, per, tm)
    i3 = lax.broadcasted_iota(I32, (G, per, tm), 1)
    m1, hit = _first_max(b3, i3, 1, per)
    m2 = jnp.max(jnp.where(hit, neg, b3), axis=1, keepdims=True)
    gs = (m1 + m2).reshape(G, tm)
    gi = lax.broadcasted_iota(I32, (G, tm), 0)
    ok = jnp.zeros((G, tm), jnp.bool_)
    for _ in range(TOPK_GROUPS):
        _, hit = _first_max(gs, gi, 0, G)
        ok = jnp.logical_or(ok, hit)
        gs = jnp.where(hit, neg, gs)
    ok_e = jnp.broadcast_to(ok.reshape(G, 1, tm), (G, per, tm)).reshape(E, tm)
    cand = jnp.where(ok_e, biased, neg)
    ei = lax.broadcasted_iota(I32, (E, tm), 0)
    sels, gates = [], []
    for _ in range(TOP_K):
        _, hit = _first_max(cand, ei, 0, E)
        sels.append(hit)
        gates.append(jnp.sum(jnp.where(hit, scores, 0.0), axis=0, keepdims=True))
        cand = jnp.where(hit, neg, cand)
    denom = gates[0]
    for gk in gates[1:]:
        denom = denom + gk
    chosen = sels[0]
    for s in sels[1:]:
        chosen = jnp.logical_or(chosen, s)
    cnt = jnp.where(chosen, 1.0, 0.0)
    earlier = (lax.broadcasted_iota(I32, (tm, tm), 0) < lax.broadcasted_iota(I32, (tm, tm), 1)).astype(BF16)
    rank = _dot(cnt.astype(BF16), earlier) + car_ref[...][:, 0:1]
    car_ref[...] = car_ref[...] + jnp.sum(cnt, axis=1, keepdims=True)
    cnt_ref[...] = car_ref[...]
    zi = jnp.zeros((1, tm), I32)
    zf = jnp.zeros((1, tm), F32)
    e_rows = [jnp.sum(jnp.where(s, ei, 0), axis=0, keepdims=True) for s in sels]
    p_rows = [jnp.sum(jnp.where(s, rank, 0.0), axis=0, keepdims=True).astype(I32) for s in sels]
    g_rows = [gk / denom * ROUTED_SCALE for gk in gates]
    e_ref[...] = jnp.concatenate(e_rows + [zi, zi], axis=0)
    p_ref[...] = jnp.concatenate(p_rows + [zi, zi], axis=0)
    g_ref[...] = jnp.concatenate(g_rows + [zf, zf], axis=0)


def _router(x, sc, sh, w_router, router_bias, tm=256):
    T, D = x.shape
    E = N_EXPERTS
    kern = functools.partial(_router_kernel, tm=tm)
    rowspec = pl.BlockSpec((1, D), lambda i: (0, 0))
    kspec = pl.BlockSpec((SUBLANES, tm), lambda i: (0, i))
    return pl.pallas_call(
        kern,
        grid=(T // tm,),
        in_specs=[pl.BlockSpec((tm, D), lambda i: (i, 0)), rowspec, rowspec,
                  pl.BlockSpec((E, D), lambda i: (0, 0)), pl.BlockSpec((E, 1), lambda i: (0, 0))],
        out_specs=[pl.BlockSpec((tm, D // 2), lambda i: (i, 0)), kspec, kspec, kspec,
                   pl.BlockSpec((E, LANES), lambda i: (0, 0))],
        out_shape=[jax.ShapeDtypeStruct((T, D // 2), U32),
                   jax.ShapeDtypeStruct((SUBLANES, T), I32),
                   jax.ShapeDtypeStruct((SUBLANES, T), F32),
                   jax.ShapeDtypeStruct((SUBLANES, T), I32),
                   jax.ShapeDtypeStruct((E, LANES), F32)],
        scratch_shapes=[pltpu.VMEM((E, LANES), F32)],
        compiler_params=_cp("arbitrary"),
        name="router",
    )(x, sc, sh, w_router.T, router_bias.reshape(E, 1))


def _dispatch_kernel(dest_ref, fill_ref, u_ref, xs_ref, z_ref, sem, zsem, *, tm, bm):
    i = pl.program_id(0)

    @pl.when(i == 0)
    def _():
        z_ref[...] = jnp.zeros_like(z_ref)

        def zcopy(e):
            return pltpu.make_async_copy(z_ref, xs_ref.at[pl.ds(pl.multiple_of(fill_ref[e], bm), bm)], zsem)

        def zstart(e, c):
            zcopy(e).start()
            return c

        def zwait(e, c):
            zcopy(e).wait()
            return c

        lax.fori_loop(0, N_EXPERTS, zstart, 0)
        lax.fori_loop(0, N_EXPERTS, zwait, 0)

    def start(g, c):
        for t in range(ROW_UNROLL):
            for k in range(TOP_K):
                tok = g * ROW_UNROLL + t
                pltpu.make_async_copy(_row(u_ref, tok), _row(xs_ref, dest_ref[k, tok]), sem).start()
        return c

    lax.fori_loop(0, tm // ROW_UNROLL, start, 0)
    n = tm * TOP_K
    pltpu.make_async_copy(xs_ref.at[pl.ds(0, n)], xs_ref.at[pl.ds(0, n)], sem).wait()


def _dispatch(u, dest, fill, n_slots, tm=256, bm=MOE_BM):
    T, D = u.shape
    kern = functools.partial(_dispatch_kernel, tm=tm, bm=bm)
    return pl.pallas_call(
        kern,
        grid=(T // tm,),
        in_specs=[pl.BlockSpec((SUBLANES, tm), lambda i: (0, i), memory_space=pltpu.SMEM),
                  pl.BlockSpec(memory_space=pltpu.SMEM),
                  pl.BlockSpec((tm, D), lambda i: (i, 0))],
        out_specs=pl.BlockSpec(memory_space=pl.ANY),
        out_shape=jax.ShapeDtypeStruct((n_slots, D), u.dtype),
        scratch_shapes=[pltpu.VMEM((bm, D), u.dtype),
                        pltpu.SemaphoreType.DMA(()), pltpu.SemaphoreType.DMA(())],
        compiler_params=_cp("arbitrary"),
        name="moe_dispatch",
    )(dest, fill, u)


def _expert_kernel(be_ref, nv_ref, x_ref, wg_ref, wu_ref, wd_ref, y_ref, wgb_ref, wub_ref, wdb_ref, *, bm):
    b = pl.program_id(0)
    valid = b < nv_ref[0]
    new_expert = jnp.logical_or(b == 0, be_ref[b] != be_ref[jnp.maximum(b - 1, 0)])

    @pl.when(jnp.logical_and(valid, new_expert))
    def _():
        wgb_ref[...] = wg_ref[0].astype(BF16)
        wub_ref[...] = wu_ref[0].astype(BF16)
        wdb_ref[...] = wd_ref[0].astype(BF16)

    @pl.when(valid)
    def _():
        x = _unpack_pairs(x_ref[...]).astype(BF16)
        h = _silu(_dot(x, wgb_ref[...])) * _dot(x, wub_ref[...])
        y_ref[...] = _pack_pairs(_dot(h.astype(BF16), wdb_ref[...]))


def _experts(xs, block_expert, n_valid, w_gate, w_up, w_down, layer, bm=MOE_BM):
    n_slots, W = xs.shape
    D = 2 * W
    nb = n_slots // bm
    F = w_gate.shape[-1]
    blk = lambda b, be, nv: (jnp.minimum(b, nv[0] - 1), 0)
    wsel = lambda b, be, nv: (layer, be[jnp.minimum(b, nv[0] - 1)], 0, 0)
    gs = pltpu.PrefetchScalarGridSpec(
        num_scalar_prefetch=2,
        grid=(nb,),
        in_specs=[pl.BlockSpec((bm, W), blk),
                  pl.BlockSpec((None, 1, D, F), wsel),
                  pl.BlockSpec((None, 1, D, F), wsel),
                  pl.BlockSpec((None, 1, F, D), wsel)],
        out_specs=pl.BlockSpec((bm, W), blk),
        scratch_shapes=[pltpu.VMEM((D, F), BF16), pltpu.VMEM((D, F), BF16), pltpu.VMEM((F, D), BF16)],
    )
    return pl.pallas_call(
        functools.partial(_expert_kernel, bm=bm),
        grid_spec=gs,
        out_shape=jax.ShapeDtypeStruct((n_slots, W), U32),
        compiler_params=_cp("arbitrary"),
        name="moe_experts",
    )(block_expert, n_valid, xs, w_gate, w_up, w_down)


def _combine_kernel(dest_ref, dnext_ref, ys_ref, u_ref, x_ref, gt_ref, wg_ref, wu_ref, wd_ref, gate_ref, lg_ref,
                    lb_ref, o_ref, buf_ref, sem, *, tm):
    i = pl.program_id(0)
    slot = lax.rem(i, 2)

    def gather(d_ref, s):
        def start(g, c):
            for t in range(ROW_UNROLL):
                for k in range(TOP_K):
                    tok = g * ROW_UNROLL + t
                    pltpu.make_async_copy(_row(ys_ref, d_ref[k, tok]), _row(buf_ref.at[s, k], tok), sem.at[s]).start()
            return c

        lax.fori_loop(0, tm // ROW_UNROLL, start, 0)

    @pl.when(i == 0)
    def _():
        gather(dest_ref, slot)

    @pl.when(i + 1 < pl.num_programs(0))
    def _():
        gather(dnext_ref, 1 - slot)

    u = _unpack_pairs(u_ref[...]).astype(BF16)
    h = _silu(_dot(u, wg_ref[...])) * _dot(u, wu_ref[...])
    ffn = _dot(h.astype(BF16), wd_ref[...])

    pltpu.make_async_copy(buf_ref.at[slot], buf_ref.at[slot], sem.at[slot]).wait()
    gt = gt_ref[...]
    for k in range(TOP_K):
        ffn = ffn + gt[:, k:k + 1] * _unpack_pairs(buf_ref[slot, k])
    hres = DEEPNORM_ALPHA * x_ref[...] + (1.0 + gate_ref[...]) * ffn
    o_ref[...] = _layer_norm(hres, lg_ref[...], lb_ref[...])


def _combine(ys, dest, u, x, gates_t, w_sg, w_su, w_sd, gate, ln_g, ln_b, tm=128):
    T, D = x.shape
    W = ys.shape[1]
    F = w_sg.shape[-1]
    n = T // tm
    kern = functools.partial(_combine_kernel, tm=tm)
    spec = pl.BlockSpec((tm, D), lambda i: (i, 0))
    rowspec = pl.BlockSpec((1, D), lambda i: (0, 0))
    return pl.pallas_call(
        kern,
        grid=(n,),
        in_specs=[pl.BlockSpec((SUBLANES, tm), lambda i: (0, i), memory_space=pltpu.SMEM),
                  pl.BlockSpec((SUBLANES, tm), lambda i: (0, jnp.minimum(i + 1, n - 1)), memory_space=pltpu.SMEM),
                  pl.BlockSpec(memory_space=pl.ANY),
                  pl.BlockSpec((tm, W), lambda i: (i, 0)), spec,
                  pl.BlockSpec((tm, SUBLANES), lambda i: (i, 0)),
                  pl.BlockSpec((D, F), lambda i: (0, 0)),
                  pl.BlockSpec((D, F), lambda i: (0, 0)),
                  pl.BlockSpec((F, D), lambda i: (0, 0)),
                  rowspec, rowspec, rowspec],
        out_specs=spec,
        out_shape=jax.ShapeDtypeStruct((T, D), F32),
        scratch_shapes=[pltpu.VMEM((2, TOP_K, tm, W), ys.dtype), pltpu.SemaphoreType.DMA((2,))],
        compiler_params=_cp("arbitrary"),
        name="moe_combine_norm",
    )(dest, dest, ys, u, x, gates_t, w_sg, w_su, w_sd, gate, ln_g.reshape(1, D), ln_b.reshape(1, D))


def _moe_plan(top_e, rank, counts, bm, n_blocks):
    counts = counts.astype(I32)
    padded = (counts + bm - 1) // bm * bm
    padded_end = jnp.cumsum(padded)
    padded_start = padded_end - padded
    onehot = top_e[:, :, None] == jnp.arange(N_EXPERTS, dtype=I32)
    dest = jnp.sum(jnp.where(onehot, padded_start, 0), axis=-1) + rank
    fill = jnp.maximum(padded_end - bm, padded_start)
    fill = jnp.minimum(fill, (n_blocks - 1) * bm)
    block_start = jnp.arange(n_blocks, dtype=I32) * bm
    block_expert = jnp.minimum(jnp.sum((padded_end[None, :] <= block_start[:, None]).astype(I32), axis=1),
                               N_EXPERTS - 1)
    n_valid = jnp.maximum(padded_end[-1:] // bm, 1).astype(I32)
    return dest.astype(I32), fill.astype(I32), block_expert, n_valid


def _in_proj_weights(w_in):
    o = SB_COLS
    gq = w_in[:, o:o + 2 * GLA_KEY_DIM + GLA_VALUE_DIM]
    o += 2 * GLA_KEY_DIM + GLA_VALUE_DIM
    glr = w_in[:, o:o + GLA_GATE_RANK]
    o += GLA_GATE_RANK
    gout = w_in[:, o:o + GLA_VALUE_DIM]
    o += GLA_VALUE_DIM
    pad = jnp.zeros((w_in.shape[0], LANES - GLA_GATE_RANK), w_in.dtype)
    w_gla = jnp.concatenate([gq, gout, glr, pad], axis=1)
    w_rw = w_in[:, o:o + RWKV_COLS]
    o += RWKV_COLS
    w_mg = w_in[:, o:o + MERGE_COLS]
    return w_in[:, :SB_COLS], w_gla, w_rw, w_mg


def kernel(x, c, w_ada, b_ada, w_in, gla_gk_up, gla_gk_bias, gla_norm_g, rw_mu, rw_w0, rw_w2, rw_a0,
           rw_a2, rw_g2, rw_k_k, rw_k_a, rw_r_k, rw_lnx_g, rw_lnx_b, rw_v0, rw_v1, rw_v2,
           w_br_sb, w_br_gla, w_br_rw, w_o, ln_g, ln_b, w_router, router_bias,
           w_exp_gate, w_exp_up, w_exp_down, w_sh_gate, w_sh_up, w_sh_down):
    B, T, D = x.shape
    assert B == 1 and D == D_MODEL
    n_blocks = -(-(T * TOP_K) // MOE_BM) + N_EXPERTS
    mod = _ada(c, w_ada, b_ada)
    xs = x.reshape(T, D)
    v_first = None
    for l in range(DEPTH):
        sh1, sc1, g1, sh2, sc2, g2 = (mod[l, j * D:(j + 1) * D].reshape(1, D) for j in range(6))
        w_sb, w_gla, w_rw, w_mg = _in_proj_weights(w_in[l])
        u1 = _modulate(xs, sc1, sh1)
        p_sb = _wmm(u1, w_sb, BF16, 1024, name="inproj_sb")
        p_gla = _wmm(u1, w_gla, F32, 640, name="inproj_gla")
        p_rw = _wmm(u1, w_rw, F32, 832 * 2, name="inproj_rwkv")
        p_mg = _wmm(u1, w_mg, BF16, 1024, name="inproj_merge")

        o_sb = _sb_attention(p_sb)
        o_gla = _gla(p_gla, gla_gk_up[l], gla_gk_bias[l], gla_norm_g[l])
        if l == 0:
            r, w, k, v, a, b, g = _rwkv_prep(p_rw, None, rw_mu[l], rw_w0[l], rw_w2[l], rw_a0[l], rw_a2[l],
                                             rw_g2[l], rw_k_k[l], rw_k_a[l], None, None, None)
            v_first = v
        else:
            r, w, k, v, a, b, g = _rwkv_prep(p_rw, v_first, rw_mu[l], rw_w0[l], rw_w2[l], rw_a0[l], rw_a2[l],
                                             rw_g2[l], rw_k_k[l], rw_k_a[l], rw_v0[l - 1], rw_v1[l - 1],
                                             rw_v2[l - 1])
        y = _rwkv_recurrence(r, w, k, v, a, b)
        o_rw = _rwkv_post(y, r, k, v, g, rw_lnx_g[l], rw_lnx_b[l], rw_r_k[l])

        merged = _merge(o_sb, o_gla, o_rw, p_mg, w_br_sb[l].astype(BF16), w_br_gla[l].astype(BF16),
                        w_br_rw[l].astype(BF16))
        xs = _oproj(merged, w_o[l].astype(BF16), xs, g1, ln_g[l, 0], ln_b[l, 0])

        u2, top_e, gates, rank, counts = _router(xs, sc2, sh2, w_router[l], router_bias[l])
        dest, fill, block_expert, n_valid = _moe_plan(top_e, rank, counts[:, 0], MOE_BM, n_blocks)
        x_sorted = _dispatch(u2, dest, fill, n_blocks * MOE_BM)
        y_sorted = _experts(x_sorted, block_expert, n_valid, w_exp_gate, w_exp_up, w_exp_down, l)
        xs = _combine(y_sorted, dest, u2, xs, gates.T, w_sh_gate[l].astype(BF16), w_sh_up[l].astype(BF16),
                      w_sh_down[l].astype(BF16), g2, ln_g[l, 1], ln_b[l, 1])
    return xs.reshape(B, T, D)
```

```python
import functools

import numpy as np
import jax
import jax.numpy as jnp
from jax import lax
from jax.experimental import pallas as pl
from jax.experimental.pallas import tpu as pltpu

F32 = jnp.float32
BF16 = jnp.bfloat16
I32 = jnp.int32
U32 = jnp.uint32

D_MODEL = 2048
DEPTH = 2
LN_EPS = 1e-5
DEEPNORM_ALPHA = (2 * DEPTH) ** 0.25

SB_HEADS = 8
SB_HEAD_DIM = 128
SB_WIDTH = SB_HEADS * SB_HEAD_DIM

GLA_HEADS = 4
GLA_HEAD_K = 128
GLA_HEAD_V = 256
GLA_KEY_DIM = GLA_HEADS * GLA_HEAD_K
GLA_VALUE_DIM = GLA_HEADS * GLA_HEAD_V
GLA_GATE_RANK = 16
GLA_GATE_NORMALIZER = 16.0
GLA_CHUNK = 64
GLA_NORM_EPS = 1e-5

RWKV_HEADS = 16
RWKV_HEAD_DIM = 64
RWKV_WIDTH = RWKV_HEADS * RWKV_HEAD_DIM
RWKV_DECAY_RANK = 64
RWKV_AAA_RANK = 64
RWKV_VALUE_RANK = 32
RWKV_GATE_RANK = 128
RWKV_LNX_EPS = 64e-5

N_EXPERTS = 64
TOP_K = 6
EXPERT_DIM = 512
SHARED_DIM = 512
N_GROUPS = 8
TOPK_GROUPS = 4
ROUTED_SCALE = 2.5

SB_COLS = 3 * SB_WIDTH
GLA_COLS = 2 * GLA_KEY_DIM + GLA_VALUE_DIM + GLA_GATE_RANK + GLA_VALUE_DIM
RWKV_COLS = 3 * RWKV_WIDTH + RWKV_DECAY_RANK + RWKV_AAA_RANK + RWKV_GATE_RANK
MERGE_COLS = 3 * D_MODEL

LANES = 128
SUBLANES = 8
VMEM_LIMIT = 56 * 1024 * 1024
MOE_BM = 512
SB_EXIT = -104.0
SB_GROUPS = 2
ROW_UNROLL = 4

NN = (((1,), (0,)), ((), ()))
NT = (((1,), (1,)), ((), ()))
TN = (((0,), (0,)), ((), ()))


def _cp(*sem):
    return pltpu.CompilerParams(dimension_semantics=sem, vmem_limit_bytes=VMEM_LIMIT)


def _dot(a, b, dims=NN):
    return lax.dot_general(a, b, dims, preferred_element_type=F32)


def _split(a, n):
    out = []
    r = a
    for i in range(n):
        p = r.astype(BF16)
        out.append(p)
        if i + 1 < n:
            r = r - p.astype(F32)
    return out


def _dot_fx(a, e, n, dims=NN):
    acc = None
    for p in _split(a, n):
        t = _dot(p, e, dims)
        acc = t if acc is None else acc + t
    return acc


def _dot_xf(e, b, n, dims=NN):
    acc = None
    for p in _split(b, n):
        t = _dot(e, p, dims)
        acc = t if acc is None else acc + t
    return acc


def _dot_ff(a, b, dims=NN):
    a1, a2, a3 = _split(a, 3)
    b1, b2, b3 = _split(b, 3)
    acc = _dot(a1, b1, dims)
    for p, q in ((a1, b2), (a2, b1), (a2, b2), (a1, b3), (a3, b1)):
        acc = acc + _dot(p, q, dims)
    return acc


def _sigmoid(x):
    return 1.0 / (1.0 + jnp.exp(-x))


def _softplus(x):
    return jnp.maximum(x, 0.0) + jnp.log1p(jnp.exp(-jnp.abs(x)))


def _silu(x):
    return x * _sigmoid(x)


def _row(ref, t):
    return ref.at[pl.ds(t, 1)]


def _pack_pairs(x):
    w = x.shape[1] // 2
    bits = lax.bitcast_convert_type(x.astype(BF16).astype(F32), U32)
    return (bits[:, :w] >> 16) | bits[:, w:]


def _unpack_pairs(p):
    lo = lax.bitcast_convert_type(p << 16, F32)
    hi = lax.bitcast_convert_type(p & jnp.uint32(0xFFFF0000), F32)
    return jnp.concatenate([lo, hi], axis=1)


def _ada_kernel(c_ref, w_ref, b_ref, o_ref):
    c = c_ref[...]
    o_ref[0] = _mm(_silu(c), w_ref[0]) + b_ref[0]


def _ada(c, w_ada, b_ada):
    L, D, N = w_ada.shape
    tn = 512
    c8 = jnp.broadcast_to(c, (SUBLANES, D))
    out = pl.pallas_call(
        _ada_kernel,
        grid=(L, N // tn),
        in_specs=[pl.BlockSpec((SUBLANES, D), lambda l, j: (0, 0)),
                  pl.BlockSpec((1, D, tn), lambda l, j: (l, 0, j)),
                  pl.BlockSpec((1, 1, tn), lambda l, j: (l, 0, j))],
        out_specs=pl.BlockSpec((1, SUBLANES, tn), lambda l, j: (l, 0, j)),
        out_shape=jax.ShapeDtypeStruct((L, SUBLANES, N), F32),
        compiler_params=_cp("parallel", "parallel"),
        name="ada",
    )(c8, w_ada, b_ada.reshape(L, 1, N))
    return out[:, 0, :]


def _modulate_kernel(x_ref, sc_ref, sh_ref, o_ref):
    o_ref[...] = (x_ref[...] * (1.0 + sc_ref[...]) + sh_ref[...]).astype(o_ref.dtype)


def _modulate(x, sc, sh, tm=1024):
    T, D = x.shape
    spec = pl.BlockSpec((tm, D), lambda i: (i, 0))
    rowspec = pl.BlockSpec((1, D), lambda i: (0, 0))
    return pl.pallas_call(
        _modulate_kernel,
        grid=(T // tm,),
        in_specs=[spec, rowspec, rowspec],
        out_specs=spec,
        out_shape=jax.ShapeDtypeStruct((T, D), BF16),
        compiler_params=_cp("parallel"),
        name="modulate",
    )(x, sc, sh)


def _wmm_kernel(u_ref, w_ref, o_ref, wb_ref):
    @pl.when(pl.program_id(1) == 0)
    def _():
        wb_ref[...] = w_ref[...].astype(BF16)

    o_ref[...] = _dot(u_ref[...], wb_ref[...]).astype(o_ref.dtype)


def _wmm(u, w, out_dtype, tn, tm=512, name="wmm"):
    T, D = u.shape
    N = w.shape[1]
    return pl.pallas_call(
        _wmm_kernel,
        grid=(N // tn, T // tm),
        in_specs=[pl.BlockSpec((tm, D), lambda j, i: (i, 0)),
                  pl.BlockSpec((D, tn), lambda j, i: (0, j))],
        out_specs=pl.BlockSpec((tm, tn), lambda j, i: (i, j)),
        out_shape=jax.ShapeDtypeStruct((T, N), out_dtype),
        scratch_shapes=[pltpu.VMEM((D, tn), BF16)],
        compiler_params=_cp("parallel", "arbitrary"),
        name=name,
    )(u, w)


def _sb_kernel(q_ref, k_ref, v_ref, o_ref, acc_ref, car_ref, *, bq, scale):
    i = pl.program_id(1)
    q = q_ref[...]
    r_id = lax.broadcasted_iota(I32, (bq, bq), 0)
    c_id = lax.broadcasted_iota(I32, (bq, bq), 1)
    later = (r_id > c_id).astype(BF16)
    causal = c_id < r_id

    def block(j, carry, diagonal):
        off = pl.multiple_of(j * bq, bq)
        kb = k_ref[pl.ds(off, bq), :]
        vb = v_ref[pl.ds(off, bq), :]
        grp = [slice(g * (bq // SB_GROUPS), (g + 1) * (bq // SB_GROUPS)) for g in range(SB_GROUPS)]
        z = [_dot(q[s], kb, NT) * scale for s in grp]
        drop = [jnp.maximum(t, 0.0) + jnp.log(1.0 + jnp.exp(-jnp.abs(t))) for t in z]
        if diagonal:
            drop = [jnp.where(causal[s], d, 0.0) for s, d in zip(grp, drop)]
        after = [_dot_fx(d, later, 2) + carry[s] for s, d in zip(grp, drop)]
        w = [jnp.exp(t - d - a) for t, d, a in zip(z, drop, after)]
        if diagonal:
            w = [jnp.where(causal[s], x, 0.0) for s, x in zip(grp, w)]
        pv = jnp.concatenate([_dot(x.astype(BF16), vb) for x in w], axis=0)
        new_carry = jnp.concatenate([a[:, 0:1] + d[:, 0:1] for a, d in zip(after, drop)], axis=0)
        return pv, new_carry

    pv, carry = block(i, jnp.zeros((bq, 1), F32), True)
    acc_ref[...] = pv
    car_ref[...] = carry

    def body(c):
        j, _ = c
        pv, carry = block(j, car_ref[...], False)
        acc_ref[...] += pv
        car_ref[...] = carry
        return j - 1, (jnp.min(carry) > -SB_EXIT).astype(I32)

    lax.while_loop(lambda c: jnp.logical_and(c[0] >= 0, c[1] == 0), body, (i - 1, jnp.int32(0)))
    o_ref[...] = acc_ref[...].astype(o_ref.dtype)


def _sb_attention(qkv, bq=256):
    T = qkv.shape[0]
    H, Dh = SB_HEADS, SB_HEAD_DIM
    kern = functools.partial(_sb_kernel, bq=bq, scale=Dh ** -0.5)
    return pl.pallas_call(
        kern,
        grid=(H, T // bq),
        in_specs=[pl.BlockSpec((bq, Dh), lambda h, i: (i, h)),
                  pl.BlockSpec((T, Dh), lambda h, i: (0, H + h)),
                  pl.BlockSpec((T, Dh), lambda h, i: (0, 2 * H + h))],
        out_specs=pl.BlockSpec((bq, Dh), lambda h, i: (i, h)),
        out_shape=jax.ShapeDtypeStruct((T, H * Dh), BF16),
        scratch_shapes=[pltpu.VMEM((bq, Dh), F32), pltpu.VMEM((bq, 1), F32)],
        compiler_params=_cp("parallel", "parallel"),
        name="sb_attention",
    )(qkv, qkv, qkv)


_GLA_LEVELS = (32, 16, 8, 4, 2, 1)


def _gla_constants():
    C = GLA_CHUNK
    i = np.arange(C)[:, None]
    s = np.arange(C)[None, :]
    sel = []
    masks = []
    for m in _GLA_LEVELS:
        same = (i // m) == (s // m)
        sel.append(same & (s <= i) & ((i // m) % 2 == 1))
        sel.append(same & (s > i) & ((i // m) % 2 == 0))
        masks.append(((i // (2 * m)) == (s // (2 * m))) & ((i // m) % 2 == 1) & ((s // m) % 2 == 0))
    sel.append(s <= i)
    sel.append(s > i)
    masks.append(i == s)
    return (np.concatenate(sel, 0).astype(np.float32), np.stack(masks, 0).astype(np.float32))


def _gla_kernel(q_ref, k_ref, v_ref, go_ref, glr_ref, up_ref, bias_ref, ng_ref, sel_ref, msk_ref,
                o_ref, st_ref, *, n_chunks):
    C = GLA_CHUNK
    nl = len(_GLA_LEVELS)

    @pl.when(pl.program_id(1) == 0)
    def _():
        st_ref[...] = jnp.zeros_like(st_ref)

    sel = sel_ref[...]
    up = up_ref[...]
    bias = bias_ref[...]
    ng = ng_ref[...]
    rows = [slice(c * C, (c + 1) * C) for c in range(n_chunks)]
    blk = lambda f, j: f[j * C:(j + 1) * C]
    q = [q_ref[s, :] * (GLA_HEAD_K ** -0.5) for s in rows]
    k = [k_ref[s, :] for s in rows]
    v = [v_ref[s, :].astype(BF16) for s in rows]
    x = [_dot(glr_ref[s, :].astype(BF16), up) + bias for s in rows]
    g = [-_softplus(-t) * (1.0 / GLA_GATE_NORMALIZER) for t in x]
    f = [jnp.exp(_dot_xf(sel, t, 2)) for t in g]
    scores = [_dot(a.astype(BF16), b.astype(BF16), NT) * msk_ref[nl] for a, b in zip(q, k)]
    for l in range(nl):
        ql = [(a * blk(t, 2 * l)).astype(BF16) for a, t in zip(q, f)]
        kl = [(b * blk(t, 2 * l + 1)).astype(BF16) for b, t in zip(k, f)]
        scores = [s + _dot(a, b, NT) * msk_ref[l] for s, a, b in zip(scores, ql, kl)]
    o_intra = [_dot(s.astype(BF16), b) for s, b in zip(scores, v)]
    qe = [(a * blk(t, 2 * nl)).astype(BF16) for a, t in zip(q, f)]
    upd = [_dot(b, (a * blk(t, 2 * nl + 1)).astype(BF16), TN) for b, a, t in zip(v, k, f)]
    dec = [blk(t, 2 * nl)[C - 1:C, :] for t in f]

    st = st_ref[...]
    for c in range(n_chunks):
        o = _dot(qe[c], st.astype(BF16), NT) + o_intra[c]
        st = st * dec[c] + upd[c]
        rms = lax.rsqrt(jnp.mean(o * o, axis=-1, keepdims=True) + GLA_NORM_EPS)
        o_ref[rows[c], :] = (o * rms * ng * _silu(go_ref[rows[c], :])).astype(o_ref.dtype)
    st_ref[...] = st


def _gla(p_gla, gk_up, gk_bias, norm_g, tt=512):
    T = p_gla.shape[0]
    H, dk, dv = GLA_HEADS, GLA_HEAD_K, GLA_HEAD_V
    sel, msk = _gla_constants()
    up = jnp.zeros((LANES, GLA_KEY_DIM), F32).at[:GLA_GATE_RANK].set(gk_up).astype(BF16)
    kern = functools.partial(_gla_kernel, n_chunks=tt // GLA_CHUNK)
    return pl.pallas_call(
        kern,
        grid=(H, T // tt),
        in_specs=[pl.BlockSpec((tt, dk), lambda h, i: (i, h)),
                  pl.BlockSpec((tt, dk), lambda h, i: (i, H + h)),
                  pl.BlockSpec((tt, dv), lambda h, i: (i, H + h)),
                  pl.BlockSpec((tt, dv), lambda h, i: (i, 2 * H + h)),
                  pl.BlockSpec((tt, LANES), lambda h, i: (i, 3 * GLA_VALUE_DIM // LANES)),
                  pl.BlockSpec((LANES, dk), lambda h, i: (0, h)),
                  pl.BlockSpec((1, dk), lambda h, i: (0, h)),
                  pl.BlockSpec((1, dv), lambda h, i: (0, 0)),
                  pl.BlockSpec(sel.shape, lambda h, i: (0, 0)),
                  pl.BlockSpec(msk.shape, lambda h, i: (0, 0, 0))],
        out_specs=pl.BlockSpec((tt, dv), lambda h, i: (i, h)),
        out_shape=jax.ShapeDtypeStruct((T, H * dv), BF16),
        scratch_shapes=[pltpu.VMEM((dv, dk), F32)],
        compiler_params=_cp("parallel", "arbitrary"),
        name="gla",
    )(p_gla, p_gla, p_gla, p_gla, p_gla, up, gk_bias.reshape(1, -1), norm_g.reshape(1, -1),
      jnp.asarray(sel, BF16), jnp.asarray(msk, F32))


def _head_ones():
    l = np.arange(LANES)
    return (l[:, None] // RWKV_HEAD_DIM == l[None, :] // RWKV_HEAD_DIM).astype(np.float32)


def _head_diag():
    i = np.arange(RWKV_HEAD_DIM)[:, None]
    l = np.arange(LANES)[None, :]
    return (l % RWKV_HEAD_DIM == i).astype(np.float32)


def _rwprep_kernel(p_ref, pp_ref, mu_ref, w0_ref, w2_ref, a0_ref, a2_ref, g2_ref, kk_ref, ka_ref,
                   v0_ref, v1_ref, v2_ref, vf_ref, bo_ref,
                   r_ref, w_ref, k_ref, v_ref, a_ref, b_ref, g_ref, *, first_layer):
    W = RWKV_WIDTH
    i = pl.program_id(0)
    p = p_ref[...]
    tm = p.shape[0]
    prev_row = jnp.where(i == 0, 0.0, pp_ref[SUBLANES - 1:SUBLANES, :])
    rid = lax.broadcasted_iota(I32, p.shape, 0)
    prev = jnp.where(rid == 0, prev_row, pltpu.roll(p, 1, axis=0))
    ps = p + (prev - p) * mu_ref[...]
    r = ps[:, 0:W]
    kr = ps[:, W:2 * W]
    vr = ps[:, 2 * W:3 * W]
    lr = ps[:, 3 * W:3 * W + LANES]
    gl = ps[:, 3 * W + LANES:3 * W + 2 * LANES]
    w_log = -_softplus(-(w0_ref[...] + _dot(jnp.tanh(lr).astype(BF16), w2_ref[...]))) - 0.5
    w_ref[...] = -jnp.exp(w_log)
    if not first_layer:
        lo = _dot(vr.astype(BF16), v1_ref[...])
        gate = _sigmoid(v0_ref[...] + _dot(lo.astype(BF16), v2_ref[...]))
        vr = vr + (vf_ref[...] - vr) * gate
    v_ref[...] = vr
    a = _sigmoid(a0_ref[...] + _dot(lr.astype(BF16), a2_ref[...]))
    g_ref[...] = _dot(_sigmoid(gl).astype(BF16), g2_ref[...])
    kk = kr * kk_ref[...]
    bo = bo_ref[...]
    for hp in range(W // LANES):
        sl = slice(hp * LANES, (hp + 1) * LANES)
        kh = kk[:, sl]
        ss = _dot_fx(kh * kh, bo, 3)
        kn = kh / jnp.maximum(jnp.sqrt(ss), 1e-12)
        a_ref[:, sl] = -kn
        b_ref[:, sl] = kn * a[:, sl]
    r_ref[...] = r
    k_ref[...] = kr * (1.0 + (a - 1.0) * ka_ref[...])


def _rwkv_prep(p_rw, v_first, mu, w0, w2, a0, a2, g2, k_k, k_a, v0, v1, v2, tm=256):
    T, NC = p_rw.shape
    W = RWKV_WIDTH
    first = v_first is None
    w2p = jnp.zeros((LANES, W), F32).at[:RWKV_DECAY_RANK].set(w2).astype(BF16)
    a2p = jnp.zeros((LANES, W), F32).at[RWKV_DECAY_RANK:].set(a2).astype(BF16)
    if first:
        v_first = jnp.zeros((SUBLANES, W), F32)
        v0 = jnp.zeros((W,), F32)
        v1p = jnp.zeros((W, LANES), BF16)
        v2p = jnp.zeros((LANES, W), BF16)
        vf_spec = pl.BlockSpec((SUBLANES, W), lambda i: (0, 0))
    else:
        v1p = jnp.zeros((W, LANES), F32).at[:, :RWKV_VALUE_RANK].set(v1).astype(BF16)
        v2p = jnp.zeros((LANES, W), F32).at[:RWKV_VALUE_RANK].set(v2).astype(BF16)
        vf_spec = pl.BlockSpec((tm, W), lambda i: (i, 0))
    row = lambda a: a.reshape(1, -1)
    full = lambda shape: pl.BlockSpec(shape, lambda i: (0,) * len(shape))
    out = jax.ShapeDtypeStruct((T, W), F32)
    ospec = pl.BlockSpec((tm, W), lambda i: (i, 0))
    kern = functools.partial(_rwprep_kernel, first_layer=first)
    return pl.pallas_call(
        kern,
        grid=(T // tm,),
        in_specs=[pl.BlockSpec((tm, NC), lambda i: (i, 0)),
                  pl.BlockSpec((SUBLANES, NC), lambda i: (jnp.maximum(i * (tm // SUBLANES) - 1, 0), 0)),
                  full((1, NC)), full((1, W)), full((LANES, W)), full((1, W)), full((LANES, W)),
                  full((LANES, W)), full((1, W)), full((1, W)),
                  full((1, W)), full((W, LANES)), full((LANES, W)), vf_spec, full((LANES, LANES))],
        out_specs=[ospec] * 7,
        out_shape=[out] * 7,
        compiler_params=_cp("parallel"),
        name="rwkv_prep",
    )(p_rw, p_rw, row(mu), row(w0), w2p, row(a0), a2p, g2.astype(BF16), row(k_k), row(k_a),
      row(v0), v1p, v2p, v_first, jnp.asarray(_head_ones(), BF16))


RW_CHUNK = 64


def _rw_constants():
    C = RW_CHUNK
    i = np.arange(C)[:, None]
    s = np.arange(C)[None, :]
    i2 = np.arange(2 * C)[:, None]
    s2 = np.arange(2 * C)[None, :]
    same = (i2 // C) == (s2 // C)
    cm = np.stack([same & (s2 < i2), same & (s2 <= i2), (i2 // 16) == (s2 // 16), (i2 // 32) == (s2 // 32),
                   same], 0).astype(np.float32)
    return (s <= i).astype(np.float32), cm


def _mm(a, b, dims=NN):
    a1, a2 = _split(a, 2)
    b1, b2 = _split(b, 2)
    r = _dot(a1, jnp.concatenate([b1, b2], axis=0 if dims == NT else 1), dims)
    n = r.shape[1] // 2
    return (r[:, :n] + r[:, n:]) + _dot(a2, b1, dims)


def _mm1(a, b, dims=NN):
    return _dot(a.astype(BF16), b.astype(BF16), dims)


def _unit_lower_inverse(nms, eye, bd16, bd32):
    nd = [n * bd16 for n in nms]
    x = [eye + n for n in nd]
    p = [_mm(n, n) for n in nd]
    for step in range(3):
        x = [a + _mm(a, q) for a, q in zip(x, p)]
        if step < 2:
            p = [_mm(q, q) for q in p]
    for lo_mask in (bd32 - bd16, 1.0 - bd32):
        t = [_mm(n * lo_mask, a) for n, a in zip(nms, x)]
        x = [a + _mm(a, q) for a, q in zip(x, t)]
    return x


def _rwrec_kernel(r_ref, lw_ref, k_ref, v_ref, a_ref, b_ref, lt_ref, cm_ref, y_ref, mt_ref, *, n_chunks):
    C = RW_CHUNK
    C2 = 2 * C

    @pl.when(pl.program_id(1) == 0)
    def _():
        mt_ref[...] = jnp.zeros_like(mt_ref)

    lt = lt_ref[...]
    sl, li, bd16, bd32, bdh = cm_ref[0], cm_ref[1], cm_ref[2], cm_ref[3], cm_ref[4]
    eye = li - sl
    lane = lax.broadcasted_iota(I32, (1, LANES), 1)
    m0 = jnp.where(lane < RWKV_HEAD_DIM, 1.0, 0.0)
    m1 = 1.0 - m0
    stack = lambda x: jnp.concatenate([x * m0, x * m1], axis=0)
    fold = lambda x: x[:C] + x[C:]
    rows = [slice(c * C, (c + 1) * C) for c in range(n_chunks)]

    lw = [lw_ref[s, :] for s in rows]
    cw = [_dot_xf(lt, x, 3) for x in lw]
    en = [jnp.exp(-x) for x in cw]
    at = [a_ref[s, :] * jnp.exp(c - l) for s, c, l in zip(rows, cw, lw)]
    rt = [r_ref[s, :] * jnp.exp(c) for s, c in zip(rows, cw)]
    bh = [b_ref[s, :] * e for s, e in zip(rows, en)]
    kh = [k_ref[s, :] * e for s, e in zip(rows, en)]
    v = [v_ref[s, :] for s in rows]
    a2 = [stack(x) for x in at]
    ar2 = [jnp.concatenate([x, stack(y)], axis=0) for x, y in zip(a2, rt)]
    gb = [_mm(x, stack(y), NT) for x, y in zip(ar2, bh)]
    gk = [_mm(x, stack(y), NT) for x, y in zip(ar2, kh)]
    tinv = _unit_lower_inverse([g[:C2] * sl for g in gb], eye, bd16, bd32)
    akv = [_mm(jnp.concatenate([g[:C2] * sl, g[C2:] * li], axis=0), stack(x)) for g, x in zip(gk, v)]
    tav = [_mm(t, jnp.concatenate([x, y[:C2]], axis=1)) for t, x, y in zip(tinv, a2, akv)]
    rav = [_mm1(g[C2:] * li, x) for g, x in zip(gb, tav)]
    ta = [fold(x[:, :LANES]) for x in tav]
    u0 = [fold(x[:, LANES:]) for x in tav]
    qe = [x + fold(y[:, :LANES]) for x, y in zip(rt, rav)]
    y0 = [fold(x[:, LANES:] + y[C2:]) for x, y in zip(rav, akv)]
    dl = [jnp.exp(x[C - 1:C, :]) for x in cw]
    gmt = [(eye + _mm(x, y, TN) * bdh) * d for x, y, d in zip(ta, bh, dl)]
    hmt = [(_mm(x, y, TN) + _mm(z, w, TN)) * bdh * d for x, y, z, w, d in zip(u0, bh, v, kh, dl)]

    mt = mt_ref[...]
    for c in range(n_chunks):
        y_ref[rows[c], :] = _mm1(qe[c], mt, NT) + y0[c]
        mt = _mm(mt, gmt[c]) + hmt[c]
    mt_ref[...] = mt


def _rwkv_recurrence(r, lw, k, v, a, b, tt=512):
    T, W = r.shape
    lt, cm = _rw_constants()
    spec = pl.BlockSpec((tt, LANES), lambda p, i: (i, p))
    kern = functools.partial(_rwrec_kernel, n_chunks=tt // RW_CHUNK)
    return pl.pallas_call(
        kern,
        grid=(W // LANES, T // tt),
        in_specs=[spec] * 6 + [pl.BlockSpec(lt.shape, lambda p, i: (0, 0)),
                               pl.BlockSpec(cm.shape, lambda p, i: (0, 0, 0))],
        out_specs=spec,
        out_shape=jax.ShapeDtypeStruct((T, W), F32),
        scratch_shapes=[pltpu.VMEM((LANES, LANES), F32)],
        compiler_params=_cp("parallel", "arbitrary"),
        name="rwkv_recurrence",
    )(r, lw, k, v, a, b, jnp.asarray(lt, BF16), jnp.asarray(cm, F32))


def _rwpost_kernel(y_ref, r_ref, k_ref, v_ref, g_ref, lg_ref, lb_ref, rk_ref, bo_ref, o_ref):
    bo = bo_ref[...]
    inv = 1.0 / RWKV_HEAD_DIM
    for hp in range(RWKV_WIDTH // LANES):
        sl = slice(hp * LANES, (hp + 1) * LANES)
        y = y_ref[:, sl]
        mu = _dot_fx(y, bo, 3) * inv
        d = y - mu
        var = _dot_fx(d * d, bo, 3) * inv
        yn = d * lax.rsqrt(var + RWKV_LNX_EPS) * lg_ref[:, sl] + lb_ref[:, sl]
        bonus = _dot_fx(r_ref[:, sl] * k_ref[:, sl] * rk_ref[:, sl], bo, 3) * v_ref[:, sl]
        o_ref[:, sl] = ((yn + bonus) * g_ref[:, sl]).astype(o_ref.dtype)


def _rwkv_post(y, r, k, v, g, lnx_g, lnx_b, r_k, tm=512):
    T, W = y.shape
    spec = pl.BlockSpec((tm, W), lambda i: (i, 0))
    rowspec = pl.BlockSpec((1, W), lambda i: (0, 0))
    return pl.pallas_call(
        _rwpost_kernel,
        grid=(T // tm,),
        in_specs=[spec] * 5 + [rowspec] * 3 + [pl.BlockSpec((LANES, LANES), lambda i: (0, 0))],
        out_specs=spec,
        out_shape=jax.ShapeDtypeStruct((T, W), BF16),
        compiler_params=_cp("parallel"),
        name="rwkv_post",
    )(y, r, k, v, g, lnx_g.reshape(1, W), lnx_b.reshape(1, W), r_k.reshape(1, W),
      jnp.asarray(_head_ones(), BF16))


def _merge_kernel(osb_ref, ogl_ref, orw_ref, g1_ref, g2_ref, g3_ref, w1_ref, w2_ref, w3_ref, o_ref):
    acc = _sigmoid(g1_ref[...].astype(F32)) * _dot(osb_ref[...], w1_ref[...])
    acc = acc + _sigmoid(g2_ref[...].astype(F32)) * _dot(ogl_ref[...], w2_ref[...])
    acc = acc + _sigmoid(g3_ref[...].astype(F32)) * _dot(orw_ref[...], w3_ref[...])
    o_ref[...] = acc.astype(o_ref.dtype)


def _merge(o_sb, o_gla, o_rw, p_merge, w_sb, w_gla, w_rw, tm=512, tn=1024):
    T = o_sb.shape[0]
    D = D_MODEL
    nb = D // tn
    aspec = pl.BlockSpec((tm, 1024), lambda j, i: (i, 0))
    wspec = pl.BlockSpec((1024, tn), lambda j, i: (0, j))
    gspec = lambda b: pl.BlockSpec((tm, tn), lambda j, i: (i, b * nb + j))
    return pl.pallas_call(
        _merge_kernel,
        grid=(nb, T // tm),
        in_specs=[aspec, aspec, aspec, gspec(0), gspec(1), gspec(2), wspec, wspec, wspec],
        out_specs=pl.BlockSpec((tm, tn), lambda j, i: (i, j)),
        out_shape=jax.ShapeDtypeStruct((T, D), BF16),
        compiler_params=_cp("parallel", "parallel"),
        name="merge",
    )(o_sb, o_gla, o_rw, p_merge, p_merge, p_merge, w_sb, w_gla, w_rw)


def _layer_norm(h, g, b):
    mu = jnp.mean(h, axis=-1, keepdims=True)
    d = h - mu
    var = jnp.mean(d * d, axis=-1, keepdims=True)
    return d * lax.rsqrt(var + LN_EPS) * g + b


def _oproj_kernel(m_ref, w_ref, x_ref, gate_ref, lg_ref, lb_ref, o_ref):
    mix = _dot(m_ref[...], w_ref[...])
    h = DEEPNORM_ALPHA * x_ref[...] + (1.0 + gate_ref[...]) * mix
    o_ref[...] = _layer_norm(h, lg_ref[...], lb_ref[...])


def _oproj(merged, w_o, x, gate, ln_g, ln_b, tm=512):
    T, D = x.shape
    spec = pl.BlockSpec((tm, D), lambda i: (i, 0))
    rowspec = pl.BlockSpec((1, D), lambda i: (0, 0))
    return pl.pallas_call(
        _oproj_kernel,
        grid=(T // tm,),
        in_specs=[spec, pl.BlockSpec((D, D), lambda i: (0, 0)), spec, rowspec, rowspec, rowspec],
        out_specs=spec,
        out_shape=jax.ShapeDtypeStruct((T, D), F32),
        compiler_params=_cp("parallel"),
        name="oproj_norm",
    )(merged, w_o, x, gate, ln_g.reshape(1, D), ln_b.reshape(1, D))


def _first_max(x, idx, axis, big):
    m = jnp.max(x, axis=axis, keepdims=True)
    first = jnp.min(jnp.where(x == m, idx, big), axis=axis, keepdims=True)
    return m, idx == first


def _router_kernel(x_ref, sc_ref, sh_ref, wr_ref, rb_ref, u_ref, e_ref, g_ref, p_ref, cnt_ref, car_ref, *, tm):
    E, G = N_EXPERTS, N_GROUPS
    per = E // G
    neg = -jnp.inf

    @pl.when(pl.program_id(0) == 0)
    def _():
        car_ref[...] = jnp.zeros_like(car_ref)

    u = x_ref[...] * (1.0 + sc_ref[...]) + sh_ref[...]
    u_ref[...] = _pack_pairs(u)
    scores =_sigmoid(_dot_ff(wr_ref[...], u, NT))
    biased = scores + rb_ref[...]
    b3 = biased.reshape(G, per, tm)
    i3 = lax.broadcasted_iota(I32, (G, per, tm), 1)
    m1, hit = _first_max(b3, i3, 1, per)
    m2 = jnp.max(jnp.where(hit, neg, b3), axis=1, keepdims=True)
    gs = (m1 + m2).reshape(G, tm)
    gi = lax.broadcasted_iota(I32, (G, tm), 0)
    ok = jnp.zeros((G, tm), jnp.bool_)
    for _ in range(TOPK_GROUPS):
        _, hit = _first_max(gs, gi, 0, G)
        ok = jnp.logical_or(ok, hit)
        gs = jnp.where(hit, neg, gs)
    ok_e = jnp.broadcast_to(ok.reshape(G, 1, tm), (G, per, tm)).reshape(E, tm)
    cand = jnp.where(ok_e, biased, neg)
    ei = lax.broadcasted_iota(I32, (E, tm), 0)
    sels, gates = [], []
    for _ in range(TOP_K):
        _, hit = _first_max(cand, ei, 0, E)
        sels.append(hit)
        gates.append(jnp.sum(jnp.where(hit, scores, 0.0), axis=0, keepdims=True))
        cand = jnp.where(hit, neg, cand)
    denom = gates[0]
    for gk in gates[1:]:
        denom = denom + gk
    chosen = sels[0]
    for s in sels[1:]:
        chosen = jnp.logical_or(chosen, s)
    cnt = jnp.where(chosen, 1.0, 0.0)
    earlier = (lax.broadcasted_iota(I32, (tm, tm), 0) < lax.broadcasted_iota(I32, (tm, tm), 1)).astype(BF16)
    rank = _dot(cnt.astype(BF16), earlier) + car_ref[...][:, 0:1]
    car_ref[...] = car_ref[...] + jnp.sum(cnt, axis=1, keepdims=True)
    cnt_ref[...] = car_ref[...]
    zi = jnp.zeros((1, tm), I32)
    zf = jnp.zeros((1, tm), F32)
    e_rows = [jnp.sum(jnp.where(s, ei, 0), axis=0, keepdims=True) for s in sels]
    p_rows = [jnp.sum(jnp.where(s, rank, 0.0), axis=0, keepdims=True).astype(I32) for s in sels]
    g_rows = [gk / denom * ROUTED_SCALE for gk in gates]
    e_ref[...] = jnp.concatenate(e_rows + [zi, zi], axis=0)
    p_ref[...] = jnp.concatenate(p_rows + [zi, zi], axis=0)
    g_ref[...] = jnp.concatenate(g_rows + [zf, zf], axis=0)


def _router(x, sc, sh, w_router, router_bias, tm=256):
    T, D = x.shape
    E = N_EXPERTS
    kern = functools.partial(_router_kernel, tm=tm)
    rowspec = pl.BlockSpec((1, D), lambda i: (0, 0))
    kspec = pl.BlockSpec((SUBLANES, tm), lambda i: (0, i))
    return pl.pallas_call(
        kern,
        grid=(T // tm,),
        in_specs=[pl.BlockSpec((tm, D), lambda i: (i, 0)), rowspec, rowspec,
                  pl.BlockSpec((E, D), lambda i: (0, 0)), pl.BlockSpec((E, 1), lambda i: (0, 0))],
        out_specs=[pl.BlockSpec((tm, D // 2), lambda i: (i, 0)), kspec, kspec, kspec,
                   pl.BlockSpec((E, LANES), lambda i: (0, 0))],
        out_shape=[jax.ShapeDtypeStruct((T, D // 2), U32),
                   jax.ShapeDtypeStruct((SUBLANES, T), I32),
                   jax.ShapeDtypeStruct((SUBLANES, T), F32),
                   jax.ShapeDtypeStruct((SUBLANES, T), I32),
                   jax.ShapeDtypeStruct((E, LANES), F32)],
        scratch_shapes=[pltpu.VMEM((E, LANES), F32)],
        compiler_params=_cp("arbitrary"),
        name="router",
    )(x, sc, sh, w_router.T, router_bias.reshape(E, 1))


def _dispatch_kernel(dest_ref, fill_ref, u_ref, xs_ref, z_ref, sem, zsem, *, tm, bm):
    i = pl.program_id(0)

    @pl.when(i == 0)
    def _():
        z_ref[...] = jnp.zeros_like(z_ref)

        def zcopy(e):
            return pltpu.make_async_copy(z_ref, xs_ref.at[pl.ds(pl.multiple_of(fill_ref[e], bm), bm)], zsem)

        def zstart(e, c):
            zcopy(e).start()
            return c

        def zwait(e, c):
            zcopy(e).wait()
            return c

        lax.fori_loop(0, N_EXPERTS, zstart, 0)
        lax.fori_loop(0, N_EXPERTS, zwait, 0)

    def start(g, c):
        for t in range(ROW_UNROLL):
            for k in range(TOP_K):
                tok = g * ROW_UNROLL + t
                pltpu.make_async_copy(_row(u_ref, tok), _row(xs_ref, dest_ref[k, tok]), sem).start()
        return c

    lax.fori_loop(0, tm // ROW_UNROLL, start, 0)
    n = tm * TOP_K
    pltpu.make_async_copy(xs_ref.at[pl.ds(0, n)], xs_ref.at[pl.ds(0, n)], sem).wait()


def _dispatch(u, dest, fill, n_slots, tm=256, bm=MOE_BM):
    T, D = u.shape
    kern = functools.partial(_dispatch_kernel, tm=tm, bm=bm)
    return pl.pallas_call(
        kern,
        grid=(T // tm,),
        in_specs=[pl.BlockSpec((SUBLANES, tm), lambda i: (0, i), memory_space=pltpu.SMEM),
                  pl.BlockSpec(memory_space=pltpu.SMEM),
                  pl.BlockSpec((tm, D), lambda i: (i, 0))],
        out_specs=pl.BlockSpec(memory_space=pl.ANY),
        out_shape=jax.ShapeDtypeStruct((n_slots, D), u.dtype),
        scratch_shapes=[pltpu.VMEM((bm, D), u.dtype),
                        pltpu.SemaphoreType.DMA(()), pltpu.SemaphoreType.DMA(())],
        compiler_params=_cp("arbitrary"),
        name="moe_dispatch",
    )(dest, fill, u)


def _expert_kernel(be_ref, nv_ref, x_ref, wg_ref, wu_ref, wd_ref, y_ref, wgb_ref, wub_ref, wdb_ref, *, bm):
    b = pl.program_id(0)
    valid = b < nv_ref[0]
    new_expert = jnp.logical_or(b == 0, be_ref[b] != be_ref[jnp.maximum(b - 1, 0)])

    @pl.when(jnp.logical_and(valid, new_expert))
    def _():
        wgb_ref[...] = wg_ref[0].astype(BF16)
        wub_ref[...] = wu_ref[0].astype(BF16)
        wdb_ref[...] = wd_ref[0].astype(BF16)

    @pl.when(valid)
    def _():
        x = _unpack_pairs(x_ref[...]).astype(BF16)
        h = _silu(_dot(x, wgb_ref[...])) * _dot(x, wub_ref[...])
        y_ref[...] = _pack_pairs(_dot(h.astype(BF16), wdb_ref[...]))


def _experts(xs, block_expert, n_valid, w_gate, w_up, w_down, layer, bm=MOE_BM):
    n_slots, W = xs.shape
    D = 2 * W
    nb = n_slots // bm
    F = w_gate.shape[-1]
    blk = lambda b, be, nv: (jnp.minimum(b, nv[0] - 1), 0)
    wsel = lambda b, be, nv: (layer, be[jnp.minimum(b, nv[0] - 1)], 0, 0)
    gs = pltpu.PrefetchScalarGridSpec(
        num_scalar_prefetch=2,
        grid=(nb,),
        in_specs=[pl.BlockSpec((bm, W), blk),
                  pl.BlockSpec((None, 1, D, F), wsel),
                  pl.BlockSpec((None, 1, D, F), wsel),
                  pl.BlockSpec((None, 1, F, D), wsel)],
        out_specs=pl.BlockSpec((bm, W), blk),
        scratch_shapes=[pltpu.VMEM((D, F), BF16), pltpu.VMEM((D, F), BF16), pltpu.VMEM((F, D), BF16)],
    )
    return pl.pallas_call(
        functools.partial(_expert_kernel, bm=bm),
        grid_spec=gs,
        out_shape=jax.ShapeDtypeStruct((n_slots, W), U32),
        compiler_params=_cp("arbitrary"),
        name="moe_experts",
    )(block_expert, n_valid, xs, w_gate, w_up, w_down)


def _combine_kernel(dest_ref, dnext_ref, ys_ref, u_ref, x_ref, gt_ref, wg_ref, wu_ref, wd_ref, gate_ref, lg_ref,
                    lb_ref, o_ref, buf_ref, sem, *, tm):
    i = pl.program_id(0)
    slot = lax.rem(i, 2)

    def gather(d_ref, s):
        def start(g, c):
            for t in range(ROW_UNROLL):
                for k in range(TOP_K):
                    tok = g * ROW_UNROLL + t
                    pltpu.make_async_copy(_row(ys_ref, d_ref[k, tok]), _row(buf_ref.at[s, k], tok), sem.at[s]).start()
            return c

        lax.fori_loop(0, tm // ROW_UNROLL, start, 0)

    @pl.when(i == 0)
    def _():
        gather(dest_ref, slot)

    @pl.when(i + 1 < pl.num_programs(0))
    def _():
        gather(dnext_ref, 1 - slot)

    u = _unpack_pairs(u_ref[...]).astype(BF16)
    h = _silu(_dot(u, wg_ref[...])) * _dot(u, wu_ref[...])
    ffn = _dot(h.astype(BF16), wd_ref[...])

    pltpu.make_async_copy(buf_ref.at[slot], buf_ref.at[slot], sem.at[slot]).wait()
    gt = gt_ref[...]
    for k in range(TOP_K):
        ffn = ffn + gt[:, k:k + 1] * _unpack_pairs(buf_ref[slot, k])
    hres = DEEPNORM_ALPHA * x_ref[...] + (1.0 + gate_ref[...]) * ffn
    o_ref[...] = _layer_norm(hres, lg_ref[...], lb_ref[...])


def _combine(ys, dest, u, x, gates_t, w_sg, w_su, w_sd, gate, ln_g, ln_b, tm=128):
    T, D = x.shape
    W = ys.shape[1]
    F = w_sg.shape[-1]
    n = T // tm
    kern = functools.partial(_combine_kernel, tm=tm)
    spec = pl.BlockSpec((tm, D), lambda i: (i, 0))
    rowspec = pl.BlockSpec((1, D), lambda i: (0, 0))
    return pl.pallas_call(
        kern,
        grid=(n,),
        in_specs=[pl.BlockSpec((SUBLANES, tm), lambda i: (0, i), memory_space=pltpu.SMEM),
                  pl.BlockSpec((SUBLANES, tm), lambda i: (0, jnp.minimum(i + 1, n - 1)), memory_space=pltpu.SMEM),
                  pl.BlockSpec(memory_space=pl.ANY),
                  pl.BlockSpec((tm, W), lambda i: (i, 0)), spec,
                  pl.BlockSpec((tm, SUBLANES), lambda i: (i, 0)),
                  pl.BlockSpec((D, F), lambda i: (0, 0)),
                  pl.BlockSpec((D, F), lambda i: (0, 0)),
                  pl.BlockSpec((F, D), lambda i: (0, 0)),
                  rowspec, rowspec, rowspec],
        out_specs=spec,
        out_shape=jax.ShapeDtypeStruct((T, D), F32),
        scratch_shapes=[pltpu.VMEM((2, TOP_K, tm, W), ys.dtype), pltpu.SemaphoreType.DMA((2,))],
        compiler_params=_cp("arbitrary"),
        name="moe_combine_norm",
    )(dest, dest, ys, u, x, gates_t, w_sg, w_su, w_sd, gate, ln_g.reshape(1, D), ln_b.reshape(1, D))


def _moe_plan(top_e, rank, counts, bm, n_blocks):
    counts = counts.astype(I32)
    padded = (counts + bm - 1) // bm * bm
    padded_end = jnp.cumsum(padded)
    padded_start = padded_end - padded
    onehot = top_e[:, :, None] == jnp.arange(N_EXPERTS, dtype=I32)
    dest = jnp.sum(jnp.where(onehot, padded_start, 0), axis=-1) + rank
    fill = jnp.maximum(padded_end - bm, padded_start)
    fill = jnp.minimum(fill, (n_blocks - 1) * bm)
    block_start = jnp.arange(n_blocks, dtype=I32) * bm
    block_expert = jnp.minimum(jnp.sum((padded_end[None, :] <= block_start[:, None]).astype(I32), axis=1),
                               N_EXPERTS - 1)
    n_valid = jnp.maximum(padded_end[-1:] // bm, 1).astype(I32)
    return dest.astype(I32), fill.astype(I32), block_expert, n_valid


def _in_proj_weights(w_in):
    o = SB_COLS
    gq = w_in[:, o:o + 2 * GLA_KEY_DIM + GLA_VALUE_DIM]
    o += 2 * GLA_KEY_DIM + GLA_VALUE_DIM
    glr = w_in[:, o:o + GLA_GATE_RANK]
    o += GLA_GATE_RANK
    gout = w_in[:, o:o + GLA_VALUE_DIM]
    o += GLA_VALUE_DIM
    pad = jnp.zeros((w_in.shape[0], LANES - GLA_GATE_RANK), w_in.dtype)
    w_gla = jnp.concatenate([gq, gout, glr, pad], axis=1)
    w_rw = w_in[:, o:o + RWKV_COLS]
    o += RWKV_COLS
    w_mg = w_in[:, o:o + MERGE_COLS]
    return w_in[:, :SB_COLS], w_gla, w_rw, w_mg


def kernel(x, c, w_ada, b_ada, w_in, gla_gk_up, gla_gk_bias, gla_norm_g, rw_mu, rw_w0, rw_w2, rw_a0,
           rw_a2, rw_g2, rw_k_k, rw_k_a, rw_r_k, rw_lnx_g, rw_lnx_b, rw_v0, rw_v1, rw_v2,
           w_br_sb, w_br_gla, w_br_rw, w_o, ln_g, ln_b, w_router, router_bias,
           w_exp_gate, w_exp_up, w_exp_down, w_sh_gate, w_sh_up, w_sh_down):
    B, T, D = x.shape
    assert B == 1 and D == D_MODEL
    n_blocks = -(-(T * TOP_K) // MOE_BM) + N_EXPERTS
    mod = _ada(c, w_ada, b_ada)
    xs = x.reshape(T, D)
    v_first = None
    for l in range(DEPTH):
        sh1, sc1, g1, sh2, sc2, g2 = (mod[l, j * D:(j + 1) * D].reshape(1, D) for j in range(6))
        w_sb, w_gla, w_rw, w_mg = _in_proj_weights(w_in[l])
        u1 = _modulate(xs, sc1, sh1)
        p_sb = _wmm(u1, w_sb, BF16, 1024, name="inproj_sb")
        p_gla = _wmm(u1, w_gla, F32, 640, name="inproj_gla")
        p_rw = _wmm(u1, w_rw, F32, 832 * 2, name="inproj_rwkv")
        p_mg = _wmm(u1, w_mg, BF16, 1024, name="inproj_merge")

        o_sb = _sb_attention(p_sb)
        o_gla = _gla(p_gla, gla_gk_up[l], gla_gk_bias[l], gla_norm_g[l])
        if l == 0:
            r, w, k, v, a, b, g = _rwkv_prep(p_rw, None, rw_mu[l], rw_w0[l], rw_w2[l], rw_a0[l], rw_a2[l],
                                             rw_g2[l], rw_k_k[l], rw_k_a[l], None, None, None)
            v_first = v
        else:
            r, w, k, v, a, b, g = _rwkv_prep(p_rw, v_first, rw_mu[l], rw_w0[l], rw_w2[l], rw_a0[l], rw_a2[l],
                                             rw_g2[l], rw_k_k[l], rw_k_a[l], rw_v0[l - 1], rw_v1[l - 1],
                                             rw_v2[l - 1])
        y = _rwkv_recurrence(r, w, k, v, a, b)
        o_rw = _rwkv_post(y, r, k, v, g, rw_lnx_g[l], rw_lnx_b[l], rw_r_k[l])

        merged = _merge(o_sb, o_gla, o_rw, p_mg, w_br_sb[l].astype(BF16), w_br_gla[l].astype(BF16),
                        w_br_rw[l].astype(BF16))
        xs = _oproj(merged, w_o[l].astype(BF16), xs, g1, ln_g[l, 0], ln_b[l, 0])

        u2, top_e, gates, rank, counts = _router(xs, sc2, sh2, w_router[l], router_bias[l])
        dest, fill, block_expert, n_valid = _moe_plan(top_e, rank, counts[:, 0], MOE_BM, n_blocks)
        x_sorted = _dispatch(u2, dest, fill, n_blocks * MOE_BM)
        y_sorted = _experts(x_sorted, block_expert, n_valid, w_exp_gate, w_exp_up, w_exp_down, l)
        xs = _combine(y_sorted, dest, u2, xs, gates.T, w_sh_gate[l].astype(BF16), w_sh_up[l].astype(BF16),
                      w_sh_down[l].astype(BF16), g2, ln_g[l, 1], ln_b[l, 1])
    return xs.reshape(B, T, D)
```

```python
import functools

import numpy as np
import jax
import jax.numpy as jnp
from jax import lax
from jax.experimental import pallas as pl
from jax.experimental.pallas import tpu as pltpu

F32 = jnp.float32
BF16 = jnp.bfloat16
I32 = jnp.int32
U32 = jnp.uint32

D_MODEL = 2048
DEPTH = 2
LN_EPS = 1e-5
DEEPNORM_ALPHA = (2 * DEPTH) ** 0.25

SB_HEADS = 8
SB_HEAD_DIM = 128
SB_WIDTH = SB_HEADS * SB_HEAD_DIM

GLA_HEADS = 4
GLA_HEAD_K = 128
GLA_HEAD_V = 256
GLA_KEY_DIM = GLA_HEADS * GLA_HEAD_K
GLA_VALUE_DIM = GLA_HEADS * GLA_HEAD_V
GLA_GATE_RANK = 16
GLA_GATE_NORMALIZER = 16.0
GLA_CHUNK = 64
GLA_NORM_EPS = 1e-5

RWKV_HEADS = 16
RWKV_HEAD_DIM = 64
RWKV_WIDTH = RWKV_HEADS * RWKV_HEAD_DIM
RWKV_DECAY_RANK = 64
RWKV_AAA_RANK = 64
RWKV_VALUE_RANK = 32
RWKV_GATE_RANK = 128
RWKV_LNX_EPS = 64e-5

N_EXPERTS = 64
TOP_K = 6
EXPERT_DIM = 512
SHARED_DIM = 512
N_GROUPS = 8
TOPK_GROUPS = 4
ROUTED_SCALE = 2.5

SB_COLS = 3 * SB_WIDTH
GLA_COLS = 2 * GLA_KEY_DIM + GLA_VALUE_DIM + GLA_GATE_RANK + GLA_VALUE_DIM
RWKV_COLS = 3 * RWKV_WIDTH + RWKV_DECAY_RANK + RWKV_AAA_RANK + RWKV_GATE_RANK
MERGE_COLS = 3 * D_MODEL

LANES = 128
SUBLANES = 8
VMEM_LIMIT = 56 * 1024 * 1024
MOE_BM = 512
SB_EXIT = -104.0
SB_GROUPS = 2

NN = (((1,), (0,)), ((), ()))
NT = (((1,), (1,)), ((), ()))
TN = (((0,), (0,)), ((), ()))


def _cp(*sem):
    return pltpu.CompilerParams(dimension_semantics=sem, vmem_limit_bytes=VMEM_LIMIT)


def _dot(a, b, dims=NN):
    return lax.dot_general(a, b, dims, preferred_element_type=F32)


def _split(a, n):
    out = []
    r = a
    for i in range(n):
        p = r.astype(BF16)
        out.append(p)
        if i + 1 < n:
            r = r - p.astype(F32)
    return out


def _dot_fx(a, e, n, dims=NN):
    acc = None
    for p in _split(a, n):
        t = _dot(p, e, dims)
        acc = t if acc is None else acc + t
    return acc


def _dot_xf(e, b, n, dims=NN):
    acc = None
    for p in _split(b, n):
        t = _dot(e, p, dims)
        acc = t if acc is None else acc + t
    return acc


def _dot_ff(a, b, dims=NN):
    a1, a2, a3 = _split(a, 3)
    b1, b2, b3 = _split(b, 3)
    acc = _dot(a1, b1, dims)
    for p, q in ((a1, b2), (a2, b1), (a2, b2), (a1, b3), (a3, b1)):
        acc = acc + _dot(p, q, dims)
    return acc


def _sigmoid(x):
    return 1.0 / (1.0 + jnp.exp(-x))


def _softplus(x):
    return jnp.maximum(x, 0.0) + jnp.log1p(jnp.exp(-jnp.abs(x)))


def _silu(x):
    return x * _sigmoid(x)


def _row(ref, t):
    return ref.at[pl.ds(t, 1)]


def _pack_pairs(x):
    w = x.shape[1] // 2
    bits = lax.bitcast_convert_type(x.astype(BF16).astype(F32), U32)
    return (bits[:, :w] >> 16) | bits[:, w:]


def _unpack_pairs(p):
    lo = lax.bitcast_convert_type(p << 16, F32)
    hi = lax.bitcast_convert_type(p & jnp.uint32(0xFFFF0000), F32)
    return jnp.concatenate([lo, hi], axis=1)


def _ada_kernel(c_ref, w_ref, b_ref, o_ref):
    c = c_ref[...]
    o_ref[0] = _mm(_silu(c), w_ref[0]) + b_ref[0]


def _ada(c, w_ada, b_ada):
    L, D, N = w_ada.shape
    tn = 512
    c8 = jnp.broadcast_to(c, (SUBLANES, D))
    out = pl.pallas_call(
        _ada_kernel,
        grid=(L, N // tn),
        in_specs=[pl.BlockSpec((SUBLANES, D), lambda l, j: (0, 0)),
                  pl.BlockSpec((1, D, tn), lambda l, j: (l, 0, j)),
                  pl.BlockSpec((1, 1, tn), lambda l, j: (l, 0, j))],
        out_specs=pl.BlockSpec((1, SUBLANES, tn), lambda l, j: (l, 0, j)),
        out_shape=jax.ShapeDtypeStruct((L, SUBLANES, N), F32),
        compiler_params=_cp("parallel", "parallel"),
        name="ada",
    )(c8, w_ada, b_ada.reshape(L, 1, N))
    return out[:, 0, :]


def _modulate_kernel(x_ref, sc_ref, sh_ref, o_ref):
    o_ref[...] = (x_ref[...] * (1.0 + sc_ref[...]) + sh_ref[...]).astype(o_ref.dtype)


def _modulate(x, sc, sh, tm=1024):
    T, D = x.shape
    spec = pl.BlockSpec((tm, D), lambda i: (i, 0))
    rowspec = pl.BlockSpec((1, D), lambda i: (0, 0))
    return pl.pallas_call(
        _modulate_kernel,
        grid=(T // tm,),
        in_specs=[spec, rowspec, rowspec],
        out_specs=spec,
        out_shape=jax.ShapeDtypeStruct((T, D), BF16),
        compiler_params=_cp("parallel"),
        name="modulate",
    )(x, sc, sh)


def _wmm_kernel(u_ref, w_ref, o_ref, wb_ref):
    @pl.when(pl.program_id(1) == 0)
    def _():
        wb_ref[...] = w_ref[...].astype(BF16)

    o_ref[...] = _dot(u_ref[...], wb_ref[...]).astype(o_ref.dtype)


def _wmm(u, w, out_dtype, tn, tm=512, name="wmm"):
    T, D = u.shape
    N = w.shape[1]
    return pl.pallas_call(
        _wmm_kernel,
        grid=(N // tn, T // tm),
        in_specs=[pl.BlockSpec((tm, D), lambda j, i: (i, 0)),
                  pl.BlockSpec((D, tn), lambda j, i: (0, j))],
        out_specs=pl.BlockSpec((tm, tn), lambda j, i: (i, j)),
        out_shape=jax.ShapeDtypeStruct((T, N), out_dtype),
        scratch_shapes=[pltpu.VMEM((D, tn), BF16)],
        compiler_params=_cp("parallel", "arbitrary"),
        name=name,
    )(u, w)


def _sb_kernel(q_ref, k_ref, v_ref, o_ref, acc_ref, car_ref, *, bq, scale):
    i = pl.program_id(1)
    q = q_ref[...]
    r_id = lax.broadcasted_iota(I32, (bq, bq), 0)
    c_id = lax.broadcasted_iota(I32, (bq, bq), 1)
    later = (r_id > c_id).astype(BF16)
    causal = c_id < r_id

    def block(j, carry, diagonal):
        off = pl.multiple_of(j * bq, bq)
        kb = k_ref[pl.ds(off, bq), :]
        vb = v_ref[pl.ds(off, bq), :]
        grp = [slice(g * (bq // SB_GROUPS), (g + 1) * (bq // SB_GROUPS)) for g in range(SB_GROUPS)]
        z = [_dot(q[s], kb, NT) * scale for s in grp]
        drop = [jnp.maximum(t, 0.0) + jnp.log(1.0 + jnp.exp(-jnp.abs(t))) for t in z]
        if diagonal:
            drop = [jnp.where(causal[s], d, 0.0) for s, d in zip(grp, drop)]
        after = [_dot_fx(d, later, 2) + carry[s] for s, d in zip(grp, drop)]
        w = [jnp.exp(t - d - a) for t, d, a in zip(z, drop, after)]
        if diagonal:
            w = [jnp.where(causal[s], x, 0.0) for s, x in zip(grp, w)]
        pv = jnp.concatenate([_dot(x.astype(BF16), vb) for x in w], axis=0)
        new_carry = jnp.concatenate([a[:, 0:1] + d[:, 0:1] for a, d in zip(after, drop)], axis=0)
        return pv, new_carry

    pv, carry = block(i, jnp.zeros((bq, 1), F32), True)
    acc_ref[...] = pv
    car_ref[...] = carry

    def body(c):
        j, _ = c
        pv, carry = block(j, car_ref[...], False)
        acc_ref[...] += pv
        car_ref[...] = carry
        return j - 1, (jnp.min(carry) > -SB_EXIT).astype(I32)

    lax.while_loop(lambda c: jnp.logical_and(c[0] >= 0, c[1] == 0), body, (i - 1, jnp.int32(0)))
    o_ref[...] = acc_ref[...].astype(o_ref.dtype)


def _sb_attention(qkv, bq=256):
    T = qkv.shape[0]
    H, Dh = SB_HEADS, SB_HEAD_DIM
    kern = functools.partial(_sb_kernel, bq=bq, scale=Dh ** -0.5)
    return pl.pallas_call(
        kern,
        grid=(H, T // bq),
        in_specs=[pl.BlockSpec((bq, Dh), lambda h, i: (i, h)),
                  pl.BlockSpec((T, Dh), lambda h, i: (0, H + h)),
                  pl.BlockSpec((T, Dh), lambda h, i: (0, 2 * H + h))],
        out_specs=pl.BlockSpec((bq, Dh), lambda h, i: (i, h)),
        out_shape=jax.ShapeDtypeStruct((T, H * Dh), BF16),
        scratch_shapes=[pltpu.VMEM((bq, Dh), F32), pltpu.VMEM((bq, 1), F32)],
        compiler_params=_cp("parallel", "parallel"),
        name="sb_attention",
    )(qkv, qkv, qkv)


_GLA_LEVELS = (32, 16, 8, 4, 2, 1)


def _gla_constants():
    C = GLA_CHUNK
    i = np.arange(C)[:, None]
    s = np.arange(C)[None, :]
    sel = []
    masks = []
    for m in _GLA_LEVELS:
        same = (i // m) == (s // m)
        sel.append(same & (s <= i) & ((i // m) % 2 == 1))
        sel.append(same & (s > i) & ((i // m) % 2 == 0))
        masks.append(((i // (2 * m)) == (s // (2 * m))) & ((i // m) % 2 == 1) & ((s // m) % 2 == 0))
    sel.append(s <= i)
    sel.append(s > i)
    masks.append(i == s)
    return (np.concatenate(sel, 0).astype(np.float32), np.stack(masks, 0).astype(np.float32))


def _gla_kernel(q_ref, k_ref, v_ref, go_ref, glr_ref, up_ref, bias_ref, ng_ref, sel_ref, msk_ref,
                o_ref, st_ref, *, n_chunks):
    C = GLA_CHUNK
    nl = len(_GLA_LEVELS)

    @pl.when(pl.program_id(1) == 0)
    def _():
        st_ref[...] = jnp.zeros_like(st_ref)

    sel = sel_ref[...]
    up = up_ref[...]
    bias = bias_ref[...]
    ng = ng_ref[...]
    rows = [slice(c * C, (c + 1) * C) for c in range(n_chunks)]
    blk = lambda f, j: f[j * C:(j + 1) * C]
    q = [q_ref[s, :] * (GLA_HEAD_K ** -0.5) for s in rows]
    k = [k_ref[s, :] for s in rows]
    v = [v_ref[s, :].astype(BF16) for s in rows]
    x = [_dot(glr_ref[s, :].astype(BF16), up) + bias for s in rows]
    g = [-_softplus(-t) * (1.0 / GLA_GATE_NORMALIZER) for t in x]
    f = [jnp.exp(_dot_xf(sel, t, 2)) for t in g]
    scores = [_dot(a.astype(BF16), b.astype(BF16), NT) * msk_ref[nl] for a, b in zip(q, k)]
    for l in range(nl):
        ql = [(a * blk(t, 2 * l)).astype(BF16) for a, t in zip(q, f)]
        kl = [(b * blk(t, 2 * l + 1)).astype(BF16) for b, t in zip(k, f)]
        scores = [s + _dot(a, b, NT) * msk_ref[l] for s, a, b in zip(scores, ql, kl)]
    o_intra = [_dot(s.astype(BF16), b) for s, b in zip(scores, v)]
    qe = [(a * blk(t, 2 * nl)).astype(BF16) for a, t in zip(q, f)]
    upd = [_dot(b, (a * blk(t, 2 * nl + 1)).astype(BF16), TN) for b, a, t in zip(v, k, f)]
    dec = [blk(t, 2 * nl)[C - 1:C, :] for t in f]

    st = st_ref[...]
    for c in range(n_chunks):
        o = _dot(qe[c], st.astype(BF16), NT) + o_intra[c]
        st = st * dec[c] + upd[c]
        rms = lax.rsqrt(jnp.mean(o * o, axis=-1, keepdims=True) + GLA_NORM_EPS)
        o_ref[rows[c], :] = (o * rms * ng * _silu(go_ref[rows[c], :])).astype(o_ref.dtype)
    st_ref[...] = st


def _gla(p_gla, gk_up, gk_bias, norm_g, tt=512):
    T = p_gla.shape[0]
    H, dk, dv = GLA_HEADS, GLA_HEAD_K, GLA_HEAD_V
    sel, msk = _gla_constants()
    up = jnp.zeros((LANES, GLA_KEY_DIM), F32).at[:GLA_GATE_RANK].set(gk_up).astype(BF16)
    kern = functools.partial(_gla_kernel, n_chunks=tt // GLA_CHUNK)
    return pl.pallas_call(
        kern,
        grid=(H, T // tt),
        in_specs=[pl.BlockSpec((tt, dk), lambda h, i: (i, h)),
                  pl.BlockSpec((tt, dk), lambda h, i: (i, H + h)),
                  pl.BlockSpec((tt, dv), lambda h, i: (i, H + h)),
                  pl.BlockSpec((tt, dv), lambda h, i: (i, 2 * H + h)),
                  pl.BlockSpec((tt, LANES), lambda h, i: (i, 3 * GLA_VALUE_DIM // LANES)),
                  pl.BlockSpec((LANES, dk), lambda h, i: (0, h)),
                  pl.BlockSpec((1, dk), lambda h, i: (0, h)),
                  pl.BlockSpec((1, dv), lambda h, i: (0, 0)),
                  pl.BlockSpec(sel.shape, lambda h, i: (0, 0)),
                  pl.BlockSpec(msk.shape, lambda h, i: (0, 0, 0))],
        out_specs=pl.BlockSpec((tt, dv), lambda h, i: (i, h)),
        out_shape=jax.ShapeDtypeStruct((T, H * dv), BF16),
        scratch_shapes=[pltpu.VMEM((dv, dk), F32)],
        compiler_params=_cp("parallel", "arbitrary"),
        name="gla",
    )(p_gla, p_gla, p_gla, p_gla, p_gla, up, gk_bias.reshape(1, -1), norm_g.reshape(1, -1),
      jnp.asarray(sel, BF16), jnp.asarray(msk, F32))


def _head_ones():
    l = np.arange(LANES)
    return (l[:, None] // RWKV_HEAD_DIM == l[None, :] // RWKV_HEAD_DIM).astype(np.float32)


def _head_diag():
    i = np.arange(RWKV_HEAD_DIM)[:, None]
    l = np.arange(LANES)[None, :]
    return (l % RWKV_HEAD_DIM == i).astype(np.float32)


def _rwprep_kernel(p_ref, pp_ref, mu_ref, w0_ref, w2_ref, a0_ref, a2_ref, g2_ref, kk_ref, ka_ref,
                   v0_ref, v1_ref, v2_ref, vf_ref, bo_ref,
                   r_ref, w_ref, k_ref, v_ref, a_ref, b_ref, g_ref, *, first_layer):
    W = RWKV_WIDTH
    i = pl.program_id(0)
    p = p_ref[...]
    tm = p.shape[0]
    prev_row = jnp.where(i == 0, 0.0, pp_ref[SUBLANES - 1:SUBLANES, :])
    rid = lax.broadcasted_iota(I32, p.shape, 0)
    prev = jnp.where(rid == 0, prev_row, pltpu.roll(p, 1, axis=0))
    ps = p + (prev - p) * mu_ref[...]
    r = ps[:, 0:W]
    kr = ps[:, W:2 * W]
    vr = ps[:, 2 * W:3 * W]
    lr = ps[:, 3 * W:3 * W + LANES]
    gl = ps[:, 3 * W + LANES:3 * W + 2 * LANES]
    w_log = -_softplus(-(w0_ref[...] + _dot(jnp.tanh(lr).astype(BF16), w2_ref[...]))) - 0.5
    w_ref[...] = -jnp.exp(w_log)
    if not first_layer:
        lo = _dot(vr.astype(BF16), v1_ref[...])
        gate = _sigmoid(v0_ref[...] + _dot(lo.astype(BF16), v2_ref[...]))
        vr = vr + (vf_ref[...] - vr) * gate
    v_ref[...] = vr
    a = _sigmoid(a0_ref[...] + _dot(lr.astype(BF16), a2_ref[...]))
    g_ref[...] = _dot(_sigmoid(gl).astype(BF16), g2_ref[...])
    kk = kr * kk_ref[...]
    bo = bo_ref[...]
    for hp in range(W // LANES):
        sl = slice(hp * LANES, (hp + 1) * LANES)
        kh = kk[:, sl]
        ss = _dot_fx(kh * kh, bo, 3)
        kn = kh / jnp.maximum(jnp.sqrt(ss), 1e-12)
        a_ref[:, sl] = -kn
        b_ref[:, sl] = kn * a[:, sl]
    r_ref[...] = r
    k_ref[...] = kr * (1.0 + (a - 1.0) * ka_ref[...])


def _rwkv_prep(p_rw, v_first, mu, w0, w2, a0, a2, g2, k_k, k_a, v0, v1, v2, tm=256):
    T, NC = p_rw.shape
    W = RWKV_WIDTH
    first = v_first is None
    w2p = jnp.zeros((LANES, W), F32).at[:RWKV_DECAY_RANK].set(w2).astype(BF16)
    a2p = jnp.zeros((LANES, W), F32).at[RWKV_DECAY_RANK:].set(a2).astype(BF16)
    if first:
        v_first = jnp.zeros((SUBLANES, W), F32)
        v0 = jnp.zeros((W,), F32)
        v1p = jnp.zeros((W, LANES), BF16)
        v2p = jnp.zeros((LANES, W), BF16)
        vf_spec = pl.BlockSpec((SUBLANES, W), lambda i: (0, 0))
    else:
        v1p = jnp.zeros((W, LANES), F32).at[:, :RWKV_VALUE_RANK].set(v1).astype(BF16)
        v2p = jnp.zeros((LANES, W), F32).at[:RWKV_VALUE_RANK].set(v2).astype(BF16)
        vf_spec = pl.BlockSpec((tm, W), lambda i: (i, 0))
    row = lambda a: a.reshape(1, -1)
    full = lambda shape: pl.BlockSpec(shape, lambda i: (0,) * len(shape))
    out = jax.ShapeDtypeStruct((T, W), F32)
    ospec = pl.BlockSpec((tm, W), lambda i: (i, 0))
    kern = functools.partial(_rwprep_kernel, first_layer=first)
    return pl.pallas_call(
        kern,
        grid=(T // tm,),
        in_specs=[pl.BlockSpec((tm, NC), lambda i: (i, 0)),
                  pl.BlockSpec((SUBLANES, NC), lambda i: (jnp.maximum(i * (tm // SUBLANES) - 1, 0), 0)),
                  full((1, NC)), full((1, W)), full((LANES, W)), full((1, W)), full((LANES, W)),
                  full((LANES, W)), full((1, W)), full((1, W)),
                  full((1, W)), full((W, LANES)), full((LANES, W)), vf_spec, full((LANES, LANES))],
        out_specs=[ospec] * 7,
        out_shape=[out] * 7,
        compiler_params=_cp("parallel"),
        name="rwkv_prep",
    )(p_rw, p_rw, row(mu), row(w0), w2p, row(a0), a2p, g2.astype(BF16), row(k_k), row(k_a),
      row(v0), v1p, v2p, v_first, jnp.asarray(_head_ones(), BF16))


RW_CHUNK = 64


def _rw_constants():
    C = RW_CHUNK
    i = np.arange(C)[:, None]
    s = np.arange(C)[None, :]
    i2 = np.arange(2 * C)[:, None]
    s2 = np.arange(2 * C)[None, :]
    same = (i2 // C) == (s2 // C)
    cm = np.stack([same & (s2 < i2), same & (s2 <= i2), (i2 // 16) == (s2 // 16), (i2 // 32) == (s2 // 32),
                   same], 0).astype(np.float32)
    return (s <= i).astype(np.float32), cm


def _mm(a, b, dims=NN):
    a1, a2 = _split(a, 2)
    b1, b2 = _split(b, 2)
    r = _dot(a1, jnp.concatenate([b1, b2], axis=0 if dims == NT else 1), dims)
    n = r.shape[1] // 2
    return (r[:, :n] + r[:, n:]) + _dot(a2, b1, dims)


def _mm1(a, b, dims=NN):
    return _dot(a.astype(BF16), b.astype(BF16), dims)


def _unit_lower_inverse(nms, eye, bd16, bd32):
    nd = [n * bd16 for n in nms]
    x = [eye + n for n in nd]
    p = [_mm(n, n) for n in nd]
    for step in range(3):
        x = [a + _mm(a, q) for a, q in zip(x, p)]
        if step < 2:
            p = [_mm(q, q) for q in p]
    for lo_mask in (bd32 - bd16, 1.0 - bd32):
        t = [_mm(n * lo_mask, a) for n, a in zip(nms, x)]
        x = [a + _mm(a, q) for a, q in zip(x, t)]
    return x


def _rwrec_kernel(r_ref, lw_ref, k_ref, v_ref, a_ref, b_ref, lt_ref, cm_ref, y_ref, mt_ref, *, n_chunks):
    C = RW_CHUNK
    C2 = 2 * C

    @pl.when(pl.program_id(1) == 0)
    def _():
        mt_ref[...] = jnp.zeros_like(mt_ref)

    lt = lt_ref[...]
    sl, li, bd16, bd32, bdh = cm_ref[0], cm_ref[1], cm_ref[2], cm_ref[3], cm_ref[4]
    eye = li - sl
    lane = lax.broadcasted_iota(I32, (1, LANES), 1)
    m0 = jnp.where(lane < RWKV_HEAD_DIM, 1.0, 0.0)
    m1 = 1.0 - m0
    stack = lambda x: jnp.concatenate([x * m0, x * m1], axis=0)
    fold = lambda x: x[:C] + x[C:]
    rows = [slice(c * C, (c + 1) * C) for c in range(n_chunks)]

    lw = [lw_ref[s, :] for s in rows]
    cw = [_dot_xf(lt, x, 3) for x in lw]
    en = [jnp.exp(-x) for x in cw]
    at = [a_ref[s, :] * jnp.exp(c - l) for s, c, l in zip(rows, cw, lw)]
    rt = [r_ref[s, :] * jnp.exp(c) for s, c in zip(rows, cw)]
    bh = [b_ref[s, :] * e for s, e in zip(rows, en)]
    kh = [k_ref[s, :] * e for s, e in zip(rows, en)]
    v = [v_ref[s, :] for s in rows]
    a2 = [stack(x) for x in at]
    ar2 = [jnp.concatenate([x, stack(y)], axis=0) for x, y in zip(a2, rt)]
    gb = [_mm(x, stack(y), NT) for x, y in zip(ar2, bh)]
    gk = [_mm(x, stack(y), NT) for x, y in zip(ar2, kh)]
    tinv = _unit_lower_inverse([g[:C2] * sl for g in gb], eye, bd16, bd32)
    akv = [_mm(jnp.concatenate([g[:C2] * sl, g[C2:] * li], axis=0), stack(x)) for g, x in zip(gk, v)]
    tav = [_mm(t, jnp.concatenate([x, y[:C2]], axis=1)) for t, x, y in zip(tinv, a2, akv)]
    rav = [_mm1(g[C2:] * li, x) for g, x in zip(gb, tav)]
    ta = [fold(x[:, :LANES]) for x in tav]
    u0 = [fold(x[:, LANES:]) for x in tav]
    qe = [x + fold(y[:, :LANES]) for x, y in zip(rt, rav)]
    y0 = [fold(x[:, LANES:] + y[C2:]) for x, y in zip(rav, akv)]
    dl = [jnp.exp(x[C - 1:C, :]) for x in cw]
    gmt = [(eye + _mm(x, y, TN) * bdh) * d for x, y, d in zip(ta, bh, dl)]
    hmt = [(_mm(x, y, TN) + _mm(z, w, TN)) * bdh * d for x, y, z, w, d in zip(u0, bh, v, kh, dl)]

    mt = mt_ref[...]
    for c in range(n_chunks):
        y_ref[rows[c], :] = _mm1(qe[c], mt, NT) + y0[c]
        mt = _mm(mt, gmt[c]) + hmt[c]
    mt_ref[...] = mt


def _rwkv_recurrence(r, lw, k, v, a, b, tt=512):
    T, W = r.shape
    lt, cm = _rw_constants()
    spec = pl.BlockSpec((tt, LANES), lambda p, i: (i, p))
    kern = functools.partial(_rwrec_kernel, n_chunks=tt // RW_CHUNK)
    return pl.pallas_call(
        kern,
        grid=(W // LANES, T // tt),
        in_specs=[spec] * 6 + [pl.BlockSpec(lt.shape, lambda p, i: (0, 0)),
                               pl.BlockSpec(cm.shape, lambda p, i: (0, 0, 0))],
        out_specs=spec,
        out_shape=jax.ShapeDtypeStruct((T, W), F32),
        scratch_shapes=[pltpu.VMEM((LANES, LANES), F32)],
        compiler_params=_cp("parallel", "arbitrary"),
        name="rwkv_recurrence",
    )(r, lw, k, v, a, b, jnp.asarray(lt, BF16), jnp.asarray(cm, F32))


def _rwpost_kernel(y_ref, r_ref, k_ref, v_ref, g_ref, lg_ref, lb_ref, rk_ref, bo_ref, o_ref):
    bo = bo_ref[...]
    inv = 1.0 / RWKV_HEAD_DIM
    for hp in range(RWKV_WIDTH // LANES):
        sl = slice(hp * LANES, (hp + 1) * LANES)
        y = y_ref[:, sl]
        mu = _dot_fx(y, bo, 3) * inv
        d = y - mu
        var = _dot_fx(d * d, bo, 3) * inv
        yn = d * lax.rsqrt(var + RWKV_LNX_EPS) * lg_ref[:, sl] + lb_ref[:, sl]
        bonus = _dot_fx(r_ref[:, sl] * k_ref[:, sl] * rk_ref[:, sl], bo, 3) * v_ref[:, sl]
        o_ref[:, sl] = ((yn + bonus) * g_ref[:, sl]).astype(o_ref.dtype)


def _rwkv_post(y, r, k, v, g, lnx_g, lnx_b, r_k, tm=512):
    T, W = y.shape
    spec = pl.BlockSpec((tm, W), lambda i: (i, 0))
    rowspec = pl.BlockSpec((1, W), lambda i: (0, 0))
    return pl.pallas_call(
        _rwpost_kernel,
        grid=(T // tm,),
        in_specs=[spec] * 5 + [rowspec] * 3 + [pl.BlockSpec((LANES, LANES), lambda i: (0, 0))],
        out_specs=spec,
        out_shape=jax.ShapeDtypeStruct((T, W), BF16),
        compiler_params=_cp("parallel"),
        name="rwkv_post",
    )(y, r, k, v, g, lnx_g.reshape(1, W), lnx_b.reshape(1, W), r_k.reshape(1, W),
      jnp.asarray(_head_ones(), BF16))


def _merge_kernel(osb_ref, ogl_ref, orw_ref, g1_ref, g2_ref, g3_ref, w1_ref, w2_ref, w3_ref, o_ref):
    acc = _sigmoid(g1_ref[...].astype(F32)) * _dot(osb_ref[...], w1_ref[...])
    acc = acc + _sigmoid(g2_ref[...].astype(F32)) * _dot(ogl_ref[...], w2_ref[...])
    acc = acc + _sigmoid(g3_ref[...].astype(F32)) * _dot(orw_ref[...], w3_ref[...])
    o_ref[...] = acc.astype(o_ref.dtype)


def _merge(o_sb, o_gla, o_rw, p_merge, w_sb, w_gla, w_rw, tm=512, tn=1024):
    T = o_sb.shape[0]
    D = D_MODEL
    nb = D // tn
    aspec = pl.BlockSpec((tm, 1024), lambda j, i: (i, 0))
    wspec = pl.BlockSpec((1024, tn), lambda j, i: (0, j))
    gspec = lambda b: pl.BlockSpec((tm, tn), lambda j, i: (i, b * nb + j))
    return pl.pallas_call(
        _merge_kernel,
        grid=(nb, T // tm),
        in_specs=[aspec, aspec, aspec, gspec(0), gspec(1), gspec(2), wspec, wspec, wspec],
        out_specs=pl.BlockSpec((tm, tn), lambda j, i: (i, j)),
        out_shape=jax.ShapeDtypeStruct((T, D), BF16),
        compiler_params=_cp("parallel", "parallel"),
        name="merge",
    )(o_sb, o_gla, o_rw, p_merge, p_merge, p_merge, w_sb, w_gla, w_rw)


def _layer_norm(h, g, b):
    mu = jnp.mean(h, axis=-1, keepdims=True)
    d = h - mu
    var = jnp.mean(d * d, axis=-1, keepdims=True)
    return d * lax.rsqrt(var + LN_EPS) * g + b


def _oproj_kernel(m_ref, w_ref, x_ref, gate_ref, lg_ref, lb_ref, o_ref):
    mix = _dot(m_ref[...], w_ref[...])
    h = DEEPNORM_ALPHA * x_ref[...] + (1.0 + gate_ref[...]) * mix
    o_ref[...] = _layer_norm(h, lg_ref[...], lb_ref[...])


def _oproj(merged, w_o, x, gate, ln_g, ln_b, tm=512):
    T, D = x.shape
    spec = pl.BlockSpec((tm, D), lambda i: (i, 0))
    rowspec = pl.BlockSpec((1, D), lambda i: (0, 0))
    return pl.pallas_call(
        _oproj_kernel,
        grid=(T // tm,),
        in_specs=[spec, pl.BlockSpec((D, D), lambda i: (0, 0)), spec, rowspec, rowspec, rowspec],
        out_specs=spec,
        out_shape=jax.ShapeDtypeStruct((T, D), F32),
        compiler_params=_cp("parallel"),
        name="oproj_norm",
    )(merged, w_o, x, gate, ln_g.reshape(1, D), ln_b.reshape(1, D))


def _first_max(x, idx, axis, big):
    m = jnp.max(x, axis=axis, keepdims=True)
    first = jnp.min(jnp.where(x == m, idx, big), axis=axis, keepdims=True)
    return m, idx == first


def _router_kernel(x_ref, sc_ref, sh_ref, wr_ref, rb_ref, u_ref, e_ref, g_ref, p_ref, cnt_ref, car_ref, *, tm):
    E, G = N_EXPERTS, N_GROUPS
    per = E // G
    neg = -jnp.inf

    @pl.when(pl.program_id(0) == 0)
    def _():
        car_ref[...] = jnp.zeros_like(car_ref)

    u = x_ref[...] * (1.0 + sc_ref[...]) + sh_ref[...]
    u_ref[...] = _pack_pairs(u)
    scores =_sigmoid(_dot_ff(wr_ref[...], u, NT))
    biased = scores + rb_ref[...]
    b3 = biased.reshape(G, per, tm)
    i3 = lax.broadcasted_iota(I32, (G, per, tm), 1)
    m1, hit = _first_max(b3, i3, 1, per)
    m2 = jnp.max(jnp.where(hit, neg, b3), axis=1, keepdims=True)
    gs = (m1 + m2).reshape(G, tm)
    gi = lax.broadcasted_iota(I32, (G, tm), 0)
    ok = jnp.zeros((G, tm), jnp.bool_)
    for _ in range(TOPK_GROUPS):
        _, hit = _first_max(gs, gi, 0, G)
        ok = jnp.logical_or(ok, hit)
        gs = jnp.where(hit, neg, gs)
    ok_e = jnp.broadcast_to(ok.reshape(G, 1, tm), (G, per, tm)).reshape(E, tm)
    cand = jnp.where(ok_e, biased, neg)
    ei = lax.broadcasted_iota(I32, (E, tm), 0)
    sels, gates = [], []
    for _ in range(TOP_K):
        _, hit = _first_max(cand, ei, 0, E)
        sels.append(hit)
        gates.append(jnp.sum(jnp.where(hit, scores, 0.0), axis=0, keepdims=True))
        cand = jnp.where(hit, neg, cand)
    denom = gates[0]
    for gk in gates[1:]:
        denom = denom + gk
    chosen = sels[0]
    for s in sels[1:]:
        chosen = jnp.logical_or(chosen, s)
    cnt = jnp.where(chosen, 1.0, 0.0)
    earlier = (lax.broadcasted_iota(I32, (tm, tm), 0) < lax.broadcasted_iota(I32, (tm, tm), 1)).astype(BF16)
    rank = _dot(cnt.astype(BF16), earlier) + car_ref[...][:, 0:1]
    car_ref[...] = car_ref[...] + jnp.sum(cnt, axis=1, keepdims=True)
    cnt_ref[...] = car_ref[...]
    zi = jnp.zeros((1, tm), I32)
    zf = jnp.zeros((1, tm), F32)
    e_rows = [jnp.sum(jnp.where(s, ei, 0), axis=0, keepdims=True) for s in sels]
    p_rows = [jnp.sum(jnp.where(s, rank, 0.0), axis=0, keepdims=True).astype(I32) for s in sels]
    g_rows = [gk / denom * ROUTED_SCALE for gk in gates]
    e_ref[...] = jnp.concatenate(e_rows + [zi, zi], axis=0)
    p_ref[...] = jnp.concatenate(p_rows + [zi, zi], axis=0)
    g_ref[...] = jnp.concatenate(g_rows + [zf, zf], axis=0)


def _router(x, sc, sh, w_router, router_bias, tm=256):
    T, D = x.shape
    E = N_EXPERTS
    kern = functools.partial(_router_kernel, tm=tm)
    rowspec = pl.BlockSpec((1, D), lambda i: (0, 0))
    kspec = pl.BlockSpec((SUBLANES, tm), lambda i: (0, i))
    return pl.pallas_call(
        kern,
        grid=(T // tm,),
        in_specs=[pl.BlockSpec((tm, D), lambda i: (i, 0)), rowspec, rowspec,
                  pl.BlockSpec((E, D), lambda i: (0, 0)), pl.BlockSpec((E, 1), lambda i: (0, 0))],
        out_specs=[pl.BlockSpec((tm, D // 2), lambda i: (i, 0)), kspec, kspec, kspec,
                   pl.BlockSpec((E, LANES), lambda i: (0, 0))],
        out_shape=[jax.ShapeDtypeStruct((T, D // 2), U32),
                   jax.ShapeDtypeStruct((SUBLANES, T), I32),
                   jax.ShapeDtypeStruct((SUBLANES, T), F32),
                   jax.ShapeDtypeStruct((SUBLANES, T), I32),
                   jax.ShapeDtypeStruct((E, LANES), F32)],
        scratch_shapes=[pltpu.VMEM((E, LANES), F32)],
        compiler_params=_cp("arbitrary"),
        name="router",
    )(x, sc, sh, w_router.T, router_bias.reshape(E, 1))


def _dispatch_kernel(dest_ref, fill_ref, u_ref, xs_ref, z_ref, sem, zsem, *, tm, bm):
    i = pl.program_id(0)

    @pl.when(i == 0)
    def _():
        z_ref[...] = jnp.zeros_like(z_ref)

        def zcopy(e):
            return pltpu.make_async_copy(z_ref, xs_ref.at[pl.ds(pl.multiple_of(fill_ref[e], bm), bm)], zsem)

        def zstart(e, c):
            zcopy(e).start()
            return c

        def zwait(e, c):
            zcopy(e).wait()
            return c

        lax.fori_loop(0, N_EXPERTS, zstart, 0)
        lax.fori_loop(0, N_EXPERTS, zwait, 0)

    for tok in range(tm):
        for k in range(TOP_K):
            pltpu.make_async_copy(_row(u_ref, tok), _row(xs_ref, dest_ref[k, tok]), sem).start()
    n = tm * TOP_K
    pltpu.make_async_copy(xs_ref.at[pl.ds(0, n)], xs_ref.at[pl.ds(0, n)], sem).wait()


def _dispatch(u, dest, fill, n_slots, tm=256, bm=MOE_BM):
    T, D = u.shape
    kern = functools.partial(_dispatch_kernel, tm=tm, bm=bm)
    return pl.pallas_call(
        kern,
        grid=(T // tm,),
        in_specs=[pl.BlockSpec((SUBLANES, tm), lambda i: (0, i), memory_space=pltpu.SMEM),
                  pl.BlockSpec(memory_space=pltpu.SMEM),
                  pl.BlockSpec((tm, D), lambda i: (i, 0))],
        out_specs=pl.BlockSpec(memory_space=pl.ANY),
        out_shape=jax.ShapeDtypeStruct((n_slots, D), u.dtype),
        scratch_shapes=[pltpu.VMEM((bm, D), u.dtype),
                        pltpu.SemaphoreType.DMA(()), pltpu.SemaphoreType.DMA(())],
        compiler_params=_cp("arbitrary"),
        name="moe_dispatch",
    )(dest, fill, u)


def _expert_kernel(be_ref, nv_ref, x_ref, wg_ref, wu_ref, wd_ref, y_ref, wgb_ref, wub_ref, wdb_ref, *, bm):
    b = pl.program_id(0)
    valid = b < nv_ref[0]
    new_expert = jnp.logical_or(b == 0, be_ref[b] != be_ref[jnp.maximum(b - 1, 0)])

    @pl.when(jnp.logical_and(valid, new_expert))
    def _():
        wgb_ref[...] = wg_ref[0].astype(BF16)
        wub_ref[...] = wu_ref[0].astype(BF16)
        wdb_ref[...] = wd_ref[0].astype(BF16)

    @pl.when(valid)
    def _():
        x = _unpack_pairs(x_ref[...]).astype(BF16)
        h = _silu(_dot(x, wgb_ref[...])) * _dot(x, wub_ref[...])
        y_ref[...] = _pack_pairs(_dot(h.astype(BF16), wdb_ref[...]))


def _experts(xs, block_expert, n_valid, w_gate, w_up, w_down, layer, bm=MOE_BM):
    n_slots, W = xs.shape
    D = 2 * W
    nb = n_slots // bm
    F = w_gate.shape[-1]
    blk = lambda b, be, nv: (jnp.minimum(b, nv[0] - 1), 0)
    wsel = lambda b, be, nv: (layer, be[jnp.minimum(b, nv[0] - 1)], 0, 0)
    gs = pltpu.PrefetchScalarGridSpec(
        num_scalar_prefetch=2,
        grid=(nb,),
        in_specs=[pl.BlockSpec((bm, W), blk),
                  pl.BlockSpec((None, 1, D, F), wsel),
                  pl.BlockSpec((None, 1, D, F), wsel),
                  pl.BlockSpec((None, 1, F, D), wsel)],
        out_specs=pl.BlockSpec((bm, W), blk),
        scratch_shapes=[pltpu.VMEM((D, F), BF16), pltpu.VMEM((D, F), BF16), pltpu.VMEM((F, D), BF16)],
    )
    return pl.pallas_call(
        functools.partial(_expert_kernel, bm=bm),
        grid_spec=gs,
        out_shape=jax.ShapeDtypeStruct((n_slots, W), U32),
        compiler_params=_cp("arbitrary"),
        name="moe_experts",
    )(block_expert, n_valid, xs, w_gate, w_up, w_down)


def _combine_kernel(dest_ref, dnext_ref, ys_ref, u_ref, x_ref, gt_ref, wg_ref, wu_ref, wd_ref, gate_ref, lg_ref,
                    lb_ref, o_ref, buf_ref, sem, *, tm):
    i = pl.program_id(0)
    last = pl.num_programs(0) - 1
    slot = lax.rem(i, 2)

    def row_copy(d_ref, s, tok, k):
        return pltpu.make_async_copy(_row(ys_ref, d_ref[k, tok]), _row(buf_ref.at[s, k], tok), sem.at[s])

    def wait_half(s):
        pltpu.make_async_copy(buf_ref.at[s], buf_ref.at[s], sem.at[s]).wait()

    @pl.when(i == 0)
    def _():
        def start(tok, c):
            for k in range(TOP_K):
                row_copy(dest_ref, slot, tok, k).start()
            return c

        lax.fori_loop(0, tm, start, 0)

    wait_half(slot)
    for tok in range(tm):
        for k in range(TOP_K):
            row_copy(dnext_ref, 1 - slot, tok, k).start()

    u = _unpack_pairs(u_ref[...]).astype(BF16)
    h = _silu(_dot(u, wg_ref[...])) * _dot(u, wu_ref[...])
    ffn = _dot(h.astype(BF16), wd_ref[...])
    gt = gt_ref[...]
    for k in range(TOP_K):
        ffn = ffn + gt[:, k:k + 1] * _unpack_pairs(buf_ref[slot, k])
    hres = DEEPNORM_ALPHA * x_ref[...] + (1.0 + gate_ref[...]) * ffn
    o_ref[...] = _layer_norm(hres, lg_ref[...], lb_ref[...])

    @pl.when(i == last)
    def _():
        wait_half(1 - slot)


def _combine(ys, dest, u, x, gates_t, w_sg, w_su, w_sd, gate, ln_g, ln_b, tm=128):
    T, D = x.shape
    W = ys.shape[1]
    F = w_sg.shape[-1]
    n = T // tm
    kern = functools.partial(_combine_kernel, tm=tm)
    spec = pl.BlockSpec((tm, D), lambda i: (i, 0))
    rowspec = pl.BlockSpec((1, D), lambda i: (0, 0))
    return pl.pallas_call(
        kern,
        grid=(n,),
        in_specs=[pl.BlockSpec((SUBLANES, tm), lambda i: (0, i), memory_space=pltpu.SMEM),
                  pl.BlockSpec((SUBLANES, tm), lambda i: (0, jnp.minimum(i + 1, n - 1)), memory_space=pltpu.SMEM),
                  pl.BlockSpec(memory_space=pl.ANY),
                  pl.BlockSpec((tm, W), lambda i: (i, 0)), spec,
                  pl.BlockSpec((tm, SUBLANES), lambda i: (i, 0)),
                  pl.BlockSpec((D, F), lambda i: (0, 0)),
                  pl.BlockSpec((D, F), lambda i: (0, 0)),
                  pl.BlockSpec((F, D), lambda i: (0, 0)),
                  rowspec, rowspec, rowspec],
        out_specs=spec,
        out_shape=jax.ShapeDtypeStruct((T, D), F32),
        scratch_shapes=[pltpu.VMEM((2, TOP_K, tm, W), ys.dtype), pltpu.SemaphoreType.DMA((2,))],
        compiler_params=_cp("arbitrary"),
        name="moe_combine_norm",
    )(dest, dest, ys, u, x, gates_t, w_sg, w_su, w_sd, gate, ln_g.reshape(1, D), ln_b.reshape(1, D))


def _moe_plan(top_e, rank, counts, bm, n_blocks):
    counts = counts.astype(I32)
    padded = (counts + bm - 1) // bm * bm
    padded_end = jnp.cumsum(padded)
    padded_start = padded_end - padded
    onehot = top_e[:, :, None] == jnp.arange(N_EXPERTS, dtype=I32)
    dest = jnp.sum(jnp.where(onehot, padded_start, 0), axis=-1) + rank
    fill = jnp.maximum(padded_end - bm, padded_start)
    fill = jnp.minimum(fill, (n_blocks - 1) * bm)
    block_start = jnp.arange(n_blocks, dtype=I32) * bm
    block_expert = jnp.minimum(jnp.sum((padded_end[None, :] <= block_start[:, None]).astype(I32), axis=1),
                               N_EXPERTS - 1)
    n_valid = jnp.maximum(padded_end[-1:] // bm, 1).astype(I32)
    return dest.astype(I32), fill.astype(I32), block_expert, n_valid


def _in_proj_weights(w_in):
    o = SB_COLS
    gq = w_in[:, o:o + 2 * GLA_KEY_DIM + GLA_VALUE_DIM]
    o += 2 * GLA_KEY_DIM + GLA_VALUE_DIM
    glr = w_in[:, o:o + GLA_GATE_RANK]
    o += GLA_GATE_RANK
    gout = w_in[:, o:o + GLA_VALUE_DIM]
    o += GLA_VALUE_DIM
    pad = jnp.zeros((w_in.shape[0], 2 * LANES - GLA_GATE_RANK), w_in.dtype)
    w_gla = jnp.concatenate([gq, gout, glr, pad], axis=1)
    w_rw = w_in[:, o:o + RWKV_COLS]
    o += RWKV_COLS
    w_mg = w_in[:, o:o + MERGE_COLS]
    return w_in[:, :SB_COLS], w_gla, w_rw, w_mg


def kernel(x, c, w_ada, b_ada, w_in, gla_gk_up, gla_gk_bias, gla_norm_g, rw_mu, rw_w0, rw_w2, rw_a0,
           rw_a2, rw_g2, rw_k_k, rw_k_a, rw_r_k, rw_lnx_g, rw_lnx_b, rw_v0, rw_v1, rw_v2,
           w_br_sb, w_br_gla, w_br_rw, w_o, ln_g, ln_b, w_router, router_bias,
           w_exp_gate, w_exp_up, w_exp_down, w_sh_gate, w_sh_up, w_sh_down):
    B, T, D = x.shape
    assert B == 1 and D == D_MODEL
    n_blocks = -(-(T * TOP_K) // MOE_BM) + N_EXPERTS
    mod = _ada(c, w_ada, b_ada)
    xs = x.reshape(T, D)
    v_first = None
    for l in range(DEPTH):
        sh1, sc1, g1, sh2, sc2, g2 = (mod[l, j * D:(j + 1) * D].reshape(1, D) for j in range(6))
        w_sb, w_gla, w_rw, w_mg = _in_proj_weights(w_in[l])
        u1 = _modulate(xs, sc1, sh1)
        p_sb = _wmm(u1, w_sb, BF16, 1024, name="inproj_sb")
        p_gla = _wmm(u1, w_gla, F32, 832 * 2, name="inproj_gla")
        p_rw = _wmm(u1, w_rw, F32, 832 * 2, name="inproj_rwkv")
        p_mg = _wmm(u1, w_mg, BF16, 1024, name="inproj_merge")

        o_sb = _sb_attention(p_sb)
        o_gla = _gla(p_gla, gla_gk_up[l], gla_gk_bias[l], gla_norm_g[l])
        if l == 0:
            r, w, k, v, a, b, g = _rwkv_prep(p_rw, None, rw_mu[l], rw_w0[l], rw_w2[l], rw_a0[l], rw_a2[l],
                                             rw_g2[l], rw_k_k[l], rw_k_a[l], None, None, None)
            v_first = v
        else:
            r, w, k, v, a, b, g = _rwkv_prep(p_rw, v_first, rw_mu[l], rw_w0[l], rw_w2[l], rw_a0[l], rw_a2[l],
                                             rw_g2[l], rw_k_k[l], rw_k_a[l], rw_v0[l - 1], rw_v1[l - 1],
                                             rw_v2[l - 1])
        y = _rwkv_recurrence(r, w, k, v, a, b)
        o_rw = _rwkv_post(y, r, k, v, g, rw_lnx_g[l], rw_lnx_b[l], rw_r_k[l])

        merged = _merge(o_sb, o_gla, o_rw, p_mg, w_br_sb[l].astype(BF16), w_br_gla[l].astype(BF16),
                        w_br_rw[l].astype(BF16))
        xs = _oproj(merged, w_o[l].astype(BF16), xs, g1, ln_g[l, 0], ln_b[l, 0])

        u2, top_e, gates, rank, counts = _router(xs, sc2, sh2, w_router[l], router_bias[l])
        dest, fill, block_expert, n_valid = _moe_plan(top_e, rank, counts[:, 0], MOE_BM, n_blocks)
        x_sorted = _dispatch(u2, dest, fill, n_blocks * MOE_BM)
        y_sorted = _experts(x_sorted, block_expert, n_valid, w_exp_gate, w_exp_up, w_exp_down, l)
        xs = _combine(y_sorted, dest, u2, xs, gates.T, w_sh_gate[l].astype(BF16), w_sh_up[l].astype(BF16),
                      w_sh_down[l].astype(BF16), g2, ln_g[l, 1], ln_b[l, 1])
    return xs.reshape(B, T, D)
```

```python
import functools

import numpy as np
import jax
import jax.numpy as jnp
from jax import lax
from jax.experimental import pallas as pl
from jax.experimental.pallas import tpu as pltpu

F32 = jnp.float32
BF16 = jnp.bfloat16
I32 = jnp.int32
U32 = jnp.uint32

D_MODEL = 2048
DEPTH = 2
LN_EPS = 1e-5
DEEPNORM_ALPHA = (2 * DEPTH) ** 0.25

SB_HEADS = 8
SB_HEAD_DIM = 128
SB_WIDTH = SB_HEADS * SB_HEAD_DIM

GLA_HEADS = 4
GLA_HEAD_K = 128
GLA_HEAD_V = 256
GLA_KEY_DIM = GLA_HEADS * GLA_HEAD_K
GLA_VALUE_DIM = GLA_HEADS * GLA_HEAD_V
GLA_GATE_RANK = 16
GLA_GATE_NORMALIZER = 16.0
GLA_CHUNK = 64
GLA_NORM_EPS = 1e-5

RWKV_HEADS = 16
RWKV_HEAD_DIM = 64
RWKV_WIDTH = RWKV_HEADS * RWKV_HEAD_DIM
RWKV_DECAY_RANK = 64
RWKV_AAA_RANK = 64
RWKV_VALUE_RANK = 32
RWKV_GATE_RANK = 128
RWKV_LNX_EPS = 64e-5

N_EXPERTS = 64
TOP_K = 6
EXPERT_DIM = 512
SHARED_DIM = 512
N_GROUPS = 8
TOPK_GROUPS = 4
ROUTED_SCALE = 2.5

SB_COLS = 3 * SB_WIDTH
GLA_COLS = 2 * GLA_KEY_DIM + GLA_VALUE_DIM + GLA_GATE_RANK + GLA_VALUE_DIM
RWKV_COLS = 3 * RWKV_WIDTH + RWKV_DECAY_RANK + RWKV_AAA_RANK + RWKV_GATE_RANK
MERGE_COLS = 3 * D_MODEL

LANES = 128
SUBLANES = 8
VMEM_LIMIT = 56 * 1024 * 1024
MOE_BM = 512
SB_EXIT = -104.0
SB_GROUPS = 2

NN = (((1,), (0,)), ((), ()))
NT = (((1,), (1,)), ((), ()))
TN = (((0,), (0,)), ((), ()))


def _cp(*sem):
    return pltpu.CompilerParams(dimension_semantics=sem, vmem_limit_bytes=VMEM_LIMIT)


def _dot(a, b, dims=NN):
    return lax.dot_general(a, b, dims, preferred_element_type=F32)


def _split(a, n):
    out = []
    r = a
    for i in range(n):
        p = r.astype(BF16)
        out.append(p)
        if i + 1 < n:
            r = r - p.astype(F32)
    return out


def _dot_fx(a, e, n, dims=NN):
    acc = None
    for p in _split(a, n):
        t = _dot(p, e, dims)
        acc = t if acc is None else acc + t
    return acc


def _dot_xf(e, b, n, dims=NN):
    acc = None
    for p in _split(b, n):
        t = _dot(e, p, dims)
        acc = t if acc is None else acc + t
    return acc


def _dot_ff(a, b, dims=NN):
    a1, a2, a3 = _split(a, 3)
    b1, b2, b3 = _split(b, 3)
    acc = _dot(a1, b1, dims)
    for p, q in ((a1, b2), (a2, b1), (a2, b2), (a1, b3), (a3, b1)):
        acc = acc + _dot(p, q, dims)
    return acc


def _sigmoid(x):
    return 1.0 / (1.0 + jnp.exp(-x))


def _softplus(x):
    return jnp.maximum(x, 0.0) + jnp.log1p(jnp.exp(-jnp.abs(x)))


def _silu(x):
    return x * _sigmoid(x)


def _row(ref, t):
    return ref.at[pl.ds(t, 1)]


def _pack_pairs(x):
    w = x.shape[1] // 2
    bits = lax.bitcast_convert_type(x.astype(BF16).astype(F32), U32)
    return (bits[:, :w] >> 16) | bits[:, w:]


def _unpack_pairs(p):
    lo = lax.bitcast_convert_type(p << 16, F32)
    hi = lax.bitcast_convert_type(p & jnp.uint32(0xFFFF0000), F32)
    return jnp.concatenate([lo, hi], axis=1)


def _ada_kernel(c_ref, w_ref, b_ref, o_ref):
    s = _silu(c_ref[...])
    o_ref[0] = jnp.sum(s * w_ref[0], axis=0, keepdims=True) + b_ref[0]


def _ada(c, w_ada, b_ada):
    L, D, N = w_ada.shape
    tn = 512
    out = pl.pallas_call(
        _ada_kernel,
        grid=(L, N // tn),
        in_specs=[pl.BlockSpec((D, 1), lambda l, j: (0, 0)),
                  pl.BlockSpec((1, D, tn), lambda l, j: (l, 0, j)),
                  pl.BlockSpec((1, 1, tn), lambda l, j: (l, 0, j))],
        out_specs=pl.BlockSpec((1, 1, tn), lambda l, j: (l, 0, j)),
        out_shape=jax.ShapeDtypeStruct((L, 1, N), F32),
        compiler_params=_cp("parallel", "parallel"),
        name="ada",
    )(c.reshape(D, 1), w_ada, b_ada.reshape(L, 1, N))
    return out[:, 0, :]


def _modulate_kernel(x_ref, sc_ref, sh_ref, o_ref):
    o_ref[...] = (x_ref[...] * (1.0 + sc_ref[...]) + sh_ref[...]).astype(o_ref.dtype)


def _modulate(x, sc, sh, tm=1024):
    T, D = x.shape
    spec = pl.BlockSpec((tm, D), lambda i: (i, 0))
    rowspec = pl.BlockSpec((1, D), lambda i: (0, 0))
    return pl.pallas_call(
        _modulate_kernel,
        grid=(T // tm,),
        in_specs=[spec, rowspec, rowspec],
        out_specs=spec,
        out_shape=jax.ShapeDtypeStruct((T, D), BF16),
        compiler_params=_cp("parallel"),
        name="modulate",
    )(x, sc, sh)


def _wmm_kernel(u_ref, w_ref, o_ref, wb_ref):
    @pl.when(pl.program_id(1) == 0)
    def _():
        wb_ref[...] = w_ref[...].astype(BF16)

    o_ref[...] = _dot(u_ref[...], wb_ref[...]).astype(o_ref.dtype)


def _wmm(u, w, out_dtype, tn, tm=512, name="wmm"):
    T, D = u.shape
    N = w.shape[1]
    return pl.pallas_call(
        _wmm_kernel,
        grid=(N // tn, T // tm),
        in_specs=[pl.BlockSpec((tm, D), lambda j, i: (i, 0)),
                  pl.BlockSpec((D, tn), lambda j, i: (0, j))],
        out_specs=pl.BlockSpec((tm, tn), lambda j, i: (i, j)),
        out_shape=jax.ShapeDtypeStruct((T, N), out_dtype),
        scratch_shapes=[pltpu.VMEM((D, tn), BF16)],
        compiler_params=_cp("parallel", "arbitrary"),
        name=name,
    )(u, w)


def _sb_kernel(q_ref, k_ref, v_ref, o_ref, acc_ref, car_ref, *, bq, scale):
    i = pl.program_id(1)
    q = q_ref[...]
    r_id = lax.broadcasted_iota(I32, (bq, bq), 0)
    c_id = lax.broadcasted_iota(I32, (bq, bq), 1)
    later = (r_id > c_id).astype(BF16)
    causal = c_id < r_id

    def block(j, carry, diagonal):
        off = pl.multiple_of(j * bq, bq)
        kb = k_ref[pl.ds(off, bq), :]
        vb = v_ref[pl.ds(off, bq), :]
        grp = [slice(g * (bq // SB_GROUPS), (g + 1) * (bq // SB_GROUPS)) for g in range(SB_GROUPS)]
        z = [_dot(q[s], kb, NT) * scale for s in grp]
        drop = [jnp.maximum(t, 0.0) + jnp.log(1.0 + jnp.exp(-jnp.abs(t))) for t in z]
        if diagonal:
            drop = [jnp.where(causal[s], d, 0.0) for s, d in zip(grp, drop)]
        after = [_dot_fx(d, later, 2) + carry[s] for s, d in zip(grp, drop)]
        w = [jnp.exp(t - d - a) for t, d, a in zip(z, drop, after)]
        if diagonal:
            w = [jnp.where(causal[s], x, 0.0) for s, x in zip(grp, w)]
        pv = jnp.concatenate([_dot(x.astype(BF16), vb) for x in w], axis=0)
        new_carry = jnp.concatenate([a[:, 0:1] + d[:, 0:1] for a, d in zip(after, drop)], axis=0)
        return pv, new_carry

    pv, carry = block(i, jnp.zeros((bq, 1), F32), True)
    acc_ref[...] = pv
    car_ref[...] = carry

    def body(c):
        j, _ = c
        pv, carry = block(j, car_ref[...], False)
        acc_ref[...] += pv
        car_ref[...] = carry
        return j - 1, (jnp.min(carry) > -SB_EXIT).astype(I32)

    lax.while_loop(lambda c: jnp.logical_and(c[0] >= 0, c[1] == 0), body, (i - 1, jnp.int32(0)))
    o_ref[...] = acc_ref[...].astype(o_ref.dtype)


def _sb_attention(qkv, bq=256):
    T = qkv.shape[0]
    H, Dh = SB_HEADS, SB_HEAD_DIM
    kern = functools.partial(_sb_kernel, bq=bq, scale=Dh ** -0.5)
    return pl.pallas_call(
        kern,
        grid=(H, T // bq),
        in_specs=[pl.BlockSpec((bq, Dh), lambda h, i: (i, h)),
                  pl.BlockSpec((T, Dh), lambda h, i: (0, H + h)),
                  pl.BlockSpec((T, Dh), lambda h, i: (0, 2 * H + h))],
        out_specs=pl.BlockSpec((bq, Dh), lambda h, i: (i, h)),
        out_shape=jax.ShapeDtypeStruct((T, H * Dh), BF16),
        scratch_shapes=[pltpu.VMEM((bq, Dh), F32), pltpu.VMEM((bq, 1), F32)],
        compiler_params=_cp("parallel", "parallel"),
        name="sb_attention",
    )(qkv, qkv, qkv)


_GLA_LEVELS = (32, 16, 8, 4, 2, 1)


def _gla_constants():
    C = GLA_CHUNK
    i = np.arange(C)[:, None]
    s = np.arange(C)[None, :]
    sel = []
    masks = []
    for m in _GLA_LEVELS:
        same = (i // m) == (s // m)
        sel.append(same & (s <= i) & ((i // m) % 2 == 1))
        sel.append(same & (s > i) & ((i // m) % 2 == 0))
        masks.append(((i // (2 * m)) == (s // (2 * m))) & ((i // m) % 2 == 1) & ((s // m) % 2 == 0))
    sel.append(s <= i)
    sel.append(s > i)
    masks.append(i == s)
    return (np.concatenate(sel, 0).astype(np.float32), np.stack(masks, 0).astype(np.float32))


def _gla_kernel(q_ref, k_ref, v_ref, go_ref, glr_ref, up_ref, bias_ref, ng_ref, sel_ref, msk_ref,
                o_ref, st_ref, *, n_chunks):
    C = GLA_CHUNK
    nl = len(_GLA_LEVELS)

    @pl.when(pl.program_id(1) == 0)
    def _():
        st_ref[...] = jnp.zeros_like(st_ref)

    sel = sel_ref[...]
    up = up_ref[...]
    bias = bias_ref[...]
    ng = ng_ref[...]
    rows = [slice(c * C, (c + 1) * C) for c in range(n_chunks)]
    blk = lambda f, j: f[j * C:(j + 1) * C]
    q = [q_ref[s, :] * (GLA_HEAD_K ** -0.5) for s in rows]
    k = [k_ref[s, :] for s in rows]
    v = [v_ref[s, :].astype(BF16) for s in rows]
    x = [_dot(glr_ref[s, :].astype(BF16), up) + bias for s in rows]
    g = [-_softplus(-t) * (1.0 / GLA_GATE_NORMALIZER) for t in x]
    f = [jnp.exp(_dot_xf(sel, t, 2)) for t in g]
    scores = [_dot(a.astype(BF16), b.astype(BF16), NT) * msk_ref[nl] for a, b in zip(q, k)]
    for l in range(nl):
        ql = [(a * blk(t, 2 * l)).astype(BF16) for a, t in zip(q, f)]
        kl = [(b * blk(t, 2 * l + 1)).astype(BF16) for b, t in zip(k, f)]
        scores = [s + _dot(a, b, NT) * msk_ref[l] for s, a, b in zip(scores, ql, kl)]
    o_intra = [_dot(s.astype(BF16), b) for s, b in zip(scores, v)]
    qe = [(a * blk(t, 2 * nl)).astype(BF16) for a, t in zip(q, f)]
    upd = [_dot(b, (a * blk(t, 2 * nl + 1)).astype(BF16), TN) for b, a, t in zip(v, k, f)]
    dec = [blk(t, 2 * nl)[C - 1:C, :] for t in f]

    st = st_ref[...]
    for c in range(n_chunks):
        o = _dot(qe[c], st.astype(BF16), NT) + o_intra[c]
        st = st * dec[c] + upd[c]
        rms = lax.rsqrt(jnp.mean(o * o, axis=-1, keepdims=True) + GLA_NORM_EPS)
        o_ref[rows[c], :] = (o * rms * ng * _silu(go_ref[rows[c], :])).astype(o_ref.dtype)
    st_ref[...] = st


def _gla(p_gla, gk_up, gk_bias, norm_g, tt=512):
    T = p_gla.shape[0]
    H, dk, dv = GLA_HEADS, GLA_HEAD_K, GLA_HEAD_V
    sel, msk = _gla_constants()
    up = jnp.zeros((LANES, GLA_KEY_DIM), F32).at[:GLA_GATE_RANK].set(gk_up).astype(BF16)
    kern = functools.partial(_gla_kernel, n_chunks=tt // GLA_CHUNK)
    return pl.pallas_call(
        kern,
        grid=(H, T // tt),
        in_specs=[pl.BlockSpec((tt, dk), lambda h, i: (i, h)),
                  pl.BlockSpec((tt, dk), lambda h, i: (i, H + h)),
                  pl.BlockSpec((tt, dv), lambda h, i: (i, H + h)),
                  pl.BlockSpec((tt, dv), lambda h, i: (i, 2 * H + h)),
                  pl.BlockSpec((tt, LANES), lambda h, i: (i, 3 * GLA_VALUE_DIM // LANES)),
                  pl.BlockSpec((LANES, dk), lambda h, i: (0, h)),
                  pl.BlockSpec((1, dk), lambda h, i: (0, h)),
                  pl.BlockSpec((1, dv), lambda h, i: (0, 0)),
                  pl.BlockSpec(sel.shape, lambda h, i: (0, 0)),
                  pl.BlockSpec(msk.shape, lambda h, i: (0, 0, 0))],
        out_specs=pl.BlockSpec((tt, dv), lambda h, i: (i, h)),
        out_shape=jax.ShapeDtypeStruct((T, H * dv), BF16),
        scratch_shapes=[pltpu.VMEM((dv, dk), F32)],
        compiler_params=_cp("parallel", "arbitrary"),
        name="gla",
    )(p_gla, p_gla, p_gla, p_gla, p_gla, up, gk_bias.reshape(1, -1), norm_g.reshape(1, -1),
      jnp.asarray(sel, BF16), jnp.asarray(msk, F32))


def _head_ones():
    l = np.arange(LANES)
    return (l[:, None] // RWKV_HEAD_DIM == l[None, :] // RWKV_HEAD_DIM).astype(np.float32)


def _head_diag():
    i = np.arange(RWKV_HEAD_DIM)[:, None]
    l = np.arange(LANES)[None, :]
    return (l % RWKV_HEAD_DIM == i).astype(np.float32)


def _rwprep_kernel(p_ref, pp_ref, mu_ref, w0_ref, w2_ref, a0_ref, a2_ref, g2_ref, kk_ref, ka_ref,
                   v0_ref, v1_ref, v2_ref, vf_ref, bo_ref,
                   r_ref, w_ref, k_ref, v_ref, a_ref, b_ref, g_ref, *, first_layer):
    W = RWKV_WIDTH
    i = pl.program_id(0)
    p = p_ref[...]
    tm = p.shape[0]
    prev_row = jnp.where(i == 0, 0.0, pp_ref[SUBLANES - 1:SUBLANES, :])
    rid = lax.broadcasted_iota(I32, p.shape, 0)
    prev = jnp.where(rid == 0, prev_row, pltpu.roll(p, 1, axis=0))
    ps = p + (prev - p) * mu_ref[...]
    r = ps[:, 0:W]
    kr = ps[:, W:2 * W]
    vr = ps[:, 2 * W:3 * W]
    lr = ps[:, 3 * W:3 * W + LANES]
    gl = ps[:, 3 * W + LANES:3 * W + 2 * LANES]
    w_log = -_softplus(-(w0_ref[...] + _dot(jnp.tanh(lr).astype(BF16), w2_ref[...]))) - 0.5
    w_ref[...] = -jnp.exp(w_log)
    if not first_layer:
        lo = _dot(vr.astype(BF16), v1_ref[...])
        gate = _sigmoid(v0_ref[...] + _dot(lo.astype(BF16), v2_ref[...]))
        vr = vr + (vf_ref[...] - vr) * gate
    v_ref[...] = vr
    a = _sigmoid(a0_ref[...] + _dot(lr.astype(BF16), a2_ref[...]))
    g_ref[...] = _dot(_sigmoid(gl).astype(BF16), g2_ref[...])
    kk = kr * kk_ref[...]
    bo = bo_ref[...]
    for hp in range(W // LANES):
        sl = slice(hp * LANES, (hp + 1) * LANES)
        kh = kk[:, sl]
        ss = _dot_fx(kh * kh, bo, 3)
        kn = kh / jnp.maximum(jnp.sqrt(ss), 1e-12)
        a_ref[:, sl] = -kn
        b_ref[:, sl] = kn * a[:, sl]
    r_ref[...] = r
    k_ref[...] = kr * (1.0 + (a - 1.0) * ka_ref[...])


def _rwkv_prep(p_rw, v_first, mu, w0, w2, a0, a2, g2, k_k, k_a, v0, v1, v2, tm=256):
    T, NC = p_rw.shape
    W = RWKV_WIDTH
    first = v_first is None
    w2p = jnp.zeros((LANES, W), F32).at[:RWKV_DECAY_RANK].set(w2).astype(BF16)
    a2p = jnp.zeros((LANES, W), F32).at[RWKV_DECAY_RANK:].set(a2).astype(BF16)
    if first:
        v_first = jnp.zeros((SUBLANES, W), F32)
        v0 = jnp.zeros((W,), F32)
        v1p = jnp.zeros((W, LANES), BF16)
        v2p = jnp.zeros((LANES, W), BF16)
        vf_spec = pl.BlockSpec((SUBLANES, W), lambda i: (0, 0))
    else:
        v1p = jnp.zeros((W, LANES), F32).at[:, :RWKV_VALUE_RANK].set(v1).astype(BF16)
        v2p = jnp.zeros((LANES, W), F32).at[:RWKV_VALUE_RANK].set(v2).astype(BF16)
        vf_spec = pl.BlockSpec((tm, W), lambda i: (i, 0))
    row = lambda a: a.reshape(1, -1)
    full = lambda shape: pl.BlockSpec(shape, lambda i: (0,) * len(shape))
    out = jax.ShapeDtypeStruct((T, W), F32)
    ospec = pl.BlockSpec((tm, W), lambda i: (i, 0))
    kern = functools.partial(_rwprep_kernel, first_layer=first)
    return pl.pallas_call(
        kern,
        grid=(T // tm,),
        in_specs=[pl.BlockSpec((tm, NC), lambda i: (i, 0)),
                  pl.BlockSpec((SUBLANES, NC), lambda i: (jnp.maximum(i * (tm // SUBLANES) - 1, 0), 0)),
                  full((1, NC)), full((1, W)), full((LANES, W)), full((1, W)), full((LANES, W)),
                  full((LANES, W)), full((1, W)), full((1, W)),
                  full((1, W)), full((W, LANES)), full((LANES, W)), vf_spec, full((LANES, LANES))],
        out_specs=[ospec] * 7,
        out_shape=[out] * 7,
        compiler_params=_cp("parallel"),
        name="rwkv_prep",
    )(p_rw, p_rw, row(mu), row(w0), w2p, row(a0), a2p, g2.astype(BF16), row(k_k), row(k_a),
      row(v0), v1p, v2p, v_first, jnp.asarray(_head_ones(), BF16))


RW_CHUNK = 64


def _rw_constants():
    C = RW_CHUNK
    i = np.arange(C)[:, None]
    s = np.arange(C)[None, :]
    i2 = np.arange(2 * C)[:, None]
    s2 = np.arange(2 * C)[None, :]
    same = (i2 // C) == (s2 // C)
    cm = np.stack([same & (s2 < i2), same & (s2 <= i2), (i2 // 16) == (s2 // 16), (i2 // 32) == (s2 // 32),
                   same], 0).astype(np.float32)
    return (s <= i).astype(np.float32), cm


def _mm(a, b, dims=NN):
    a1, a2 = _split(a, 2)
    b1, b2 = _split(b, 2)
    r = _dot(a1, jnp.concatenate([b1, b2], axis=0 if dims == NT else 1), dims)
    n = r.shape[1] // 2
    return (r[:, :n] + r[:, n:]) + _dot(a2, b1, dims)


def _mm1(a, b, dims=NN):
    return _dot(a.astype(BF16), b.astype(BF16), dims)


def _unit_lower_inverse(nms, eye, bd16, bd32):
    size = eye.shape[0]
    nb = size // 16

    def pack(m):
        acc = m[0:16]
        for b in range(1, nb):
            acc = acc + m[16 * b:16 * (b + 1)]
        return acc

    def unpack(s):
        return jnp.concatenate([s] * nb, axis=0) * bd16

    nd = [n * bd16 for n in nms]
    x = [pack(eye + n) for n in nd]
    p = [_mm(pack(n), n) for n in nd]
    for step in range(3):
        pf = [unpack(q) for q in p]
        x = [a + _mm(a, q) for a, q in zip(x, pf)]
        if step < 2:
            p = [_mm(q, qf) for q, qf in zip(p, pf)]
    x = [unpack(a) for a in x]
    for lo_mask in (bd32 - bd16, 1.0 - bd32):
        t = [_mm(n * lo_mask, a) for n, a in zip(nms, x)]
        x = [a + _mm(a, q) for a, q in zip(x, t)]
    return x


def _rwrec_kernel(r_ref, lw_ref, k_ref, v_ref, a_ref, b_ref, lt_ref, cm_ref, y_ref, mt_ref, *, n_chunks):
    C = RW_CHUNK
    C2 = 2 * C

    @pl.when(pl.program_id(1) == 0)
    def _():
        mt_ref[...] = jnp.zeros_like(mt_ref)

    lt = lt_ref[...]
    sl, li, bd16, bd32, bdh = cm_ref[0], cm_ref[1], cm_ref[2], cm_ref[3], cm_ref[4]
    eye = li - sl
    lane = lax.broadcasted_iota(I32, (1, LANES), 1)
    m0 = jnp.where(lane < RWKV_HEAD_DIM, 1.0, 0.0)
    m1 = 1.0 - m0
    stack = lambda x: jnp.concatenate([x * m0, x * m1], axis=0)
    fold = lambda x: x[:C] + x[C:]
    rows = [slice(c * C, (c + 1) * C) for c in range(n_chunks)]

    lw = [lw_ref[s, :] for s in rows]
    cw = [_dot_xf(lt, x, 3) for x in lw]
    en = [jnp.exp(-x) for x in cw]
    at = [a_ref[s, :] * jnp.exp(c - l) for s, c, l in zip(rows, cw, lw)]
    rt = [r_ref[s, :] * jnp.exp(c) for s, c in zip(rows, cw)]
    bh = [b_ref[s, :] * e for s, e in zip(rows, en)]
    kh = [k_ref[s, :] * e for s, e in zip(rows, en)]
    v = [v_ref[s, :] for s in rows]
    a2 = [stack(x) for x in at]
    ar2 = [jnp.concatenate([x, stack(y)], axis=0) for x, y in zip(a2, rt)]
    gb = [_mm(x, stack(y), NT) for x, y in zip(ar2, bh)]
    gk = [_mm(x, stack(y), NT) for x, y in zip(ar2, kh)]
    tinv = _unit_lower_inverse([g[:C2] * sl for g in gb], eye, bd16, bd32)
    akv = [_mm(jnp.concatenate([g[:C2] * sl, g[C2:] * li], axis=0), stack(x)) for g, x in zip(gk, v)]
    tav = [_mm(t, jnp.concatenate([x, y[:C2]], axis=1)) for t, x, y in zip(tinv, a2, akv)]
    rav = [_mm1(g[C2:] * li, x) for g, x in zip(gb, tav)]
    ta = [fold(x[:, :LANES]) for x in tav]
    u0 = [fold(x[:, LANES:]) for x in tav]
    qe = [x + fold(y[:, :LANES]) for x, y in zip(rt, rav)]
    y0 = [fold(x[:, LANES:] + y[C2:]) for x, y in zip(rav, akv)]
    dl = [jnp.exp(x[C - 1:C, :]) for x in cw]
    gmt = [(eye + _mm(x, y, TN) * bdh) * d for x, y, d in zip(ta, bh, dl)]
    hmt = [(_mm(x, y, TN) + _mm(z, w, TN)) * bdh * d for x, y, z, w, d in zip(u0, bh, v, kh, dl)]

    mt = mt_ref[...]
    for c in range(n_chunks):
        y_ref[rows[c], :] = _mm1(qe[c], mt, NT) + y0[c]
        mt = _mm(mt, gmt[c]) + hmt[c]
    mt_ref[...] = mt


def _rwkv_recurrence(r, lw, k, v, a, b, tt=512):
    T, W = r.shape
    lt, cm = _rw_constants()
    spec = pl.BlockSpec((tt, LANES), lambda p, i: (i, p))
    kern = functools.partial(_rwrec_kernel, n_chunks=tt // RW_CHUNK)
    return pl.pallas_call(
        kern,
        grid=(W // LANES, T // tt),
        in_specs=[spec] * 6 + [pl.BlockSpec(lt.shape, lambda p, i: (0, 0)),
                               pl.BlockSpec(cm.shape, lambda p, i: (0, 0, 0))],
        out_specs=spec,
        out_shape=jax.ShapeDtypeStruct((T, W), F32),
        scratch_shapes=[pltpu.VMEM((LANES, LANES), F32)],
        compiler_params=_cp("parallel", "arbitrary"),
        name="rwkv_recurrence",
    )(r, lw, k, v, a, b, jnp.asarray(lt, BF16), jnp.asarray(cm, F32))


def _rwpost_kernel(y_ref, r_ref, k_ref, v_ref, g_ref, lg_ref, lb_ref, rk_ref, bo_ref, o_ref):
    bo = bo_ref[...]
    inv = 1.0 / RWKV_HEAD_DIM
    for hp in range(RWKV_WIDTH // LANES):
        sl = slice(hp * LANES, (hp + 1) * LANES)
        y = y_ref[:, sl]
        mu = _dot_fx(y, bo, 3) * inv
        d = y - mu
        var = _dot_fx(d * d, bo, 3) * inv
        yn = d * lax.rsqrt(var + RWKV_LNX_EPS) * lg_ref[:, sl] + lb_ref[:, sl]
        bonus = _dot_fx(r_ref[:, sl] * k_ref[:, sl] * rk_ref[:, sl], bo, 3) * v_ref[:, sl]
        o_ref[:, sl] = ((yn + bonus) * g_ref[:, sl]).astype(o_ref.dtype)


def _rwkv_post(y, r, k, v, g, lnx_g, lnx_b, r_k, tm=512):
    T, W = y.shape
    spec = pl.BlockSpec((tm, W), lambda i: (i, 0))
    rowspec = pl.BlockSpec((1, W), lambda i: (0, 0))
    return pl.pallas_call(
        _rwpost_kernel,
        grid=(T // tm,),
        in_specs=[spec] * 5 + [rowspec] * 3 + [pl.BlockSpec((LANES, LANES), lambda i: (0, 0))],
        out_specs=spec,
        out_shape=jax.ShapeDtypeStruct((T, W), BF16),
        compiler_params=_cp("parallel"),
        name="rwkv_post",
    )(y, r, k, v, g, lnx_g.reshape(1, W), lnx_b.reshape(1, W), r_k.reshape(1, W),
      jnp.asarray(_head_ones(), BF16))


def _merge_kernel(osb_ref, ogl_ref, orw_ref, g1_ref, g2_ref, g3_ref, w1_ref, w2_ref, w3_ref, o_ref):
    acc = _sigmoid(g1_ref[...].astype(F32)) * _dot(osb_ref[...], w1_ref[...])
    acc = acc + _sigmoid(g2_ref[...].astype(F32)) * _dot(ogl_ref[...], w2_ref[...])
    acc = acc + _sigmoid(g3_ref[...].astype(F32)) * _dot(orw_ref[...], w3_ref[...])
    o_ref[...] = acc.astype(o_ref.dtype)


def _merge(o_sb, o_gla, o_rw, p_merge, w_sb, w_gla, w_rw, tm=512, tn=1024):
    T = o_sb.shape[0]
    D = D_MODEL
    nb = D // tn
    aspec = pl.BlockSpec((tm, 1024), lambda j, i: (i, 0))
    wspec = pl.BlockSpec((1024, tn), lambda j, i: (0, j))
    gspec = lambda b: pl.BlockSpec((tm, tn), lambda j, i: (i, b * nb + j))
    return pl.pallas_call(
        _merge_kernel,
        grid=(nb, T // tm),
        in_specs=[aspec, aspec, aspec, gspec(0), gspec(1), gspec(2), wspec, wspec, wspec],
        out_specs=pl.BlockSpec((tm, tn), lambda j, i: (i, j)),
        out_shape=jax.ShapeDtypeStruct((T, D), BF16),
        compiler_params=_cp("parallel", "parallel"),
        name="merge",
    )(o_sb, o_gla, o_rw, p_merge, p_merge, p_merge, w_sb, w_gla, w_rw)


def _layer_norm(h, g, b):
    mu = jnp.mean(h, axis=-1, keepdims=True)
    d = h - mu
    var = jnp.mean(d * d, axis=-1, keepdims=True)
    return d * lax.rsqrt(var + LN_EPS) * g + b


def _oproj_kernel(m_ref, w_ref, x_ref, gate_ref, lg_ref, lb_ref, o_ref):
    mix = _dot(m_ref[...], w_ref[...])
    h = DEEPNORM_ALPHA * x_ref[...] + (1.0 + gate_ref[...]) * mix
    o_ref[...] = _layer_norm(h, lg_ref[...], lb_ref[...])


def _oproj(merged, w_o, x, gate, ln_g, ln_b, tm=512):
    T, D = x.shape
    spec = pl.BlockSpec((tm, D), lambda i: (i, 0))
    rowspec = pl.BlockSpec((1, D), lambda i: (0, 0))
    return pl.pallas_call(
        _oproj_kernel,
        grid=(T // tm,),
        in_specs=[spec, pl.BlockSpec((D, D), lambda i: (0, 0)), spec, rowspec, rowspec, rowspec],
        out_specs=spec,
        out_shape=jax.ShapeDtypeStruct((T, D), F32),
        compiler_params=_cp("parallel"),
        name="oproj_norm",
    )(merged, w_o, x, gate, ln_g.reshape(1, D), ln_b.reshape(1, D))


def _first_max(x, idx, axis, big):
    m = jnp.max(x, axis=axis, keepdims=True)
    first = jnp.min(jnp.where(x == m, idx, big), axis=axis, keepdims=True)
    return m, idx == first


def _router_kernel(x_ref, sc_ref, sh_ref, wr_ref, rb_ref, u_ref, e_ref, g_ref, p_ref, cnt_ref, car_ref, *, tm):
    E, G = N_EXPERTS, N_GROUPS
    per = E // G
    neg = -jnp.inf

    @pl.when(pl.program_id(0) == 0)
    def _():
        car_ref[...] = jnp.zeros_like(car_ref)

    u = x_ref[...] * (1.0 + sc_ref[...]) + sh_ref[...]
    u_ref[...] = _pack_pairs(u)
    scores =_sigmoid(_dot_ff(wr_ref[...], u, NT))
    biased = scores + rb_ref[...]
    b3 = biased.reshape(G, per, tm)
    i3 = lax.broadcasted_iota(I32, (G, per, tm), 1)
    m1, hit = _first_max(b3, i3, 1, per)
    m2 = jnp.max(jnp.where(hit, neg, b3), axis=1, keepdims=True)
    gs = (m1 + m2).reshape(G, tm)
    gi = lax.broadcasted_iota(I32, (G, tm), 0)
    ok = jnp.zeros((G, tm), jnp.bool_)
    for _ in range(TOPK_GROUPS):
        _, hit = _first_max(gs, gi, 0, G)
        ok = jnp.logical_or(ok, hit)
        gs = jnp.where(hit, neg, gs)
    ok_e = jnp.broadcast_to(ok.reshape(G, 1, tm), (G, per, tm)).reshape(E, tm)
    cand = jnp.where(ok_e, biased, neg)
    ei = lax.broadcasted_iota(I32, (E, tm), 0)
    sels, gates = [], []
    for _ in range(TOP_K):
        _, hit = _first_max(cand, ei, 0, E)
        sels.append(hit)
        gates.append(jnp.sum(jnp.where(hit, scores, 0.0), axis=0, keepdims=True))
        cand = jnp.where(hit, neg, cand)
    denom = gates[0]
    for gk in gates[1:]:
        denom = denom + gk
    chosen = sels[0]
    for s in sels[1:]:
        chosen = jnp.logical_or(chosen, s)
    cnt = jnp.where(chosen, 1.0, 0.0)
    earlier = (lax.broadcasted_iota(I32, (tm, tm), 0) < lax.broadcasted_iota(I32, (tm, tm), 1)).astype(BF16)
    rank = _dot(cnt.astype(BF16), earlier) + car_ref[...][:, 0:1]
    car_ref[...] = car_ref[...] + jnp.sum(cnt, axis=1, keepdims=True)
    cnt_ref[...] = car_ref[...]
    zi = jnp.zeros((1, tm), I32)
    zf = jnp.zeros((1, tm), F32)
    e_rows = [jnp.sum(jnp.where(s, ei, 0), axis=0, keepdims=True) for s in sels]
    p_rows = [jnp.sum(jnp.where(s, rank, 0.0), axis=0, keepdims=True).astype(I32) for s in sels]
    g_rows = [gk / denom * ROUTED_SCALE for gk in gates]
    e_ref[...] = jnp.concatenate(e_rows + [zi, zi], axis=0)
    p_ref[...] = jnp.concatenate(p_rows + [zi, zi], axis=0)
    g_ref[...] = jnp.concatenate(g_rows + [zf, zf], axis=0)


def _router(x, sc, sh, w_router, router_bias, tm=256):
    T, D = x.shape
    E = N_EXPERTS
    kern = functools.partial(_router_kernel, tm=tm)
    rowspec = pl.BlockSpec((1, D), lambda i: (0, 0))
    kspec = pl.BlockSpec((SUBLANES, tm), lambda i: (0, i))
    return pl.pallas_call(
        kern,
        grid=(T // tm,),
        in_specs=[pl.BlockSpec((tm, D), lambda i: (i, 0)), rowspec, rowspec,
                  pl.BlockSpec((E, D), lambda i: (0, 0)), pl.BlockSpec((E, 1), lambda i: (0, 0))],
        out_specs=[pl.BlockSpec((tm, D // 2), lambda i: (i, 0)), kspec, kspec, kspec,
                   pl.BlockSpec((E, LANES), lambda i: (0, 0))],
        out_shape=[jax.ShapeDtypeStruct((T, D // 2), U32),
                   jax.ShapeDtypeStruct((SUBLANES, T), I32),
                   jax.ShapeDtypeStruct((SUBLANES, T), F32),
                   jax.ShapeDtypeStruct((SUBLANES, T), I32),
                   jax.ShapeDtypeStruct((E, LANES), F32)],
        scratch_shapes=[pltpu.VMEM((E, LANES), F32)],
        compiler_params=_cp("arbitrary"),
        name="router",
    )(x, sc, sh, w_router.T, router_bias.reshape(E, 1))


def _dispatch_kernel(dest_ref, fill_ref, u_ref, xs_ref, z_ref, sem, zsem, *, tm, bm):
    i = pl.program_id(0)

    @pl.when(i == 0)
    def _():
        z_ref[...] = jnp.zeros_like(z_ref)

        def zcopy(e):
            return pltpu.make_async_copy(z_ref, xs_ref.at[pl.ds(pl.multiple_of(fill_ref[e], bm), bm)], zsem)

        def zstart(e, c):
            zcopy(e).start()
            return c

        def zwait(e, c):
            zcopy(e).wait()
            return c

        lax.fori_loop(0, N_EXPERTS, zstart, 0)
        lax.fori_loop(0, N_EXPERTS, zwait, 0)

    for tok in range(tm):
        for k in range(TOP_K):
            pltpu.make_async_copy(_row(u_ref, tok), _row(xs_ref, dest_ref[k, tok]), sem).start()
    n = tm * TOP_K
    pltpu.make_async_copy(xs_ref.at[pl.ds(0, n)], xs_ref.at[pl.ds(0, n)], sem).wait()


def _dispatch(u, dest, fill, n_slots, tm=256, bm=MOE_BM):
    T, D = u.shape
    kern = functools.partial(_dispatch_kernel, tm=tm, bm=bm)
    return pl.pallas_call(
        kern,
        grid=(T // tm,),
        in_specs=[pl.BlockSpec((SUBLANES, tm), lambda i: (0, i), memory_space=pltpu.SMEM),
                  pl.BlockSpec(memory_space=pltpu.SMEM),
                  pl.BlockSpec((tm, D), lambda i: (i, 0))],
        out_specs=pl.BlockSpec(memory_space=pl.ANY),
        out_shape=jax.ShapeDtypeStruct((n_slots, D), u.dtype),
        scratch_shapes=[pltpu.VMEM((bm, D), u.dtype),
                        pltpu.SemaphoreType.DMA(()), pltpu.SemaphoreType.DMA(())],
        compiler_params=_cp("arbitrary"),
        name="moe_dispatch",
    )(dest, fill, u)


def _expert_kernel(be_ref, nv_ref, x_ref, wg_ref, wu_ref, wd_ref, y_ref, wgb_ref, wub_ref, wdb_ref, *, bm):
    b = pl.program_id(0)
    valid = b < nv_ref[0]
    new_expert = jnp.logical_or(b == 0, be_ref[b] != be_ref[jnp.maximum(b - 1, 0)])

    @pl.when(jnp.logical_and(valid, new_expert))
    def _():
        wgb_ref[...] = wg_ref[0].astype(BF16)
        wub_ref[...] = wu_ref[0].astype(BF16)
        wdb_ref[...] = wd_ref[0].astype(BF16)

    @pl.when(valid)
    def _():
        x = _unpack_pairs(x_ref[...]).astype(BF16)
        h = _silu(_dot(x, wgb_ref[...])) * _dot(x, wub_ref[...])
        y_ref[...] = _pack_pairs(_dot(h.astype(BF16), wdb_ref[...]))


def _experts(xs, block_expert, n_valid, w_gate, w_up, w_down, layer, bm=MOE_BM):
    n_slots, W = xs.shape
    D = 2 * W
    nb = n_slots // bm
    F = w_gate.shape[-1]
    blk = lambda b, be, nv: (jnp.minimum(b, nv[0] - 1), 0)
    wsel = lambda b, be, nv: (layer, be[jnp.minimum(b, nv[0] - 1)], 0, 0)
    gs = pltpu.PrefetchScalarGridSpec(
        num_scalar_prefetch=2,
        grid=(nb,),
        in_specs=[pl.BlockSpec((bm, W), blk),
                  pl.BlockSpec((None, 1, D, F), wsel),
                  pl.BlockSpec((None, 1, D, F), wsel),
                  pl.BlockSpec((None, 1, F, D), wsel)],
        out_specs=pl.BlockSpec((bm, W), blk),
        scratch_shapes=[pltpu.VMEM((D, F), BF16), pltpu.VMEM((D, F), BF16), pltpu.VMEM((F, D), BF16)],
    )
    return pl.pallas_call(
        functools.partial(_expert_kernel, bm=bm),
        grid_spec=gs,
        out_shape=jax.ShapeDtypeStruct((n_slots, W), U32),
        compiler_params=_cp("arbitrary"),
        name="moe_experts",
    )(block_expert, n_valid, xs, w_gate, w_up, w_down)


def _combine_kernel(dest_ref, dnext_ref, ys_ref, u_ref, x_ref, gt_ref, wg_ref, wu_ref, wd_ref, gate_ref, lg_ref,
                    lb_ref, o_ref, buf_ref, sem, *, tm):
    i = pl.program_id(0)
    last = pl.num_programs(0) - 1
    slot = lax.rem(i, 2)

    def row_copy(d_ref, s, tok, k):
        return pltpu.make_async_copy(_row(ys_ref, d_ref[k, tok]), _row(buf_ref.at[s, k], tok), sem.at[s])

    def wait_half(s):
        pltpu.make_async_copy(buf_ref.at[s], buf_ref.at[s], sem.at[s]).wait()

    @pl.when(i == 0)
    def _():
        def start(tok, c):
            for k in range(TOP_K):
                row_copy(dest_ref, slot, tok, k).start()
            return c

        lax.fori_loop(0, tm, start, 0)

    wait_half(slot)
    for tok in range(tm):
        for k in range(TOP_K):
            row_copy(dnext_ref, 1 - slot, tok, k).start()

    u = _unpack_pairs(u_ref[...]).astype(BF16)
    h = _silu(_dot(u, wg_ref[...])) * _dot(u, wu_ref[...])
    ffn = _dot(h.astype(BF16), wd_ref[...])
    gt = gt_ref[...]
    for k in range(TOP_K):
        ffn = ffn + gt[:, k:k + 1] * _unpack_pairs(buf_ref[slot, k])
    hres = DEEPNORM_ALPHA * x_ref[...] + (1.0 + gate_ref[...]) * ffn
    o_ref[...] = _layer_norm(hres, lg_ref[...], lb_ref[...])

    @pl.when(i == last)
    def _():
        wait_half(1 - slot)


def _combine(ys, dest, u, x, gates_t, w_sg, w_su, w_sd, gate, ln_g, ln_b, tm=128):
    T, D = x.shape
    W = ys.shape[1]
    F = w_sg.shape[-1]
    n = T // tm
    kern = functools.partial(_combine_kernel, tm=tm)
    spec = pl.BlockSpec((tm, D), lambda i: (i, 0))
    rowspec = pl.BlockSpec((1, D), lambda i: (0, 0))
    return pl.pallas_call(
        kern,
        grid=(n,),
        in_specs=[pl.BlockSpec((SUBLANES, tm), lambda i: (0, i), memory_space=pltpu.SMEM),
                  pl.BlockSpec((SUBLANES, tm), lambda i: (0, jnp.minimum(i + 1, n - 1)), memory_space=pltpu.SMEM),
                  pl.BlockSpec(memory_space=pl.ANY),
                  pl.BlockSpec((tm, W), lambda i: (i, 0)), spec,
                  pl.BlockSpec((tm, SUBLANES), lambda i: (i, 0)),
                  pl.BlockSpec((D, F), lambda i: (0, 0)),
                  pl.BlockSpec((D, F), lambda i: (0, 0)),
                  pl.BlockSpec((F, D), lambda i: (0, 0)),
                  rowspec, rowspec, rowspec],
        out_specs=spec,
        out_shape=jax.ShapeDtypeStruct((T, D), F32),
        scratch_shapes=[pltpu.VMEM((2, TOP_K, tm, W), ys.dtype), pltpu.SemaphoreType.DMA((2,))],
        compiler_params=_cp("arbitrary"),
        name="moe_combine_norm",
    )(dest, dest, ys, u, x, gates_t, w_sg, w_su, w_sd, gate, ln_g.reshape(1, D), ln_b.reshape(1, D))


def _moe_plan(top_e, rank, counts, bm, n_blocks):
    counts = counts.astype(I32)
    padded = (counts + bm - 1) // bm * bm
    padded_end = jnp.cumsum(padded)
    padded_start = padded_end - padded
    onehot = top_e[:, :, None] == jnp.arange(N_EXPERTS, dtype=I32)
    dest = jnp.sum(jnp.where(onehot, padded_start, 0), axis=-1) + rank
    fill = jnp.maximum(padded_end - bm, padded_start)
    fill = jnp.minimum(fill, (n_blocks - 1) * bm)
    block_start = jnp.arange(n_blocks, dtype=I32) * bm
    block_expert = jnp.minimum(jnp.sum((padded_end[None, :] <= block_start[:, None]).astype(I32), axis=1),
                               N_EXPERTS - 1)
    n_valid = jnp.maximum(padded_end[-1:] // bm, 1).astype(I32)
    return dest.astype(I32), fill.astype(I32), block_expert, n_valid


def _in_proj_weights(w_in):
    o = SB_COLS
    gq = w_in[:, o:o + 2 * GLA_KEY_DIM + GLA_VALUE_DIM]
    o += 2 * GLA_KEY_DIM + GLA_VALUE_DIM
    glr = w_in[:, o:o + GLA_GATE_RANK]
    o += GLA_GATE_RANK
    gout = w_in[:, o:o + GLA_VALUE_DIM]
    o += GLA_VALUE_DIM
    pad = jnp.zeros((w_in.shape[0], 2 * LANES - GLA_GATE_RANK), w_in.dtype)
    w_gla = jnp.concatenate([gq, gout, glr, pad], axis=1)
    w_rw = w_in[:, o:o + RWKV_COLS]
    o += RWKV_COLS
    w_mg = w_in[:, o:o + MERGE_COLS]
    return w_in[:, :SB_COLS], w_gla, w_rw, w_mg


def kernel(x, c, w_ada, b_ada, w_in, gla_gk_up, gla_gk_bias, gla_norm_g, rw_mu, rw_w0, rw_w2, rw_a0,
           rw_a2, rw_g2, rw_k_k, rw_k_a, rw_r_k, rw_lnx_g, rw_lnx_b, rw_v0, rw_v1, rw_v2,
           w_br_sb, w_br_gla, w_br_rw, w_o, ln_g, ln_b, w_router, router_bias,
           w_exp_gate, w_exp_up, w_exp_down, w_sh_gate, w_sh_up, w_sh_down):
    B, T, D = x.shape
    assert B == 1 and D == D_MODEL
    n_blocks = -(-(T * TOP_K) // MOE_BM) + N_EXPERTS
    mod = _ada(c, w_ada, b_ada)
    xs = x.reshape(T, D)
    v_first = None
    for l in range(DEPTH):
        sh1, sc1, g1, sh2, sc2, g2 = (mod[l, j * D:(j + 1) * D].reshape(1, D) for j in range(6))
        w_sb, w_gla, w_rw, w_mg = _in_proj_weights(w_in[l])
        u1 = _modulate(xs, sc1, sh1)
        p_sb = _wmm(u1, w_sb, BF16, 1024, name="inproj_sb")
        p_gla = _wmm(u1, w_gla, F32, 832 * 2, name="inproj_gla")
        p_rw = _wmm(u1, w_rw, F32, 832 * 2, name="inproj_rwkv")
        p_mg = _wmm(u1, w_mg, BF16, 1024, name="inproj_merge")

        o_sb = _sb_attention(p_sb)
        o_gla = _gla(p_gla, gla_gk_up[l], gla_gk_bias[l], gla_norm_g[l])
        if l == 0:
            r, w, k, v, a, b, g = _rwkv_prep(p_rw, None, rw_mu[l], rw_w0[l], rw_w2[l], rw_a0[l], rw_a2[l],
                                             rw_g2[l], rw_k_k[l], rw_k_a[l], None, None, None)
            v_first = v
        else:
            r, w, k, v, a, b, g = _rwkv_prep(p_rw, v_first, rw_mu[l], rw_w0[l], rw_w2[l], rw_a0[l], rw_a2[l],
                                             rw_g2[l], rw_k_k[l], rw_k_a[l], rw_v0[l - 1], rw_v1[l - 1],
                                             rw_v2[l - 1])
        y = _rwkv_recurrence(r, w, k, v, a, b)
        o_rw = _rwkv_post(y, r, k, v, g, rw_lnx_g[l], rw_lnx_b[l], rw_r_k[l])

        merged = _merge(o_sb, o_gla, o_rw, p_mg, w_br_sb[l].astype(BF16), w_br_gla[l].astype(BF16),
                        w_br_rw[l].astype(BF16))
        xs = _oproj(merged, w_o[l].astype(BF16), xs, g1, ln_g[l, 0], ln_b[l, 0])

        u2, top_e, gates, rank, counts = _router(xs, sc2, sh2, w_router[l], router_bias[l])
        dest, fill, block_expert, n_valid = _moe_plan(top_e, rank, counts[:, 0], MOE_BM, n_blocks)
        x_sorted = _dispatch(u2, dest, fill, n_blocks * MOE_BM)
        y_sorted = _experts(x_sorted, block_expert, n_valid, w_exp_gate, w_exp_up, w_exp_down, l)
        xs = _combine(y_sorted, dest, u2, xs, gates.T, w_sh_gate[l].astype(BF16), w_sh_up[l].astype(BF16),
                      w_sh_down[l].astype(BF16), g2, ln_g[l, 1], ln_b[l, 1])
    return xs.reshape(B, T, D)
```

```python
import functools

import numpy as np
import jax
import jax.numpy as jnp
from jax import lax
from jax.experimental import pallas as pl
from jax.experimental.pallas import tpu as pltpu

F32 = jnp.float32
BF16 = jnp.bfloat16
I32 = jnp.int32
U32 = jnp.uint32

D_MODEL = 2048
DEPTH = 2
LN_EPS = 1e-5
DEEPNORM_ALPHA = (2 * DEPTH) ** 0.25

SB_HEADS = 8
SB_HEAD_DIM = 128
SB_WIDTH = SB_HEADS * SB_HEAD_DIM

GLA_HEADS = 4
GLA_HEAD_K = 128
GLA_HEAD_V = 256
GLA_KEY_DIM = GLA_HEADS * GLA_HEAD_K
GLA_VALUE_DIM = GLA_HEADS * GLA_HEAD_V
GLA_GATE_RANK = 16
GLA_GATE_NORMALIZER = 16.0
GLA_CHUNK = 64
GLA_NORM_EPS = 1e-5

RWKV_HEADS = 16
RWKV_HEAD_DIM = 64
RWKV_WIDTH = RWKV_HEADS * RWKV_HEAD_DIM
RWKV_DECAY_RANK = 64
RWKV_AAA_RANK = 64
RWKV_VALUE_RANK = 32
RWKV_GATE_RANK = 128
RWKV_LNX_EPS = 64e-5

N_EXPERTS = 64
TOP_K = 6
EXPERT_DIM = 512
SHARED_DIM = 512
N_GROUPS = 8
TOPK_GROUPS = 4
ROUTED_SCALE = 2.5

SB_COLS = 3 * SB_WIDTH
GLA_COLS = 2 * GLA_KEY_DIM + GLA_VALUE_DIM + GLA_GATE_RANK + GLA_VALUE_DIM
RWKV_COLS = 3 * RWKV_WIDTH + RWKV_DECAY_RANK + RWKV_AAA_RANK + RWKV_GATE_RANK
MERGE_COLS = 3 * D_MODEL

LANES = 128
SUBLANES = 8
VMEM_LIMIT = 56 * 1024 * 1024
MOE_BM = 512
SB_EXIT = -104.0
SB_GROUPS = 2

NN = (((1,), (0,)), ((), ()))
NT = (((1,), (1,)), ((), ()))
TN = (((0,), (0,)), ((), ()))


def _cp(*sem):
    return pltpu.CompilerParams(dimension_semantics=sem, vmem_limit_bytes=VMEM_LIMIT)


def _dot(a, b, dims=NN):
    return lax.dot_general(a, b, dims, preferred_element_type=F32)


def _split(a, n):
    out = []
    r = a
    for i in range(n):
        p = r.astype(BF16)
        out.append(p)
        if i + 1 < n:
            r = r - p.astype(F32)
    return out


def _dot_fx(a, e, n, dims=NN):
    acc = None
    for p in _split(a, n):
        t = _dot(p, e, dims)
        acc = t if acc is None else acc + t
    return acc


def _dot_xf(e, b, n, dims=NN):
    acc = None
    for p in _split(b, n):
        t = _dot(e, p, dims)
        acc = t if acc is None else acc + t
    return acc


def _dot_ff(a, b, dims=NN):
    a1, a2, a3 = _split(a, 3)
    b1, b2, b3 = _split(b, 3)
    acc = _dot(a1, b1, dims)
    for p, q in ((a1, b2), (a2, b1), (a2, b2), (a1, b3), (a3, b1)):
        acc = acc + _dot(p, q, dims)
    return acc


def _sigmoid(x):
    return 1.0 / (1.0 + jnp.exp(-x))


def _softplus(x):
    return jnp.maximum(x, 0.0) + jnp.log1p(jnp.exp(-jnp.abs(x)))


def _silu(x):
    return x * _sigmoid(x)


def _row(ref, t):
    return ref.at[pl.ds(t, 1)]


def _pack_pairs(x):
    w = x.shape[1] // 2
    bits = lax.bitcast_convert_type(x.astype(BF16).astype(F32), U32)
    return (bits[:, :w] >> 16) | bits[:, w:]


def _unpack_pairs(p):
    lo = lax.bitcast_convert_type(p << 16, F32)
    hi = lax.bitcast_convert_type(p & jnp.uint32(0xFFFF0000), F32)
    return jnp.concatenate([lo, hi], axis=1)


def _ada_kernel(c_ref, w_ref, b_ref, o_ref):
    s = _silu(c_ref[...])
    o_ref[0] = jnp.sum(s * w_ref[0], axis=0, keepdims=True) + b_ref[0]


def _ada(c, w_ada, b_ada):
    L, D, N = w_ada.shape
    tn = 512
    out = pl.pallas_call(
        _ada_kernel,
        grid=(L, N // tn),
        in_specs=[pl.BlockSpec((D, 1), lambda l, j: (0, 0)),
                  pl.BlockSpec((1, D, tn), lambda l, j: (l, 0, j)),
                  pl.BlockSpec((1, 1, tn), lambda l, j: (l, 0, j))],
        out_specs=pl.BlockSpec((1, 1, tn), lambda l, j: (l, 0, j)),
        out_shape=jax.ShapeDtypeStruct((L, 1, N), F32),
        compiler_params=_cp("parallel", "parallel"),
        name="ada",
    )(c.reshape(D, 1), w_ada, b_ada.reshape(L, 1, N))
    return out[:, 0, :]


def _modulate_kernel(x_ref, sc_ref, sh_ref, o_ref):
    o_ref[...] = (x_ref[...] * (1.0 + sc_ref[...]) + sh_ref[...]).astype(o_ref.dtype)


def _modulate(x, sc, sh, tm=1024):
    T, D = x.shape
    spec = pl.BlockSpec((tm, D), lambda i: (i, 0))
    rowspec = pl.BlockSpec((1, D), lambda i: (0, 0))
    return pl.pallas_call(
        _modulate_kernel,
        grid=(T // tm,),
        in_specs=[spec, rowspec, rowspec],
        out_specs=spec,
        out_shape=jax.ShapeDtypeStruct((T, D), BF16),
        compiler_params=_cp("parallel"),
        name="modulate",
    )(x, sc, sh)


def _wmm_kernel(u_ref, w_ref, o_ref, wb_ref):
    @pl.when(pl.program_id(1) == 0)
    def _():
        wb_ref[...] = w_ref[...].astype(BF16)

    o_ref[...] = _dot(u_ref[...], wb_ref[...]).astype(o_ref.dtype)


def _wmm(u, w, out_dtype, tn, tm=512, name="wmm"):
    T, D = u.shape
    N = w.shape[1]
    return pl.pallas_call(
        _wmm_kernel,
        grid=(N // tn, T // tm),
        in_specs=[pl.BlockSpec((tm, D), lambda j, i: (i, 0)),
                  pl.BlockSpec((D, tn), lambda j, i: (0, j))],
        out_specs=pl.BlockSpec((tm, tn), lambda j, i: (i, j)),
        out_shape=jax.ShapeDtypeStruct((T, N), out_dtype),
        scratch_shapes=[pltpu.VMEM((D, tn), BF16)],
        compiler_params=_cp("parallel", "arbitrary"),
        name=name,
    )(u, w)


def _sb_kernel(q_ref, k_ref, v_ref, o_ref, acc_ref, car_ref, *, bq, scale):
    i = pl.program_id(1)
    q = q_ref[...]
    r_id = lax.broadcasted_iota(I32, (bq, bq), 0)
    c_id = lax.broadcasted_iota(I32, (bq, bq), 1)
    later = (r_id > c_id).astype(BF16)
    causal = c_id < r_id

    def block(j, carry, diagonal):
        off = pl.multiple_of(j * bq, bq)
        kb = k_ref[pl.ds(off, bq), :]
        vb = v_ref[pl.ds(off, bq), :]
        grp = [slice(g * (bq // SB_GROUPS), (g + 1) * (bq // SB_GROUPS)) for g in range(SB_GROUPS)]
        z = [_dot(q[s], kb, NT) * scale for s in grp]
        drop = [jnp.maximum(t, 0.0) + jnp.log(1.0 + jnp.exp(-jnp.abs(t))) for t in z]
        if diagonal:
            drop = [jnp.where(causal[s], d, 0.0) for s, d in zip(grp, drop)]
        after = [_dot_fx(d, later, 2) + carry[s] for s, d in zip(grp, drop)]
        w = [jnp.exp(t - d - a) for t, d, a in zip(z, drop, after)]
        if diagonal:
            w = [jnp.where(causal[s], x, 0.0) for s, x in zip(grp, w)]
        pv = jnp.concatenate([_dot(x.astype(BF16), vb) for x in w], axis=0)
        new_carry = jnp.concatenate([a[:, 0:1] + d[:, 0:1] for a, d in zip(after, drop)], axis=0)
        return pv, new_carry

    pv, carry = block(i, jnp.zeros((bq, 1), F32), True)
    acc_ref[...] = pv
    car_ref[...] = carry

    def body(c):
        j, _ = c
        pv, carry = block(j, car_ref[...], False)
        acc_ref[...] += pv
        car_ref[...] = carry
        return j - 1, (jnp.min(carry) > -SB_EXIT).astype(I32)

    lax.while_loop(lambda c: jnp.logical_and(c[0] >= 0, c[1] == 0), body, (i - 1, jnp.int32(0)))
    o_ref[...] = acc_ref[...].astype(o_ref.dtype)


def _sb_attention(qkv, bq=256):
    T = qkv.shape[0]
    H, Dh = SB_HEADS, SB_HEAD_DIM
    kern = functools.partial(_sb_kernel, bq=bq, scale=Dh ** -0.5)
    return pl.pallas_call(
        kern,
        grid=(H, T // bq),
        in_specs=[pl.BlockSpec((bq, Dh), lambda h, i: (i, h)),
                  pl.BlockSpec((T, Dh), lambda h, i: (0, H + h)),
                  pl.BlockSpec((T, Dh), lambda h, i: (0, 2 * H + h))],
        out_specs=pl.BlockSpec((bq, Dh), lambda h, i: (i, h)),
        out_shape=jax.ShapeDtypeStruct((T, H * Dh), BF16),
        scratch_shapes=[pltpu.VMEM((bq, Dh), F32), pltpu.VMEM((bq, 1), F32)],
        compiler_params=_cp("parallel", "parallel"),
        name="sb_attention",
    )(qkv, qkv, qkv)


_GLA_LEVELS = (32, 16, 8, 4, 2, 1)


def _gla_constants():
    C = GLA_CHUNK
    i = np.arange(C)[:, None]
    s = np.arange(C)[None, :]
    sel = []
    masks = []
    for m in _GLA_LEVELS:
        same = (i // m) == (s // m)
        sel.append(same & (s <= i) & ((i // m) % 2 == 1))
        sel.append(same & (s > i) & ((i // m) % 2 == 0))
        masks.append(((i // (2 * m)) == (s // (2 * m))) & ((i // m) % 2 == 1) & ((s // m) % 2 == 0))
    sel.append(s <= i)
    sel.append(s > i)
    masks.append(i == s)
    return (np.concatenate(sel, 0).astype(np.float32), np.stack(masks, 0).astype(np.float32))


def _gla_kernel(q_ref, k_ref, v_ref, go_ref, glr_ref, up_ref, bias_ref, ng_ref, sel_ref, msk_ref,
                o_ref, st_ref, *, n_chunks):
    C = GLA_CHUNK
    nl = len(_GLA_LEVELS)

    @pl.when(pl.program_id(1) == 0)
    def _():
        st_ref[...] = jnp.zeros_like(st_ref)

    sel = sel_ref[...]
    up = up_ref[...]
    bias = bias_ref[...]
    ng = ng_ref[...]
    rows = [slice(c * C, (c + 1) * C) for c in range(n_chunks)]
    blk = lambda f, j: f[j * C:(j + 1) * C]
    q = [q_ref[s, :] * (GLA_HEAD_K ** -0.5) for s in rows]
    k = [k_ref[s, :] for s in rows]
    v = [v_ref[s, :].astype(BF16) for s in rows]
    x = [_dot(glr_ref[s, :].astype(BF16), up) + bias for s in rows]
    g = [-_softplus(-t) * (1.0 / GLA_GATE_NORMALIZER) for t in x]
    f = [jnp.exp(_dot_xf(sel, t, 2)) for t in g]
    scores = [_dot(a.astype(BF16), b.astype(BF16), NT) * msk_ref[nl] for a, b in zip(q, k)]
    for l in range(nl):
        ql = [(a * blk(t, 2 * l)).astype(BF16) for a, t in zip(q, f)]
        kl = [(b * blk(t, 2 * l + 1)).astype(BF16) for b, t in zip(k, f)]
        scores = [s + _dot(a, b, NT) * msk_ref[l] for s, a, b in zip(scores, ql, kl)]
    o_intra = [_dot(s.astype(BF16), b) for s, b in zip(scores, v)]
    qe = [(a * blk(t, 2 * nl)).astype(BF16) for a, t in zip(q, f)]
    upd = [_dot(b, (a * blk(t, 2 * nl + 1)).astype(BF16), TN) for b, a, t in zip(v, k, f)]
    dec = [blk(t, 2 * nl)[C - 1:C, :] for t in f]

    st = st_ref[...]
    for c in range(n_chunks):
        o = _dot(qe[c], st.astype(BF16), NT) + o_intra[c]
        st = st * dec[c] + upd[c]
        rms = lax.rsqrt(jnp.mean(o * o, axis=-1, keepdims=True) + GLA_NORM_EPS)
        o_ref[rows[c], :] = (o * rms * ng * _silu(go_ref[rows[c], :])).astype(o_ref.dtype)
    st_ref[...] = st


def _gla(p_gla, gk_up, gk_bias, norm_g, tt=512):
    T = p_gla.shape[0]
    H, dk, dv = GLA_HEADS, GLA_HEAD_K, GLA_HEAD_V
    sel, msk = _gla_constants()
    up = jnp.zeros((LANES, GLA_KEY_DIM), F32).at[:GLA_GATE_RANK].set(gk_up).astype(BF16)
    kern = functools.partial(_gla_kernel, n_chunks=tt // GLA_CHUNK)
    return pl.pallas_call(
        kern,
        grid=(H, T // tt),
        in_specs=[pl.BlockSpec((tt, dk), lambda h, i: (i, h)),
                  pl.BlockSpec((tt, dk), lambda h, i: (i, H + h)),
                  pl.BlockSpec((tt, dv), lambda h, i: (i, H + h)),
                  pl.BlockSpec((tt, dv), lambda h, i: (i, 2 * H + h)),
                  pl.BlockSpec((tt, LANES), lambda h, i: (i, 3 * GLA_VALUE_DIM // LANES)),
                  pl.BlockSpec((LANES, dk), lambda h, i: (0, h)),
                  pl.BlockSpec((1, dk), lambda h, i: (0, h)),
                  pl.BlockSpec((1, dv), lambda h, i: (0, 0)),
                  pl.BlockSpec(sel.shape, lambda h, i: (0, 0)),
                  pl.BlockSpec(msk.shape, lambda h, i: (0, 0, 0))],
        out_specs=pl.BlockSpec((tt, dv), lambda h, i: (i, h)),
        out_shape=jax.ShapeDtypeStruct((T, H * dv), BF16),
        scratch_shapes=[pltpu.VMEM((dv, dk), F32)],
        compiler_params=_cp("parallel", "arbitrary"),
        name="gla",
    )(p_gla, p_gla, p_gla, p_gla, p_gla, up, gk_bias.reshape(1, -1), norm_g.reshape(1, -1),
      jnp.asarray(sel, BF16), jnp.asarray(msk, F32))


def _head_ones():
    l = np.arange(LANES)
    return (l[:, None] // RWKV_HEAD_DIM == l[None, :] // RWKV_HEAD_DIM).astype(np.float32)


def _rwprep_kernel(p_ref, pp_ref, mu_ref, w0_ref, w2_ref, a0_ref, a2_ref, g2_ref, kk_ref, ka_ref,
                   v0_ref, v1_ref, v2_ref, vf_ref, bo_ref,
                   r_ref, w_ref, k_ref, v_ref, a_ref, b_ref, g_ref, *, first_layer):
    W = RWKV_WIDTH
    i = pl.program_id(0)
    p = p_ref[...]
    tm = p.shape[0]
    prev_row = jnp.where(i == 0, 0.0, pp_ref[SUBLANES - 1:SUBLANES, :])
    rid = lax.broadcasted_iota(I32, p.shape, 0)
    prev = jnp.where(rid == 0, prev_row, pltpu.roll(p, 1, axis=0))
    ps = p + (prev - p) * mu_ref[...]
    r = ps[:, 0:W]
    kr = ps[:, W:2 * W]
    vr = ps[:, 2 * W:3 * W]
    lr = ps[:, 3 * W:3 * W + LANES]
    gl = ps[:, 3 * W + LANES:3 * W + 2 * LANES]
    w_log = -_softplus(-(w0_ref[...] + _dot(jnp.tanh(lr).astype(BF16), w2_ref[...]))) - 0.5
    w_ref[...] = -jnp.exp(w_log)
    if not first_layer:
        lo = _dot(vr.astype(BF16), v1_ref[...])
        gate = _sigmoid(v0_ref[...] + _dot(lo.astype(BF16), v2_ref[...]))
        vr = vr + (vf_ref[...] - vr) * gate
    v_ref[...] = vr
    a = _sigmoid(a0_ref[...] + _dot(lr.astype(BF16), a2_ref[...]))
    g_ref[...] = _dot(_sigmoid(gl).astype(BF16), g2_ref[...])
    kk = kr * kk_ref[...]
    bo = bo_ref[...]
    for hp in range(W // LANES):
        sl = slice(hp * LANES, (hp + 1) * LANES)
        kh = kk[:, sl]
        ss = _dot_fx(kh * kh, bo, 3)
        kn = kh / jnp.maximum(jnp.sqrt(ss), 1e-12)
        a_ref[:, sl] = -kn
        b_ref[:, sl] = kn * a[:, sl]
    r_ref[...] = r
    k_ref[...] = kr * (1.0 + (a - 1.0) * ka_ref[...])


def _rwkv_prep(p_rw, v_first, mu, w0, w2, a0, a2, g2, k_k, k_a, v0, v1, v2, tm=256):
    T, NC = p_rw.shape
    W = RWKV_WIDTH
    first = v_first is None
    w2p = jnp.zeros((LANES, W), F32).at[:RWKV_DECAY_RANK].set(w2).astype(BF16)
    a2p = jnp.zeros((LANES, W), F32).at[RWKV_DECAY_RANK:].set(a2).astype(BF16)
    if first:
        v_first = jnp.zeros((SUBLANES, W), F32)
        v0 = jnp.zeros((W,), F32)
        v1p = jnp.zeros((W, LANES), BF16)
        v2p = jnp.zeros((LANES, W), BF16)
        vf_spec = pl.BlockSpec((SUBLANES, W), lambda i: (0, 0))
    else:
        v1p = jnp.zeros((W, LANES), F32).at[:, :RWKV_VALUE_RANK].set(v1).astype(BF16)
        v2p = jnp.zeros((LANES, W), F32).at[:RWKV_VALUE_RANK].set(v2).astype(BF16)
        vf_spec = pl.BlockSpec((tm, W), lambda i: (i, 0))
    row = lambda a: a.reshape(1, -1)
    full = lambda shape: pl.BlockSpec(shape, lambda i: (0,) * len(shape))
    out = jax.ShapeDtypeStruct((T, W), F32)
    ospec = pl.BlockSpec((tm, W), lambda i: (i, 0))
    kern = functools.partial(_rwprep_kernel, first_layer=first)
    return pl.pallas_call(
        kern,
        grid=(T // tm,),
        in_specs=[pl.BlockSpec((tm, NC), lambda i: (i, 0)),
                  pl.BlockSpec((SUBLANES, NC), lambda i: (jnp.maximum(i * (tm // SUBLANES) - 1, 0), 0)),
                  full((1, NC)), full((1, W)), full((LANES, W)), full((1, W)), full((LANES, W)),
                  full((LANES, W)), full((1, W)), full((1, W)),
                  full((1, W)), full((W, LANES)), full((LANES, W)), vf_spec, full((LANES, LANES))],
        out_specs=[ospec] * 7,
        out_shape=[out] * 7,
        compiler_params=_cp("parallel"),
        name="rwkv_prep",
    )(p_rw, p_rw, row(mu), row(w0), w2p, row(a0), a2p, g2.astype(BF16), row(k_k), row(k_a),
      row(v0), v1p, v2p, v_first, jnp.asarray(_head_ones(), BF16))


RW_CHUNK = 64


def _rw_constants():
    C = RW_CHUNK
    i = np.arange(C)[:, None]
    s = np.arange(C)[None, :]
    i2 = np.arange(2 * C)[:, None]
    s2 = np.arange(2 * C)[None, :]
    same = (i2 // C) == (s2 // C)
    cm = np.stack([same & (s2 < i2), same & (s2 <= i2), (i2 // 16) == (s2 // 16), (i2 // 32) == (s2 // 32),
                   same], 0).astype(np.float32)
    return (s <= i).astype(np.float32), cm


def _mm(a, b, dims=NN):
    a1, a2 = _split(a, 2)
    b1, b2 = _split(b, 2)
    r = _dot(a1, jnp.concatenate([b1, b2], axis=0 if dims == NT else 1), dims)
    n = r.shape[1] // 2
    return (r[:, :n] + r[:, n:]) + _dot(a2, b1, dims)


def _mm1(a, b, dims=NN):
    return _dot(a.astype(BF16), b.astype(BF16), dims)


def _unit_lower_inverse(nms, eye, low, bd16, bd32, bd64):
    size = eye.shape[0]

    def pack(m, s):
        acc = m[0:s]
        for b in range(1, size // s):
            acc = acc + m[s * b:s * (b + 1)]
        return acc

    def unpack(p, mask):
        return jnp.concatenate([p] * (size // p.shape[0]), axis=0) * mask

    nd = [n * bd16 for n in nms]
    x = [pack(eye + n, 16) for n in nd]
    p = [_mm(pack(n, 16), n) for n in nd]
    for step in range(3):
        pf = [unpack(q, bd16) for q in p]
        x = [a + _mm(a, q) for a, q in zip(x, pf)]
        if step < 2:
            p = [_mm(q, qf) for q, qf in zip(p, pf)]
    x = [unpack(a, bd16) for a in x]
    for s, inner, outer in ((16, bd16, bd32), (32, bd32, bd64)):
        place = (outer - inner) * low
        t = [unpack(_mm(pack(n * place, s), a), place) for n, a in zip(nms, x)]
        x = [a + unpack(_mm(pack(a, s), q), place) for a, q in zip(x, t)]
    return x


def _rwrec_kernel(r_ref, lw_ref, k_ref, v_ref, a_ref, b_ref, lt_ref, cm_ref, y_ref, mt_ref, *, n_chunks):
    C = RW_CHUNK
    C2 = 2 * C

    @pl.when(pl.program_id(1) == 0)
    def _():
        mt_ref[...] = jnp.zeros_like(mt_ref)

    lt = lt_ref[...]
    sl, li, bd16, bd32, bdh = cm_ref[0], cm_ref[1], cm_ref[2], cm_ref[3], cm_ref[4]
    eye = li - sl
    lane = lax.broadcasted_iota(I32, (1, LANES), 1)
    m0 = jnp.where(lane < RWKV_HEAD_DIM, 1.0, 0.0)
    m1 = 1.0 - m0
    stack = lambda x: jnp.concatenate([x * m0, x * m1], axis=0)
    fold = lambda x: x[:C] + x[C:]
    rows = [slice(c * C, (c + 1) * C) for c in range(n_chunks)]

    lw = [lw_ref[s, :] for s in rows]
    cw = [_dot_xf(lt, x, 3) for x in lw]
    en = [jnp.exp(-x) for x in cw]
    at = [a_ref[s, :] * jnp.exp(c - l) for s, c, l in zip(rows, cw, lw)]
    rt = [r_ref[s, :] * jnp.exp(c) for s, c in zip(rows, cw)]
    bh = [b_ref[s, :] * e for s, e in zip(rows, en)]
    kh = [k_ref[s, :] * e for s, e in zip(rows, en)]
    v = [v_ref[s, :] for s in rows]
    a2 = [stack(x) for x in at]
    ar = [jnp.concatenate([x, y], axis=0) for x, y in zip(at, rt)]
    gb = [_mm(x, stack(y), NT) for x, y in zip(ar, bh)]
    gk = [_mm(x, stack(y), NT) for x, y in zip(ar, kh)]
    slf = fold(sl)
    lif = fold(li)
    tinv = _unit_lower_inverse([stack(g[:C] * slf) for g in gb], eye, sl, bd16, bd32, bdh)
    akv = [_mm(jnp.concatenate([g[:C] * slf, g[C:] * lif], axis=0), stack(x))
           for g, x in zip(gk, v)]
    tav = [_mm(fold(t), jnp.concatenate([x, stack(y[:C])], axis=1)) for t, x, y in zip(tinv, a2, akv)]
    ta = [x[:, :LANES] for x in tav]
    u0 = [x[:, LANES:] for x in tav]
    rav = [_mm1(g[C:] * lif, jnp.concatenate([stack(x), stack(y)], axis=1))
           for g, x, y in zip(gb, ta, u0)]
    qe = [x + y[:, :LANES] for x, y in zip(rt, rav)]
    y0 = [x[:, LANES:] + y[C:] for x, y in zip(rav, akv)]
    dl = [jnp.exp(x[C - 1:C, :]) for x in cw]
    gmt = [(eye + _mm(x, y, TN) * bdh) * d for x, y, d in zip(ta, bh, dl)]
    hmt = [(_mm(x, y, TN) + _mm(z, w, TN)) * bdh * d for x, y, z, w, d in zip(u0, bh, v, kh, dl)]

    mt = mt_ref[...]
    for c in range(n_chunks):
        y_ref[rows[c], :] = _mm1(qe[c], mt, NT) + y0[c]
        mt = _mm(mt, gmt[c]) + hmt[c]
    mt_ref[...] = mt


def _rwkv_recurrence(r, lw, k, v, a, b, tt=512):
    T, W = r.shape
    lt, cm = _rw_constants()
    spec = pl.BlockSpec((tt, LANES), lambda p, i: (i, p))
    kern = functools.partial(_rwrec_kernel, n_chunks=tt // RW_CHUNK)
    return pl.pallas_call(
        kern,
        grid=(W // LANES, T // tt),
        in_specs=[spec] * 6 + [pl.BlockSpec(lt.shape, lambda p, i: (0, 0)),
                               pl.BlockSpec(cm.shape, lambda p, i: (0, 0, 0))],
        out_specs=spec,
        out_shape=jax.ShapeDtypeStruct((T, W), F32),
        scratch_shapes=[pltpu.VMEM((LANES, LANES), F32)],
        compiler_params=_cp("parallel", "arbitrary"),
        name="rwkv_recurrence",
    )(r, lw, k, v, a, b, jnp.asarray(lt, BF16), jnp.asarray(cm, F32))


def _rwpost_kernel(y_ref, r_ref, k_ref, v_ref, g_ref, lg_ref, lb_ref, rk_ref, bo_ref, o_ref):
    bo = bo_ref[...]
    inv = 1.0 / RWKV_HEAD_DIM
    for hp in range(RWKV_WIDTH // LANES):
        sl = slice(hp * LANES, (hp + 1) * LANES)
        y = y_ref[:, sl]
        mu = _dot_fx(y, bo, 3) * inv
        d = y - mu
        var = _dot_fx(d * d, bo, 3) * inv
        yn = d * lax.rsqrt(var + RWKV_LNX_EPS) * lg_ref[:, sl] + lb_ref[:, sl]
        bonus = _dot_fx(r_ref[:, sl] * k_ref[:, sl] * rk_ref[:, sl], bo, 3) * v_ref[:, sl]
        o_ref[:, sl] = ((yn + bonus) * g_ref[:, sl]).astype(o_ref.dtype)


def _rwkv_post(y, r, k, v, g, lnx_g, lnx_b, r_k, tm=512):
    T, W = y.shape
    spec = pl.BlockSpec((tm, W), lambda i: (i, 0))
    rowspec = pl.BlockSpec((1, W), lambda i: (0, 0))
    return pl.pallas_call(
        _rwpost_kernel,
        grid=(T // tm,),
        in_specs=[spec] * 5 + [rowspec] * 3 + [pl.BlockSpec((LANES, LANES), lambda i: (0, 0))],
        out_specs=spec,
        out_shape=jax.ShapeDtypeStruct((T, W), BF16),
        compiler_params=_cp("parallel"),
        name="rwkv_post",
    )(y, r, k, v, g, lnx_g.reshape(1, W), lnx_b.reshape(1, W), r_k.reshape(1, W),
      jnp.asarray(_head_ones(), BF16))


def _merge_kernel(osb_ref, ogl_ref, orw_ref, g1_ref, g2_ref, g3_ref, w1_ref, w2_ref, w3_ref, o_ref):
    acc = _sigmoid(g1_ref[...].astype(F32)) * _dot(osb_ref[...], w1_ref[...])
    acc = acc + _sigmoid(g2_ref[...].astype(F32)) * _dot(ogl_ref[...], w2_ref[...])
    acc = acc + _sigmoid(g3_ref[...].astype(F32)) * _dot(orw_ref[...], w3_ref[...])
    o_ref[...] = acc.astype(o_ref.dtype)


def _merge(o_sb, o_gla, o_rw, p_merge, w_sb, w_gla, w_rw, tm=512, tn=1024):
    T = o_sb.shape[0]
    D = D_MODEL
    nb = D // tn
    aspec = pl.BlockSpec((tm, 1024), lambda j, i: (i, 0))
    wspec = pl.BlockSpec((1024, tn), lambda j, i: (0, j))
    gspec = lambda b: pl.BlockSpec((tm, tn), lambda j, i: (i, b * nb + j))
    return pl.pallas_call(
        _merge_kernel,
        grid=(nb, T // tm),
        in_specs=[aspec, aspec, aspec, gspec(0), gspec(1), gspec(2), wspec, wspec, wspec],
        out_specs=pl.BlockSpec((tm, tn), lambda j, i: (i, j)),
        out_shape=jax.ShapeDtypeStruct((T, D), BF16),
        compiler_params=_cp("parallel", "parallel"),
        name="merge",
    )(o_sb, o_gla, o_rw, p_merge, p_merge, p_merge, w_sb, w_gla, w_rw)


def _layer_norm(h, g, b):
    mu = jnp.mean(h, axis=-1, keepdims=True)
    d = h - mu
    var = jnp.mean(d * d, axis=-1, keepdims=True)
    return d * lax.rsqrt(var + LN_EPS) * g + b


def _oproj_kernel(m_ref, w_ref, x_ref, gate_ref, lg_ref, lb_ref, o_ref):
    mix = _dot(m_ref[...], w_ref[...])
    h = DEEPNORM_ALPHA * x_ref[...] + (1.0 + gate_ref[...]) * mix
    o_ref[...] = _layer_norm(h, lg_ref[...], lb_ref[...])


def _oproj(merged, w_o, x, gate, ln_g, ln_b, tm=512):
    T, D = x.shape
    spec = pl.BlockSpec((tm, D), lambda i: (i, 0))
    rowspec = pl.BlockSpec((1, D), lambda i: (0, 0))
    return pl.pallas_call(
        _oproj_kernel,
        grid=(T // tm,),
        in_specs=[spec, pl.BlockSpec((D, D), lambda i: (0, 0)), spec, rowspec, rowspec, rowspec],
        out_specs=spec,
        out_shape=jax.ShapeDtypeStruct((T, D), F32),
        compiler_params=_cp("parallel"),
        name="oproj_norm",
    )(merged, w_o, x, gate, ln_g.reshape(1, D), ln_b.reshape(1, D))


def _first_max(x, idx, axis, big):
    m = jnp.max(x, axis=axis, keepdims=True)
    first = jnp.min(jnp.where(x == m, idx, big), axis=axis, keepdims=True)
    return m, idx == first


def _router_kernel(x_ref, sc_ref, sh_ref, wr_ref, rb_ref, u_ref, e_ref, g_ref, p_ref, cnt_ref, car_ref, *, tm):
    E, G = N_EXPERTS, N_GROUPS
    per = E // G
    neg = -jnp.inf

    @pl.when(pl.program_id(0) == 0)
    def _():
        car_ref[...] = jnp.zeros_like(car_ref)

    u = x_ref[...] * (1.0 + sc_ref[...]) + sh_ref[...]
    u_ref[...] = _pack_pairs(u)
    scores =_sigmoid(_dot_ff(wr_ref[...], u, NT))
    biased = scores + rb_ref[...]
    b3 = biased.reshape(G, per, tm)
    i3 = lax.broadcasted_iota(I32, (G, per, tm), 1)
    m1, hit = _first_max(b3, i3, 1, per)
    m2 = jnp.max(jnp.where(hit, neg, b3), axis=1, keepdims=True)
    gs = (m1 + m2).reshape(G, tm)
    gi = lax.broadcasted_iota(I32, (G, tm), 0)
    ok = jnp.zeros((G, tm), jnp.bool_)
    for _ in range(TOPK_GROUPS):
        _, hit = _first_max(gs, gi, 0, G)
        ok = jnp.logical_or(ok, hit)
        gs = jnp.where(hit, neg, gs)
    ok_e = jnp.broadcast_to(ok.reshape(G, 1, tm), (G, per, tm)).reshape(E, tm)
    cand = jnp.where(ok_e, biased, neg)
    ei = lax.broadcasted_iota(I32, (E, tm), 0)
    sels, gates = [], []
    for _ in range(TOP_K):
        _, hit = _first_max(cand, ei, 0, E)
        sels.append(hit)
        gates.append(jnp.sum(jnp.where(hit, scores, 0.0), axis=0, keepdims=True))
        cand = jnp.where(hit, neg, cand)
    denom = gates[0]
    for gk in gates[1:]:
        denom = denom + gk
    chosen = sels[0]
    for s in sels[1:]:
        chosen = jnp.logical_or(chosen, s)
    cnt = jnp.where(chosen, 1.0, 0.0)
    earlier = (lax.broadcasted_iota(I32, (tm, tm), 0) < lax.broadcasted_iota(I32, (tm, tm), 1)).astype(BF16)
    rank = _dot(cnt.astype(BF16), earlier) + car_ref[...][:, 0:1]
    car_ref[...] = car_ref[...] + jnp.sum(cnt, axis=1, keepdims=True)
    cnt_ref[...] = car_ref[...]
    zi = jnp.zeros((1, tm), I32)
    zf = jnp.zeros((1, tm), F32)
    e_rows = [jnp.sum(jnp.where(s, ei, 0), axis=0, keepdims=True) for s in sels]
    p_rows = [jnp.sum(jnp.where(s, rank, 0.0), axis=0, keepdims=True).astype(I32) for s in sels]
    g_rows = [gk / denom * ROUTED_SCALE for gk in gates]
    e_ref[...] = jnp.concatenate(e_rows + [zi, zi], axis=0)
    p_ref[...] = jnp.concatenate(p_rows + [zi, zi], axis=0)
    g_ref[...] = jnp.concatenate(g_rows + [zf, zf], axis=0)


def _router(x, sc, sh, w_router, router_bias, tm=256):
    T, D = x.shape
    E = N_EXPERTS
    kern = functools.partial(_router_kernel, tm=tm)
    rowspec = pl.BlockSpec((1, D), lambda i: (0, 0))
    kspec = pl.BlockSpec((SUBLANES, tm), lambda i: (0, i))
    return pl.pallas_call(
        kern,
        grid=(T // tm,),
        in_specs=[pl.BlockSpec((tm, D), lambda i: (i, 0)), rowspec, rowspec,
                  pl.BlockSpec((E, D), lambda i: (0, 0)), pl.BlockSpec((E, 1), lambda i: (0, 0))],
        out_specs=[pl.BlockSpec((tm, D // 2), lambda i: (i, 0)), kspec, kspec, kspec,
                   pl.BlockSpec((E, LANES), lambda i: (0, 0))],
        out_shape=[jax.ShapeDtypeStruct((T, D // 2), U32),
                   jax.ShapeDtypeStruct((SUBLANES, T), I32),
                   jax.ShapeDtypeStruct((SUBLANES, T), F32),
                   jax.ShapeDtypeStruct((SUBLANES, T), I32),
                   jax.ShapeDtypeStruct((E, LANES), F32)],
        scratch_shapes=[pltpu.VMEM((E, LANES), F32)],
        compiler_params=_cp("arbitrary"),
        name="router",
    )(x, sc, sh, w_router.T, router_bias.reshape(E, 1))


def _dispatch_kernel(dest_ref, fill_ref, u_ref, xs_ref, z_ref, sem, zsem, *, tm, bm):
    i = pl.program_id(0)

    @pl.when(i == 0)
    def _():
        z_ref[...] = jnp.zeros_like(z_ref)

        def zcopy(e):
            return pltpu.make_async_copy(z_ref, xs_ref.at[pl.ds(pl.multiple_of(fill_ref[e], bm), bm)], zsem)

        def zstart(e, c):
            zcopy(e).start()
            return c

        def zwait(e, c):
            zcopy(e).wait()
            return c

        lax.fori_loop(0, N_EXPERTS, zstart, 0)
        lax.fori_loop(0, N_EXPERTS, zwait, 0)

    for tok in range(tm):
        for k in range(TOP_K):
            pltpu.make_async_copy(_row(u_ref, tok), _row(xs_ref, dest_ref[k, tok]), sem).start()
    n = tm * TOP_K
    pltpu.make_async_copy(xs_ref.at[pl.ds(0, n)], xs_ref.at[pl.ds(0, n)], sem).wait()


def _dispatch(u, dest, fill, n_slots, tm=256, bm=MOE_BM):
    T, D = u.shape
    kern = functools.partial(_dispatch_kernel, tm=tm, bm=bm)
    return pl.pallas_call(
        kern,
        grid=(T // tm,),
        in_specs=[pl.BlockSpec((SUBLANES, tm), lambda i: (0, i), memory_space=pltpu.SMEM),
                  pl.BlockSpec(memory_space=pltpu.SMEM),
                  pl.BlockSpec((tm, D), lambda i: (i, 0))],
        out_specs=pl.BlockSpec(memory_space=pl.ANY),
        out_shape=jax.ShapeDtypeStruct((n_slots, D), u.dtype),
        scratch_shapes=[pltpu.VMEM((bm, D), u.dtype),
                        pltpu.SemaphoreType.DMA(()), pltpu.SemaphoreType.DMA(())],
        compiler_params=_cp("arbitrary"),
        name="moe_dispatch",
    )(dest, fill, u)


def _expert_kernel(be_ref, nv_ref, x_ref, wg_ref, wu_ref, wd_ref, y_ref, wgb_ref, wub_ref, wdb_ref):
    b = pl.program_id(0)
    valid = b < nv_ref[0]
    new_expert = jnp.logical_or(b == 0, be_ref[b] != be_ref[jnp.maximum(b - 1, 0)])

    @pl.when(jnp.logical_and(valid, new_expert))
    def _():
        wgb_ref[...] = wg_ref[0].astype(BF16)
        wub_ref[...] = wu_ref[0].astype(BF16)
        wdb_ref[...] = wd_ref[0].astype(BF16)

    @pl.when(valid)
    def _():
        x = _unpack_pairs(x_ref[...]).astype(BF16)
        h = _silu(_dot(x, wgb_ref[...])) * _dot(x, wub_ref[...])
        y_ref[...] = _pack_pairs(_dot(h.astype(BF16), wdb_ref[...]))


def _experts(xs, block_expert, n_valid, w_gate, w_up, w_down, layer, bm=MOE_BM):
    n_slots, W = xs.shape
    D = 2 * W
    nb = n_slots // bm
    F = w_gate.shape[-1]
    blk = lambda b, be, nv: (jnp.minimum(b, nv[0] - 1), 0)
    wsel = lambda b, be, nv: (layer, be[jnp.minimum(b, nv[0] - 1)], 0, 0)
    gs = pltpu.PrefetchScalarGridSpec(
        num_scalar_prefetch=2,
        grid=(nb,),
        in_specs=[pl.BlockSpec((bm, W), blk),
                  pl.BlockSpec((None, 1, D, F), wsel),
                  pl.BlockSpec((None, 1, D, F), wsel),
                  pl.BlockSpec((None, 1, F, D), wsel)],
        out_specs=pl.BlockSpec((bm, W), blk),
        scratch_shapes=[pltpu.VMEM((D, F), BF16), pltpu.VMEM((D, F), BF16), pltpu.VMEM((F, D), BF16)],
    )
    return pl.pallas_call(
        _expert_kernel,
        grid_spec=gs,
        out_shape=jax.ShapeDtypeStruct((n_slots, W), U32),
        compiler_params=_cp("arbitrary"),
        name="moe_experts",
    )(block_expert, n_valid, xs, w_gate, w_up, w_down)


def _combine_kernel(dest_ref, dnext_ref, ys_ref, u_ref, x_ref, gt_ref, wg_ref, wu_ref, wd_ref, gate_ref, lg_ref,
                    lb_ref, o_ref, buf_ref, sem, *, tm):
    i = pl.program_id(0)
    last = pl.num_programs(0) - 1
    slot = lax.rem(i, 2)

    def row_copy(d_ref, s, tok, k):
        return pltpu.make_async_copy(_row(ys_ref, d_ref[k, tok]), _row(buf_ref.at[s, k], tok), sem.at[s])

    def wait_half(s):
        pltpu.make_async_copy(buf_ref.at[s], buf_ref.at[s], sem.at[s]).wait()

    @pl.when(i == 0)
    def _():
        def start(tok, c):
            for k in range(TOP_K):
                row_copy(dest_ref, slot, tok, k).start()
            return c

        lax.fori_loop(0, tm, start, 0)

    wait_half(slot)
    for tok in range(tm):
        for k in range(TOP_K):
            row_copy(dnext_ref, 1 - slot, tok, k).start()

    u = _unpack_pairs(u_ref[...]).astype(BF16)
    h = _silu(_dot(u, wg_ref[...])) * _dot(u, wu_ref[...])
    ffn = _dot(h.astype(BF16), wd_ref[...])
    gt = gt_ref[...]
    for k in range(TOP_K):
        ffn = ffn + gt[:, k:k + 1] * _unpack_pairs(buf_ref[slot, k])
    hres = DEEPNORM_ALPHA * x_ref[...] + (1.0 + gate_ref[...]) * ffn
    o_ref[...] = _layer_norm(hres, lg_ref[...], lb_ref[...])

    @pl.when(i == last)
    def _():
        wait_half(1 - slot)


def _combine(ys, dest, u, x, gates_t, w_sg, w_su, w_sd, gate, ln_g, ln_b, tm=128):
    T, D = x.shape
    W = ys.shape[1]
    F = w_sg.shape[-1]
    n = T // tm
    kern = functools.partial(_combine_kernel, tm=tm)
    spec = pl.BlockSpec((tm, D), lambda i: (i, 0))
    rowspec = pl.BlockSpec((1, D), lambda i: (0, 0))
    return pl.pallas_call(
        kern,
        grid=(n,),
        in_specs=[pl.BlockSpec((SUBLANES, tm), lambda i: (0, i), memory_space=pltpu.SMEM),
                  pl.BlockSpec((SUBLANES, tm), lambda i: (0, jnp.minimum(i + 1, n - 1)), memory_space=pltpu.SMEM),
                  pl.BlockSpec(memory_space=pl.ANY),
                  pl.BlockSpec((tm, W), lambda i: (i, 0)), spec,
                  pl.BlockSpec((tm, SUBLANES), lambda i: (i, 0)),
                  pl.BlockSpec((D, F), lambda i: (0, 0)),
                  pl.BlockSpec((D, F), lambda i: (0, 0)),
                  pl.BlockSpec((F, D), lambda i: (0, 0)),
                  rowspec, rowspec, rowspec],
        out_specs=spec,
        out_shape=jax.ShapeDtypeStruct((T, D), F32),
        scratch_shapes=[pltpu.VMEM((2, TOP_K, tm, W), ys.dtype), pltpu.SemaphoreType.DMA((2,))],
        compiler_params=_cp("arbitrary"),
        name="moe_combine_norm",
    )(dest, dest, ys, u, x, gates_t, w_sg, w_su, w_sd, gate, ln_g.reshape(1, D), ln_b.reshape(1, D))


def _moe_plan(top_e, rank, counts, bm, n_blocks):
    counts = counts.astype(I32)
    padded = (counts + bm - 1) // bm * bm
    padded_end = jnp.cumsum(padded)
    padded_start = padded_end - padded
    onehot = top_e[:, :, None] == jnp.arange(N_EXPERTS, dtype=I32)
    dest = jnp.sum(jnp.where(onehot, padded_start, 0), axis=-1) + rank
    fill = jnp.maximum(padded_end - bm, padded_start)
    fill = jnp.minimum(fill, (n_blocks - 1) * bm)
    block_start = jnp.arange(n_blocks, dtype=I32) * bm
    block_expert = jnp.minimum(jnp.sum((padded_end[None, :] <= block_start[:, None]).astype(I32), axis=1),
                               N_EXPERTS - 1)
    n_valid = jnp.maximum(padded_end[-1:] // bm, 1).astype(I32)
    return dest.astype(I32), fill.astype(I32), block_expert, n_valid


def _in_proj_weights(w_in):
    o = SB_COLS
    gq = w_in[:, o:o + 2 * GLA_KEY_DIM + GLA_VALUE_DIM]
    o += 2 * GLA_KEY_DIM + GLA_VALUE_DIM
    glr = w_in[:, o:o + GLA_GATE_RANK]
    o += GLA_GATE_RANK
    gout = w_in[:, o:o + GLA_VALUE_DIM]
    o += GLA_VALUE_DIM
    pad = jnp.zeros((w_in.shape[0], 2 * LANES - GLA_GATE_RANK), w_in.dtype)
    w_gla = jnp.concatenate([gq, gout, glr, pad], axis=1)
    w_rw = w_in[:, o:o + RWKV_COLS]
    o += RWKV_COLS
    w_mg = w_in[:, o:o + MERGE_COLS]
    return w_in[:, :SB_COLS], w_gla, w_rw, w_mg


def kernel(x, c, w_ada, b_ada, w_in, gla_gk_up, gla_gk_bias, gla_norm_g, rw_mu, rw_w0, rw_w2, rw_a0,
           rw_a2, rw_g2, rw_k_k, rw_k_a, rw_r_k, rw_lnx_g, rw_lnx_b, rw_v0, rw_v1, rw_v2,
           w_br_sb, w_br_gla, w_br_rw, w_o, ln_g, ln_b, w_router, router_bias,
           w_exp_gate, w_exp_up, w_exp_down, w_sh_gate, w_sh_up, w_sh_down):
    B, T, D = x.shape
    assert B == 1 and D == D_MODEL
    n_blocks = -(-(T * TOP_K) // MOE_BM) + N_EXPERTS
    mod = _ada(c, w_ada, b_ada)
    xs = x.reshape(T, D)
    v_first = None
    for l in range(DEPTH):
        sh1, sc1, g1, sh2, sc2, g2 = (mod[l, j * D:(j + 1) * D].reshape(1, D) for j in range(6))
        w_sb, w_gla, w_rw, w_mg = _in_proj_weights(w_in[l])
        u1 = _modulate(xs, sc1, sh1)
        p_sb = _wmm(u1, w_sb, BF16, 1024, name="inproj_sb")
        p_gla = _wmm(u1, w_gla, F32, 832 * 2, name="inproj_gla")
        p_rw = _wmm(u1, w_rw, F32, 832 * 2, name="inproj_rwkv")
        p_mg = _wmm(u1, w_mg, BF16, 1024, name="inproj_merge")

        o_sb = _sb_attention(p_sb)
        o_gla = _gla(p_gla, gla_gk_up[l], gla_gk_bias[l], gla_norm_g[l])
        if l == 0:
            r, w, k, v, a, b, g = _rwkv_prep(p_rw, None, rw_mu[l], rw_w0[l], rw_w2[l], rw_a0[l], rw_a2[l],
                                             rw_g2[l], rw_k_k[l], rw_k_a[l], None, None, None)
            v_first = v
        else:
            r, w, k, v, a, b, g = _rwkv_prep(p_rw, v_first, rw_mu[l], rw_w0[l], rw_w2[l], rw_a0[l], rw_a2[l],
                                             rw_g2[l], rw_k_k[l], rw_k_a[l], rw_v0[l - 1], rw_v1[l - 1],
                                             rw_v2[l - 1])
        y = _rwkv_recurrence(r, w, k, v, a, b)
        o_rw = _rwkv_post(y, r, k, v, g, rw_lnx_g[l], rw_lnx_b[l], rw_r_k[l])

        merged = _merge(o_sb, o_gla, o_rw, p_mg, w_br_sb[l].astype(BF16), w_br_gla[l].astype(BF16),
                        w_br_rw[l].astype(BF16))
        xs = _oproj(merged, w_o[l].astype(BF16), xs, g1, ln_g[l, 0], ln_b[l, 0])

        u2, top_e, gates, rank, counts = _router(xs, sc2, sh2, w_router[l], router_bias[l])
        dest, fill, block_expert, n_valid = _moe_plan(top_e, rank, counts[:, 0], MOE_BM, n_blocks)
        x_sorted = _dispatch(u2, dest, fill, n_blocks * MOE_BM)
        y_sorted = _experts(x_sorted, block_expert, n_valid, w_exp_gate, w_exp_up, w_exp_down, l)
        xs = _combine(y_sorted, dest, u2, xs, gates.T, w_sh_gate[l].astype(BF16), w_sh_up[l].astype(BF16),
                      w_sh_down[l].astype(BF16), g2, ln_g[l, 1], ln_b[l, 1])
    return xs.reshape(B, T, D)
```

```python
import functools

import numpy as np
import jax
import jax.numpy as jnp
from jax import lax
from jax.experimental import pallas as pl
from jax.experimental.pallas import tpu as pltpu

F32 = jnp.float32
BF16 = jnp.bfloat16
I32 = jnp.int32
U32 = jnp.uint32

D_MODEL = 2048
DEPTH = 2
LN_EPS = 1e-5
DEEPNORM_ALPHA = (2 * DEPTH) ** 0.25

SB_HEADS = 8
SB_HEAD_DIM = 128
SB_WIDTH = SB_HEADS * SB_HEAD_DIM

GLA_HEADS = 4
GLA_HEAD_K = 128
GLA_HEAD_V = 256
GLA_KEY_DIM = GLA_HEADS * GLA_HEAD_K
GLA_VALUE_DIM = GLA_HEADS * GLA_HEAD_V
GLA_GATE_RANK = 16
GLA_GATE_NORMALIZER = 16.0
GLA_CHUNK = 64
GLA_NORM_EPS = 1e-5

RWKV_HEADS = 16
RWKV_HEAD_DIM = 64
RWKV_WIDTH = RWKV_HEADS * RWKV_HEAD_DIM
RWKV_DECAY_RANK = 64
RWKV_AAA_RANK = 64
RWKV_VALUE_RANK = 32
RWKV_GATE_RANK = 128
RWKV_LNX_EPS = 64e-5

N_EXPERTS = 64
TOP_K = 6
EXPERT_DIM = 512
SHARED_DIM = 512
N_GROUPS = 8
TOPK_GROUPS = 4
ROUTED_SCALE = 2.5

SB_COLS = 3 * SB_WIDTH
GLA_COLS = 2 * GLA_KEY_DIM + GLA_VALUE_DIM + GLA_GATE_RANK + GLA_VALUE_DIM
RWKV_COLS = 3 * RWKV_WIDTH + RWKV_DECAY_RANK + RWKV_AAA_RANK + RWKV_GATE_RANK
MERGE_COLS = 3 * D_MODEL

LANES = 128
SUBLANES = 8
VMEM_LIMIT = 56 * 1024 * 1024
MOE_BM = 512
SB_EXIT = -104.0
SB_GROUPS = 2

NN = (((1,), (0,)), ((), ()))
NT = (((1,), (1,)), ((), ()))
TN = (((0,), (0,)), ((), ()))


def _cp(*sem):
    return pltpu.CompilerParams(dimension_semantics=sem, vmem_limit_bytes=VMEM_LIMIT)


def _dot(a, b, dims=NN):
    return lax.dot_general(a, b, dims, preferred_element_type=F32)


def _split(a, n):
    out = []
    r = a
    for i in range(n):
        p = r.astype(BF16)
        out.append(p)
        if i + 1 < n:
            r = r - p.astype(F32)
    return out


def _dot_fx(a, e, n, dims=NN):
    acc = None
    for p in _split(a, n):
        t = _dot(p, e, dims)
        acc = t if acc is None else acc + t
    return acc


def _dot_xf(e, b, n, dims=NN):
    acc = None
    for p in _split(b, n):
        t = _dot(e, p, dims)
        acc = t if acc is None else acc + t
    return acc


def _dot_ff(a, b, dims=NN):
    a1, a2, a3 = _split(a, 3)
    b1, b2, b3 = _split(b, 3)
    acc = _dot(a1, b1, dims)
    for p, q in ((a1, b2), (a2, b1), (a2, b2), (a1, b3), (a3, b1)):
        acc = acc + _dot(p, q, dims)
    return acc


def _sigmoid(x):
    return 1.0 / (1.0 + jnp.exp(-x))


def _softplus(x):
    return jnp.maximum(x, 0.0) + jnp.log1p(jnp.exp(-jnp.abs(x)))


def _silu(x):
    return x * _sigmoid(x)


def _row(ref, t):
    return ref.at[pl.ds(t, 1)]


def _pack_pairs(x):
    w = x.shape[1] // 2
    bits = lax.bitcast_convert_type(x.astype(BF16).astype(F32), U32)
    return (bits[:, :w] >> 16) | bits[:, w:]


def _unpack_pairs(p):
    lo = lax.bitcast_convert_type(p << 16, F32)
    hi = lax.bitcast_convert_type(p & jnp.uint32(0xFFFF0000), F32)
    return jnp.concatenate([lo, hi], axis=1)


def _ada_kernel(c_ref, w_ref, b_ref, o_ref):
    s = _silu(c_ref[...])
    o_ref[0] = jnp.sum(s * w_ref[0], axis=0, keepdims=True) + b_ref[0]


def _ada(c, w_ada, b_ada):
    L, D, N = w_ada.shape
    tn = 512
    out = pl.pallas_call(
        _ada_kernel,
        grid=(L, N // tn),
        in_specs=[pl.BlockSpec((D, 1), lambda l, j: (0, 0)),
                  pl.BlockSpec((1, D, tn), lambda l, j: (l, 0, j)),
                  pl.BlockSpec((1, 1, tn), lambda l, j: (l, 0, j))],
        out_specs=pl.BlockSpec((1, 1, tn), lambda l, j: (l, 0, j)),
        out_shape=jax.ShapeDtypeStruct((L, 1, N), F32),
        compiler_params=_cp("parallel", "parallel"),
        name="ada",
    )(c.reshape(D, 1), w_ada, b_ada.reshape(L, 1, N))
    return out[:, 0, :]


def _modulate_kernel(x_ref, sc_ref, sh_ref, o_ref):
    o_ref[...] = (x_ref[...] * (1.0 + sc_ref[...]) + sh_ref[...]).astype(o_ref.dtype)


def _modulate(x, sc, sh, tm=1024):
    T, D = x.shape
    spec = pl.BlockSpec((tm, D), lambda i: (i, 0))
    rowspec = pl.BlockSpec((1, D), lambda i: (0, 0))
    return pl.pallas_call(
        _modulate_kernel,
        grid=(T // tm,),
        in_specs=[spec, rowspec, rowspec],
        out_specs=spec,
        out_shape=jax.ShapeDtypeStruct((T, D), BF16),
        compiler_params=_cp("parallel"),
        name="modulate",
    )(x, sc, sh)


def _wmm_kernel(u_ref, w_ref, o_ref, wb_ref):
    @pl.when(pl.program_id(1) == 0)
    def _():
        wb_ref[...] = w_ref[...].astype(BF16)

    o_ref[...] = _dot(u_ref[...], wb_ref[...]).astype(o_ref.dtype)


def _wmm(u, w, out_dtype, tn, tm=512, name="wmm"):
    T, D = u.shape
    N = w.shape[1]
    return pl.pallas_call(
        _wmm_kernel,
        grid=(N // tn, T // tm),
        in_specs=[pl.BlockSpec((tm, D), lambda j, i: (i, 0)),
                  pl.BlockSpec((D, tn), lambda j, i: (0, j))],
        out_specs=pl.BlockSpec((tm, tn), lambda j, i: (i, j)),
        out_shape=jax.ShapeDtypeStruct((T, N), out_dtype),
        scratch_shapes=[pltpu.VMEM((D, tn), BF16)],
        compiler_params=_cp("parallel", "arbitrary"),
        name=name,
    )(u, w)


def _sb_kernel(q_ref, k_ref, v_ref, o_ref, acc_ref, car_ref, z_ref, *, bq, scale):
    i = pl.program_id(1)
    q = q_ref[...]
    r_id = lax.broadcasted_iota(I32, (bq, bq), 0)
    c_id = lax.broadcasted_iota(I32, (bq, bq), 1)
    later = (r_id > c_id).astype(BF16)
    causal = c_id < r_id
    grp = [slice(g * (bq // SB_GROUPS), (g + 1) * (bq // SB_GROUPS)) for g in range(SB_GROUPS)]

    def scores(j):
        kb = k_ref[pl.ds(pl.multiple_of(jnp.maximum(j, 0) * bq, bq), bq), :]
        return [_dot(q[s], kb, NT) for s in grp]

    def block(j, zs, carry, diagonal):
        vb = v_ref[pl.ds(pl.multiple_of(j * bq, bq), bq), :]
        z = [t * scale for t in zs]
        drop = [jnp.maximum(t, 0.0) + jnp.log(1.0 + jnp.exp(-jnp.abs(t))) for t in z]
        if diagonal:
            drop = [jnp.where(causal[s], d, 0.0) for s, d in zip(grp, drop)]
        after = [_dot_fx(d, later, 2) + carry[s] for s, d in zip(grp, drop)]
        w = [jnp.exp(t - d - a) for t, d, a in zip(z, drop, after)]
        if diagonal:
            w = [jnp.where(causal[s], x, 0.0) for s, x in zip(grp, w)]
        pv = jnp.concatenate([_dot(x.astype(BF16), vb) for x in w], axis=0)
        new_carry = jnp.concatenate([a[:, 0:1] + d[:, 0:1] for a, d in zip(after, drop)], axis=0)
        return pv, new_carry

    def stash(zs):
        for s, t in zip(grp, zs):
            z_ref[s, :] = t

    z_next = scores(i - 1)
    pv, carry = block(i, scores(i), jnp.zeros((bq, 1), F32), True)
    stash(z_next)
    acc_ref[...] = pv
    car_ref[...] = carry

    def body(c):
        j, _ = c
        zs = [z_ref[s, :] for s in grp]
        z_next = scores(j - 1)
        pv, carry = block(j, zs, car_ref[...], False)
        stash(z_next)
        acc_ref[...] += pv
        car_ref[...] = carry
        return j - 1, (jnp.min(carry) > -SB_EXIT).astype(I32)

    lax.while_loop(lambda c: jnp.logical_and(c[0] >= 0, c[1] == 0), body, (i - 1, jnp.int32(0)))
    o_ref[...] = acc_ref[...].astype(o_ref.dtype)


def _sb_attention(qkv, bq=256):
    T = qkv.shape[0]
    H, Dh = SB_HEADS, SB_HEAD_DIM
    kern = functools.partial(_sb_kernel, bq=bq, scale=Dh ** -0.5)
    return pl.pallas_call(
        kern,
        grid=(H, T // bq),
        in_specs=[pl.BlockSpec((bq, Dh), lambda h, i: (i, h)),
                  pl.BlockSpec((T, Dh), lambda h, i: (0, H + h)),
                  pl.BlockSpec((T, Dh), lambda h, i: (0, 2 * H + h))],
        out_specs=pl.BlockSpec((bq, Dh), lambda h, i: (i, h)),
        out_shape=jax.ShapeDtypeStruct((T, H * Dh), BF16),
        scratch_shapes=[pltpu.VMEM((bq, Dh), F32), pltpu.VMEM((bq, 1), F32), pltpu.VMEM((bq, bq), F32)],
        compiler_params=_cp("parallel", "parallel"),
        name="sb_attention",
    )(qkv, qkv, qkv)


_GLA_LEVELS = (32, 16, 8, 4, 2, 1)


def _gla_constants():
    C = GLA_CHUNK
    i = np.arange(C)[:, None]
    s = np.arange(C)[None, :]
    sel = []
    masks = []
    for m in _GLA_LEVELS:
        same = (i // m) == (s // m)
        sel.append(same & (s <= i) & ((i // m) % 2 == 1))
        sel.append(same & (s > i) & ((i // m) % 2 == 0))
        masks.append(((i // (2 * m)) == (s // (2 * m))) & ((i // m) % 2 == 1) & ((s // m) % 2 == 0))
    sel.append(s <= i)
    sel.append(s > i)
    masks.append(i == s)
    return (np.concatenate(sel, 0).astype(np.float32), np.stack(masks, 0).astype(np.float32))


def _gla_kernel(q_ref, k_ref, v_ref, go_ref, glr_ref, up_ref, bias_ref, ng_ref, sel_ref, msk_ref,
                o_ref, st_ref, *, n_chunks):
    C = GLA_CHUNK
    nl = len(_GLA_LEVELS)

    @pl.when(pl.program_id(1) == 0)
    def _():
        st_ref[...] = jnp.zeros_like(st_ref)

    sel = sel_ref[...]
    up = up_ref[...]
    bias = bias_ref[...]
    ng = ng_ref[...]
    rows = [slice(c * C, (c + 1) * C) for c in range(n_chunks)]
    blk = lambda f, j: f[j * C:(j + 1) * C]
    q = [q_ref[s, :] * (GLA_HEAD_K ** -0.5) for s in rows]
    k = [k_ref[s, :] for s in rows]
    v = [v_ref[s, :].astype(BF16) for s in rows]
    x = [_dot(glr_ref[s, :].astype(BF16), up) + bias for s in rows]
    g = [-_softplus(-t) * (1.0 / GLA_GATE_NORMALIZER) for t in x]
    f = [jnp.exp(_dot_xf(sel, t, 2)) for t in g]
    scores = [_dot(a.astype(BF16), b.astype(BF16), NT) * msk_ref[nl] for a, b in zip(q, k)]
    for l in range(nl):
        ql = [(a * blk(t, 2 * l)).astype(BF16) for a, t in zip(q, f)]
        kl = [(b * blk(t, 2 * l + 1)).astype(BF16) for b, t in zip(k, f)]
        scores = [s + _dot(a, b, NT) * msk_ref[l] for s, a, b in zip(scores, ql, kl)]
    o_intra = [_dot(s.astype(BF16), b) for s, b in zip(scores, v)]
    qe = [(a * blk(t, 2 * nl)).astype(BF16) for a, t in zip(q, f)]
    upd = [_dot(b, (a * blk(t, 2 * nl + 1)).astype(BF16), TN) for b, a, t in zip(v, k, f)]
    dec = [blk(t, 2 * nl)[C - 1:C, :] for t in f]

    st = st_ref[...]
    for c in range(n_chunks):
        o = _dot(qe[c], st.astype(BF16), NT) + o_intra[c]
        st = st * dec[c] + upd[c]
        rms = lax.rsqrt(jnp.mean(o * o, axis=-1, keepdims=True) + GLA_NORM_EPS)
        o_ref[rows[c], :] = (o * rms * ng * _silu(go_ref[rows[c], :])).astype(o_ref.dtype)
    st_ref[...] = st


def _gla(p_gla, gk_up, gk_bias, norm_g, tt=512):
    T = p_gla.shape[0]
    H, dk, dv = GLA_HEADS, GLA_HEAD_K, GLA_HEAD_V
    sel, msk = _gla_constants()
    up = jnp.zeros((LANES, GLA_KEY_DIM), F32).at[:GLA_GATE_RANK].set(gk_up).astype(BF16)
    kern = functools.partial(_gla_kernel, n_chunks=tt // GLA_CHUNK)
    return pl.pallas_call(
        kern,
        grid=(H, T // tt),
        in_specs=[pl.BlockSpec((tt, dk), lambda h, i: (i, h)),
                  pl.BlockSpec((tt, dk), lambda h, i: (i, H + h)),
                  pl.BlockSpec((tt, dv), lambda h, i: (i, H + h)),
                  pl.BlockSpec((tt, dv), lambda h, i: (i, 2 * H + h)),
                  pl.BlockSpec((tt, LANES), lambda h, i: (i, 3 * GLA_VALUE_DIM // LANES)),
                  pl.BlockSpec((LANES, dk), lambda h, i: (0, h)),
                  pl.BlockSpec((1, dk), lambda h, i: (0, h)),
                  pl.BlockSpec((1, dv), lambda h, i: (0, 0)),
                  pl.BlockSpec(sel.shape, lambda h, i: (0, 0)),
                  pl.BlockSpec(msk.shape, lambda h, i: (0, 0, 0))],
        out_specs=pl.BlockSpec((tt, dv), lambda h, i: (i, h)),
        out_shape=jax.ShapeDtypeStruct((T, H * dv), BF16),
        scratch_shapes=[pltpu.VMEM((dv, dk), F32)],
        compiler_params=_cp("parallel", "arbitrary"),
        name="gla",
    )(p_gla, p_gla, p_gla, p_gla, p_gla, up, gk_bias.reshape(1, -1), norm_g.reshape(1, -1),
      jnp.asarray(sel, BF16), jnp.asarray(msk, F32))


def _head_ones():
    l = np.arange(LANES)
    return (l[:, None] // RWKV_HEAD_DIM == l[None, :] // RWKV_HEAD_DIM).astype(np.float32)


def _rwprep_kernel(p_ref, pp_ref, mu_ref, w0_ref, w2_ref, a0_ref, a2_ref, g2_ref, kk_ref, ka_ref,
                   v0_ref, v1_ref, v2_ref, vf_ref, bo_ref,
                   r_ref, w_ref, k_ref, v_ref, a_ref, b_ref, g_ref, *, first_layer):
    W = RWKV_WIDTH
    i = pl.program_id(0)
    p = p_ref[...]
    tm = p.shape[0]
    prev_row = jnp.where(i == 0, 0.0, pp_ref[SUBLANES - 1:SUBLANES, :])
    rid = lax.broadcasted_iota(I32, p.shape, 0)
    prev = jnp.where(rid == 0, prev_row, pltpu.roll(p, 1, axis=0))
    ps = p + (prev - p) * mu_ref[...]
    r = ps[:, 0:W]
    kr = ps[:, W:2 * W]
    vr = ps[:, 2 * W:3 * W]
    lr = ps[:, 3 * W:3 * W + LANES]
    gl = ps[:, 3 * W + LANES:3 * W + 2 * LANES]
    w_log = -_softplus(-(w0_ref[...] + _dot(jnp.tanh(lr).astype(BF16), w2_ref[...]))) - 0.5
    w_ref[...] = -jnp.exp(w_log)
    if not first_layer:
        lo = _dot(vr.astype(BF16), v1_ref[...])
        gate = _sigmoid(v0_ref[...] + _dot(lo.astype(BF16), v2_ref[...]))
        vr = vr + (vf_ref[...] - vr) * gate
    v_ref[...] = vr
    a = _sigmoid(a0_ref[...] + _dot(lr.astype(BF16), a2_ref[...]))
    g_ref[...] = _dot(_sigmoid(gl).astype(BF16), g2_ref[...])
    kk = kr * kk_ref[...]
    bo = bo_ref[...]
    for hp in range(W // LANES):
        sl = slice(hp * LANES, (hp + 1) * LANES)
        kh = kk[:, sl]
        ss = _dot_fx(kh * kh, bo, 3)
        kn = kh / jnp.maximum(jnp.sqrt(ss), 1e-12)
        a_ref[:, sl] = -kn
        b_ref[:, sl] = kn * a[:, sl]
    r_ref[...] = r
    k_ref[...] = kr * (1.0 + (a - 1.0) * ka_ref[...])


def _rwkv_prep(p_rw, v_first, mu, w0, w2, a0, a2, g2, k_k, k_a, v0, v1, v2, tm=256):
    T, NC = p_rw.shape
    W = RWKV_WIDTH
    first = v_first is None
    w2p = jnp.zeros((LANES, W), F32).at[:RWKV_DECAY_RANK].set(w2).astype(BF16)
    a2p = jnp.zeros((LANES, W), F32).at[RWKV_DECAY_RANK:].set(a2).astype(BF16)
    if first:
        v_first = jnp.zeros((SUBLANES, W), F32)
        v0 = jnp.zeros((W,), F32)
        v1p = jnp.zeros((W, LANES), BF16)
        v2p = jnp.zeros((LANES, W), BF16)
        vf_spec = pl.BlockSpec((SUBLANES, W), lambda i: (0, 0))
    else:
        v1p = jnp.zeros((W, LANES), F32).at[:, :RWKV_VALUE_RANK].set(v1).astype(BF16)
        v2p = jnp.zeros((LANES, W), F32).at[:RWKV_VALUE_RANK].set(v2).astype(BF16)
        vf_spec = pl.BlockSpec((tm, W), lambda i: (i, 0))
    row = lambda a: a.reshape(1, -1)
    full = lambda shape: pl.BlockSpec(shape, lambda i: (0,) * len(shape))
    out = jax.ShapeDtypeStruct((T, W), F32)
    ospec = pl.BlockSpec((tm, W), lambda i: (i, 0))
    kern = functools.partial(_rwprep_kernel, first_layer=first)
    return pl.pallas_call(
        kern,
        grid=(T // tm,),
        in_specs=[pl.BlockSpec((tm, NC), lambda i: (i, 0)),
                  pl.BlockSpec((SUBLANES, NC), lambda i: (jnp.maximum(i * (tm // SUBLANES) - 1, 0), 0)),
                  full((1, NC)), full((1, W)), full((LANES, W)), full((1, W)), full((LANES, W)),
                  full((LANES, W)), full((1, W)), full((1, W)),
                  full((1, W)), full((W, LANES)), full((LANES, W)), vf_spec, full((LANES, LANES))],
        out_specs=[ospec] * 7,
        out_shape=[out] * 7,
        compiler_params=_cp("parallel"),
        name="rwkv_prep",
    )(p_rw, p_rw, row(mu), row(w0), w2p, row(a0), a2p, g2.astype(BF16), row(k_k), row(k_a),
      row(v0), v1p, v2p, v_first, jnp.asarray(_head_ones(), BF16))


RW_CHUNK = 64


def _rw_constants():
    C = RW_CHUNK
    i = np.arange(C)[:, None]
    s = np.arange(C)[None, :]
    i2 = np.arange(2 * C)[:, None]
    s2 = np.arange(2 * C)[None, :]
    same = (i2 // C) == (s2 // C)
    cm = np.stack([same & (s2 < i2), same & (s2 <= i2), (i2 // 16) == (s2 // 16), (i2 // 32) == (s2 // 32),
                   same], 0).astype(np.float32)
    return (s <= i).astype(np.float32), cm


def _mm(a, b, dims=NN):
    a1, a2 = _split(a, 2)
    b1, b2 = _split(b, 2)
    r = _dot(a1, jnp.concatenate([b1, b2], axis=0 if dims == NT else 1), dims)
    n = r.shape[1] // 2
    return (r[:, :n] + r[:, n:]) + _dot(a2, b1, dims)


def _mm1(a, b, dims=NN):
    return _dot(a.astype(BF16), b.astype(BF16), dims)


def _unit_lower_inverse(nms, eye, low, bd16, bd32, bd64):
    size = eye.shape[0]

    def pack(m, s):
        acc = m[0:s]
        for b in range(1, size // s):
            acc = acc + m[s * b:s * (b + 1)]
        return acc

    def unpack(p, mask):
        return jnp.concatenate([p] * (size // p.shape[0]), axis=0) * mask

    nd = [n * bd16 for n in nms]
    x = [pack(eye + n, 16) for n in nd]
    p = [_mm(pack(n, 16), n) for n in nd]
    for step in range(3):
        pf = [unpack(q, bd16) for q in p]
        x = [a + _mm(a, q) for a, q in zip(x, pf)]
        if step < 2:
            p = [_mm(q, qf) for q, qf in zip(p, pf)]
    x = [unpack(a, bd16) for a in x]
    for s, inner, outer in ((16, bd16, bd32), (32, bd32, bd64)):
        place = (outer - inner) * low
        t = [unpack(_mm(pack(n * place, s), a), place) for n, a in zip(nms, x)]
        x = [a + unpack(_mm(pack(a, s), q), place) for a, q in zip(x, t)]
    return x


def _rwrec_kernel(r_ref, lw_ref, k_ref, v_ref, a_ref, b_ref, lt_ref, cm_ref, y_ref, mt_ref, *, n_chunks):
    C = RW_CHUNK
    C2 = 2 * C

    @pl.when(pl.program_id(1) == 0)
    def _():
        mt_ref[...] = jnp.zeros_like(mt_ref)

    lt = lt_ref[...]
    sl, li, bd16, bd32, bdh = cm_ref[0], cm_ref[1], cm_ref[2], cm_ref[3], cm_ref[4]
    eye = li - sl
    lane = lax.broadcasted_iota(I32, (1, LANES), 1)
    m0 = jnp.where(lane < RWKV_HEAD_DIM, 1.0, 0.0)
    m1 = 1.0 - m0
    stack = lambda x: jnp.concatenate([x * m0, x * m1], axis=0)
    fold = lambda x: x[:C] + x[C:]
    rows = [slice(c * C, (c + 1) * C) for c in range(n_chunks)]

    lw = [lw_ref[s, :] for s in rows]
    cw = [_dot_xf(lt, x, 3) for x in lw]
    en = [jnp.exp(-x) for x in cw]
    at = [a_ref[s, :] * jnp.exp(c - l) for s, c, l in zip(rows, cw, lw)]
    rt = [r_ref[s, :] * jnp.exp(c) for s, c in zip(rows, cw)]
    bh = [b_ref[s, :] * e for s, e in zip(rows, en)]
    kh = [k_ref[s, :] * e for s, e in zip(rows, en)]
    v = [v_ref[s, :] for s in rows]
    a2 = [stack(x) for x in at]
    ar = [jnp.concatenate([x, y], axis=0) for x, y in zip(at, rt)]
    gb = [_mm(x, stack(y), NT) for x, y in zip(ar, bh)]
    gk = [_mm(x, stack(y), NT) for x, y in zip(ar, kh)]
    slf = fold(sl)
    lif = fold(li)
    tinv = _unit_lower_inverse([stack(g[:C] * slf) for g in gb], eye, sl, bd16, bd32, bdh)
    akv = [_mm(jnp.concatenate([g[:C] * slf, g[C:] * lif], axis=0), stack(x))
           for g, x in zip(gk, v)]
    tav = [_mm(fold(t), jnp.concatenate([x, stack(y[:C])], axis=1)) for t, x, y in zip(tinv, a2, akv)]
    ta = [x[:, :LANES] for x in tav]
    u0 = [x[:, LANES:] for x in tav]
    rav = [_mm1(g[C:] * lif, jnp.concatenate([stack(x), stack(y)], axis=1))
           for g, x, y in zip(gb, ta, u0)]
    qe = [x + y[:, :LANES] for x, y in zip(rt, rav)]
    y0 = [x[:, LANES:] + y[C:] for x, y in zip(rav, akv)]
    dl = [jnp.exp(x[C - 1:C, :]) for x in cw]
    gmt = [(eye + _mm(x, y, TN) * bdh) * d for x, y, d in zip(ta, bh, dl)]
    hmt = [(_mm(x, y, TN) + _mm(z, w, TN)) * bdh * d for x, y, z, w, d in zip(u0, bh, v, kh, dl)]

    mt = mt_ref[...]
    for c in range(n_chunks):
        y_ref[rows[c], :] = _mm1(qe[c], mt, NT) + y0[c]
        mt = _mm(mt, gmt[c]) + hmt[c]
    mt_ref[...] = mt


def _rwkv_recurrence(r, lw, k, v, a, b, tt=512):
    T, W = r.shape
    lt, cm = _rw_constants()
    spec = pl.BlockSpec((tt, LANES), lambda p, i: (i, p))
    kern = functools.partial(_rwrec_kernel, n_chunks=tt // RW_CHUNK)
    return pl.pallas_call(
        kern,
        grid=(W // LANES, T // tt),
        in_specs=[spec] * 6 + [pl.BlockSpec(lt.shape, lambda p, i: (0, 0)),
                               pl.BlockSpec(cm.shape, lambda p, i: (0, 0, 0))],
        out_specs=spec,
        out_shape=jax.ShapeDtypeStruct((T, W), F32),
        scratch_shapes=[pltpu.VMEM((LANES, LANES), F32)],
        compiler_params=_cp("parallel", "arbitrary"),
        name="rwkv_recurrence",
    )(r, lw, k, v, a, b, jnp.asarray(lt, BF16), jnp.asarray(cm, F32))


def _rwpost_kernel(y_ref, r_ref, k_ref, v_ref, g_ref, lg_ref, lb_ref, rk_ref, bo_ref, o_ref):
    bo = bo_ref[...]
    inv = 1.0 / RWKV_HEAD_DIM
    for hp in range(RWKV_WIDTH // LANES):
        sl = slice(hp * LANES, (hp + 1) * LANES)
        y = y_ref[:, sl]
        mu = _dot_fx(y, bo, 3) * inv
        d = y - mu
        var = _dot_fx(d * d, bo, 3) * inv
        yn = d * lax.rsqrt(var + RWKV_LNX_EPS) * lg_ref[:, sl] + lb_ref[:, sl]
        bonus = _dot_fx(r_ref[:, sl] * k_ref[:, sl] * rk_ref[:, sl], bo, 3) * v_ref[:, sl]
        o_ref[:, sl] = ((yn + bonus) * g_ref[:, sl]).astype(o_ref.dtype)


def _rwkv_post(y, r, k, v, g, lnx_g, lnx_b, r_k, tm=512):
    T, W = y.shape
    spec = pl.BlockSpec((tm, W), lambda i: (i, 0))
    rowspec = pl.BlockSpec((1, W), lambda i: (0, 0))
    return pl.pallas_call(
        _rwpost_kernel,
        grid=(T // tm,),
        in_specs=[spec] * 5 + [rowspec] * 3 + [pl.BlockSpec((LANES, LANES), lambda i: (0, 0))],
        out_specs=spec,
        out_shape=jax.ShapeDtypeStruct((T, W), BF16),
        compiler_params=_cp("parallel"),
        name="rwkv_post",
    )(y, r, k, v, g, lnx_g.reshape(1, W), lnx_b.reshape(1, W), r_k.reshape(1, W),
      jnp.asarray(_head_ones(), BF16))


def _merge_kernel(osb_ref, ogl_ref, orw_ref, g1_ref, g2_ref, g3_ref, w1_ref, w2_ref, w3_ref, o_ref):
    acc = _sigmoid(g1_ref[...].astype(F32)) * _dot(osb_ref[...], w1_ref[...])
    acc = acc + _sigmoid(g2_ref[...].astype(F32)) * _dot(ogl_ref[...], w2_ref[...])
    acc = acc + _sigmoid(g3_ref[...].astype(F32)) * _dot(orw_ref[...], w3_ref[...])
    o_ref[...] = acc.astype(o_ref.dtype)


def _merge(o_sb, o_gla, o_rw, p_merge, w_sb, w_gla, w_rw, tm=512, tn=1024):
    T = o_sb.shape[0]
    D = D_MODEL
    nb = D // tn
    aspec = pl.BlockSpec((tm, 1024), lambda j, i: (i, 0))
    wspec = pl.BlockSpec((1024, tn), lambda j, i: (0, j))
    gspec = lambda b: pl.BlockSpec((tm, tn), lambda j, i: (i, b * nb + j))
    return pl.pallas_call(
        _merge_kernel,
        grid=(nb, T // tm),
        in_specs=[aspec, aspec, aspec, gspec(0), gspec(1), gspec(2), wspec, wspec, wspec],
        out_specs=pl.BlockSpec((tm, tn), lambda j, i: (i, j)),
        out_shape=jax.ShapeDtypeStruct((T, D), BF16),
        compiler_params=_cp("parallel", "parallel"),
        name="merge",
    )(o_sb, o_gla, o_rw, p_merge, p_merge, p_merge, w_sb, w_gla, w_rw)


def _layer_norm(h, g, b):
    mu = jnp.mean(h, axis=-1, keepdims=True)
    d = h - mu
    var = jnp.mean(d * d, axis=-1, keepdims=True)
    return d * lax.rsqrt(var + LN_EPS) * g + b


def _oproj_kernel(m_ref, w_ref, x_ref, gate_ref, lg_ref, lb_ref, o_ref):
    mix = _dot(m_ref[...], w_ref[...])
    h = DEEPNORM_ALPHA * x_ref[...] + (1.0 + gate_ref[...]) * mix
    o_ref[...] = _layer_norm(h, lg_ref[...], lb_ref[...])


def _oproj(merged, w_o, x, gate, ln_g, ln_b, tm=512):
    T, D = x.shape
    spec = pl.BlockSpec((tm, D), lambda i: (i, 0))
    rowspec = pl.BlockSpec((1, D), lambda i: (0, 0))
    return pl.pallas_call(
        _oproj_kernel,
        grid=(T // tm,),
        in_specs=[spec, pl.BlockSpec((D, D), lambda i: (0, 0)), spec, rowspec, rowspec, rowspec],
        out_specs=spec,
        out_shape=jax.ShapeDtypeStruct((T, D), F32),
        compiler_params=_cp("parallel"),
        name="oproj_norm",
    )(merged, w_o, x, gate, ln_g.reshape(1, D), ln_b.reshape(1, D))


def _first_max(x, idx, axis, big):
    m = jnp.max(x, axis=axis, keepdims=True)
    first = jnp.min(jnp.where(x == m, idx, big), axis=axis, keepdims=True)
    return m, idx == first


def _router_kernel(x_ref, sc_ref, sh_ref, wr_ref, rb_ref, u_ref, e_ref, g_ref, p_ref, cnt_ref, car_ref, *, tm):
    E, G = N_EXPERTS, N_GROUPS
    per = E // G
    neg = -jnp.inf

    @pl.when(pl.program_id(0) == 0)
    def _():
        car_ref[...] = jnp.zeros_like(car_ref)

    u = x_ref[...] * (1.0 + sc_ref[...]) + sh_ref[...]
    u_ref[...] = _pack_pairs(u)
    scores =_sigmoid(_dot_ff(wr_ref[...], u, NT))
    biased = scores + rb_ref[...]
    b3 = biased.reshape(G, per, tm)
    i3 = lax.broadcasted_iota(I32, (G, per, tm), 1)
    m1, hit = _first_max(b3, i3, 1, per)
    m2 = jnp.max(jnp.where(hit, neg, b3), axis=1, keepdims=True)
    gs = (m1 + m2).reshape(G, tm)
    gi = lax.broadcasted_iota(I32, (G, tm), 0)
    ok = jnp.zeros((G, tm), jnp.bool_)
    for _ in range(TOPK_GROUPS):
        _, hit = _first_max(gs, gi, 0, G)
        ok = jnp.logical_or(ok, hit)
        gs = jnp.where(hit, neg, gs)
    ok_e = jnp.broadcast_to(ok.reshape(G, 1, tm), (G, per, tm)).reshape(E, tm)
    cand = jnp.where(ok_e, biased, neg)
    ei = lax.broadcasted_iota(I32, (E, tm), 0)
    sels, gates = [], []
    for _ in range(TOP_K):
        _, hit = _first_max(cand, ei, 0, E)
        sels.append(hit)
        gates.append(jnp.sum(jnp.where(hit, scores, 0.0), axis=0, keepdims=True))
        cand = jnp.where(hit, neg, cand)
    denom = gates[0]
    for gk in gates[1:]:
        denom = denom + gk
    chosen = sels[0]
    for s in sels[1:]:
        chosen = jnp.logical_or(chosen, s)
    cnt = jnp.where(chosen, 1.0, 0.0)
    earlier = (lax.broadcasted_iota(I32, (tm, tm), 0) < lax.broadcasted_iota(I32, (tm, tm), 1)).astype(BF16)
    rank = _dot(cnt.astype(BF16), earlier) + car_ref[...][:, 0:1]
    car_ref[...] = car_ref[...] + jnp.sum(cnt, axis=1, keepdims=True)
    cnt_ref[...] = car_ref[...]
    zi = jnp.zeros((1, tm), I32)
    zf = jnp.zeros((1, tm), F32)
    e_rows = [jnp.sum(jnp.where(s, ei, 0), axis=0, keepdims=True) for s in sels]
    p_rows = [jnp.sum(jnp.where(s, rank, 0.0), axis=0, keepdims=True).astype(I32) for s in sels]
    g_rows = [gk / denom * ROUTED_SCALE for gk in gates]
    e_ref[...] = jnp.concatenate(e_rows + [zi, zi], axis=0)
    p_ref[...] = jnp.concatenate(p_rows + [zi, zi], axis=0)
    g_ref[...] = jnp.concatenate(g_rows + [zf, zf], axis=0)


def _router(x, sc, sh, w_router, router_bias, tm=256):
    T, D = x.shape
    E = N_EXPERTS
    kern = functools.partial(_router_kernel, tm=tm)
    rowspec = pl.BlockSpec((1, D), lambda i: (0, 0))
    kspec = pl.BlockSpec((SUBLANES, tm), lambda i: (0, i))
    return pl.pallas_call(
        kern,
        grid=(T // tm,),
        in_specs=[pl.BlockSpec((tm, D), lambda i: (i, 0)), rowspec, rowspec,
                  pl.BlockSpec((E, D), lambda i: (0, 0)), pl.BlockSpec((E, 1), lambda i: (0, 0))],
        out_specs=[pl.BlockSpec((tm, D // 2), lambda i: (i, 0)), kspec, kspec, kspec,
                   pl.BlockSpec((E, LANES), lambda i: (0, 0))],
        out_shape=[jax.ShapeDtypeStruct((T, D // 2), U32),
                   jax.ShapeDtypeStruct((SUBLANES, T), I32),
                   jax.ShapeDtypeStruct((SUBLANES, T), F32),
                   jax.ShapeDtypeStruct((SUBLANES, T), I32),
                   jax.ShapeDtypeStruct((E, LANES), F32)],
        scratch_shapes=[pltpu.VMEM((E, LANES), F32)],
        compiler_params=_cp("arbitrary"),
        name="router",
    )(x, sc, sh, w_router.T, router_bias.reshape(E, 1))


def _dispatch_kernel(dest_ref, fill_ref, u_ref, xs_ref, z_ref, sem, zsem, *, tm, bm):
    i = pl.program_id(0)

    @pl.when(i == 0)
    def _():
        z_ref[...] = jnp.zeros_like(z_ref)

        def zcopy(e):
            return pltpu.make_async_copy(z_ref, xs_ref.at[pl.ds(pl.multiple_of(fill_ref[e], bm), bm)], zsem)

        def zstart(e, c):
            zcopy(e).start()
            return c

        def zwait(e, c):
            zcopy(e).wait()
            return c

        lax.fori_loop(0, N_EXPERTS, zstart, 0)
        lax.fori_loop(0, N_EXPERTS, zwait, 0)

    for tok in range(tm):
        for k in range(TOP_K):
            pltpu.make_async_copy(_row(u_ref, tok), _row(xs_ref, dest_ref[k, tok]), sem).start()
    n = tm * TOP_K
    pltpu.make_async_copy(xs_ref.at[pl.ds(0, n)], xs_ref.at[pl.ds(0, n)], sem).wait()


def _dispatch(u, dest, fill, n_slots, tm=256, bm=MOE_BM):
    T, D = u.shape
    kern = functools.partial(_dispatch_kernel, tm=tm, bm=bm)
    return pl.pallas_call(
        kern,
        grid=(T // tm,),
        in_specs=[pl.BlockSpec((SUBLANES, tm), lambda i: (0, i), memory_space=pltpu.SMEM),
                  pl.BlockSpec(memory_space=pltpu.SMEM),
                  pl.BlockSpec((tm, D), lambda i: (i, 0))],
        out_specs=pl.BlockSpec(memory_space=pl.ANY),
        out_shape=jax.ShapeDtypeStruct((n_slots, D), u.dtype),
        scratch_shapes=[pltpu.VMEM((bm, D), u.dtype),
                        pltpu.SemaphoreType.DMA(()), pltpu.SemaphoreType.DMA(())],
        compiler_params=_cp("arbitrary"),
        name="moe_dispatch",
    )(dest, fill, u)


def _expert_kernel(be_ref, nv_ref, x_ref, wg_ref, wu_ref, wd_ref, y_ref, wgb_ref, wub_ref, wdb_ref):
    b = pl.program_id(0)
    valid = b < nv_ref[0]
    new_expert = jnp.logical_or(b == 0, be_ref[b] != be_ref[jnp.maximum(b - 1, 0)])

    @pl.when(jnp.logical_and(valid, new_expert))
    def _():
        wgb_ref[...] = wg_ref[0].astype(BF16)
        wub_ref[...] = wu_ref[0].astype(BF16)
        wdb_ref[...] = wd_ref[0].astype(BF16)

    @pl.when(valid)
    def _():
        x = _unpack_pairs(x_ref[...]).astype(BF16)
        h = _silu(_dot(x, wgb_ref[...])) * _dot(x, wub_ref[...])
        y_ref[...] = _pack_pairs(_dot(h.astype(BF16), wdb_ref[...]))


def _experts(xs, block_expert, n_valid, w_gate, w_up, w_down, layer, bm=MOE_BM):
    n_slots, W = xs.shape
    D = 2 * W
    nb = n_slots // bm
    F = w_gate.shape[-1]
    blk = lambda b, be, nv: (jnp.minimum(b, nv[0] - 1), 0)
    wsel = lambda b, be, nv: (layer, be[jnp.minimum(b, nv[0] - 1)], 0, 0)
    gs = pltpu.PrefetchScalarGridSpec(
        num_scalar_prefetch=2,
        grid=(nb,),
        in_specs=[pl.BlockSpec((bm, W), blk),
                  pl.BlockSpec((None, 1, D, F), wsel),
                  pl.BlockSpec((None, 1, D, F), wsel),
                  pl.BlockSpec((None, 1, F, D), wsel)],
        out_specs=pl.BlockSpec((bm, W), blk),
        scratch_shapes=[pltpu.VMEM((D, F), BF16), pltpu.VMEM((D, F), BF16), pltpu.VMEM((F, D), BF16)],
    )
    return pl.pallas_call(
        _expert_kernel,
        grid_spec=gs,
        out_shape=jax.ShapeDtypeStruct((n_slots, W), U32),
        compiler_params=_cp("arbitrary"),
        name="moe_experts",
    )(block_expert, n_valid, xs, w_gate, w_up, w_down)


def _combine_kernel(dest_ref, dnext_ref, ys_ref, u_ref, x_ref, gt_ref, wg_ref, wu_ref, wd_ref, gate_ref, lg_ref,
                    lb_ref, o_ref, buf_ref, sem, *, tm):
    i = pl.program_id(0)
    last = pl.num_programs(0) - 1
    slot = lax.rem(i, 2)

    def row_copy(d_ref, s, tok, k):
        return pltpu.make_async_copy(_row(ys_ref, d_ref[k, tok]), _row(buf_ref.at[s, k], tok), sem.at[s])

    def wait_half(s):
        pltpu.make_async_copy(buf_ref.at[s], buf_ref.at[s], sem.at[s]).wait()

    @pl.when(i == 0)
    def _():
        def start(tok, c):
            for k in range(TOP_K):
                row_copy(dest_ref, slot, tok, k).start()
            return c

        lax.fori_loop(0, tm, start, 0)

    wait_half(slot)
    for tok in range(tm):
        for k in range(TOP_K):
            row_copy(dnext_ref, 1 - slot, tok, k).start()

    u = _unpack_pairs(u_ref[...]).astype(BF16)
    h = _silu(_dot(u, wg_ref[...])) * _dot(u, wu_ref[...])
    ffn = _dot(h.astype(BF16), wd_ref[...])
    gt = gt_ref[...]
    for k in range(TOP_K):
        ffn = ffn + gt[:, k:k + 1] * _unpack_pairs(buf_ref[slot, k])
    hres = DEEPNORM_ALPHA * x_ref[...] + (1.0 + gate_ref[...]) * ffn
    o_ref[...] = _layer_norm(hres, lg_ref[...], lb_ref[...])

    @pl.when(i == last)
    def _():
        wait_half(1 - slot)


def _combine(ys, dest, u, x, gates_t, w_sg, w_su, w_sd, gate, ln_g, ln_b, tm=128):
    T, D = x.shape
    W = ys.shape[1]
    F = w_sg.shape[-1]
    n = T // tm
    kern = functools.partial(_combine_kernel, tm=tm)
    spec = pl.BlockSpec((tm, D), lambda i: (i, 0))
    rowspec = pl.BlockSpec((1, D), lambda i: (0, 0))
    return pl.pallas_call(
        kern,
        grid=(n,),
        in_specs=[pl.BlockSpec((SUBLANES, tm), lambda i: (0, i), memory_space=pltpu.SMEM),
                  pl.BlockSpec((SUBLANES, tm), lambda i: (0, jnp.minimum(i + 1, n - 1)), memory_space=pltpu.SMEM),
                  pl.BlockSpec(memory_space=pl.ANY),
                  pl.BlockSpec((tm, W), lambda i: (i, 0)), spec,
                  pl.BlockSpec((tm, SUBLANES), lambda i: (i, 0)),
                  pl.BlockSpec((D, F), lambda i: (0, 0)),
                  pl.BlockSpec((D, F), lambda i: (0, 0)),
                  pl.BlockSpec((F, D), lambda i: (0, 0)),
                  rowspec, rowspec, rowspec],
        out_specs=spec,
        out_shape=jax.ShapeDtypeStruct((T, D), F32),
        scratch_shapes=[pltpu.VMEM((2, TOP_K, tm, W), ys.dtype), pltpu.SemaphoreType.DMA((2,))],
        compiler_params=_cp("arbitrary"),
        name="moe_combine_norm",
    )(dest, dest, ys, u, x, gates_t, w_sg, w_su, w_sd, gate, ln_g.reshape(1, D), ln_b.reshape(1, D))


def _moe_plan(top_e, rank, counts, bm, n_blocks):
    counts = counts.astype(I32)
    padded = (counts + bm - 1) // bm * bm
    padded_end = jnp.cumsum(padded)
    padded_start = padded_end - padded
    onehot = top_e[:, :, None] == jnp.arange(N_EXPERTS, dtype=I32)
    dest = jnp.sum(jnp.where(onehot, padded_start, 0), axis=-1) + rank
    fill = jnp.maximum(padded_end - bm, padded_start)
    fill = jnp.minimum(fill, (n_blocks - 1) * bm)
    block_start = jnp.arange(n_blocks, dtype=I32) * bm
    block_expert = jnp.minimum(jnp.sum((padded_end[None, :] <= block_start[:, None]).astype(I32), axis=1),
                               N_EXPERTS - 1)
    n_valid = jnp.maximum(padded_end[-1:] // bm, 1).astype(I32)
    return dest.astype(I32), fill.astype(I32), block_expert, n_valid


def _in_proj_weights(w_in):
    o = SB_COLS
    gq = w_in[:, o:o + 2 * GLA_KEY_DIM + GLA_VALUE_DIM]
    o += 2 * GLA_KEY_DIM + GLA_VALUE_DIM
    glr = w_in[:, o:o + GLA_GATE_RANK]
    o += GLA_GATE_RANK
    gout = w_in[:, o:o + GLA_VALUE_DIM]
    o += GLA_VALUE_DIM
    pad = jnp.zeros((w_in.shape[0], 2 * LANES - GLA_GATE_RANK), w_in.dtype)
    w_gla = jnp.concatenate([gq, gout, glr, pad], axis=1)
    w_rw = w_in[:, o:o + RWKV_COLS]
    o += RWKV_COLS
    w_mg = w_in[:, o:o + MERGE_COLS]
    return w_in[:, :SB_COLS], w_gla, w_rw, w_mg


def kernel(x, c, w_ada, b_ada, w_in, gla_gk_up, gla_gk_bias, gla_norm_g, rw_mu, rw_w0, rw_w2, rw_a0,
           rw_a2, rw_g2, rw_k_k, rw_k_a, rw_r_k, rw_lnx_g, rw_lnx_b, rw_v0, rw_v1, rw_v2,
           w_br_sb, w_br_gla, w_br_rw, w_o, ln_g, ln_b, w_router, router_bias,
           w_exp_gate, w_exp_up, w_exp_down, w_sh_gate, w_sh_up, w_sh_down):
    B, T, D = x.shape
    assert B == 1 and D == D_MODEL
    n_blocks = -(-(T * TOP_K) // MOE_BM) + N_EXPERTS
    mod = _ada(c, w_ada, b_ada)
    xs = x.reshape(T, D)
    v_first = None
    for l in range(DEPTH):
        sh1, sc1, g1, sh2, sc2, g2 = (mod[l, j * D:(j + 1) * D].reshape(1, D) for j in range(6))
        w_sb, w_gla, w_rw, w_mg = _in_proj_weights(w_in[l])
        u1 = _modulate(xs, sc1, sh1)
        p_sb = _wmm(u1, w_sb, BF16, 1024, name="inproj_sb")
        p_gla = _wmm(u1, w_gla, F32, 13 * LANES, name="inproj_gla")
        p_rw = _wmm(u1, w_rw, F32, 13 * LANES, name="inproj_rwkv")
        p_mg = _wmm(u1, w_mg, BF16, 1024, name="inproj_merge")

        o_sb = _sb_attention(p_sb)
        o_gla = _gla(p_gla, gla_gk_up[l], gla_gk_bias[l], gla_norm_g[l])
        if l == 0:
            r, w, k, v, a, b, g = _rwkv_prep(p_rw, None, rw_mu[l], rw_w0[l], rw_w2[l], rw_a0[l], rw_a2[l],
                                             rw_g2[l], rw_k_k[l], rw_k_a[l], None, None, None)
            v_first = v
        else:
            r, w, k, v, a, b, g = _rwkv_prep(p_rw, v_first, rw_mu[l], rw_w0[l], rw_w2[l], rw_a0[l], rw_a2[l],
                                             rw_g2[l], rw_k_k[l], rw_k_a[l], rw_v0[l - 1], rw_v1[l - 1],
                                             rw_v2[l - 1])
        y = _rwkv_recurrence(r, w, k, v, a, b)
        o_rw = _rwkv_post(y, r, k, v, g, rw_lnx_g[l], rw_lnx_b[l], rw_r_k[l])

        merged = _merge(o_sb, o_gla, o_rw, p_mg, w_br_sb[l].astype(BF16), w_br_gla[l].astype(BF16),
                        w_br_rw[l].astype(BF16))
        xs = _oproj(merged, w_o[l].astype(BF16), xs, g1, ln_g[l, 0], ln_b[l, 0])

        u2, top_e, gates, rank, counts = _router(xs, sc2, sh2, w_router[l], router_bias[l])
        dest, fill, block_expert, n_valid = _moe_plan(top_e, rank, counts[:, 0], MOE_BM, n_blocks)
        x_sorted = _dispatch(u2, dest, fill, n_blocks * MOE_BM)
        y_sorted = _experts(x_sorted, block_expert, n_valid, w_exp_gate, w_exp_up, w_exp_down, l)
        xs = _combine(y_sorted, dest, u2, xs, gates.T, w_sh_gate[l].astype(BF16), w_sh_up[l].astype(BF16),
                      w_sh_down[l].astype(BF16), g2, ln_g[l, 1], ln_b[l, 1])
    return xs.reshape(B, T, D)
```

```python
import functools

import numpy as np
import jax
import jax.numpy as jnp
from jax import lax
from jax.experimental import pallas as pl
from jax.experimental.pallas import tpu as pltpu

F32 = jnp.float32
BF16 = jnp.bfloat16
I32 = jnp.int32
U32 = jnp.uint32

D_MODEL = 2048
DEPTH = 2
LN_EPS = 1e-5
DEEPNORM_ALPHA = (2 * DEPTH) ** 0.25

SB_HEADS = 8
SB_HEAD_DIM = 128
SB_WIDTH = SB_HEADS * SB_HEAD_DIM

GLA_HEADS = 4
GLA_HEAD_K = 128
GLA_HEAD_V = 256
GLA_KEY_DIM = GLA_HEADS * GLA_HEAD_K
GLA_VALUE_DIM = GLA_HEADS * GLA_HEAD_V
GLA_GATE_RANK = 16
GLA_GATE_NORMALIZER = 16.0
GLA_CHUNK = 64
GLA_NORM_EPS = 1e-5

RWKV_HEADS = 16
RWKV_HEAD_DIM = 64
RWKV_WIDTH = RWKV_HEADS * RWKV_HEAD_DIM
RWKV_DECAY_RANK = 64
RWKV_AAA_RANK = 64
RWKV_VALUE_RANK = 32
RWKV_GATE_RANK = 128
RWKV_LNX_EPS = 64e-5

N_EXPERTS = 64
TOP_K = 6
EXPERT_DIM = 512
SHARED_DIM = 512
N_GROUPS = 8
TOPK_GROUPS = 4
ROUTED_SCALE = 2.5

SB_COLS = 3 * SB_WIDTH
GLA_COLS = 2 * GLA_KEY_DIM + GLA_VALUE_DIM + GLA_GATE_RANK + GLA_VALUE_DIM
RWKV_COLS = 3 * RWKV_WIDTH + RWKV_DECAY_RANK + RWKV_AAA_RANK + RWKV_GATE_RANK
MERGE_COLS = 3 * D_MODEL

LANES = 128
SUBLANES = 8
VMEM_LIMIT = 56 * 1024 * 1024
MOE_BM = 512
SB_EXIT = -104.0
SB_GROUPS = 2

NN = (((1,), (0,)), ((), ()))
NT = (((1,), (1,)), ((), ()))
TN = (((0,), (0,)), ((), ()))


def _cp(*sem):
    return pltpu.CompilerParams(dimension_semantics=sem, vmem_limit_bytes=VMEM_LIMIT)


def _dot(a, b, dims=NN):
    return lax.dot_general(a, b, dims, preferred_element_type=F32)


def _split(a, n):
    out = []
    r = a
    for i in range(n):
        p = r.astype(BF16)
        out.append(p)
        if i + 1 < n:
            r = r - p.astype(F32)
    return out


def _dot_fx(a, e, n, dims=NN):
    acc = None
    for p in _split(a, n):
        t = _dot(p, e, dims)
        acc = t if acc is None else acc + t
    return acc


def _dot_xf(e, b, n, dims=NN):
    acc = None
    for p in _split(b, n):
        t = _dot(e, p, dims)
        acc = t if acc is None else acc + t
    return acc


def _dot_ff(a, b, dims=NN):
    a1, a2, a3 = _split(a, 3)
    b1, b2, b3 = _split(b, 3)
    acc = _dot(a1, b1, dims)
    for p, q in ((a1, b2), (a2, b1), (a2, b2), (a1, b3), (a3, b1)):
        acc = acc + _dot(p, q, dims)
    return acc


def _sigmoid(x):
    return 1.0 / (1.0 + jnp.exp(-x))


def _softplus(x):
    return jnp.maximum(x, 0.0) + jnp.log1p(jnp.exp(-jnp.abs(x)))


def _silu(x):
    return x * _sigmoid(x)


def _row(ref, t):
    return ref.at[pl.ds(t, 1)]


def _pack_pairs(x):
    w = x.shape[1] // 2
    bits = lax.bitcast_convert_type(x.astype(BF16).astype(F32), U32)
    return (bits[:, :w] >> 16) | bits[:, w:]


def _unpack_pairs(p):
    lo = lax.bitcast_convert_type(p << 16, F32)
    hi = lax.bitcast_convert_type(p & jnp.uint32(0xFFFF0000), F32)
    return jnp.concatenate([lo, hi], axis=1)


def _ada_kernel(c_ref, w_ref, b_ref, o_ref):
    s = _silu(c_ref[...])
    o_ref[0] = jnp.sum(s * w_ref[0], axis=0, keepdims=True) + b_ref[0]


def _ada(c, w_ada, b_ada):
    L, D, N = w_ada.shape
    tn = 512
    out = pl.pallas_call(
        _ada_kernel,
        grid=(L, N // tn),
        in_specs=[pl.BlockSpec((D, 1), lambda l, j: (0, 0)),
                  pl.BlockSpec((1, D, tn), lambda l, j: (l, 0, j)),
                  pl.BlockSpec((1, 1, tn), lambda l, j: (l, 0, j))],
        out_specs=pl.BlockSpec((1, 1, tn), lambda l, j: (l, 0, j)),
        out_shape=jax.ShapeDtypeStruct((L, 1, N), F32),
        compiler_params=_cp("parallel", "parallel"),
        name="ada",
    )(c.reshape(D, 1), w_ada, b_ada.reshape(L, 1, N))
    return out[:, 0, :]


def _modulate_kernel(x_ref, sc_ref, sh_ref, o_ref):
    o_ref[...] = (x_ref[...] * (1.0 + sc_ref[...]) + sh_ref[...]).astype(o_ref.dtype)


def _modulate(x, sc, sh, tm=1024):
    T, D = x.shape
    spec = pl.BlockSpec((tm, D), lambda i: (i, 0))
    rowspec = pl.BlockSpec((1, D), lambda i: (0, 0))
    return pl.pallas_call(
        _modulate_kernel,
        grid=(T // tm,),
        in_specs=[spec, rowspec, rowspec],
        out_specs=spec,
        out_shape=jax.ShapeDtypeStruct((T, D), BF16),
        compiler_params=_cp("parallel"),
        name="modulate",
    )(x, sc, sh)


def _wmm_kernel(u_ref, w_ref, o_ref, wb_ref):
    @pl.when(pl.program_id(1) == 0)
    def _():
        wb_ref[...] = w_ref[...].astype(BF16)

    o_ref[...] = _dot(u_ref[...], wb_ref[...]).astype(o_ref.dtype)


def _wmm(u, w, out_dtype, tn, tm=512, name="wmm"):
    T, D = u.shape
    N = w.shape[1]
    return pl.pallas_call(
        _wmm_kernel,
        grid=(N // tn, T // tm),
        in_specs=[pl.BlockSpec((tm, D), lambda j, i: (i, 0)),
                  pl.BlockSpec((D, tn), lambda j, i: (0, j))],
        out_specs=pl.BlockSpec((tm, tn), lambda j, i: (i, j)),
        out_shape=jax.ShapeDtypeStruct((T, N), out_dtype),
        scratch_shapes=[pltpu.VMEM((D, tn), BF16)],
        compiler_params=_cp("parallel", "arbitrary"),
        name=name,
    )(u, w)


def _sb_kernel(q_ref, k_ref, v_ref, o_ref, acc_ref, car_ref, z_ref, *, bq, scale):
    i = pl.program_id(1)
    q = q_ref[...]
    r_id = lax.broadcasted_iota(I32, (bq, bq), 0)
    c_id = lax.broadcasted_iota(I32, (bq, bq), 1)
    later = (r_id > c_id).astype(BF16)
    causal = c_id < r_id
    grp = [slice(g * (bq // SB_GROUPS), (g + 1) * (bq // SB_GROUPS)) for g in range(SB_GROUPS)]

    def scores(j):
        kb = k_ref[pl.ds(pl.multiple_of(jnp.maximum(j, 0) * bq, bq), bq), :]
        return [_dot(q[s], kb, NT) for s in grp]

    def block(j, zs, carry, diagonal):
        vb = v_ref[pl.ds(pl.multiple_of(j * bq, bq), bq), :]
        z = [t * scale for t in zs]
        drop = [jnp.maximum(t, 0.0) + jnp.log(1.0 + jnp.exp(-jnp.abs(t))) for t in z]
        if diagonal:
            drop = [jnp.where(causal[s], d, 0.0) for s, d in zip(grp, drop)]
        after = [_dot_fx(d, later, 2) + carry[s] for s, d in zip(grp, drop)]
        w = [jnp.exp(t - d - a) for t, d, a in zip(z, drop, after)]
        if diagonal:
            w = [jnp.where(causal[s], x, 0.0) for s, x in zip(grp, w)]
        pv = jnp.concatenate([_dot(x.astype(BF16), vb) for x in w], axis=0)
        new_carry = jnp.concatenate([a[:, 0:1] + d[:, 0:1] for a, d in zip(after, drop)], axis=0)
        return pv, new_carry

    def stash(zs):
        for s, t in zip(grp, zs):
            z_ref[s, :] = t

    z_next = scores(i - 1)
    pv, carry = block(i, scores(i), jnp.zeros((bq, 1), F32), True)
    stash(z_next)
    acc_ref[...] = pv
    car_ref[...] = carry

    def body(c):
        j, _ = c
        zs = [z_ref[s, :] for s in grp]
        z_next = scores(j - 1)
        pv, carry = block(j, zs, car_ref[...], False)
        stash(z_next)
        acc_ref[...] += pv
        car_ref[...] = carry
        return j - 1, (jnp.min(carry) > -SB_EXIT).astype(I32)

    lax.while_loop(lambda c: jnp.logical_and(c[0] >= 0, c[1] == 0), body, (i - 1, jnp.int32(0)))
    o_ref[...] = acc_ref[...].astype(o_ref.dtype)


def _sb_attention(qkv, bq=256):
    T = qkv.shape[0]
    H, Dh = SB_HEADS, SB_HEAD_DIM
    kern = functools.partial(_sb_kernel, bq=bq, scale=Dh ** -0.5)
    return pl.pallas_call(
        kern,
        grid=(H, T // bq),
        in_specs=[pl.BlockSpec((bq, Dh), lambda h, i: (i, h)),
                  pl.BlockSpec((T, Dh), lambda h, i: (0, H + h)),
                  pl.BlockSpec((T, Dh), lambda h, i: (0, 2 * H + h))],
        out_specs=pl.BlockSpec((bq, Dh), lambda h, i: (i, h)),
        out_shape=jax.ShapeDtypeStruct((T, H * Dh), BF16),
        scratch_shapes=[pltpu.VMEM((bq, Dh), F32), pltpu.VMEM((bq, 1), F32), pltpu.VMEM((bq, bq), F32)],
        compiler_params=_cp("parallel", "parallel"),
        name="sb_attention",
    )(qkv, qkv, qkv)


_GLA_LEVELS = (32, 16, 8, 4, 2, 1)


def _gla_constants():
    C = GLA_CHUNK
    i = np.arange(C)[:, None]
    s = np.arange(C)[None, :]
    sel = []
    masks = []
    for m in _GLA_LEVELS:
        same = (i // m) == (s // m)
        if m < SUBLANES:
            sel.append(same & (s <= i) & ((i // m) % 2 == 1))
            sel.append(same & (s > i) & ((i // m) % 2 == 0))
        masks.append(((i // (2 * m)) == (s // (2 * m))) & ((i // m) % 2 == 1) & ((s // m) % 2 == 0))
    sel.append(s <= i)
    sel.append(s > i)
    masks.append(i == s)
    return (np.concatenate(sel, 0).astype(np.float32), np.stack(masks, 0).astype(np.float32))


def _gla_kernel(q_ref, k_ref, v_ref, go_ref, glr_ref, up_ref, bias_ref, ng_ref, sel_ref, msk_ref,
                o_ref, st_ref, *, n_chunks):
    C = GLA_CHUNK
    nl = len(_GLA_LEVELS)

    @pl.when(pl.program_id(1) == 0)
    def _():
        st_ref[...] = jnp.zeros_like(st_ref)

    sel = sel_ref[...]
    up = up_ref[...]
    bias = bias_ref[...]
    ng = ng_ref[...]
    rows = [slice(c * C, (c + 1) * C) for c in range(n_chunks)]
    blk = lambda f, j: f[j * C:(j + 1) * C]
    q = [q_ref[s, :] * (GLA_HEAD_K ** -0.5) for s in rows]
    k = [k_ref[s, :] for s in rows]
    v = [v_ref[s, :].astype(BF16) for s in rows]
    x = [_dot(glr_ref[s, :].astype(BF16), up) + bias for s in rows]
    g = [-_softplus(-t) * (1.0 / GLA_GATE_NORMALIZER) for t in x]
    e = [_dot_xf(sel, t, 2) for t in g]
    n_fine = sum(1 for m in _GLA_LEVELS if m < SUBLANES)
    bcum = [blk(t, 2 * n_fine) for t in e]
    rid = lax.broadcasted_iota(I32, (C, 1), 0)
    fq, fk = [], []
    for l, m in enumerate(_GLA_LEVELS):
        if m < SUBLANES:
            j = l - (nl - n_fine)
            fq.append([jnp.exp(blk(t, 2 * j)) for t in e])
            fk.append([jnp.exp(blk(t, 2 * j + 1)) for t in e])
        else:
            odd = (rid // m) % 2 == 1
            end = [jnp.broadcast_to(b.reshape(C // m, m, -1)[:, m - 1:m, :], (C // m, m, b.shape[-1])).reshape(C, -1)
                   for b in bcum]
            fq.append([jnp.exp(jnp.where(odd, b - pltpu.roll(t, m, axis=0), 0.0)) for b, t in zip(bcum, end)])
            fk.append([jnp.exp(jnp.where(odd, 0.0, t - b)) for b, t in zip(bcum, end)])
    f_start = [jnp.exp(b) for b in bcum]
    f_end = [jnp.exp(blk(t, 2 * n_fine + 1)) for t in e]
    scores = [_dot(a.astype(BF16), b.astype(BF16), NT) * msk_ref[nl] for a, b in zip(q, k)]
    for l in range(nl):
        ql = [(a * t).astype(BF16) for a, t in zip(q, fq[l])]
        kl = [(b * t).astype(BF16) for b, t in zip(k, fk[l])]
        scores = [s + _dot(a, b, NT) * msk_ref[l] for s, a, b in zip(scores, ql, kl)]
    o_intra = [_dot(s.astype(BF16), b) for s, b in zip(scores, v)]
    qe = [(a * t).astype(BF16) for a, t in zip(q, f_start)]
    upd = [_dot(b, (a * t).astype(BF16), TN) for b, a, t in zip(v, k, f_end)]
    dec = [t[C - 1:C, :] for t in f_start]

    st = st_ref[...]
    for c in range(n_chunks):
        o = _dot(qe[c], st.astype(BF16), NT) + o_intra[c]
        st = st * dec[c] + upd[c]
        rms = lax.rsqrt(jnp.mean(o * o, axis=-1, keepdims=True) + GLA_NORM_EPS)
        o_ref[rows[c], :] = (o * rms * ng * _silu(go_ref[rows[c], :])).astype(o_ref.dtype)
    st_ref[...] = st


def _gla(p_gla, gk_up, gk_bias, norm_g, tt=512):
    T = p_gla.shape[0]
    H, dk, dv = GLA_HEADS, GLA_HEAD_K, GLA_HEAD_V
    sel, msk = _gla_constants()
    up = jnp.zeros((LANES, GLA_KEY_DIM), F32).at[:GLA_GATE_RANK].set(gk_up).astype(BF16)
    kern = functools.partial(_gla_kernel, n_chunks=tt // GLA_CHUNK)
    return pl.pallas_call(
        kern,
        grid=(H, T // tt),
        in_specs=[pl.BlockSpec((tt, dk), lambda h, i: (i, h)),
                  pl.BlockSpec((tt, dk), lambda h, i: (i, H + h)),
                  pl.BlockSpec((tt, dv), lambda h, i: (i, H + h)),
                  pl.BlockSpec((tt, dv), lambda h, i: (i, 2 * H + h)),
                  pl.BlockSpec((tt, LANES), lambda h, i: (i, 3 * GLA_VALUE_DIM // LANES)),
                  pl.BlockSpec((LANES, dk), lambda h, i: (0, h)),
                  pl.BlockSpec((1, dk), lambda h, i: (0, h)),
                  pl.BlockSpec((1, dv), lambda h, i: (0, 0)),
                  pl.BlockSpec(sel.shape, lambda h, i: (0, 0)),
                  pl.BlockSpec(msk.shape, lambda h, i: (0, 0, 0))],
        out_specs=pl.BlockSpec((tt, dv), lambda h, i: (i, h)),
        out_shape=jax.ShapeDtypeStruct((T, H * dv), BF16),
        scratch_shapes=[pltpu.VMEM((dv, dk), F32)],
        compiler_params=_cp("parallel", "arbitrary"),
        name="gla",
    )(p_gla, p_gla, p_gla, p_gla, p_gla, up, gk_bias.reshape(1, -1), norm_g.reshape(1, -1),
      jnp.asarray(sel, BF16), jnp.asarray(msk, F32))


def _head_ones():
    l = np.arange(LANES)
    return (l[:, None] // RWKV_HEAD_DIM == l[None, :] // RWKV_HEAD_DIM).astype(np.float32)


def _rwprep_kernel(p_ref, pp_ref, mu_ref, w0_ref, w2_ref, a0_ref, a2_ref, g2_ref, kk_ref, ka_ref,
                   v0_ref, v1_ref, v2_ref, vf_ref, bo_ref,
                   r_ref, w_ref, k_ref, v_ref, a_ref, b_ref, g_ref, *, first_layer):
    W = RWKV_WIDTH
    i = pl.program_id(0)
    p = p_ref[...]
    tm = p.shape[0]
    prev_row = jnp.where(i == 0, 0.0, pp_ref[SUBLANES - 1:SUBLANES, :])
    rid = lax.broadcasted_iota(I32, p.shape, 0)
    prev = jnp.where(rid == 0, prev_row, pltpu.roll(p, 1, axis=0))
    ps = p + (prev - p) * mu_ref[...]
    r = ps[:, 0:W]
    kr = ps[:, W:2 * W]
    vr = ps[:, 2 * W:3 * W]
    lr = ps[:, 3 * W:3 * W + LANES]
    gl = ps[:, 3 * W + LANES:3 * W + 2 * LANES]
    w_log = -_softplus(-(w0_ref[...] + _dot(jnp.tanh(lr).astype(BF16), w2_ref[...]))) - 0.5
    w_ref[...] = -jnp.exp(w_log)
    if not first_layer:
        lo = _dot(vr.astype(BF16), v1_ref[...])
        gate = _sigmoid(v0_ref[...] + _dot(lo.astype(BF16), v2_ref[...]))
        vr = vr + (vf_ref[...] - vr) * gate
    v_ref[...] = vr
    a = _sigmoid(a0_ref[...] + _dot(lr.astype(BF16), a2_ref[...]))
    g_ref[...] = _dot(_sigmoid(gl).astype(BF16), g2_ref[...])
    kk = kr * kk_ref[...]
    bo = bo_ref[...]
    for hp in range(W // LANES):
        sl = slice(hp * LANES, (hp + 1) * LANES)
        kh = kk[:, sl]
        ss = _dot_fx(kh * kh, bo, 3)
        kn = kh / jnp.maximum(jnp.sqrt(ss), 1e-12)
        a_ref[:, sl] = -kn
        b_ref[:, sl] = kn * a[:, sl]
    r_ref[...] = r
    k_ref[...] = kr * (1.0 + (a - 1.0) * ka_ref[...])


def _rwkv_prep(p_rw, v_first, mu, w0, w2, a0, a2, g2, k_k, k_a, v0, v1, v2, tm=256):
    T, NC = p_rw.shape
    W = RWKV_WIDTH
    first = v_first is None
    w2p = jnp.zeros((LANES, W), F32).at[:RWKV_DECAY_RANK].set(w2).astype(BF16)
    a2p = jnp.zeros((LANES, W), F32).at[RWKV_DECAY_RANK:].set(a2).astype(BF16)
    if first:
        v_first = jnp.zeros((SUBLANES, W), F32)
        v0 = jnp.zeros((W,), F32)
        v1p = jnp.zeros((W, LANES), BF16)
        v2p = jnp.zeros((LANES, W), BF16)
        vf_spec = pl.BlockSpec((SUBLANES, W), lambda i: (0, 0))
    else:
        v1p = jnp.zeros((W, LANES), F32).at[:, :RWKV_VALUE_RANK].set(v1).astype(BF16)
        v2p = jnp.zeros((LANES, W), F32).at[:RWKV_VALUE_RANK].set(v2).astype(BF16)
        vf_spec = pl.BlockSpec((tm, W), lambda i: (i, 0))
    row = lambda a: a.reshape(1, -1)
    full = lambda shape: pl.BlockSpec(shape, lambda i: (0,) * len(shape))
    out = jax.ShapeDtypeStruct((T, W), F32)
    ospec = pl.BlockSpec((tm, W), lambda i: (i, 0))
    kern = functools.partial(_rwprep_kernel, first_layer=first)
    return pl.pallas_call(
        kern,
        grid=(T // tm,),
        in_specs=[pl.BlockSpec((tm, NC), lambda i: (i, 0)),
                  pl.BlockSpec((SUBLANES, NC), lambda i: (jnp.maximum(i * (tm // SUBLANES) - 1, 0), 0)),
                  full((1, NC)), full((1, W)), full((LANES, W)), full((1, W)), full((LANES, W)),
                  full((LANES, W)), full((1, W)), full((1, W)),
                  full((1, W)), full((W, LANES)), full((LANES, W)), vf_spec, full((LANES, LANES))],
        out_specs=[ospec] * 7,
        out_shape=[out] * 7,
        compiler_params=_cp("parallel"),
        name="rwkv_prep",
    )(p_rw, p_rw, row(mu), row(w0), w2p, row(a0), a2p, g2.astype(BF16), row(k_k), row(k_a),
      row(v0), v1p, v2p, v_first, jnp.asarray(_head_ones(), BF16))


RW_CHUNK = 64


def _rw_constants():
    C = RW_CHUNK
    i = np.arange(C)[:, None]
    s = np.arange(C)[None, :]
    i2 = np.arange(2 * C)[:, None]
    s2 = np.arange(2 * C)[None, :]
    same = (i2 // C) == (s2 // C)
    cm = np.stack([same & (s2 < i2), same & (s2 <= i2), (i2 // 16) == (s2 // 16), (i2 // 32) == (s2 // 32),
                   same], 0).astype(np.float32)
    return (s <= i).astype(np.float32), cm


def _mm(a, b, dims=NN):
    a1, a2 = _split(a, 2)
    b1, b2 = _split(b, 2)
    r = _dot(a1, jnp.concatenate([b1, b2], axis=0 if dims == NT else 1), dims)
    n = r.shape[1] // 2
    return (r[:, :n] + r[:, n:]) + _dot(a2, b1, dims)


def _mm1(a, b, dims=NN):
    return _dot(a.astype(BF16), b.astype(BF16), dims)


def _unit_lower_inverse(nms, eye, low, bd16, bd32, bd64):
    size = eye.shape[0]

    def pack(m, s):
        acc = m[0:s]
        for b in range(1, size // s):
            acc = acc + m[s * b:s * (b + 1)]
        return acc

    def unpack(p, mask):
        return jnp.concatenate([p] * (size // p.shape[0]), axis=0) * mask

    nd = [n * bd16 for n in nms]
    x = [pack(eye + n, 16) for n in nd]
    p = [_mm(pack(n, 16), n) for n in nd]
    for step in range(3):
        pf = [unpack(q, bd16) for q in p]
        x = [a + _mm(a, q) for a, q in zip(x, pf)]
        if step < 2:
            p = [_mm(q, qf) for q, qf in zip(p, pf)]
    x = [unpack(a, bd16) for a in x]
    for s, inner, outer in ((16, bd16, bd32), (32, bd32, bd64)):
        place = (outer - inner) * low
        t = [unpack(_mm(pack(n * place, s), a), place) for n, a in zip(nms, x)]
        x = [a + unpack(_mm(pack(a, s), q), place) for a, q in zip(x, t)]
    return x


def _rwrec_kernel(r_ref, lw_ref, k_ref, v_ref, a_ref, b_ref, lt_ref, cm_ref, y_ref, mt_ref, *, n_chunks):
    C = RW_CHUNK
    C2 = 2 * C

    @pl.when(pl.program_id(1) == 0)
    def _():
        mt_ref[...] = jnp.zeros_like(mt_ref)

    lt = lt_ref[...]
    sl, li, bd16, bd32, bdh = cm_ref[0], cm_ref[1], cm_ref[2], cm_ref[3], cm_ref[4]
    eye = li - sl
    lane = lax.broadcasted_iota(I32, (1, LANES), 1)
    m0 = jnp.where(lane < RWKV_HEAD_DIM, 1.0, 0.0)
    m1 = 1.0 - m0
    stack = lambda x: jnp.concatenate([x * m0, x * m1], axis=0)
    fold = lambda x: x[:C] + x[C:]
    rows = [slice(c * C, (c + 1) * C) for c in range(n_chunks)]

    lw = [lw_ref[s, :] for s in rows]
    cw = [_dot_xf(lt, x, 3) for x in lw]
    en = [jnp.exp(-x) for x in cw]
    at = [a_ref[s, :] * jnp.exp(c - l) for s, c, l in zip(rows, cw, lw)]
    rt = [r_ref[s, :] * jnp.exp(c) for s, c in zip(rows, cw)]
    bh = [b_ref[s, :] * e for s, e in zip(rows, en)]
    kh = [k_ref[s, :] * e for s, e in zip(rows, en)]
    v = [v_ref[s, :] for s in rows]
    a2 = [stack(x) for x in at]
    ar = [jnp.concatenate([x, y], axis=0) for x, y in zip(at, rt)]
    gb = [_mm(x, stack(y), NT) for x, y in zip(ar, bh)]
    gk = [_mm(x, stack(y), NT) for x, y in zip(ar, kh)]
    slf = fold(sl)
    lif = fold(li)
    tinv = _unit_lower_inverse([stack(g[:C] * slf) for g in gb], eye, sl, bd16, bd32, bdh)
    akv = [_mm(jnp.concatenate([g[:C] * slf, g[C:] * lif], axis=0), stack(x))
           for g, x in zip(gk, v)]
    tav = [_mm(fold(t), jnp.concatenate([x, stack(y[:C])], axis=1)) for t, x, y in zip(tinv, a2, akv)]
    ta = [x[:, :LANES] for x in tav]
    u0 = [x[:, LANES:] for x in tav]
    rav = [_mm1(g[C:] * lif, jnp.concatenate([stack(x), stack(y)], axis=1))
           for g, x, y in zip(gb, ta, u0)]
    qe = [x + y[:, :LANES] for x, y in zip(rt, rav)]
    y0 = [x[:, LANES:] + y[C:] for x, y in zip(rav, akv)]
    dl = [jnp.exp(x[C - 1:C, :]) for x in cw]
    gmt = [(eye + _mm(x, y, TN) * bdh) * d for x, y, d in zip(ta, bh, dl)]
    hmt = [(_mm(x, y, TN) + _mm(z, w, TN)) * bdh * d for x, y, z, w, d in zip(u0, bh, v, kh, dl)]

    mt = mt_ref[...]
    for c in range(n_chunks):
        y_ref[rows[c], :] = _mm1(qe[c], mt, NT) + y0[c]
        mt = _mm(mt, gmt[c]) + hmt[c]
    mt_ref[...] = mt


def _rwkv_recurrence(r, lw, k, v, a, b, tt=512):
    T, W = r.shape
    lt, cm = _rw_constants()
    spec = pl.BlockSpec((tt, LANES), lambda p, i: (i, p))
    kern = functools.partial(_rwrec_kernel, n_chunks=tt // RW_CHUNK)
    return pl.pallas_call(
        kern,
        grid=(W // LANES, T // tt),
        in_specs=[spec] * 6 + [pl.BlockSpec(lt.shape, lambda p, i: (0, 0)),
                               pl.BlockSpec(cm.shape, lambda p, i: (0, 0, 0))],
        out_specs=spec,
        out_shape=jax.ShapeDtypeStruct((T, W), F32),
        scratch_shapes=[pltpu.VMEM((LANES, LANES), F32)],
        compiler_params=_cp("parallel", "arbitrary"),
        name="rwkv_recurrence",
    )(r, lw, k, v, a, b, jnp.asarray(lt, BF16), jnp.asarray(cm, F32))


def _rwpost_kernel(y_ref, r_ref, k_ref, v_ref, g_ref, lg_ref, lb_ref, rk_ref, bo_ref, o_ref):
    bo = bo_ref[...]
    inv = 1.0 / RWKV_HEAD_DIM
    for hp in range(RWKV_WIDTH // LANES):
        sl = slice(hp * LANES, (hp + 1) * LANES)
        y = y_ref[:, sl]
        mu = _dot_fx(y, bo, 3) * inv
        d = y - mu
        var = _dot_fx(d * d, bo, 3) * inv
        yn = d * lax.rsqrt(var + RWKV_LNX_EPS) * lg_ref[:, sl] + lb_ref[:, sl]
        bonus = _dot_fx(r_ref[:, sl] * k_ref[:, sl] * rk_ref[:, sl], bo, 3) * v_ref[:, sl]
        o_ref[:, sl] = ((yn + bonus) * g_ref[:, sl]).astype(o_ref.dtype)


def _rwkv_post(y, r, k, v, g, lnx_g, lnx_b, r_k, tm=512):
    T, W = y.shape
    spec = pl.BlockSpec((tm, W), lambda i: (i, 0))
    rowspec = pl.BlockSpec((1, W), lambda i: (0, 0))
    return pl.pallas_call(
        _rwpost_kernel,
        grid=(T // tm,),
        in_specs=[spec] * 5 + [rowspec] * 3 + [pl.BlockSpec((LANES, LANES), lambda i: (0, 0))],
        out_specs=spec,
        out_shape=jax.ShapeDtypeStruct((T, W), BF16),
        compiler_params=_cp("parallel"),
        name="rwkv_post",
    )(y, r, k, v, g, lnx_g.reshape(1, W), lnx_b.reshape(1, W), r_k.reshape(1, W),
      jnp.asarray(_head_ones(), BF16))


def _merge_kernel(osb_ref, ogl_ref, orw_ref, g1_ref, g2_ref, g3_ref, w1_ref, w2_ref, w3_ref, o_ref):
    acc = _sigmoid(g1_ref[...].astype(F32)) * _dot(osb_ref[...], w1_ref[...])
    acc = acc + _sigmoid(g2_ref[...].astype(F32)) * _dot(ogl_ref[...], w2_ref[...])
    acc = acc + _sigmoid(g3_ref[...].astype(F32)) * _dot(orw_ref[...], w3_ref[...])
    o_ref[...] = acc.astype(o_ref.dtype)


def _merge(o_sb, o_gla, o_rw, p_merge, w_sb, w_gla, w_rw, tm=512, tn=1024):
    T = o_sb.shape[0]
    D = D_MODEL
    nb = D // tn
    aspec = pl.BlockSpec((tm, 1024), lambda j, i: (i, 0))
    wspec = pl.BlockSpec((1024, tn), lambda j, i: (0, j))
    gspec = lambda b: pl.BlockSpec((tm, tn), lambda j, i: (i, b * nb + j))
    return pl.pallas_call(
        _merge_kernel,
        grid=(nb, T // tm),
        in_specs=[aspec, aspec, aspec, gspec(0), gspec(1), gspec(2), wspec, wspec, wspec],
        out_specs=pl.BlockSpec((tm, tn), lambda j, i: (i, j)),
        out_shape=jax.ShapeDtypeStruct((T, D), BF16),
        compiler_params=_cp("parallel", "parallel"),
        name="merge",
    )(o_sb, o_gla, o_rw, p_merge, p_merge, p_merge, w_sb, w_gla, w_rw)


def _layer_norm(h, g, b):
    mu = jnp.mean(h, axis=-1, keepdims=True)
    d = h - mu
    var = jnp.mean(d * d, axis=-1, keepdims=True)
    return d * lax.rsqrt(var + LN_EPS) * g + b


def _oproj_kernel(m_ref, w_ref, x_ref, gate_ref, lg_ref, lb_ref, o_ref):
    mix = _dot(m_ref[...], w_ref[...])
    h = DEEPNORM_ALPHA * x_ref[...] + (1.0 + gate_ref[...]) * mix
    o_ref[...] = _layer_norm(h, lg_ref[...], lb_ref[...])


def _oproj(merged, w_o, x, gate, ln_g, ln_b, tm=512):
    T, D = x.shape
    spec = pl.BlockSpec((tm, D), lambda i: (i, 0))
    rowspec = pl.BlockSpec((1, D), lambda i: (0, 0))
    return pl.pallas_call(
        _oproj_kernel,
        grid=(T // tm,),
        in_specs=[spec, pl.BlockSpec((D, D), lambda i: (0, 0)), spec, rowspec, rowspec, rowspec],
        out_specs=spec,
        out_shape=jax.ShapeDtypeStruct((T, D), F32),
        compiler_params=_cp("parallel"),
        name="oproj_norm",
    )(merged, w_o, x, gate, ln_g.reshape(1, D), ln_b.reshape(1, D))


def _first_max(x, idx, axis, big):
    m = jnp.max(x, axis=axis, keepdims=True)
    first = jnp.min(jnp.where(x == m, idx, big), axis=axis, keepdims=True)
    return m, idx == first


def _router_kernel(x_ref, sc_ref, sh_ref, wr_ref, rb_ref, u_ref, e_ref, g_ref, p_ref, cnt_ref, car_ref, *, tm):
    E, G = N_EXPERTS, N_GROUPS
    per = E // G
    neg = -jnp.inf

    @pl.when(pl.program_id(0) == 0)
    def _():
        car_ref[...] = jnp.zeros_like(car_ref)

    u = x_ref[...] * (1.0 + sc_ref[...]) + sh_ref[...]
    u_ref[...] = _pack_pairs(u)
    scores =_sigmoid(_dot_ff(wr_ref[...], u, NT))
    biased = scores + rb_ref[...]
    b3 = biased.reshape(G, per, tm)
    i3 = lax.broadcasted_iota(I32, (G, per, tm), 1)
    m1, hit = _first_max(b3, i3, 1, per)
    m2 = jnp.max(jnp.where(hit, neg, b3), axis=1, keepdims=True)
    gs = (m1 + m2).reshape(G, tm)
    gi = lax.broadcasted_iota(I32, (G, tm), 0)
    ok = jnp.zeros((G, tm), jnp.bool_)
    for _ in range(TOPK_GROUPS):
        _, hit = _first_max(gs, gi, 0, G)
        ok = jnp.logical_or(ok, hit)
        gs = jnp.where(hit, neg, gs)
    ok_e = jnp.broadcast_to(ok.reshape(G, 1, tm), (G, per, tm)).reshape(E, tm)
    cand = jnp.where(ok_e, biased, neg)
    ei = lax.broadcasted_iota(I32, (E, tm), 0)
    sels, gates = [], []
    for _ in range(TOP_K):
        _, hit = _first_max(cand, ei, 0, E)
        sels.append(hit)
        gates.append(jnp.sum(jnp.where(hit, scores, 0.0), axis=0, keepdims=True))
        cand = jnp.where(hit, neg, cand)
    denom = gates[0]
    for gk in gates[1:]:
        denom = denom + gk
    chosen = sels[0]
    for s in sels[1:]:
        chosen = jnp.logical_or(chosen, s)
    cnt = jnp.where(chosen, 1.0, 0.0)
    earlier = (lax.broadcasted_iota(I32, (tm, tm), 0) < lax.broadcasted_iota(I32, (tm, tm), 1)).astype(BF16)
    rank = _dot(cnt.astype(BF16), earlier) + car_ref[...][:, 0:1]
    car_ref[...] = car_ref[...] + jnp.sum(cnt, axis=1, keepdims=True)
    cnt_ref[...] = car_ref[...]
    zi = jnp.zeros((1, tm), I32)
    zf = jnp.zeros((1, tm), F32)
    e_rows = [jnp.sum(jnp.where(s, ei, 0), axis=0, keepdims=True) for s in sels]
    p_rows = [jnp.sum(jnp.where(s, rank, 0.0), axis=0, keepdims=True).astype(I32) for s in sels]
    g_rows = [gk / denom * ROUTED_SCALE for gk in gates]
    e_ref[...] = jnp.concatenate(e_rows + [zi, zi], axis=0)
    p_ref[...] = jnp.concatenate(p_rows + [zi, zi], axis=0)
    g_ref[...] = jnp.concatenate(g_rows + [zf, zf], axis=0)


def _router(x, sc, sh, w_router, router_bias, tm=256):
    T, D = x.shape
    E = N_EXPERTS
    kern = functools.partial(_router_kernel, tm=tm)
    rowspec = pl.BlockSpec((1, D), lambda i: (0, 0))
    kspec = pl.BlockSpec((SUBLANES, tm), lambda i: (0, i))
    return pl.pallas_call(
        kern,
        grid=(T // tm,),
        in_specs=[pl.BlockSpec((tm, D), lambda i: (i, 0)), rowspec, rowspec,
                  pl.BlockSpec((E, D), lambda i: (0, 0)), pl.BlockSpec((E, 1), lambda i: (0, 0))],
        out_specs=[pl.BlockSpec((tm, D // 2), lambda i: (i, 0)), kspec, kspec, kspec,
                   pl.BlockSpec((E, LANES), lambda i: (0, 0))],
        out_shape=[jax.ShapeDtypeStruct((T, D // 2), U32),
                   jax.ShapeDtypeStruct((SUBLANES, T), I32),
                   jax.ShapeDtypeStruct((SUBLANES, T), F32),
                   jax.ShapeDtypeStruct((SUBLANES, T), I32),
                   jax.ShapeDtypeStruct((E, LANES), F32)],
        scratch_shapes=[pltpu.VMEM((E, LANES), F32)],
        compiler_params=_cp("arbitrary"),
        name="router",
    )(x, sc, sh, w_router.T, router_bias.reshape(E, 1))


def _dispatch_kernel(dest_ref, fill_ref, u_ref, xs_ref, z_ref, sem, zsem, *, tm, bm):
    i = pl.program_id(0)

    @pl.when(i == 0)
    def _():
        z_ref[...] = jnp.zeros_like(z_ref)

        def zcopy(e):
            return pltpu.make_async_copy(z_ref, xs_ref.at[pl.ds(pl.multiple_of(fill_ref[e], bm), bm)], zsem)

        def zstart(e, c):
            zcopy(e).start()
            return c

        def zwait(e, c):
            zcopy(e).wait()
            return c

        lax.fori_loop(0, N_EXPERTS, zstart, 0)
        lax.fori_loop(0, N_EXPERTS, zwait, 0)

    for tok in range(tm):
        for k in range(TOP_K):
            pltpu.make_async_copy(_row(u_ref, tok), _row(xs_ref, dest_ref[k, tok]), sem).start()
    n = tm * TOP_K
    pltpu.make_async_copy(xs_ref.at[pl.ds(0, n)], xs_ref.at[pl.ds(0, n)], sem).wait()


def _dispatch(u, dest, fill, n_slots, tm=256, bm=MOE_BM):
    T, D = u.shape
    kern = functools.partial(_dispatch_kernel, tm=tm, bm=bm)
    return pl.pallas_call(
        kern,
        grid=(T // tm,),
        in_specs=[pl.BlockSpec((SUBLANES, tm), lambda i: (0, i), memory_space=pltpu.SMEM),
                  pl.BlockSpec(memory_space=pltpu.SMEM),
                  pl.BlockSpec((tm, D), lambda i: (i, 0))],
        out_specs=pl.BlockSpec(memory_space=pl.ANY),
        out_shape=jax.ShapeDtypeStruct((n_slots, D), u.dtype),
        scratch_shapes=[pltpu.VMEM((bm, D), u.dtype),
                        pltpu.SemaphoreType.DMA(()), pltpu.SemaphoreType.DMA(())],
        compiler_params=_cp("arbitrary"),
        name="moe_dispatch",
    )(dest, fill, u)


def _expert_kernel(be_ref, nv_ref, x_ref, wg_ref, wu_ref, wd_ref, y_ref, wgb_ref, wub_ref, wdb_ref):
    b = pl.program_id(0)
    valid = b < nv_ref[0]
    new_expert = jnp.logical_or(b == 0, be_ref[b] != be_ref[jnp.maximum(b - 1, 0)])

    @pl.when(jnp.logical_and(valid, new_expert))
    def _():
        wgb_ref[...] = wg_ref[0].astype(BF16)
        wub_ref[...] = wu_ref[0].astype(BF16)
        wdb_ref[...] = wd_ref[0].astype(BF16)

    @pl.when(valid)
    def _():
        x = _unpack_pairs(x_ref[...]).astype(BF16)
        h = _silu(_dot(x, wgb_ref[...])) * _dot(x, wub_ref[...])
        y_ref[...] = _pack_pairs(_dot(h.astype(BF16), wdb_ref[...]))


def _experts(xs, block_expert, n_valid, w_gate, w_up, w_down, layer, bm=MOE_BM):
    n_slots, W = xs.shape
    D = 2 * W
    nb = n_slots // bm
    F = w_gate.shape[-1]
    blk = lambda b, be, nv: (jnp.minimum(b, nv[0] - 1), 0)
    wsel = lambda b, be, nv: (layer, be[jnp.minimum(b, nv[0] - 1)], 0, 0)
    gs = pltpu.PrefetchScalarGridSpec(
        num_scalar_prefetch=2,
        grid=(nb,),
        in_specs=[pl.BlockSpec((bm, W), blk),
                  pl.BlockSpec((None, 1, D, F), wsel),
                  pl.BlockSpec((None, 1, D, F), wsel),
                  pl.BlockSpec((None, 1, F, D), wsel)],
        out_specs=pl.BlockSpec((bm, W), blk),
        scratch_shapes=[pltpu.VMEM((D, F), BF16), pltpu.VMEM((D, F), BF16), pltpu.VMEM((F, D), BF16)],
    )
    return pl.pallas_call(
        _expert_kernel,
        grid_spec=gs,
        out_shape=jax.ShapeDtypeStruct((n_slots, W), U32),
        compiler_params=_cp("arbitrary"),
        name="moe_experts",
    )(block_expert, n_valid, xs, w_gate, w_up, w_down)


def _combine_kernel(dest_ref, dnext_ref, ys_ref, u_ref, x_ref, gt_ref, wg_ref, wu_ref, wd_ref, gate_ref, lg_ref,
                    lb_ref, o_ref, buf_ref, sem, *, tm):
    i = pl.program_id(0)
    last = pl.num_programs(0) - 1
    slot = lax.rem(i, 2)

    def row_copy(d_ref, s, tok, k):
        return pltpu.make_async_copy(_row(ys_ref, d_ref[k, tok]), _row(buf_ref.at[s, k], tok), sem.at[s])

    def wait_half(s):
        pltpu.make_async_copy(buf_ref.at[s], buf_ref.at[s], sem.at[s]).wait()

    @pl.when(i == 0)
    def _():
        def start(tok, c):
            for k in range(TOP_K):
                row_copy(dest_ref, slot, tok, k).start()
            return c

        lax.fori_loop(0, tm, start, 0)

    wait_half(slot)
    for tok in range(tm):
        for k in range(TOP_K):
            row_copy(dnext_ref, 1 - slot, tok, k).start()

    u = _unpack_pairs(u_ref[...]).astype(BF16)
    h = _silu(_dot(u, wg_ref[...])) * _dot(u, wu_ref[...])
    ffn = _dot(h.astype(BF16), wd_ref[...])
    gt = gt_ref[...]
    for k in range(TOP_K):
        ffn = ffn + gt[:, k:k + 1] * _unpack_pairs(buf_ref[slot, k])
    hres = DEEPNORM_ALPHA * x_ref[...] + (1.0 + gate_ref[...]) * ffn
    o_ref[...] = _layer_norm(hres, lg_ref[...], lb_ref[...])

    @pl.when(i == last)
    def _():
        wait_half(1 - slot)


def _combine(ys, dest, u, x, gates_t, w_sg, w_su, w_sd, gate, ln_g, ln_b, tm=128):
    T, D = x.shape
    W = ys.shape[1]
    F = w_sg.shape[-1]
    n = T // tm
    kern = functools.partial(_combine_kernel, tm=tm)
    spec = pl.BlockSpec((tm, D), lambda i: (i, 0))
    rowspec = pl.BlockSpec((1, D), lambda i: (0, 0))
    return pl.pallas_call(
        kern,
        grid=(n,),
        in_specs=[pl.BlockSpec((SUBLANES, tm), lambda i: (0, i), memory_space=pltpu.SMEM),
                  pl.BlockSpec((SUBLANES, tm), lambda i: (0, jnp.minimum(i + 1, n - 1)), memory_space=pltpu.SMEM),
                  pl.BlockSpec(memory_space=pl.ANY),
                  pl.BlockSpec((tm, W), lambda i: (i, 0)), spec,
                  pl.BlockSpec((tm, SUBLANES), lambda i: (i, 0)),
                  pl.BlockSpec((D, F), lambda i: (0, 0)),
                  pl.BlockSpec((D, F), lambda i: (0, 0)),
                  pl.BlockSpec((F, D), lambda i: (0, 0)),
                  rowspec, rowspec, rowspec],
        out_specs=spec,
        out_shape=jax.ShapeDtypeStruct((T, D), F32),
        scratch_shapes=[pltpu.VMEM((2, TOP_K, tm, W), ys.dtype), pltpu.SemaphoreType.DMA((2,))],
        compiler_params=_cp("arbitrary"),
        name="moe_combine_norm",
    )(dest, dest, ys, u, x, gates_t, w_sg, w_su, w_sd, gate, ln_g.reshape(1, D), ln_b.reshape(1, D))


def _moe_plan(top_e, rank, counts, bm, n_blocks):
    counts = counts.astype(I32)
    padded = (counts + bm - 1) // bm * bm
    padded_end = jnp.cumsum(padded)
    padded_start = padded_end - padded
    onehot = top_e[:, :, None] == jnp.arange(N_EXPERTS, dtype=I32)
    dest = jnp.sum(jnp.where(onehot, padded_start, 0), axis=-1) + rank
    fill = jnp.maximum(padded_end - bm, padded_start)
    fill = jnp.minimum(fill, (n_blocks - 1) * bm)
    block_start = jnp.arange(n_blocks, dtype=I32) * bm
    block_expert = jnp.minimum(jnp.sum((padded_end[None, :] <= block_start[:, None]).astype(I32), axis=1),
                               N_EXPERTS - 1)
    n_valid = jnp.maximum(padded_end[-1:] // bm, 1).astype(I32)
    return dest.astype(I32), fill.astype(I32), block_expert, n_valid


def _in_proj_weights(w_in):
    o = SB_COLS
    gq = w_in[:, o:o + 2 * GLA_KEY_DIM + GLA_VALUE_DIM]
    o += 2 * GLA_KEY_DIM + GLA_VALUE_DIM
    glr = w_in[:, o:o + GLA_GATE_RANK]
    o += GLA_GATE_RANK
    gout = w_in[:, o:o + GLA_VALUE_DIM]
    o += GLA_VALUE_DIM
    pad = jnp.zeros((w_in.shape[0], 2 * LANES - GLA_GATE_RANK), w_in.dtype)
    w_gla = jnp.concatenate([gq, gout, glr, pad], axis=1)
    w_rw = w_in[:, o:o + RWKV_COLS]
    o += RWKV_COLS
    w_mg = w_in[:, o:o + MERGE_COLS]
    return w_in[:, :SB_COLS], w_gla, w_rw, w_mg


def kernel(x, c, w_ada, b_ada, w_in, gla_gk_up, gla_gk_bias, gla_norm_g, rw_mu, rw_w0, rw_w2, rw_a0,
           rw_a2, rw_g2, rw_k_k, rw_k_a, rw_r_k, rw_lnx_g, rw_lnx_b, rw_v0, rw_v1, rw_v2,
           w_br_sb, w_br_gla, w_br_rw, w_o, ln_g, ln_b, w_router, router_bias,
           w_exp_gate, w_exp_up, w_exp_down, w_sh_gate, w_sh_up, w_sh_down):
    B, T, D = x.shape
    assert B == 1 and D == D_MODEL
    n_blocks = -(-(T * TOP_K) // MOE_BM) + N_EXPERTS
    mod = _ada(c, w_ada, b_ada)
    xs = x.reshape(T, D)
    v_first = None
    for l in range(DEPTH):
        sh1, sc1, g1, sh2, sc2, g2 = (mod[l, j * D:(j + 1) * D].reshape(1, D) for j in range(6))
        w_sb, w_gla, w_rw, w_mg = _in_proj_weights(w_in[l])
        u1 = _modulate(xs, sc1, sh1)
        p_sb = _wmm(u1, w_sb, BF16, 1024, name="inproj_sb")
        p_gla = _wmm(u1, w_gla, F32, 13 * LANES, name="inproj_gla")
        p_rw = _wmm(u1, w_rw, F32, 13 * LANES, name="inproj_rwkv")
        p_mg = _wmm(u1, w_mg, BF16, 1024, name="inproj_merge")

        o_sb = _sb_attention(p_sb)
        o_gla = _gla(p_gla, gla_gk_up[l], gla_gk_bias[l], gla_norm_g[l])
        if l == 0:
            r, w, k, v, a, b, g = _rwkv_prep(p_rw, None, rw_mu[l], rw_w0[l], rw_w2[l], rw_a0[l], rw_a2[l],
                                             rw_g2[l], rw_k_k[l], rw_k_a[l], None, None, None)
            v_first = v
        else:
            r, w, k, v, a, b, g = _rwkv_prep(p_rw, v_first, rw_mu[l], rw_w0[l], rw_w2[l], rw_a0[l], rw_a2[l],
                                             rw_g2[l], rw_k_k[l], rw_k_a[l], rw_v0[l - 1], rw_v1[l - 1],
                                             rw_v2[l - 1])
        y = _rwkv_recurrence(r, w, k, v, a, b)
        o_rw = _rwkv_post(y, r, k, v, g, rw_lnx_g[l], rw_lnx_b[l], rw_r_k[l])

        merged = _merge(o_sb, o_gla, o_rw, p_mg, w_br_sb[l].astype(BF16), w_br_gla[l].astype(BF16),
                        w_br_rw[l].astype(BF16))
        xs = _oproj(merged, w_o[l].astype(BF16), xs, g1, ln_g[l, 0], ln_b[l, 0])

        u2, top_e, gates, rank, counts = _router(xs, sc2, sh2, w_router[l], router_bias[l])
        dest, fill, block_expert, n_valid = _moe_plan(top_e, rank, counts[:, 0], MOE_BM, n_blocks)
        x_sorted = _dispatch(u2, dest, fill, n_blocks * MOE_BM)
        y_sorted = _experts(x_sorted, block_expert, n_valid, w_exp_gate, w_exp_up, w_exp_down, l)
        xs = _combine(y_sorted, dest, u2, xs, gates.T, w_sh_gate[l].astype(BF16), w_sh_up[l].astype(BF16),
                      w_sh_down[l].astype(BF16), g2, ln_g[l, 1], ln_b[l, 1])
    return xs.reshape(B, T, D)
```

```python
import functools

import numpy as np
import jax
import jax.numpy as jnp
from jax import lax
from jax.experimental import pallas as pl
from jax.experimental.pallas import tpu as pltpu

F32 = jnp.float32
BF16 = jnp.bfloat16
I32 = jnp.int32
U32 = jnp.uint32

D_MODEL = 2048
DEPTH = 2
LN_EPS = 1e-5
DEEPNORM_ALPHA = (2 * DEPTH) ** 0.25

SB_HEADS = 8
SB_HEAD_DIM = 128
SB_WIDTH = SB_HEADS * SB_HEAD_DIM

GLA_HEADS = 4
GLA_HEAD_K = 128
GLA_HEAD_V = 256
GLA_KEY_DIM = GLA_HEADS * GLA_HEAD_K
GLA_VALUE_DIM = GLA_HEADS * GLA_HEAD_V
GLA_GATE_RANK = 16
GLA_GATE_NORMALIZER = 16.0
GLA_CHUNK = 64
GLA_NORM_EPS = 1e-5

RWKV_HEADS = 16
RWKV_HEAD_DIM = 64
RWKV_WIDTH = RWKV_HEADS * RWKV_HEAD_DIM
RWKV_DECAY_RANK = 64
RWKV_AAA_RANK = 64
RWKV_VALUE_RANK = 32
RWKV_GATE_RANK = 128
RWKV_LNX_EPS = 64e-5

N_EXPERTS = 64
TOP_K = 6
EXPERT_DIM = 512
SHARED_DIM = 512
N_GROUPS = 8
TOPK_GROUPS = 4
ROUTED_SCALE = 2.5

SB_COLS = 3 * SB_WIDTH
GLA_COLS = 2 * GLA_KEY_DIM + GLA_VALUE_DIM + GLA_GATE_RANK + GLA_VALUE_DIM
RWKV_COLS = 3 * RWKV_WIDTH + RWKV_DECAY_RANK + RWKV_AAA_RANK + RWKV_GATE_RANK
MERGE_COLS = 3 * D_MODEL

LANES = 128
SUBLANES = 8
VMEM_LIMIT = 56 * 1024 * 1024
MOE_BM = 512
SB_EXIT = -104.0
SB_GROUPS = 2
SB_PACK = 2

NN = (((1,), (0,)), ((), ()))
NT = (((1,), (1,)), ((), ()))
TN = (((0,), (0,)), ((), ()))


def _cp(*sem):
    return pltpu.CompilerParams(dimension_semantics=sem, vmem_limit_bytes=VMEM_LIMIT)


def _dot(a, b, dims=NN):
    return lax.dot_general(a, b, dims, preferred_element_type=F32)


def _split(a, n):
    out = []
    r = a
    for i in range(n):
        p = r.astype(BF16)
        out.append(p)
        if i + 1 < n:
            r = r - p.astype(F32)
    return out


def _dot_fx(a, e, n, dims=NN):
    acc = None
    for p in _split(a, n):
        t = _dot(p, e, dims)
        acc = t if acc is None else acc + t
    return acc


def _dot_xf(e, b, n, dims=NN):
    acc = None
    for p in _split(b, n):
        t = _dot(e, p, dims)
        acc = t if acc is None else acc + t
    return acc


def _dot_ff(a, b, dims=NN):
    a1, a2, a3 = _split(a, 3)
    b1, b2, b3 = _split(b, 3)
    acc = _dot(a1, b1, dims)
    for p, q in ((a1, b2), (a2, b1), (a2, b2), (a1, b3), (a3, b1)):
        acc = acc + _dot(p, q, dims)
    return acc


def _sigmoid(x):
    return 1.0 / (1.0 + jnp.exp(-x))


def _softplus(x):
    return jnp.maximum(x, 0.0) + jnp.log1p(jnp.exp(-jnp.abs(x)))


def _silu(x):
    return x * _sigmoid(x)


def _row(ref, t):
    return ref.at[pl.ds(t, 1)]


def _pack_pairs(x):
    w = x.shape[1] // 2
    bits = lax.bitcast_convert_type(x.astype(BF16).astype(F32), U32)
    return (bits[:, :w] >> 16) | bits[:, w:]


def _unpack_pairs(p):
    lo = lax.bitcast_convert_type(p << 16, F32)
    hi = lax.bitcast_convert_type(p & jnp.uint32(0xFFFF0000), F32)
    return jnp.concatenate([lo, hi], axis=1)


def _ada_kernel(c_ref, w_ref, b_ref, o_ref):
    s = _silu(c_ref[...])
    o_ref[0] = jnp.sum(s * w_ref[0], axis=0, keepdims=True) + b_ref[0]


def _ada(c, w_ada, b_ada):
    L, D, N = w_ada.shape
    tn = 512
    out = pl.pallas_call(
        _ada_kernel,
        grid=(L, N // tn),
        in_specs=[pl.BlockSpec((D, 1), lambda l, j: (0, 0)),
                  pl.BlockSpec((1, D, tn), lambda l, j: (l, 0, j)),
                  pl.BlockSpec((1, 1, tn), lambda l, j: (l, 0, j))],
        out_specs=pl.BlockSpec((1, 1, tn), lambda l, j: (l, 0, j)),
        out_shape=jax.ShapeDtypeStruct((L, 1, N), F32),
        compiler_params=_cp("parallel", "parallel"),
        name="ada",
    )(c.reshape(D, 1), w_ada, b_ada.reshape(L, 1, N))
    return out[:, 0, :]


def _modulate_kernel(x_ref, sc_ref, sh_ref, o_ref):
    o_ref[...] = (x_ref[...] * (1.0 + sc_ref[...]) + sh_ref[...]).astype(o_ref.dtype)


def _modulate(x, sc, sh, tm=1024):
    T, D = x.shape
    spec = pl.BlockSpec((tm, D), lambda i: (i, 0))
    rowspec = pl.BlockSpec((1, D), lambda i: (0, 0))
    return pl.pallas_call(
        _modulate_kernel,
        grid=(T // tm,),
        in_specs=[spec, rowspec, rowspec],
        out_specs=spec,
        out_shape=jax.ShapeDtypeStruct((T, D), BF16),
        compiler_params=_cp("parallel"),
        name="modulate",
    )(x, sc, sh)


def _wmm_kernel(u_ref, w_ref, o_ref, wb_ref):
    @pl.when(pl.program_id(1) == 0)
    def _():
        wb_ref[...] = w_ref[...].astype(BF16)

    o_ref[...] = _dot(u_ref[...], wb_ref[...]).astype(o_ref.dtype)


def _wmm(u, w, out_dtype, tn, tm=512, name="wmm"):
    T, D = u.shape
    N = w.shape[1]
    return pl.pallas_call(
        _wmm_kernel,
        grid=(N // tn, T // tm),
        in_specs=[pl.BlockSpec((tm, D), lambda j, i: (i, 0)),
                  pl.BlockSpec((D, tn), lambda j, i: (0, j))],
        out_specs=pl.BlockSpec((tm, tn), lambda j, i: (i, j)),
        out_shape=jax.ShapeDtypeStruct((T, N), out_dtype),
        scratch_shapes=[pltpu.VMEM((D, tn), BF16)],
        compiler_params=_cp("parallel", "arbitrary"),
        name=name,
    )(u, w)


def _sb_kernel(q_ref, k_ref, v_ref, o_ref, acc_ref, car_ref, z_ref, *, bq, scale):
    i = pl.program_id(1)
    Dh = SB_HEAD_DIM
    lanes = [slice(h * Dh, (h + 1) * Dh) for h in range(SB_PACK)]
    r_id = lax.broadcasted_iota(I32, (bq, bq), 0)
    c_id = lax.broadcasted_iota(I32, (bq, bq), 1)
    later = (r_id > c_id).astype(BF16)
    causal = c_id < r_id
    grp = [slice(g * (bq // SB_GROUPS), (g + 1) * (bq // SB_GROUPS)) for g in range(SB_GROUPS)]

    def rows(j):
        return pl.ds(pl.multiple_of(j * bq, bq), bq)

    def scores(units, j):
        return [_dot(q_ref[s, lanes[h]], k_ref[rows(jnp.maximum(j, 0)), lanes[h]], NT) for h, s in units]

    def block(units, j, zs, diagonal):
        z = [t * scale for t in zs]
        drop = [jnp.maximum(t, 0.0) + jnp.log(1.0 + jnp.exp(-jnp.abs(t))) for t in z]
        if diagonal:
            drop = [jnp.where(causal[s], d, 0.0) for (h, s), d in zip(units, drop)]
            after = [_dot_fx(d, later, 2) for d in drop]
        else:
            after = [_dot_fx(d, later, 2) + car_ref[h, s, :] for (h, s), d in zip(units, drop)]
        w = [jnp.exp(t - d - a) for t, d, a in zip(z, drop, after)]
        if diagonal:
            w = [jnp.where(causal[s], x, 0.0) for (h, s), x in zip(units, w)]
        pv = [_dot(x.astype(BF16), v_ref[rows(j), lanes[h]]) for (h, s), x in zip(units, w)]
        return pv, [a[:, 0:1] + d[:, 0:1] for a, d in zip(after, drop)]

    every = [(h, s) for h in range(SB_PACK) for s in grp]
    z_next = scores(every, i - 1)
    pv, carry = block(every, i, scores(every, i), True)
    for (h, s), t, p, c in zip(every, z_next, pv, carry):
        z_ref[h, s, :] = t
        acc_ref[s, lanes[h]] = p
        car_ref[h, s, :] = c

    for h in range(SB_PACK):
        units = [(h, s) for s in grp]

        def body(c, units=units):
            j, _ = c
            zs = [z_ref[u, s, :] for u, s in units]
            z_next = scores(units, j - 1)
            pv, carry = block(units, j, zs, False)
            for (u, s), t, p, cr in zip(units, z_next, pv, carry):
                z_ref[u, s, :] = t
                acc_ref[s, lanes[u]] += p
                car_ref[u, s, :] = cr
            return j - 1, (jnp.min(jnp.concatenate(carry, axis=0)) > -SB_EXIT).astype(I32)

        lax.while_loop(lambda c: jnp.logical_and(c[0] >= 0, c[1] == 0), body, (i - 1, jnp.int32(0)))
    o_ref[...] = acc_ref[...].astype(o_ref.dtype)


def _sb_attention(qkv, bq=256):
    T = qkv.shape[0]
    H, Dh = SB_HEADS, SB_HEAD_DIM
    kern = functools.partial(_sb_kernel, bq=bq, scale=Dh ** -0.5)
    P = SB_PACK
    G = H // P
    return pl.pallas_call(
        kern,
        grid=(G, T // bq),
        in_specs=[pl.BlockSpec((bq, P * Dh), lambda h, i: (i, h)),
                  pl.BlockSpec((T, P * Dh), lambda h, i: (0, G + h)),
                  pl.BlockSpec((T, P * Dh), lambda h, i: (0, 2 * G + h))],
        out_specs=pl.BlockSpec((bq, P * Dh), lambda h, i: (i, h)),
        out_shape=jax.ShapeDtypeStruct((T, H * Dh), BF16),
        scratch_shapes=[pltpu.VMEM((bq, P * Dh), F32), pltpu.VMEM((P, bq, 1), F32), pltpu.VMEM((P, bq, bq), F32)],
        compiler_params=_cp("parallel", "parallel"),
        name="sb_attention",
    )(qkv, qkv, qkv)


_GLA_LEVELS = (32, 16, 8, 4, 2, 1)


def _gla_constants():
    C = GLA_CHUNK
    i = np.arange(C)[:, None]
    s = np.arange(C)[None, :]
    sel = []
    masks = []
    for m in _GLA_LEVELS:
        same = (i // m) == (s // m)
        if m < SUBLANES:
            sel.append(same & (s <= i) & ((i // m) % 2 == 1))
            sel.append(same & (s > i) & ((i // m) % 2 == 0))
        masks.append(((i // (2 * m)) == (s // (2 * m))) & ((i // m) % 2 == 1) & ((s // m) % 2 == 0))
    sel.append(s <= i)
    sel.append(s > i)
    masks.append(i == s)
    return (np.concatenate(sel, 0).astype(np.float32), np.stack(masks, 0).astype(np.float32))


def _gla_kernel(q_ref, k_ref, v_ref, go_ref, glr_ref, up_ref, bias_ref, ng_ref, sel_ref, msk_ref,
                o_ref, st_ref, *, n_chunks):
    C = GLA_CHUNK
    nl = len(_GLA_LEVELS)

    @pl.when(pl.program_id(1) == 0)
    def _():
        st_ref[...] = jnp.zeros_like(st_ref)

    sel = sel_ref[...]
    up = up_ref[...]
    bias = bias_ref[...]
    ng = ng_ref[...]
    rows = [slice(c * C, (c + 1) * C) for c in range(n_chunks)]
    blk = lambda f, j: f[j * C:(j + 1) * C]
    q = [q_ref[s, :] * (GLA_HEAD_K ** -0.5) for s in rows]
    k = [k_ref[s, :] for s in rows]
    v = [v_ref[s, :].astype(BF16) for s in rows]
    x = [_dot(glr_ref[s, :].astype(BF16), up) + bias for s in rows]
    g = [-_softplus(-t) * (1.0 / GLA_GATE_NORMALIZER) for t in x]
    e = [_dot_xf(sel, t, 2) for t in g]
    n_fine = sum(1 for m in _GLA_LEVELS if m < SUBLANES)
    bcum = [blk(t, 2 * n_fine) for t in e]
    rid = lax.broadcasted_iota(I32, (C, 1), 0)
    fq, fk = [], []
    for l, m in enumerate(_GLA_LEVELS):
        if m < SUBLANES:
            j = l - (nl - n_fine)
            fq.append([jnp.exp(blk(t, 2 * j)) for t in e])
            fk.append([jnp.exp(blk(t, 2 * j + 1)) for t in e])
        else:
            odd = (rid // m) % 2 == 1
            end = [jnp.broadcast_to(b.reshape(C // m, m, -1)[:, m - 1:m, :], (C // m, m, b.shape[-1])).reshape(C, -1)
                   for b in bcum]
            fq.append([jnp.exp(jnp.where(odd, b - pltpu.roll(t, m, axis=0), 0.0)) for b, t in zip(bcum, end)])
            fk.append([jnp.exp(jnp.where(odd, 0.0, t - b)) for b, t in zip(bcum, end)])
    f_start = [jnp.exp(b) for b in bcum]
    f_end = [jnp.exp(blk(t, 2 * n_fine + 1)) for t in e]
    scores = [_dot(a.astype(BF16), b.astype(BF16), NT) * msk_ref[nl] for a, b in zip(q, k)]
    for l in range(nl):
        ql = [(a * t).astype(BF16) for a, t in zip(q, fq[l])]
        kl = [(b * t).astype(BF16) for b, t in zip(k, fk[l])]
        scores = [s + _dot(a, b, NT) * msk_ref[l] for s, a, b in zip(scores, ql, kl)]
    o_intra = [_dot(s.astype(BF16), b) for s, b in zip(scores, v)]
    qe = [(a * t).astype(BF16) for a, t in zip(q, f_start)]
    upd = [_dot(b, (a * t).astype(BF16), TN) for b, a, t in zip(v, k, f_end)]
    dec = [t[C - 1:C, :] for t in f_start]

    st = st_ref[...]
    for c in range(n_chunks):
        o = _dot(qe[c], st.astype(BF16), NT) + o_intra[c]
        st = st * dec[c] + upd[c]
        rms = lax.rsqrt(jnp.mean(o * o, axis=-1, keepdims=True) + GLA_NORM_EPS)
        o_ref[rows[c], :] = (o * rms * ng * _silu(go_ref[rows[c], :])).astype(o_ref.dtype)
    st_ref[...] = st


def _gla(p_gla, gk_up, gk_bias, norm_g, tt=512):
    T = p_gla.shape[0]
    H, dk, dv = GLA_HEADS, GLA_HEAD_K, GLA_HEAD_V
    sel, msk = _gla_constants()
    up = jnp.zeros((LANES, GLA_KEY_DIM), F32).at[:GLA_GATE_RANK].set(gk_up).astype(BF16)
    kern = functools.partial(_gla_kernel, n_chunks=tt // GLA_CHUNK)
    return pl.pallas_call(
        kern,
        grid=(H, T // tt),
        in_specs=[pl.BlockSpec((tt, dk), lambda h, i: (i, h)),
                  pl.BlockSpec((tt, dk), lambda h, i: (i, H + h)),
                  pl.BlockSpec((tt, dv), lambda h, i: (i, H + h)),
                  pl.BlockSpec((tt, dv), lambda h, i: (i, 2 * H + h)),
                  pl.BlockSpec((tt, LANES), lambda h, i: (i, 3 * GLA_VALUE_DIM // LANES)),
                  pl.BlockSpec((LANES, dk), lambda h, i: (0, h)),
                  pl.BlockSpec((1, dk), lambda h, i: (0, h)),
                  pl.BlockSpec((1, dv), lambda h, i: (0, 0)),
                  pl.BlockSpec(sel.shape, lambda h, i: (0, 0)),
                  pl.BlockSpec(msk.shape, lambda h, i: (0, 0, 0))],
        out_specs=pl.BlockSpec((tt, dv), lambda h, i: (i, h)),
        out_shape=jax.ShapeDtypeStruct((T, H * dv), BF16),
        scratch_shapes=[pltpu.VMEM((dv, dk), F32)],
        compiler_params=_cp("parallel", "arbitrary"),
        name="gla",
    )(p_gla, p_gla, p_gla, p_gla, p_gla, up, gk_bias.reshape(1, -1), norm_g.reshape(1, -1),
      jnp.asarray(sel, BF16), jnp.asarray(msk, F32))


def _head_ones():
    l = np.arange(LANES)
    return (l[:, None] // RWKV_HEAD_DIM == l[None, :] // RWKV_HEAD_DIM).astype(np.float32)


def _rwprep_kernel(p_ref, pp_ref, mu_ref, w0_ref, w2_ref, a0_ref, a2_ref, g2_ref, kk_ref, ka_ref,
                   v0_ref, v1_ref, v2_ref, vf_ref, bo_ref,
                   r_ref, w_ref, k_ref, v_ref, a_ref, b_ref, g_ref, *, first_layer):
    W = RWKV_WIDTH
    i = pl.program_id(0)
    p = p_ref[...]
    tm = p.shape[0]
    prev_row = jnp.where(i == 0, 0.0, pp_ref[SUBLANES - 1:SUBLANES, :])
    rid = lax.broadcasted_iota(I32, p.shape, 0)
    prev = jnp.where(rid == 0, prev_row, pltpu.roll(p, 1, axis=0))
    ps = p + (prev - p) * mu_ref[...]
    r = ps[:, 0:W]
    kr = ps[:, W:2 * W]
    vr = ps[:, 2 * W:3 * W]
    lr = ps[:, 3 * W:3 * W + LANES]
    gl = ps[:, 3 * W + LANES:3 * W + 2 * LANES]
    w_log = -_softplus(-(w0_ref[...] + _dot(jnp.tanh(lr).astype(BF16), w2_ref[...]))) - 0.5
    w_ref[...] = -jnp.exp(w_log)
    if not first_layer:
        lo = _dot(vr.astype(BF16), v1_ref[...])
        gate = _sigmoid(v0_ref[...] + _dot(lo.astype(BF16), v2_ref[...]))
        vr = vr + (vf_ref[...] - vr) * gate
    v_ref[...] = vr
    a = _sigmoid(a0_ref[...] + _dot(lr.astype(BF16), a2_ref[...]))
    g_ref[...] = _dot(_sigmoid(gl).astype(BF16), g2_ref[...])
    kk = kr * kk_ref[...]
    bo = bo_ref[...]
    for hp in range(W // LANES):
        sl = slice(hp * LANES, (hp + 1) * LANES)
        kh = kk[:, sl]
        ss = _dot_fx(kh * kh, bo, 3)
        kn = kh / jnp.maximum(jnp.sqrt(ss), 1e-12)
        a_ref[:, sl] = -kn
        b_ref[:, sl] = kn * a[:, sl]
    r_ref[...] = r
    k_ref[...] = kr * (1.0 + (a - 1.0) * ka_ref[...])


def _rwkv_prep(p_rw, v_first, mu, w0, w2, a0, a2, g2, k_k, k_a, v0, v1, v2, tm=256):
    T, NC = p_rw.shape
    W = RWKV_WIDTH
    first = v_first is None
    w2p = jnp.zeros((LANES, W), F32).at[:RWKV_DECAY_RANK].set(w2).astype(BF16)
    a2p = jnp.zeros((LANES, W), F32).at[RWKV_DECAY_RANK:].set(a2).astype(BF16)
    if first:
        v_first = jnp.zeros((SUBLANES, W), F32)
        v0 = jnp.zeros((W,), F32)
        v1p = jnp.zeros((W, LANES), BF16)
        v2p = jnp.zeros((LANES, W), BF16)
        vf_spec = pl.BlockSpec((SUBLANES, W), lambda i: (0, 0))
    else:
        v1p = jnp.zeros((W, LANES), F32).at[:, :RWKV_VALUE_RANK].set(v1).astype(BF16)
        v2p = jnp.zeros((LANES, W), F32).at[:RWKV_VALUE_RANK].set(v2).astype(BF16)
        vf_spec = pl.BlockSpec((tm, W), lambda i: (i, 0))
    row = lambda a: a.reshape(1, -1)
    full = lambda shape: pl.BlockSpec(shape, lambda i: (0,) * len(shape))
    out = jax.ShapeDtypeStruct((T, W), F32)
    ospec = pl.BlockSpec((tm, W), lambda i: (i, 0))
    kern = functools.partial(_rwprep_kernel, first_layer=first)
    return pl.pallas_call(
        kern,
        grid=(T // tm,),
        in_specs=[pl.BlockSpec((tm, NC), lambda i: (i, 0)),
                  pl.BlockSpec((SUBLANES, NC), lambda i: (jnp.maximum(i * (tm // SUBLANES) - 1, 0), 0)),
                  full((1, NC)), full((1, W)), full((LANES, W)), full((1, W)), full((LANES, W)),
                  full((LANES, W)), full((1, W)), full((1, W)),
                  full((1, W)), full((W, LANES)), full((LANES, W)), vf_spec, full((LANES, LANES))],
        out_specs=[ospec] * 7,
        out_shape=[out] * 7,
        compiler_params=_cp("parallel"),
        name="rwkv_prep",
    )(p_rw, p_rw, row(mu), row(w0), w2p, row(a0), a2p, g2.astype(BF16), row(k_k), row(k_a),
      row(v0), v1p, v2p, v_first, jnp.asarray(_head_ones(), BF16))


RW_CHUNK = 64


def _rw_constants():
    C = RW_CHUNK
    i = np.arange(C)[:, None]
    s = np.arange(C)[None, :]
    i2 = np.arange(2 * C)[:, None]
    s2 = np.arange(2 * C)[None, :]
    same = (i2 // C) == (s2 // C)
    cm = np.stack([same & (s2 < i2), same & (s2 <= i2), (i2 // 16) == (s2 // 16), (i2 // 32) == (s2 // 32),
                   same], 0).astype(np.float32)
    return (s <= i).astype(np.float32), cm


def _mm(a, b, dims=NN):
    a1, a2 = _split(a, 2)
    b1, b2 = _split(b, 2)
    r = _dot(a1, jnp.concatenate([b1, b2], axis=0 if dims == NT else 1), dims)
    n = r.shape[1] // 2
    return (r[:, :n] + r[:, n:]) + _dot(a2, b1, dims)


def _mm1(a, b, dims=NN):
    return _dot(a.astype(BF16), b.astype(BF16), dims)


def _unit_lower_inverse(nms, eye, low, bd16, bd32, bd64):
    size = eye.shape[0]

    def pack(m, s):
        acc = m[0:s]
        for b in range(1, size // s):
            acc = acc + m[s * b:s * (b + 1)]
        return acc

    def unpack(p, mask):
        return jnp.concatenate([p] * (size // p.shape[0]), axis=0) * mask

    nd = [n * bd16 for n in nms]
    x = [pack(eye + n, 16) for n in nd]
    p = [_mm(pack(n, 16), n) for n in nd]
    for step in range(3):
        pf = [unpack(q, bd16) for q in p]
        x = [a + _mm(a, q) for a, q in zip(x, pf)]
        if step < 2:
            p = [_mm(q, qf) for q, qf in zip(p, pf)]
    x = [unpack(a, bd16) for a in x]
    for s, inner, outer in ((16, bd16, bd32), (32, bd32, bd64)):
        place = (outer - inner) * low
        t = [unpack(_mm(pack(n * place, s), a), place) for n, a in zip(nms, x)]
        x = [a + unpack(_mm(pack(a, s), q), place) for a, q in zip(x, t)]
    return x


def _rwrec_kernel(r_ref, lw_ref, k_ref, v_ref, a_ref, b_ref, lt_ref, cm_ref, y_ref, mt_ref, *, n_chunks):
    C = RW_CHUNK
    C2 = 2 * C

    @pl.when(pl.program_id(1) == 0)
    def _():
        mt_ref[...] = jnp.zeros_like(mt_ref)

    lt = lt_ref[...]
    sl, li, bd16, bd32, bdh = cm_ref[0], cm_ref[1], cm_ref[2], cm_ref[3], cm_ref[4]
    eye = li - sl
    lane = lax.broadcasted_iota(I32, (1, LANES), 1)
    m0 = jnp.where(lane < RWKV_HEAD_DIM, 1.0, 0.0)
    m1 = 1.0 - m0
    stack = lambda x: jnp.concatenate([x * m0, x * m1], axis=0)
    fold = lambda x: x[:C] + x[C:]
    rows = [slice(c * C, (c + 1) * C) for c in range(n_chunks)]

    lw = [lw_ref[s, :] for s in rows]
    cw = [_dot_xf(lt, x, 3) for x in lw]
    en = [jnp.exp(-x) for x in cw]
    at = [a_ref[s, :] * jnp.exp(c - l) for s, c, l in zip(rows, cw, lw)]
    rt = [r_ref[s, :] * jnp.exp(c) for s, c in zip(rows, cw)]
    bh = [b_ref[s, :] * e for s, e in zip(rows, en)]
    kh = [k_ref[s, :] * e for s, e in zip(rows, en)]
    v = [v_ref[s, :] for s in rows]
    a2 = [stack(x) for x in at]
    ar = [jnp.concatenate([x, y], axis=0) for x, y in zip(at, rt)]
    gb = [_mm(x, stack(y), NT) for x, y in zip(ar, bh)]
    gk = [_mm(x, stack(y), NT) for x, y in zip(ar, kh)]
    slf = fold(sl)
    lif = fold(li)
    tinv = _unit_lower_inverse([stack(g[:C] * slf) for g in gb], eye, sl, bd16, bd32, bdh)
    akv = [_mm(jnp.concatenate([g[:C] * slf, g[C:] * lif], axis=0), stack(x))
           for g, x in zip(gk, v)]
    tav = [_mm(fold(t), jnp.concatenate([x, stack(y[:C])], axis=1)) for t, x, y in zip(tinv, a2, akv)]
    ta = [x[:, :LANES] for x in tav]
    u0 = [x[:, LANES:] for x in tav]
    rav = [_mm1(g[C:] * lif, jnp.concatenate([stack(x), stack(y)], axis=1))
           for g, x, y in zip(gb, ta, u0)]
    qe = [x + y[:, :LANES] for x, y in zip(rt, rav)]
    y0 = [x[:, LANES:] + y[C:] for x, y in zip(rav, akv)]
    dl = [jnp.exp(x[C - 1:C, :]) for x in cw]
    gmt = [(eye + _mm(x, y, TN) * bdh) * d for x, y, d in zip(ta, bh, dl)]
    hmt = [(_mm(x, y, TN) + _mm(z, w, TN)) * bdh * d for x, y, z, w, d in zip(u0, bh, v, kh, dl)]

    mt = mt_ref[...]
    for c in range(n_chunks):
        y_ref[rows[c], :] = _mm1(qe[c], mt, NT) + y0[c]
        mt = _mm(mt, gmt[c]) + hmt[c]
    mt_ref[...] = mt


def _rwkv_recurrence(r, lw, k, v, a, b, tt=512):
    T, W = r.shape
    lt, cm = _rw_constants()
    spec = pl.BlockSpec((tt, LANES), lambda p, i: (i, p))
    kern = functools.partial(_rwrec_kernel, n_chunks=tt // RW_CHUNK)
    return pl.pallas_call(
        kern,
        grid=(W // LANES, T // tt),
        in_specs=[spec] * 6 + [pl.BlockSpec(lt.shape, lambda p, i: (0, 0)),
                               pl.BlockSpec(cm.shape, lambda p, i: (0, 0, 0))],
        out_specs=spec,
        out_shape=jax.ShapeDtypeStruct((T, W), F32),
        scratch_shapes=[pltpu.VMEM((LANES, LANES), F32)],
        compiler_params=_cp("parallel", "arbitrary"),
        name="rwkv_recurrence",
    )(r, lw, k, v, a, b, jnp.asarray(lt, BF16), jnp.asarray(cm, F32))


def _rwpost_kernel(y_ref, r_ref, k_ref, v_ref, g_ref, lg_ref, lb_ref, rk_ref, bo_ref, o_ref):
    bo = bo_ref[...]
    inv = 1.0 / RWKV_HEAD_DIM
    for hp in range(RWKV_WIDTH // LANES):
        sl = slice(hp * LANES, (hp + 1) * LANES)
        y = y_ref[:, sl]
        mu = _dot_fx(y, bo, 3) * inv
        d = y - mu
        var = _dot_fx(d * d, bo, 3) * inv
        yn = d * lax.rsqrt(var + RWKV_LNX_EPS) * lg_ref[:, sl] + lb_ref[:, sl]
        bonus = _dot_fx(r_ref[:, sl] * k_ref[:, sl] * rk_ref[:, sl], bo, 3) * v_ref[:, sl]
        o_ref[:, sl] = ((yn + bonus) * g_ref[:, sl]).astype(o_ref.dtype)


def _rwkv_post(y, r, k, v, g, lnx_g, lnx_b, r_k, tm=512):
    T, W = y.shape
    spec = pl.BlockSpec((tm, W), lambda i: (i, 0))
    rowspec = pl.BlockSpec((1, W), lambda i: (0, 0))
    return pl.pallas_call(
        _rwpost_kernel,
        grid=(T // tm,),
        in_specs=[spec] * 5 + [rowspec] * 3 + [pl.BlockSpec((LANES, LANES), lambda i: (0, 0))],
        out_specs=spec,
        out_shape=jax.ShapeDtypeStruct((T, W), BF16),
        compiler_params=_cp("parallel"),
        name="rwkv_post",
    )(y, r, k, v, g, lnx_g.reshape(1, W), lnx_b.reshape(1, W), r_k.reshape(1, W),
      jnp.asarray(_head_ones(), BF16))


def _merge_kernel(osb_ref, ogl_ref, orw_ref, g1_ref, g2_ref, g3_ref, w1_ref, w2_ref, w3_ref, o_ref):
    acc = _sigmoid(g1_ref[...].astype(F32)) * _dot(osb_ref[...], w1_ref[...])
    acc = acc + _sigmoid(g2_ref[...].astype(F32)) * _dot(ogl_ref[...], w2_ref[...])
    acc = acc + _sigmoid(g3_ref[...].astype(F32)) * _dot(orw_ref[...], w3_ref[...])
    o_ref[...] = acc.astype(o_ref.dtype)


def _merge(o_sb, o_gla, o_rw, p_merge, w_sb, w_gla, w_rw, tm=512, tn=1024):
    T = o_sb.shape[0]
    D = D_MODEL
    nb = D // tn
    aspec = pl.BlockSpec((tm, 1024), lambda j, i: (i, 0))
    wspec = pl.BlockSpec((1024, tn), lambda j, i: (0, j))
    gspec = lambda b: pl.BlockSpec((tm, tn), lambda j, i: (i, b * nb + j))
    return pl.pallas_call(
        _merge_kernel,
        grid=(nb, T // tm),
        in_specs=[aspec, aspec, aspec, gspec(0), gspec(1), gspec(2), wspec, wspec, wspec],
        out_specs=pl.BlockSpec((tm, tn), lambda j, i: (i, j)),
        out_shape=jax.ShapeDtypeStruct((T, D), BF16),
        compiler_params=_cp("parallel", "parallel"),
        name="merge",
    )(o_sb, o_gla, o_rw, p_merge, p_merge, p_merge, w_sb, w_gla, w_rw)


def _layer_norm(h, g, b):
    mu = jnp.mean(h, axis=-1, keepdims=True)
    d = h - mu
    var = jnp.mean(d * d, axis=-1, keepdims=True)
    return d * lax.rsqrt(var + LN_EPS) * g + b


def _oproj_kernel(m_ref, w_ref, x_ref, gate_ref, lg_ref, lb_ref, o_ref):
    mix = _dot(m_ref[...], w_ref[...])
    h = DEEPNORM_ALPHA * x_ref[...] + (1.0 + gate_ref[...]) * mix
    o_ref[...] = _layer_norm(h, lg_ref[...], lb_ref[...])


def _oproj(merged, w_o, x, gate, ln_g, ln_b, tm=512):
    T, D = x.shape
    spec = pl.BlockSpec((tm, D), lambda i: (i, 0))
    rowspec = pl.BlockSpec((1, D), lambda i: (0, 0))
    return pl.pallas_call(
        _oproj_kernel,
        grid=(T // tm,),
        in_specs=[spec, pl.BlockSpec((D, D), lambda i: (0, 0)), spec, rowspec, rowspec, rowspec],
        out_specs=spec,
        out_shape=jax.ShapeDtypeStruct((T, D), F32),
        compiler_params=_cp("parallel"),
        name="oproj_norm",
    )(merged, w_o, x, gate, ln_g.reshape(1, D), ln_b.reshape(1, D))


def _first_max(x, idx, axis, big):
    m = jnp.max(x, axis=axis, keepdims=True)
    first = jnp.min(jnp.where(x == m, idx, big), axis=axis, keepdims=True)
    return m, idx == first


def _router_kernel(x_ref, sc_ref, sh_ref, wr_ref, rb_ref, u_ref, e_ref, g_ref, p_ref, cnt_ref, car_ref, *, tm):
    E, G = N_EXPERTS, N_GROUPS
    per = E // G
    neg = -jnp.inf

    @pl.when(pl.program_id(0) == 0)
    def _():
        car_ref[...] = jnp.zeros_like(car_ref)

    u = x_ref[...] * (1.0 + sc_ref[...]) + sh_ref[...]
    u_ref[...] = _pack_pairs(u)
    scores =_sigmoid(_dot_ff(wr_ref[...], u, NT))
    biased = scores + rb_ref[...]
    b3 = biased.reshape(G, per, tm)
    i3 = lax.broadcasted_iota(I32, (G, per, tm), 1)
    m1, hit = _first_max(b3, i3, 1, per)
    m2 = jnp.max(jnp.where(hit, neg, b3), axis=1, keepdims=True)
    gs = (m1 + m2).reshape(G, tm)
    gi = lax.broadcasted_iota(I32, (G, tm), 0)
    ok = jnp.zeros((G, tm), jnp.bool_)
    for _ in range(TOPK_GROUPS):
        _, hit = _first_max(gs, gi, 0, G)
        ok = jnp.logical_or(ok, hit)
        gs = jnp.where(hit, neg, gs)
    ok_e = jnp.broadcast_to(ok.reshape(G, 1, tm), (G, per, tm)).reshape(E, tm)
    cand = jnp.where(ok_e, biased, neg)
    ei = lax.broadcasted_iota(I32, (E, tm), 0)
    sels, gates = [], []
    for _ in range(TOP_K):
        _, hit = _first_max(cand, ei, 0, E)
        sels.append(hit)
        gates.append(jnp.sum(jnp.where(hit, scores, 0.0), axis=0, keepdims=True))
        cand = jnp.where(hit, neg, cand)
    denom = gates[0]
    for gk in gates[1:]:
        denom = denom + gk
    chosen = sels[0]
    for s in sels[1:]:
        chosen = jnp.logical_or(chosen, s)
    cnt = jnp.where(chosen, 1.0, 0.0)
    earlier = (lax.broadcasted_iota(I32, (tm, tm), 0) < lax.broadcasted_iota(I32, (tm, tm), 1)).astype(BF16)
    rank = _dot(cnt.astype(BF16), earlier) + car_ref[...][:, 0:1]
    car_ref[...] = car_ref[...] + jnp.sum(cnt, axis=1, keepdims=True)
    cnt_ref[...] = car_ref[...]
    zi = jnp.zeros((1, tm), I32)
    zf = jnp.zeros((1, tm), F32)
    e_rows = [jnp.sum(jnp.where(s, ei, 0), axis=0, keepdims=True) for s in sels]
    p_rows = [jnp.sum(jnp.where(s, rank, 0.0), axis=0, keepdims=True).astype(I32) for s in sels]
    g_rows = [gk / denom * ROUTED_SCALE for gk in gates]
    e_ref[...] = jnp.concatenate(e_rows + [zi, zi], axis=0)
    p_ref[...] = jnp.concatenate(p_rows + [zi, zi], axis=0)
    g_ref[...] = jnp.concatenate(g_rows + [zf, zf], axis=0)


def _router(x, sc, sh, w_router, router_bias, tm=256):
    T, D = x.shape
    E = N_EXPERTS
    kern = functools.partial(_router_kernel, tm=tm)
    rowspec = pl.BlockSpec((1, D), lambda i: (0, 0))
    kspec = pl.BlockSpec((SUBLANES, tm), lambda i: (0, i))
    return pl.pallas_call(
        kern,
        grid=(T // tm,),
        in_specs=[pl.BlockSpec((tm, D), lambda i: (i, 0)), rowspec, rowspec,
                  pl.BlockSpec((E, D), lambda i: (0, 0)), pl.BlockSpec((E, 1), lambda i: (0, 0))],
        out_specs=[pl.BlockSpec((tm, D // 2), lambda i: (i, 0)), kspec, kspec, kspec,
                   pl.BlockSpec((E, LANES), lambda i: (0, 0))],
        out_shape=[jax.ShapeDtypeStruct((T, D // 2), U32),
                   jax.ShapeDtypeStruct((SUBLANES, T), I32),
                   jax.ShapeDtypeStruct((SUBLANES, T), F32),
                   jax.ShapeDtypeStruct((SUBLANES, T), I32),
                   jax.ShapeDtypeStruct((E, LANES), F32)],
        scratch_shapes=[pltpu.VMEM((E, LANES), F32)],
        compiler_params=_cp("arbitrary"),
        name="router",
    )(x, sc, sh, w_router.T, router_bias.reshape(E, 1))


def _dispatch_kernel(dest_ref, fill_ref, u_ref, xs_ref, z_ref, sem, zsem, *, tm, bm):
    i = pl.program_id(0)

    @pl.when(i == 0)
    def _():
        z_ref[...] = jnp.zeros_like(z_ref)

        def zcopy(e):
            return pltpu.make_async_copy(z_ref, xs_ref.at[pl.ds(pl.multiple_of(fill_ref[e], bm), bm)], zsem)

        def zstart(e, c):
            zcopy(e).start()
            return c

        def zwait(e, c):
            zcopy(e).wait()
            return c

        lax.fori_loop(0, N_EXPERTS, zstart, 0)
        lax.fori_loop(0, N_EXPERTS, zwait, 0)

    for tok in range(tm):
        for k in range(TOP_K):
            pltpu.make_async_copy(_row(u_ref, tok), _row(xs_ref, dest_ref[k, tok]), sem).start()
    n = tm * TOP_K
    pltpu.make_async_copy(xs_ref.at[pl.ds(0, n)], xs_ref.at[pl.ds(0, n)], sem).wait()


def _dispatch(u, dest, fill, n_slots, tm=256, bm=MOE_BM):
    T, D = u.shape
    kern = functools.partial(_dispatch_kernel, tm=tm, bm=bm)
    return pl.pallas_call(
        kern,
        grid=(T // tm,),
        in_specs=[pl.BlockSpec((SUBLANES, tm), lambda i: (0, i), memory_space=pltpu.SMEM),
                  pl.BlockSpec(memory_space=pltpu.SMEM),
                  pl.BlockSpec((tm, D), lambda i: (i, 0))],
        out_specs=pl.BlockSpec(memory_space=pl.ANY),
        out_shape=jax.ShapeDtypeStruct((n_slots, D), u.dtype),
        scratch_shapes=[pltpu.VMEM((bm, D), u.dtype),
                        pltpu.SemaphoreType.DMA(()), pltpu.SemaphoreType.DMA(())],
        compiler_params=_cp("arbitrary"),
        name="moe_dispatch",
    )(dest, fill, u)


def _expert_kernel(be_ref, nv_ref, x_ref, wg_ref, wu_ref, wd_ref, y_ref, wgb_ref, wub_ref, wdb_ref):
    b = pl.program_id(0)
    valid = b < nv_ref[0]
    new_expert = jnp.logical_or(b == 0, be_ref[b] != be_ref[jnp.maximum(b - 1, 0)])

    @pl.when(jnp.logical_and(valid, new_expert))
    def _():
        wgb_ref[...] = wg_ref[0].astype(BF16)
        wub_ref[...] = wu_ref[0].astype(BF16)
        wdb_ref[...] = wd_ref[0].astype(BF16)

    @pl.when(valid)
    def _():
        x = _unpack_pairs(x_ref[...]).astype(BF16)
        h = _silu(_dot(x, wgb_ref[...])) * _dot(x, wub_ref[...])
        y_ref[...] = _pack_pairs(_dot(h.astype(BF16), wdb_ref[...]))


def _experts(xs, block_expert, n_valid, w_gate, w_up, w_down, layer, bm=MOE_BM):
    n_slots, W = xs.shape
    D = 2 * W
    nb = n_slots // bm
    F = w_gate.shape[-1]
    blk = lambda b, be, nv: (jnp.minimum(b, nv[0] - 1), 0)
    wsel = lambda b, be, nv: (layer, be[jnp.minimum(b, nv[0] - 1)], 0, 0)
    gs = pltpu.PrefetchScalarGridSpec(
        num_scalar_prefetch=2,
        grid=(nb,),
        in_specs=[pl.BlockSpec((bm, W), blk),
                  pl.BlockSpec((None, 1, D, F), wsel),
                  pl.BlockSpec((None, 1, D, F), wsel),
                  pl.BlockSpec((None, 1, F, D), wsel)],
        out_specs=pl.BlockSpec((bm, W), blk),
        scratch_shapes=[pltpu.VMEM((D, F), BF16), pltpu.VMEM((D, F), BF16), pltpu.VMEM((F, D), BF16)],
    )
    return pl.pallas_call(
        _expert_kernel,
        grid_spec=gs,
        out_shape=jax.ShapeDtypeStruct((n_slots, W), U32),
        compiler_params=_cp("arbitrary"),
        name="moe_experts",
    )(block_expert, n_valid, xs, w_gate, w_up, w_down)


def _combine_kernel(dest_ref, dnext_ref, ys_ref, u_ref, x_ref, gt_ref, wg_ref, wu_ref, wd_ref, gate_ref, lg_ref,
                    lb_ref, o_ref, buf_ref, sem, *, tm):
    i = pl.program_id(0)
    last = pl.num_programs(0) - 1
    slot = lax.rem(i, 2)

    def row_copy(d_ref, s, tok, k):
        return pltpu.make_async_copy(_row(ys_ref, d_ref[k, tok]), _row(buf_ref.at[s, k], tok), sem.at[s])

    def wait_half(s):
        pltpu.make_async_copy(buf_ref.at[s], buf_ref.at[s], sem.at[s]).wait()

    @pl.when(i == 0)
    def _():
        def start(tok, c):
            for k in range(TOP_K):
                row_copy(dest_ref, slot, tok, k).start()
            return c

        lax.fori_loop(0, tm, start, 0)

    wait_half(slot)
    for tok in range(tm):
        for k in range(TOP_K):
            row_copy(dnext_ref, 1 - slot, tok, k).start()

    u = _unpack_pairs(u_ref[...]).astype(BF16)
    h = _silu(_dot(u, wg_ref[...])) * _dot(u, wu_ref[...])
    ffn = _dot(h.astype(BF16), wd_ref[...])
    gt = gt_ref[...]
    for k in range(TOP_K):
        ffn = ffn + gt[:, k:k + 1] * _unpack_pairs(buf_ref[slot, k])
    hres = DEEPNORM_ALPHA * x_ref[...] + (1.0 + gate_ref[...]) * ffn
    o_ref[...] = _layer_norm(hres, lg_ref[...], lb_ref[...])

    @pl.when(i == last)
    def _():
        wait_half(1 - slot)


def _combine(ys, dest, u, x, gates_t, w_sg, w_su, w_sd, gate, ln_g, ln_b, tm=128):
    T, D = x.shape
    W = ys.shape[1]
    F = w_sg.shape[-1]
    n = T // tm
    kern = functools.partial(_combine_kernel, tm=tm)
    spec = pl.BlockSpec((tm, D), lambda i: (i, 0))
    rowspec = pl.BlockSpec((1, D), lambda i: (0, 0))
    return pl.pallas_call(
        kern,
        grid=(n,),
        in_specs=[pl.BlockSpec((SUBLANES, tm), lambda i: (0, i), memory_space=pltpu.SMEM),
                  pl.BlockSpec((SUBLANES, tm), lambda i: (0, jnp.minimum(i + 1, n - 1)), memory_space=pltpu.SMEM),
                  pl.BlockSpec(memory_space=pl.ANY),
                  pl.BlockSpec((tm, W), lambda i: (i, 0)), spec,
                  pl.BlockSpec((tm, SUBLANES), lambda i: (i, 0)),
                  pl.BlockSpec((D, F), lambda i: (0, 0)),
                  pl.BlockSpec((D, F), lambda i: (0, 0)),
                  pl.BlockSpec((F, D), lambda i: (0, 0)),
                  rowspec, rowspec, rowspec],
        out_specs=spec,
        out_shape=jax.ShapeDtypeStruct((T, D), F32),
        scratch_shapes=[pltpu.VMEM((2, TOP_K, tm, W), ys.dtype), pltpu.SemaphoreType.DMA((2,))],
        compiler_params=_cp("arbitrary"),
        name="moe_combine_norm",
    )(dest, dest, ys, u, x, gates_t, w_sg, w_su, w_sd, gate, ln_g.reshape(1, D), ln_b.reshape(1, D))


def _moe_plan(top_e, rank, counts, bm, n_blocks):
    counts = counts.astype(I32)
    padded = (counts + bm - 1) // bm * bm
    padded_end = jnp.cumsum(padded)
    padded_start = padded_end - padded
    onehot = top_e[:, :, None] == jnp.arange(N_EXPERTS, dtype=I32)
    dest = jnp.sum(jnp.where(onehot, padded_start, 0), axis=-1) + rank
    fill = jnp.maximum(padded_end - bm, padded_start)
    fill = jnp.minimum(fill, (n_blocks - 1) * bm)
    block_start = jnp.arange(n_blocks, dtype=I32) * bm
    block_expert = jnp.minimum(jnp.sum((padded_end[None, :] <= block_start[:, None]).astype(I32), axis=1),
                               N_EXPERTS - 1)
    n_valid = jnp.maximum(padded_end[-1:] // bm, 1).astype(I32)
    return dest.astype(I32), fill.astype(I32), block_expert, n_valid


def _in_proj_weights(w_in):
    o = SB_COLS
    gq = w_in[:, o:o + 2 * GLA_KEY_DIM + GLA_VALUE_DIM]
    o += 2 * GLA_KEY_DIM + GLA_VALUE_DIM
    glr = w_in[:, o:o + GLA_GATE_RANK]
    o += GLA_GATE_RANK
    gout = w_in[:, o:o + GLA_VALUE_DIM]
    o += GLA_VALUE_DIM
    pad = jnp.zeros((w_in.shape[0], 2 * LANES - GLA_GATE_RANK), w_in.dtype)
    w_gla = jnp.concatenate([gq, gout, glr, pad], axis=1)
    w_rw = w_in[:, o:o + RWKV_COLS]
    o += RWKV_COLS
    w_mg = w_in[:, o:o + MERGE_COLS]
    return w_in[:, :SB_COLS], w_gla, w_rw, w_mg


def kernel(x, c, w_ada, b_ada, w_in, gla_gk_up, gla_gk_bias, gla_norm_g, rw_mu, rw_w0, rw_w2, rw_a0,
           rw_a2, rw_g2, rw_k_k, rw_k_a, rw_r_k, rw_lnx_g, rw_lnx_b, rw_v0, rw_v1, rw_v2,
           w_br_sb, w_br_gla, w_br_rw, w_o, ln_g, ln_b, w_router, router_bias,
           w_exp_gate, w_exp_up, w_exp_down, w_sh_gate, w_sh_up, w_sh_down):
    B, T, D = x.shape
    assert B == 1 and D == D_MODEL
    n_blocks = -(-(T * TOP_K) // MOE_BM) + N_EXPERTS
    mod = _ada(c, w_ada, b_ada)
    xs = x.reshape(T, D)
    v_first = None
    for l in range(DEPTH):
        sh1, sc1, g1, sh2, sc2, g2 = (mod[l, j * D:(j + 1) * D].reshape(1, D) for j in range(6))
        w_sb, w_gla, w_rw, w_mg = _in_proj_weights(w_in[l])
        u1 = _modulate(xs, sc1, sh1)
        p_sb = _wmm(u1, w_sb, BF16, 1024, name="inproj_sb")
        p_gla = _wmm(u1, w_gla, F32, 13 * LANES, name="inproj_gla")
        p_rw = _wmm(u1, w_rw, F32, 13 * LANES, name="inproj_rwkv")
        p_mg = _wmm(u1, w_mg, BF16, 1024, name="inproj_merge")

        o_sb = _sb_attention(p_sb)
        o_gla = _gla(p_gla, gla_gk_up[l], gla_gk_bias[l], gla_norm_g[l])
        if l == 0:
            r, w, k, v, a, b, g = _rwkv_prep(p_rw, None, rw_mu[l], rw_w0[l], rw_w2[l], rw_a0[l], rw_a2[l],
                                             rw_g2[l], rw_k_k[l], rw_k_a[l], None, None, None)
            v_first = v
        else:
            r, w, k, v, a, b, g = _rwkv_prep(p_rw, v_first, rw_mu[l], rw_w0[l], rw_w2[l], rw_a0[l], rw_a2[l],
                                             rw_g2[l], rw_k_k[l], rw_k_a[l], rw_v0[l - 1], rw_v1[l - 1],
                                             rw_v2[l - 1])
        y = _rwkv_recurrence(r, w, k, v, a, b)
        o_rw = _rwkv_post(y, r, k, v, g, rw_lnx_g[l], rw_lnx_b[l], rw_r_k[l])

        merged = _merge(o_sb, o_gla, o_rw, p_mg, w_br_sb[l].astype(BF16), w_br_gla[l].astype(BF16),
                        w_br_rw[l].astype(BF16))
        xs = _oproj(merged, w_o[l].astype(BF16), xs, g1, ln_g[l, 0], ln_b[l, 0])

        u2, top_e, gates, rank, counts = _router(xs, sc2, sh2, w_router[l], router_bias[l])
        dest, fill, block_expert, n_valid = _moe_plan(top_e, rank, counts[:, 0], MOE_BM, n_blocks)
        x_sorted = _dispatch(u2, dest, fill, n_blocks * MOE_BM)
        y_sorted = _experts(x_sorted, block_expert, n_valid, w_exp_gate, w_exp_up, w_exp_down, l)
        xs = _combine(y_sorted, dest, u2, xs, gates.T, w_sh_gate[l].astype(BF16), w_sh_up[l].astype(BF16),
                      w_sh_down[l].astype(BF16), g2, ln_g[l, 1], ln_b[l, 1])
    return xs.reshape(B, T, D)
```

```python
import functools

import numpy as np
import jax
import jax.numpy as jnp
from jax import lax
from jax.experimental import pallas as pl
from jax.experimental.pallas import tpu as pltpu

F32 = jnp.float32
BF16 = jnp.bfloat16
I32 = jnp.int32
U32 = jnp.uint32

D_MODEL = 2048
DEPTH = 2
LN_EPS = 1e-5
DEEPNORM_ALPHA = (2 * DEPTH) ** 0.25

SB_HEADS = 8
SB_HEAD_DIM = 128
SB_WIDTH = SB_HEADS * SB_HEAD_DIM

GLA_HEADS = 4
GLA_HEAD_K = 128
GLA_HEAD_V = 256
GLA_KEY_DIM = GLA_HEADS * GLA_HEAD_K
GLA_VALUE_DIM = GLA_HEADS * GLA_HEAD_V
GLA_GATE_RANK = 16
GLA_GATE_NORMALIZER = 16.0
GLA_CHUNK = 64
GLA_NORM_EPS = 1e-5

RWKV_HEADS = 16
RWKV_HEAD_DIM = 64
RWKV_WIDTH = RWKV_HEADS * RWKV_HEAD_DIM
RWKV_DECAY_RANK = 64
RWKV_AAA_RANK = 64
RWKV_VALUE_RANK = 32
RWKV_GATE_RANK = 128
RWKV_LNX_EPS = 64e-5

N_EXPERTS = 64
TOP_K = 6
EXPERT_DIM = 512
SHARED_DIM = 512
N_GROUPS = 8
TOPK_GROUPS = 4
ROUTED_SCALE = 2.5

SB_COLS = 3 * SB_WIDTH
GLA_COLS = 2 * GLA_KEY_DIM + GLA_VALUE_DIM + GLA_GATE_RANK + GLA_VALUE_DIM
RWKV_COLS = 3 * RWKV_WIDTH + RWKV_DECAY_RANK + RWKV_AAA_RANK + RWKV_GATE_RANK
MERGE_COLS = 3 * D_MODEL

LANES = 128
SUBLANES = 8
VMEM_LIMIT = 56 * 1024 * 1024
MOE_BM = 512
SB_EXIT = -104.0
SB_GROUPS = 2
SB_PACK = 2

NN = (((1,), (0,)), ((), ()))
NT = (((1,), (1,)), ((), ()))
TN = (((0,), (0,)), ((), ()))


def _cp(*sem):
    return pltpu.CompilerParams(dimension_semantics=sem, vmem_limit_bytes=VMEM_LIMIT)


def _dot(a, b, dims=NN):
    return lax.dot_general(a, b, dims, preferred_element_type=F32)


def _split(a, n):
    out = []
    r = a
    for i in range(n):
        p = r.astype(BF16)
        out.append(p)
        if i + 1 < n:
            r = r - p.astype(F32)
    return out


def _dot_fx(a, e, n, dims=NN):
    acc = None
    for p in _split(a, n):
        t = _dot(p, e, dims)
        acc = t if acc is None else acc + t
    return acc


def _dot_xf(e, b, n, dims=NN):
    acc = None
    for p in _split(b, n):
        t = _dot(e, p, dims)
        acc = t if acc is None else acc + t
    return acc


def _dot_ff(a, b, dims=NN):
    a1, a2, a3 = _split(a, 3)
    b1, b2, b3 = _split(b, 3)
    acc = _dot(a1, b1, dims)
    for p, q in ((a1, b2), (a2, b1), (a2, b2), (a1, b3), (a3, b1)):
        acc = acc + _dot(p, q, dims)
    return acc


def _sigmoid(x):
    return 1.0 / (1.0 + jnp.exp(-x))


def _softplus(x):
    return jnp.maximum(x, 0.0) + jnp.log1p(jnp.exp(-jnp.abs(x)))


def _silu(x):
    return x * _sigmoid(x)


def _row(ref, t):
    return ref.at[pl.ds(t, 1)]


def _pack_pairs(x):
    w = x.shape[1] // 2
    bits = lax.bitcast_convert_type(x.astype(BF16).astype(F32), U32)
    return (bits[:, :w] >> 16) | bits[:, w:]


def _unpack_pairs(p):
    lo = lax.bitcast_convert_type(p << 16, F32)
    hi = lax.bitcast_convert_type(p & jnp.uint32(0xFFFF0000), F32)
    return jnp.concatenate([lo, hi], axis=1)


def _ada_kernel(c_ref, w_ref, b_ref, o_ref):
    s = _silu(c_ref[...])
    o_ref[0] = jnp.sum(s * w_ref[0], axis=0, keepdims=True) + b_ref[0]


def _ada(c, w_ada, b_ada):
    L, D, N = w_ada.shape
    tn = 512
    out = pl.pallas_call(
        _ada_kernel,
        grid=(L, N // tn),
        in_specs=[pl.BlockSpec((D, 1), lambda l, j: (0, 0)),
                  pl.BlockSpec((1, D, tn), lambda l, j: (l, 0, j)),
                  pl.BlockSpec((1, 1, tn), lambda l, j: (l, 0, j))],
        out_specs=pl.BlockSpec((1, 1, tn), lambda l, j: (l, 0, j)),
        out_shape=jax.ShapeDtypeStruct((L, 1, N), F32),
        compiler_params=_cp("parallel", "parallel"),
        name="ada",
    )(c.reshape(D, 1), w_ada, b_ada.reshape(L, 1, N))
    return out[:, 0, :]


def _modulate_kernel(x_ref, sc_ref, sh_ref, o_ref):
    o_ref[...] = (x_ref[...] * (1.0 + sc_ref[...]) + sh_ref[...]).astype(o_ref.dtype)


def _modulate(x, sc, sh, tm=1024):
    T, D = x.shape
    spec = pl.BlockSpec((tm, D), lambda i: (i, 0))
    rowspec = pl.BlockSpec((1, D), lambda i: (0, 0))
    return pl.pallas_call(
        _modulate_kernel,
        grid=(T // tm,),
        in_specs=[spec, rowspec, rowspec],
        out_specs=spec,
        out_shape=jax.ShapeDtypeStruct((T, D), BF16),
        compiler_params=_cp("parallel"),
        name="modulate",
    )(x, sc, sh)


def _wmm_kernel(u_ref, w_ref, o_ref, wb_ref):
    @pl.when(pl.program_id(1) == 0)
    def _():
        wb_ref[...] = w_ref[...].astype(BF16)

    o_ref[...] = _dot(u_ref[...], wb_ref[...]).astype(o_ref.dtype)


def _wmm(u, w, out_dtype, tn, tm=512, name="wmm"):
    T, D = u.shape
    N = w.shape[1]
    return pl.pallas_call(
        _wmm_kernel,
        grid=(N // tn, T // tm),
        in_specs=[pl.BlockSpec((tm, D), lambda j, i: (i, 0)),
                  pl.BlockSpec((D, tn), lambda j, i: (0, j))],
        out_specs=pl.BlockSpec((tm, tn), lambda j, i: (i, j)),
        out_shape=jax.ShapeDtypeStruct((T, N), out_dtype),
        scratch_shapes=[pltpu.VMEM((D, tn), BF16)],
        compiler_params=_cp("parallel", "arbitrary"),
        name=name,
    )(u, w)


def _sb_kernel(q_ref, k_ref, v_ref, o_ref, acc_ref, car_ref, z_ref, *, bq, scale):
    i = pl.program_id(1)
    Dh = SB_HEAD_DIM
    lanes = [slice(h * Dh, (h + 1) * Dh) for h in range(SB_PACK)]
    r_id = lax.broadcasted_iota(I32, (bq, bq), 0)
    c_id = lax.broadcasted_iota(I32, (bq, bq), 1)
    later = (r_id > c_id).astype(BF16)
    causal = c_id < r_id
    grp = [slice(g * (bq // SB_GROUPS), (g + 1) * (bq // SB_GROUPS)) for g in range(SB_GROUPS)]

    def rows(j):
        return pl.ds(pl.multiple_of(j * bq, bq), bq)

    def scores(units, j):
        return [_dot(q_ref[s, lanes[h]], k_ref[rows(jnp.maximum(j, 0)), lanes[h]], NT) for h, s in units]

    def block(units, j, zs, diagonal):
        z = [t * scale for t in zs]
        drop = [jnp.maximum(t, 0.0) + jnp.log(1.0 + jnp.exp(-jnp.abs(t))) for t in z]
        if diagonal:
            drop = [jnp.where(causal[s], d, 0.0) for (h, s), d in zip(units, drop)]
            after = [_dot_fx(d, later, 2) for d in drop]
        else:
            after = [_dot_fx(d, later, 2) + car_ref[h, s, :] for (h, s), d in zip(units, drop)]
        w = [jnp.exp(t - d - a) for t, d, a in zip(z, drop, after)]
        if diagonal:
            w = [jnp.where(causal[s], x, 0.0) for (h, s), x in zip(units, w)]
        pv = [_dot(x.astype(BF16), v_ref[rows(j), lanes[h]]) for (h, s), x in zip(units, w)]
        return pv, [a[:, 0:1] + d[:, 0:1] for a, d in zip(after, drop)]

    every = [(h, s) for h in range(SB_PACK) for s in grp]
    z_next = scores(every, i - 1)
    pv, carry = block(every, i, scores(every, i), True)
    for (h, s), t, p, c in zip(every, z_next, pv, carry):
        z_ref[h, s, :] = t
        acc_ref[s, lanes[h]] = p
        car_ref[h, s, :] = c

    for h in range(SB_PACK):
        units = [(h, s) for s in grp]

        def body(c, units=units):
            j, _ = c
            zs = [z_ref[u, s, :] for u, s in units]
            z_next = scores(units, j - 1)
            pv, carry = block(units, j, zs, False)
            for (u, s), t, p, cr in zip(units, z_next, pv, carry):
                z_ref[u, s, :] = t
                acc_ref[s, lanes[u]] += p
                car_ref[u, s, :] = cr
            return j - 1, (jnp.min(jnp.concatenate(carry, axis=0)) > -SB_EXIT).astype(I32)

        lax.while_loop(lambda c: jnp.logical_and(c[0] >= 0, c[1] == 0), body, (i - 1, jnp.int32(0)))
    o_ref[...] = acc_ref[...].astype(o_ref.dtype)


def _sb_attention(qkv, bq=256):
    T = qkv.shape[0]
    H, Dh = SB_HEADS, SB_HEAD_DIM
    kern = functools.partial(_sb_kernel, bq=bq, scale=Dh ** -0.5)
    P = SB_PACK
    G = H // P
    return pl.pallas_call(
        kern,
        grid=(G, T // bq),
        in_specs=[pl.BlockSpec((bq, P * Dh), lambda h, i: (i, h)),
                  pl.BlockSpec((T, P * Dh), lambda h, i: (0, G + h)),
                  pl.BlockSpec((T, P * Dh), lambda h, i: (0, 2 * G + h))],
        out_specs=pl.BlockSpec((bq, P * Dh), lambda h, i: (i, h)),
        out_shape=jax.ShapeDtypeStruct((T, H * Dh), BF16),
        scratch_shapes=[pltpu.VMEM((bq, P * Dh), F32), pltpu.VMEM((P, bq, 1), F32), pltpu.VMEM((P, bq, bq), F32)],
        compiler_params=_cp("parallel", "parallel"),
        name="sb_attention",
    )(qkv, qkv, qkv)


_GLA_LEVELS = (32, 16, 8, 4, 2, 1)


def _gla_constants():
    C = GLA_CHUNK
    i = np.arange(C)[:, None]
    s = np.arange(C)[None, :]
    sel = []
    masks = []
    for m in _GLA_LEVELS:
        same = (i // m) == (s // m)
        if m < SUBLANES:
            sel.append(same & (s <= i) & ((i // m) % 2 == 1))
            sel.append(same & (s > i) & ((i // m) % 2 == 0))
        masks.append(((i // (2 * m)) == (s // (2 * m))) & ((i // m) % 2 == 1) & ((s // m) % 2 == 0))
    sel.append(s <= i)
    sel.append(s > i)
    masks.append(i == s)
    return (np.concatenate(sel, 0).astype(np.float32), np.stack(masks, 0).astype(np.float32))


def _gla_kernel(q_ref, k_ref, v_ref, go_ref, glr_ref, up_ref, bias_ref, ng_ref, sel_ref, msk_ref,
                o_ref, st_ref, *, n_chunks):
    C = GLA_CHUNK
    nl = len(_GLA_LEVELS)

    @pl.when(pl.program_id(1) == 0)
    def _():
        st_ref[...] = jnp.zeros_like(st_ref)

    sel = sel_ref[...]
    up = up_ref[...]
    bias = bias_ref[...]
    ng = ng_ref[...]
    rows = [slice(c * C, (c + 1) * C) for c in range(n_chunks)]
    blk = lambda f, j: f[j * C:(j + 1) * C]
    q = [q_ref[s, :] * (GLA_HEAD_K ** -0.5) for s in rows]
    k = [k_ref[s, :] for s in rows]
    v = [v_ref[s, :].astype(BF16) for s in rows]
    x = [_dot(glr_ref[s, :].astype(BF16), up) + bias for s in rows]
    g = [-_softplus(-t) * (1.0 / GLA_GATE_NORMALIZER) for t in x]
    e = [_dot_xf(sel, t, 2) for t in g]
    n_fine = sum(1 for m in _GLA_LEVELS if m < SUBLANES)
    bcum = [blk(t, 2 * n_fine) for t in e]
    rid = lax.broadcasted_iota(I32, (C, 1), 0)
    fq, fk = [], []
    for l, m in enumerate(_GLA_LEVELS):
        if m < SUBLANES:
            j = l - (nl - n_fine)
            fq.append([jnp.exp(blk(t, 2 * j)) for t in e])
            fk.append([jnp.exp(blk(t, 2 * j + 1)) for t in e])
        else:
            odd = (rid // m) % 2 == 1
            end = [jnp.broadcast_to(b.reshape(C // m, m, -1)[:, m - 1:m, :], (C // m, m, b.shape[-1])).reshape(C, -1)
                   for b in bcum]
            fq.append([jnp.exp(jnp.where(odd, b - pltpu.roll(t, m, axis=0), 0.0)) for b, t in zip(bcum, end)])
            fk.append([jnp.exp(jnp.where(odd, 0.0, t - b)) for b, t in zip(bcum, end)])
    f_start = [jnp.exp(b) for b in bcum]
    f_end = [jnp.exp(blk(t, 2 * n_fine + 1)) for t in e]
    scores = [_dot(a.astype(BF16), b.astype(BF16), NT) * msk_ref[nl] for a, b in zip(q, k)]
    for l in range(nl):
        ql = [(a * t).astype(BF16) for a, t in zip(q, fq[l])]
        kl = [(b * t).astype(BF16) for b, t in zip(k, fk[l])]
        scores = [s + _dot(a, b, NT) * msk_ref[l] for s, a, b in zip(scores, ql, kl)]
    o_intra = [_dot(s.astype(BF16), b) for s, b in zip(scores, v)]
    qe = [(a * t).astype(BF16) for a, t in zip(q, f_start)]
    upd = [_dot(b, (a * t).astype(BF16), TN) for b, a, t in zip(v, k, f_end)]
    dec = [t[C - 1:C, :] for t in f_start]

    st = st_ref[...]
    for c in range(n_chunks):
        o = _dot(qe[c], st.astype(BF16), NT) + o_intra[c]
        st = st * dec[c] + upd[c]
        rms = lax.rsqrt(jnp.mean(o * o, axis=-1, keepdims=True) + GLA_NORM_EPS)
        o_ref[rows[c], :] = (o * rms * ng * _silu(go_ref[rows[c], :])).astype(o_ref.dtype)
    st_ref[...] = st


def _gla(p_gla, gk_up, gk_bias, norm_g, tt=512):
    T = p_gla.shape[0]
    H, dk, dv = GLA_HEADS, GLA_HEAD_K, GLA_HEAD_V
    sel, msk = _gla_constants()
    up = jnp.zeros((LANES, GLA_KEY_DIM), F32).at[:GLA_GATE_RANK].set(gk_up).astype(BF16)
    kern = functools.partial(_gla_kernel, n_chunks=tt // GLA_CHUNK)
    return pl.pallas_call(
        kern,
        grid=(H, T // tt),
        in_specs=[pl.BlockSpec((tt, dk), lambda h, i: (i, h)),
                  pl.BlockSpec((tt, dk), lambda h, i: (i, H + h)),
                  pl.BlockSpec((tt, dv), lambda h, i: (i, H + h)),
                  pl.BlockSpec((tt, dv), lambda h, i: (i, 2 * H + h)),
                  pl.BlockSpec((tt, LANES), lambda h, i: (i, 3 * GLA_VALUE_DIM // LANES)),
                  pl.BlockSpec((LANES, dk), lambda h, i: (0, h)),
                  pl.BlockSpec((1, dk), lambda h, i: (0, h)),
                  pl.BlockSpec((1, dv), lambda h, i: (0, 0)),
                  pl.BlockSpec(sel.shape, lambda h, i: (0, 0)),
                  pl.BlockSpec(msk.shape, lambda h, i: (0, 0, 0))],
        out_specs=pl.BlockSpec((tt, dv), lambda h, i: (i, h)),
        out_shape=jax.ShapeDtypeStruct((T, H * dv), BF16),
        scratch_shapes=[pltpu.VMEM((dv, dk), F32)],
        compiler_params=_cp("parallel", "arbitrary"),
        name="gla",
    )(p_gla, p_gla, p_gla, p_gla, p_gla, up, gk_bias.reshape(1, -1), norm_g.reshape(1, -1),
      jnp.asarray(sel, BF16), jnp.asarray(msk, F32))


def _head_ones():
    l = np.arange(LANES)
    return (l[:, None] // RWKV_HEAD_DIM == l[None, :] // RWKV_HEAD_DIM).astype(np.float32)


def _rwprep_kernel(p_ref, pp_ref, mu_ref, w0_ref, w2_ref, a0_ref, a2_ref, g2_ref, kk_ref, ka_ref,
                   v0_ref, v1_ref, v2_ref, vf_ref, bo_ref,
                   r_ref, w_ref, k_ref, v_ref, a_ref, b_ref, g_ref, *, first_layer):
    W = RWKV_WIDTH
    i = pl.program_id(0)
    p = p_ref[...]
    tm = p.shape[0]
    prev_row = jnp.where(i == 0, 0.0, pp_ref[SUBLANES - 1:SUBLANES, :])
    rid = lax.broadcasted_iota(I32, p.shape, 0)
    prev = jnp.where(rid == 0, prev_row, pltpu.roll(p, 1, axis=0))
    ps = p + (prev - p) * mu_ref[...]
    r = ps[:, 0:W]
    kr = ps[:, W:2 * W]
    vr = ps[:, 2 * W:3 * W]
    lr = ps[:, 3 * W:3 * W + LANES]
    gl = ps[:, 3 * W + LANES:3 * W + 2 * LANES]
    w_log = -_softplus(-(w0_ref[...] + _dot(jnp.tanh(lr).astype(BF16), w2_ref[...]))) - 0.5
    w_ref[...] = -jnp.exp(w_log)
    if not first_layer:
        lo = _dot(vr.astype(BF16), v1_ref[...])
        gate = _sigmoid(v0_ref[...] + _dot(lo.astype(BF16), v2_ref[...]))
        vr = vr + (vf_ref[...] - vr) * gate
    v_ref[...] = vr
    a = _sigmoid(a0_ref[...] + _dot(lr.astype(BF16), a2_ref[...]))
    g_ref[...] = _dot(_sigmoid(gl).astype(BF16), g2_ref[...])
    kk = kr * kk_ref[...]
    bo = bo_ref[...]
    for hp in range(W // LANES):
        sl = slice(hp * LANES, (hp + 1) * LANES)
        kh = kk[:, sl]
        ss = _dot_fx(kh * kh, bo, 3)
        kn = kh / jnp.maximum(jnp.sqrt(ss), 1e-12)
        a_ref[:, sl] = -kn
        b_ref[:, sl] = kn * a[:, sl]
    r_ref[...] = r
    k_ref[...] = kr * (1.0 + (a - 1.0) * ka_ref[...])


def _rwkv_prep(p_rw, v_first, mu, w0, w2, a0, a2, g2, k_k, k_a, v0, v1, v2, tm=256):
    T, NC = p_rw.shape
    W = RWKV_WIDTH
    first = v_first is None
    w2p = jnp.zeros((LANES, W), F32).at[:RWKV_DECAY_RANK].set(w2).astype(BF16)
    a2p = jnp.zeros((LANES, W), F32).at[RWKV_DECAY_RANK:].set(a2).astype(BF16)
    if first:
        v_first = jnp.zeros((SUBLANES, W), F32)
        v0 = jnp.zeros((W,), F32)
        v1p = jnp.zeros((W, LANES), BF16)
        v2p = jnp.zeros((LANES, W), BF16)
        vf_spec = pl.BlockSpec((SUBLANES, W), lambda i: (0, 0))
    else:
        v1p = jnp.zeros((W, LANES), F32).at[:, :RWKV_VALUE_RANK].set(v1).astype(BF16)
        v2p = jnp.zeros((LANES, W), F32).at[:RWKV_VALUE_RANK].set(v2).astype(BF16)
        vf_spec = pl.BlockSpec((tm, W), lambda i: (i, 0))
    row = lambda a: a.reshape(1, -1)
    full = lambda shape: pl.BlockSpec(shape, lambda i: (0,) * len(shape))
    out = jax.ShapeDtypeStruct((T, W), F32)
    ospec = pl.BlockSpec((tm, W), lambda i: (i, 0))
    kern = functools.partial(_rwprep_kernel, first_layer=first)
    return pl.pallas_call(
        kern,
        grid=(T // tm,),
        in_specs=[pl.BlockSpec((tm, NC), lambda i: (i, 0)),
                  pl.BlockSpec((SUBLANES, NC), lambda i: (jnp.maximum(i * (tm // SUBLANES) - 1, 0), 0)),
                  full((1, NC)), full((1, W)), full((LANES, W)), full((1, W)), full((LANES, W)),
                  full((LANES, W)), full((1, W)), full((1, W)),
                  full((1, W)), full((W, LANES)), full((LANES, W)), vf_spec, full((LANES, LANES))],
        out_specs=[ospec] * 7,
        out_shape=[out] * 7,
        compiler_params=_cp("parallel"),
        name="rwkv_prep",
    )(p_rw, p_rw, row(mu), row(w0), w2p, row(a0), a2p, g2.astype(BF16), row(k_k), row(k_a),
      row(v0), v1p, v2p, v_first, jnp.asarray(_head_ones(), BF16))


RW_CHUNK = 64


def _rw_constants():
    C = RW_CHUNK
    i = np.arange(C)[:, None]
    s = np.arange(C)[None, :]
    i2 = np.arange(2 * C)[:, None]
    s2 = np.arange(2 * C)[None, :]
    same = (i2 // C) == (s2 // C)
    cm = np.stack([same & (s2 < i2), same & (s2 <= i2), (i2 // 16) == (s2 // 16), (i2 // 32) == (s2 // 32),
                   same], 0).astype(np.float32)
    return (s <= i).astype(np.float32), cm


def _mm(a, b, dims=NN):
    a1, a2 = _split(a, 2)
    b1, b2 = _split(b, 2)
    r = _dot(a1, jnp.concatenate([b1, b2], axis=0 if dims == NT else 1), dims)
    n = r.shape[1] // 2
    return (r[:, :n] + r[:, n:]) + _dot(a2, b1, dims)


def _mm1(a, b, dims=NN):
    return _dot(a.astype(BF16), b.astype(BF16), dims)


def _unit_lower_inverse(nms, eye, low, bd16, bd32, bd64):
    size = eye.shape[0]

    def pack(m, s):
        acc = m[0:s]
        for b in range(1, size // s):
            acc = acc + m[s * b:s * (b + 1)]
        return acc

    def unpack(p, mask):
        return jnp.concatenate([p] * (size // p.shape[0]), axis=0) * mask

    nd = [n * bd16 for n in nms]
    x = [pack(eye + n, 16) for n in nd]
    p = [_mm(pack(n, 16), n) for n in nd]
    for step in range(3):
        pf = [unpack(q, bd16) for q in p]
        x = [a + _mm(a, q) for a, q in zip(x, pf)]
        if step < 2:
            p = [_mm(q, qf) for q, qf in zip(p, pf)]
    x = [unpack(a, bd16) for a in x]
    for s, inner, outer in ((16, bd16, bd32), (32, bd32, bd64)):
        place = (outer - inner) * low
        t = [unpack(_mm(pack(n * place, s), a), place) for n, a in zip(nms, x)]
        x = [a + unpack(_mm(pack(a, s), q), place) for a, q in zip(x, t)]
    return x


def _rwrec_kernel(r_ref, lw_ref, k_ref, v_ref, a_ref, b_ref, lt_ref, cm_ref, y_ref, mt_ref, *, n_chunks):
    C = RW_CHUNK
    C2 = 2 * C

    @pl.when(pl.program_id(1) == 0)
    def _():
        mt_ref[...] = jnp.zeros_like(mt_ref)

    lt = lt_ref[...]
    sl, li, bd16, bd32, bdh = cm_ref[0], cm_ref[1], cm_ref[2], cm_ref[3], cm_ref[4]
    eye = li - sl
    lane = lax.broadcasted_iota(I32, (1, LANES), 1)
    m0 = jnp.where(lane < RWKV_HEAD_DIM, 1.0, 0.0)
    m1 = 1.0 - m0
    stack = lambda x: jnp.concatenate([x * m0, x * m1], axis=0)
    fold = lambda x: x[:C] + x[C:]
    rows = [slice(c * C, (c + 1) * C) for c in range(n_chunks)]

    lw = [lw_ref[s, :] for s in rows]
    cw = [_dot_xf(lt, x, 3) for x in lw]
    en = [jnp.exp(-x) for x in cw]
    at = [a_ref[s, :] * jnp.exp(c - l) for s, c, l in zip(rows, cw, lw)]
    rt = [r_ref[s, :] * jnp.exp(c) for s, c in zip(rows, cw)]
    bh = [b_ref[s, :] * e for s, e in zip(rows, en)]
    kh = [k_ref[s, :] * e for s, e in zip(rows, en)]
    v = [v_ref[s, :] for s in rows]
    a2 = [stack(x) for x in at]
    ar = [jnp.concatenate([x, y], axis=0) for x, y in zip(at, rt)]
    gb = [_mm(x, stack(y), NT) for x, y in zip(ar, bh)]
    gk = [_mm(x, stack(y), NT) for x, y in zip(ar, kh)]
    slf = fold(sl)
    lif = fold(li)
    tinv = _unit_lower_inverse([stack(g[:C] * slf) for g in gb], eye, sl, bd16, bd32, bdh)
    akv = [_mm(jnp.concatenate([g[:C] * slf, g[C:] * lif], axis=0), stack(x))
           for g, x in zip(gk, v)]
    tav = [_mm(fold(t), jnp.concatenate([x, stack(y[:C])], axis=1)) for t, x, y in zip(tinv, a2, akv)]
    ta = [x[:, :LANES] for x in tav]
    u0 = [x[:, LANES:] for x in tav]
    rav = [_mm1(g[C:] * lif, jnp.concatenate([stack(x), stack(y)], axis=1))
           for g, x, y in zip(gb, ta, u0)]
    qe = [x + y[:, :LANES] for x, y in zip(rt, rav)]
    y0 = [x[:, LANES:] + y[C:] for x, y in zip(rav, akv)]
    dl = [jnp.exp(x[C - 1:C, :]) for x in cw]
    gmt = [(eye + _mm(x, y, TN) * bdh) * d for x, y, d in zip(ta, bh, dl)]
    hmt = [(_mm(x, y, TN) + _mm(z, w, TN)) * bdh * d for x, y, z, w, d in zip(u0, bh, v, kh, dl)]

    mt = mt_ref[...]
    for c in range(n_chunks):
        y_ref[rows[c], :] = _mm1(qe[c], mt, NT) + y0[c]
        mt = _mm(mt, gmt[c]) + hmt[c]
    mt_ref[...] = mt


def _rwkv_recurrence(r, lw, k, v, a, b, tt=512):
    T, W = r.shape
    lt, cm = _rw_constants()
    spec = pl.BlockSpec((tt, LANES), lambda p, i: (i, p))
    kern = functools.partial(_rwrec_kernel, n_chunks=tt // RW_CHUNK)
    return pl.pallas_call(
        kern,
        grid=(W // LANES, T // tt),
        in_specs=[spec] * 6 + [pl.BlockSpec(lt.shape, lambda p, i: (0, 0)),
                               pl.BlockSpec(cm.shape, lambda p, i: (0, 0, 0))],
        out_specs=spec,
        out_shape=jax.ShapeDtypeStruct((T, W), F32),
        scratch_shapes=[pltpu.VMEM((LANES, LANES), F32)],
        compiler_params=_cp("parallel", "arbitrary"),
        name="rwkv_recurrence",
    )(r, lw, k, v, a, b, jnp.asarray(lt, BF16), jnp.asarray(cm, F32))


def _rwpost_kernel(y_ref, r_ref, k_ref, v_ref, g_ref, lg_ref, lb_ref, rk_ref, bo_ref, o_ref):
    bo = bo_ref[...]
    inv = 1.0 / RWKV_HEAD_DIM
    for hp in range(RWKV_WIDTH // LANES):
        sl = slice(hp * LANES, (hp + 1) * LANES)
        y = y_ref[:, sl]
        mu = _dot_fx(y, bo, 3) * inv
        d = y - mu
        var = _dot_fx(d * d, bo, 3) * inv
        yn = d * lax.rsqrt(var + RWKV_LNX_EPS) * lg_ref[:, sl] + lb_ref[:, sl]
        bonus = _dot_fx(r_ref[:, sl] * k_ref[:, sl] * rk_ref[:, sl], bo, 3) * v_ref[:, sl]
        o_ref[:, sl] = ((yn + bonus) * g_ref[:, sl]).astype(o_ref.dtype)


def _rwkv_post(y, r, k, v, g, lnx_g, lnx_b, r_k, tm=512):
    T, W = y.shape
    spec = pl.BlockSpec((tm, W), lambda i: (i, 0))
    rowspec = pl.BlockSpec((1, W), lambda i: (0, 0))
    return pl.pallas_call(
        _rwpost_kernel,
        grid=(T // tm,),
        in_specs=[spec] * 5 + [rowspec] * 3 + [pl.BlockSpec((LANES, LANES), lambda i: (0, 0))],
        out_specs=spec,
        out_shape=jax.ShapeDtypeStruct((T, W), BF16),
        compiler_params=_cp("parallel"),
        name="rwkv_post",
    )(y, r, k, v, g, lnx_g.reshape(1, W), lnx_b.reshape(1, W), r_k.reshape(1, W),
      jnp.asarray(_head_ones(), BF16))


def _merge_kernel(osb_ref, ogl_ref, orw_ref, g1_ref, g2_ref, g3_ref, w1_ref, w2_ref, w3_ref, o_ref):
    acc = _sigmoid(g1_ref[...].astype(F32)) * _dot(osb_ref[...], w1_ref[...])
    acc = acc + _sigmoid(g2_ref[...].astype(F32)) * _dot(ogl_ref[...], w2_ref[...])
    acc = acc + _sigmoid(g3_ref[...].astype(F32)) * _dot(orw_ref[...], w3_ref[...])
    o_ref[...] = acc.astype(o_ref.dtype)


def _merge(o_sb, o_gla, o_rw, p_merge, w_sb, w_gla, w_rw, tm=512, tn=1024):
    T = o_sb.shape[0]
    D = D_MODEL
    nb = D // tn
    aspec = pl.BlockSpec((tm, 1024), lambda j, i: (i, 0))
    wspec = pl.BlockSpec((1024, tn), lambda j, i: (0, j))
    gspec = lambda b: pl.BlockSpec((tm, tn), lambda j, i: (i, b * nb + j))
    return pl.pallas_call(
        _merge_kernel,
        grid=(nb, T // tm),
        in_specs=[aspec, aspec, aspec, gspec(0), gspec(1), gspec(2), wspec, wspec, wspec],
        out_specs=pl.BlockSpec((tm, tn), lambda j, i: (i, j)),
        out_shape=jax.ShapeDtypeStruct((T, D), BF16),
        compiler_params=_cp("parallel", "parallel"),
        name="merge",
    )(o_sb, o_gla, o_rw, p_merge, p_merge, p_merge, w_sb, w_gla, w_rw)


def _layer_norm(h, g, b):
    mu = jnp.mean(h, axis=-1, keepdims=True)
    d = h - mu
    var = jnp.mean(d * d, axis=-1, keepdims=True)
    return d * lax.rsqrt(var + LN_EPS) * g + b


def _oproj_kernel(m_ref, w_ref, x_ref, gate_ref, lg_ref, lb_ref, o_ref):
    mix = _dot(m_ref[...], w_ref[...])
    h = DEEPNORM_ALPHA * x_ref[...] + (1.0 + gate_ref[...]) * mix
    o_ref[...] = _layer_norm(h, lg_ref[...], lb_ref[...])


def _oproj(merged, w_o, x, gate, ln_g, ln_b, tm=512):
    T, D = x.shape
    spec = pl.BlockSpec((tm, D), lambda i: (i, 0))
    rowspec = pl.BlockSpec((1, D), lambda i: (0, 0))
    return pl.pallas_call(
        _oproj_kernel,
        grid=(T // tm,),
        in_specs=[spec, pl.BlockSpec((D, D), lambda i: (0, 0)), spec, rowspec, rowspec, rowspec],
        out_specs=spec,
        out_shape=jax.ShapeDtypeStruct((T, D), F32),
        compiler_params=_cp("parallel"),
        name="oproj_norm",
    )(merged, w_o, x, gate, ln_g.reshape(1, D), ln_b.reshape(1, D))


def _first_max(x, idx, axis, big):
    m = jnp.max(x, axis=axis, keepdims=True)
    first = jnp.min(jnp.where(x == m, idx, big), axis=axis, keepdims=True)
    return m, idx == first


def _router_kernel(x_ref, sc_ref, sh_ref, wr_ref, rb_ref, u_ref, e_ref, g_ref, p_ref, cnt_ref, car_ref, *, tm):
    E, G = N_EXPERTS, N_GROUPS
    per = E // G
    neg = -jnp.inf

    @pl.when(pl.program_id(0) == 0)
    def _():
        car_ref[...] = jnp.zeros_like(car_ref)

    u = x_ref[...] * (1.0 + sc_ref[...]) + sh_ref[...]
    u_ref[...] = _pack_pairs(u)
    scores =_sigmoid(_dot_ff(wr_ref[...], u, NT))
    biased = scores + rb_ref[...]
    b3 = biased.reshape(G, per, tm)
    i3 = lax.broadcasted_iota(I32, (G, per, tm), 1)
    m1, hit = _first_max(b3, i3, 1, per)
    m2 = jnp.max(jnp.where(hit, neg, b3), axis=1, keepdims=True)
    gs = (m1 + m2).reshape(G, tm)
    gi = lax.broadcasted_iota(I32, (G, tm), 0)
    ok = jnp.zeros((G, tm), jnp.bool_)
    for _ in range(TOPK_GROUPS):
        _, hit = _first_max(gs, gi, 0, G)
        ok = jnp.logical_or(ok, hit)
        gs = jnp.where(hit, neg, gs)
    ok_e = jnp.broadcast_to(ok.reshape(G, 1, tm), (G, per, tm)).reshape(E, tm)
    cand = jnp.where(ok_e, biased, neg)
    ei = lax.broadcasted_iota(I32, (E, tm), 0)
    sels, gates = [], []
    for _ in range(TOP_K):
        _, hit = _first_max(cand, ei, 0, E)
        sels.append(hit)
        gates.append(jnp.sum(jnp.where(hit, scores, 0.0), axis=0, keepdims=True))
        cand = jnp.where(hit, neg, cand)
    denom = gates[0]
    for gk in gates[1:]:
        denom = denom + gk
    chosen = sels[0]
    for s in sels[1:]:
        chosen = jnp.logical_or(chosen, s)
    cnt = jnp.where(chosen, 1.0, 0.0)
    earlier = (lax.broadcasted_iota(I32, (tm, tm), 0) < lax.broadcasted_iota(I32, (tm, tm), 1)).astype(BF16)
    rank = _dot(cnt.astype(BF16), earlier) + car_ref[...][:, 0:1]
    car_ref[...] = car_ref[...] + jnp.sum(cnt, axis=1, keepdims=True)
    cnt_ref[...] = car_ref[...]
    zi = jnp.zeros((1, tm), I32)
    zf = jnp.zeros((1, tm), F32)
    e_rows = [jnp.sum(jnp.where(s, ei, 0), axis=0, keepdims=True) for s in sels]
    p_rows = [jnp.sum(jnp.where(s, rank, 0.0), axis=0, keepdims=True).astype(I32) for s in sels]
    g_rows = [gk / denom * ROUTED_SCALE for gk in gates]
    e_ref[...] = jnp.concatenate(e_rows + [zi, zi], axis=0)
    p_ref[...] = jnp.concatenate(p_rows + [zi, zi], axis=0)
    g_ref[...] = jnp.concatenate(g_rows + [zf, zf], axis=0)


def _router(x, sc, sh, w_router, router_bias, tm=256):
    T, D = x.shape
    E = N_EXPERTS
    kern = functools.partial(_router_kernel, tm=tm)
    rowspec = pl.BlockSpec((1, D), lambda i: (0, 0))
    kspec = pl.BlockSpec((SUBLANES, tm), lambda i: (0, i))
    return pl.pallas_call(
        kern,
        grid=(T // tm,),
        in_specs=[pl.BlockSpec((tm, D), lambda i: (i, 0)), rowspec, rowspec,
                  pl.BlockSpec((E, D), lambda i: (0, 0)), pl.BlockSpec((E, 1), lambda i: (0, 0))],
        out_specs=[pl.BlockSpec((tm, D // 2), lambda i: (i, 0)), kspec, kspec, kspec,
                   pl.BlockSpec((E, LANES), lambda i: (0, 0))],
        out_shape=[jax.ShapeDtypeStruct((T, D // 2), U32),
                   jax.ShapeDtypeStruct((SUBLANES, T), I32),
                   jax.ShapeDtypeStruct((SUBLANES, T), F32),
                   jax.ShapeDtypeStruct((SUBLANES, T), I32),
                   jax.ShapeDtypeStruct((E, LANES), F32)],
        scratch_shapes=[pltpu.VMEM((E, LANES), F32)],
        compiler_params=_cp("arbitrary"),
        name="router",
    )(x, sc, sh, w_router.T, router_bias.reshape(E, 1))


def _dispatch_kernel(dest_ref, fill_ref, u_ref, xs_ref, z_ref, sem, zsem, *, tm, bm):
    i = pl.program_id(0)

    @pl.when(i == 0)
    def _():
        z_ref[...] = jnp.zeros_like(z_ref)

        def zcopy(e):
            return pltpu.make_async_copy(z_ref, xs_ref.at[pl.ds(pl.multiple_of(fill_ref[e], bm), bm)], zsem)

        def zstart(e, c):
            zcopy(e).start()
            return c

        def zwait(e, c):
            zcopy(e).wait()
            return c

        lax.fori_loop(0, N_EXPERTS, zstart, 0)
        lax.fori_loop(0, N_EXPERTS, zwait, 0)

    for tok in range(tm):
        for k in range(TOP_K):
            pltpu.make_async_copy(_row(u_ref, tok), _row(xs_ref, dest_ref[k, tok]), sem).start(priority=k % 2)
    n = tm * TOP_K
    pltpu.make_async_copy(xs_ref.at[pl.ds(0, n)], xs_ref.at[pl.ds(0, n)], sem).wait()


def _dispatch(u, dest, fill, n_slots, tm=256, bm=MOE_BM):
    T, D = u.shape
    kern = functools.partial(_dispatch_kernel, tm=tm, bm=bm)
    return pl.pallas_call(
        kern,
        grid=(T // tm,),
        in_specs=[pl.BlockSpec((SUBLANES, tm), lambda i: (0, i), memory_space=pltpu.SMEM),
                  pl.BlockSpec(memory_space=pltpu.SMEM),
                  pl.BlockSpec((tm, D), lambda i: (i, 0))],
        out_specs=pl.BlockSpec(memory_space=pl.ANY),
        out_shape=jax.ShapeDtypeStruct((n_slots, D), u.dtype),
        scratch_shapes=[pltpu.VMEM((bm, D), u.dtype),
                        pltpu.SemaphoreType.DMA(()), pltpu.SemaphoreType.DMA(())],
        compiler_params=_cp("arbitrary"),
        name="moe_dispatch",
    )(dest, fill, u)


def _expert_kernel(be_ref, nv_ref, x_ref, wg_ref, wu_ref, wd_ref, y_ref, wgb_ref, wub_ref, wdb_ref):
    b = pl.program_id(0)
    valid = b < nv_ref[0]
    new_expert = jnp.logical_or(b == 0, be_ref[b] != be_ref[jnp.maximum(b - 1, 0)])

    @pl.when(jnp.logical_and(valid, new_expert))
    def _():
        wgb_ref[...] = wg_ref[0].astype(BF16)
        wub_ref[...] = wu_ref[0].astype(BF16)
        wdb_ref[...] = wd_ref[0].astype(BF16)

    @pl.when(valid)
    def _():
        x = _unpack_pairs(x_ref[...]).astype(BF16)
        h = _silu(_dot(x, wgb_ref[...])) * _dot(x, wub_ref[...])
        y_ref[...] = _pack_pairs(_dot(h.astype(BF16), wdb_ref[...]))


def _experts(xs, block_expert, n_valid, w_gate, w_up, w_down, layer, bm=MOE_BM):
    n_slots, W = xs.shape
    D = 2 * W
    nb = n_slots // bm
    F = w_gate.shape[-1]
    blk = lambda b, be, nv: (jnp.minimum(b, nv[0] - 1), 0)
    wsel = lambda b, be, nv: (layer, be[jnp.minimum(b, nv[0] - 1)], 0, 0)
    gs = pltpu.PrefetchScalarGridSpec(
        num_scalar_prefetch=2,
        grid=(nb,),
        in_specs=[pl.BlockSpec((bm, W), blk),
                  pl.BlockSpec((None, 1, D, F), wsel),
                  pl.BlockSpec((None, 1, D, F), wsel),
                  pl.BlockSpec((None, 1, F, D), wsel)],
        out_specs=pl.BlockSpec((bm, W), blk),
        scratch_shapes=[pltpu.VMEM((D, F), BF16), pltpu.VMEM((D, F), BF16), pltpu.VMEM((F, D), BF16)],
    )
    return pl.pallas_call(
        _expert_kernel,
        grid_spec=gs,
        out_shape=jax.ShapeDtypeStruct((n_slots, W), U32),
        compiler_params=_cp("arbitrary"),
        name="moe_experts",
    )(block_expert, n_valid, xs, w_gate, w_up, w_down)


def _combine_kernel(dest_ref, dnext_ref, ys_ref, u_ref, x_ref, gt_ref, wg_ref, wu_ref, wd_ref, gate_ref, lg_ref,
                    lb_ref, o_ref, buf_ref, sem, *, tm):
    i = pl.program_id(0)
    last = pl.num_programs(0) - 1
    slot = lax.rem(i, 2)

    def row_copy(d_ref, s, tok, k):
        return pltpu.make_async_copy(_row(ys_ref, d_ref[k, tok]), _row(buf_ref.at[s, k], tok), sem.at[s])

    def wait_half(s):
        pltpu.make_async_copy(buf_ref.at[s], buf_ref.at[s], sem.at[s]).wait()

    @pl.when(i == 0)
    def _():
        def start(tok, c):
            for k in range(TOP_K):
                row_copy(dest_ref, slot, tok, k).start()
            return c

        lax.fori_loop(0, tm, start, 0)

    wait_half(slot)
    for tok in range(tm):
        for k in range(TOP_K):
            row_copy(dnext_ref, 1 - slot, tok, k).start(priority=k % 2)

    u = _unpack_pairs(u_ref[...]).astype(BF16)
    h = _silu(_dot(u, wg_ref[...])) * _dot(u, wu_ref[...])
    ffn = _dot(h.astype(BF16), wd_ref[...])
    gt = gt_ref[...]
    for k in range(TOP_K):
        ffn = ffn + gt[:, k:k + 1] * _unpack_pairs(buf_ref[slot, k])
    hres = DEEPNORM_ALPHA * x_ref[...] + (1.0 + gate_ref[...]) * ffn
    o_ref[...] = _layer_norm(hres, lg_ref[...], lb_ref[...])

    @pl.when(i == last)
    def _():
        wait_half(1 - slot)


def _combine(ys, dest, u, x, gates_t, w_sg, w_su, w_sd, gate, ln_g, ln_b, tm=128):
    T, D = x.shape
    W = ys.shape[1]
    F = w_sg.shape[-1]
    n = T // tm
    kern = functools.partial(_combine_kernel, tm=tm)
    spec = pl.BlockSpec((tm, D), lambda i: (i, 0))
    rowspec = pl.BlockSpec((1, D), lambda i: (0, 0))
    return pl.pallas_call(
        kern,
        grid=(n,),
        in_specs=[pl.BlockSpec((SUBLANES, tm), lambda i: (0, i), memory_space=pltpu.SMEM),
                  pl.BlockSpec((SUBLANES, tm), lambda i: (0, jnp.minimum(i + 1, n - 1)), memory_space=pltpu.SMEM),
                  pl.BlockSpec(memory_space=pl.ANY),
                  pl.BlockSpec((tm, W), lambda i: (i, 0)), spec,
                  pl.BlockSpec((tm, SUBLANES), lambda i: (i, 0)),
                  pl.BlockSpec((D, F), lambda i: (0, 0)),
                  pl.BlockSpec((D, F), lambda i: (0, 0)),
                  pl.BlockSpec((F, D), lambda i: (0, 0)),
                  rowspec, rowspec, rowspec],
        out_specs=spec,
        out_shape=jax.ShapeDtypeStruct((T, D), F32),
        scratch_shapes=[pltpu.VMEM((2, TOP_K, tm, W), ys.dtype), pltpu.SemaphoreType.DMA((2,))],
        compiler_params=_cp("arbitrary"),
        name="moe_combine_norm",
    )(dest, dest, ys, u, x, gates_t, w_sg, w_su, w_sd, gate, ln_g.reshape(1, D), ln_b.reshape(1, D))


def _moe_plan(top_e, rank, counts, bm, n_blocks):
    counts = counts.astype(I32)
    padded = (counts + bm - 1) // bm * bm
    padded_end = jnp.cumsum(padded)
    padded_start = padded_end - padded
    onehot = top_e[:, :, None] == jnp.arange(N_EXPERTS, dtype=I32)
    dest = jnp.sum(jnp.where(onehot, padded_start, 0), axis=-1) + rank
    fill = jnp.maximum(padded_end - bm, padded_start)
    fill = jnp.minimum(fill, (n_blocks - 1) * bm)
    block_start = jnp.arange(n_blocks, dtype=I32) * bm
    block_expert = jnp.minimum(jnp.sum((padded_end[None, :] <= block_start[:, None]).astype(I32), axis=1),
                               N_EXPERTS - 1)
    n_valid = jnp.maximum(padded_end[-1:] // bm, 1).astype(I32)
    return dest.astype(I32), fill.astype(I32), block_expert, n_valid


def _in_proj_weights(w_in):
    o = SB_COLS
    gq = w_in[:, o:o + 2 * GLA_KEY_DIM + GLA_VALUE_DIM]
    o += 2 * GLA_KEY_DIM + GLA_VALUE_DIM
    glr = w_in[:, o:o + GLA_GATE_RANK]
    o += GLA_GATE_RANK
    gout = w_in[:, o:o + GLA_VALUE_DIM]
    o += GLA_VALUE_DIM
    pad = jnp.zeros((w_in.shape[0], 2 * LANES - GLA_GATE_RANK), w_in.dtype)
    w_gla = jnp.concatenate([gq, gout, glr, pad], axis=1)
    w_rw = w_in[:, o:o + RWKV_COLS]
    o += RWKV_COLS
    w_mg = w_in[:, o:o + MERGE_COLS]
    return w_in[:, :SB_COLS], w_gla, w_rw, w_mg


def kernel(x, c, w_ada, b_ada, w_in, gla_gk_up, gla_gk_bias, gla_norm_g, rw_mu, rw_w0, rw_w2, rw_a0,
           rw_a2, rw_g2, rw_k_k, rw_k_a, rw_r_k, rw_lnx_g, rw_lnx_b, rw_v0, rw_v1, rw_v2,
           w_br_sb, w_br_gla, w_br_rw, w_o, ln_g, ln_b, w_router, router_bias,
           w_exp_gate, w_exp_up, w_exp_down, w_sh_gate, w_sh_up, w_sh_down):
    B, T, D = x.shape
    assert B == 1 and D == D_MODEL
    n_blocks = -(-(T * TOP_K) // MOE_BM) + N_EXPERTS
    mod = _ada(c, w_ada, b_ada)
    xs = x.reshape(T, D)
    v_first = None
    for l in range(DEPTH):
        sh1, sc1, g1, sh2, sc2, g2 = (mod[l, j * D:(j + 1) * D].reshape(1, D) for j in range(6))
        w_sb, w_gla, w_rw, w_mg = _in_proj_weights(w_in[l])
        u1 = _modulate(xs, sc1, sh1)
        p_sb = _wmm(u1, w_sb, BF16, 1024, name="inproj_sb")
        p_gla = _wmm(u1, w_gla, F32, 13 * LANES, name="inproj_gla")
        p_rw = _wmm(u1, w_rw, F32, 13 * LANES, name="inproj_rwkv")
        p_mg = _wmm(u1, w_mg, BF16, 1024, name="inproj_merge")

        o_sb = _sb_attention(p_sb)
        o_gla = _gla(p_gla, gla_gk_up[l], gla_gk_bias[l], gla_norm_g[l])
        if l == 0:
            r, w, k, v, a, b, g = _rwkv_prep(p_rw, None, rw_mu[l], rw_w0[l], rw_w2[l], rw_a0[l], rw_a2[l],
                                             rw_g2[l], rw_k_k[l], rw_k_a[l], None, None, None)
            v_first = v
        else:
            r, w, k, v, a, b, g = _rwkv_prep(p_rw, v_first, rw_mu[l], rw_w0[l], rw_w2[l], rw_a0[l], rw_a2[l],
                                             rw_g2[l], rw_k_k[l], rw_k_a[l], rw_v0[l - 1], rw_v1[l - 1],
                                             rw_v2[l - 1])
        y = _rwkv_recurrence(r, w, k, v, a, b)
        o_rw = _rwkv_post(y, r, k, v, g, rw_lnx_g[l], rw_lnx_b[l], rw_r_k[l])

        merged = _merge(o_sb, o_gla, o_rw, p_mg, w_br_sb[l].astype(BF16), w_br_gla[l].astype(BF16),
                        w_br_rw[l].astype(BF16))
        xs = _oproj(merged, w_o[l].astype(BF16), xs, g1, ln_g[l, 0], ln_b[l, 0])

        u2, top_e, gates, rank, counts = _router(xs, sc2, sh2, w_router[l], router_bias[l])
        dest, fill, block_expert, n_valid = _moe_plan(top_e, rank, counts[:, 0], MOE_BM, n_blocks)
        x_sorted = _dispatch(u2, dest, fill, n_blocks * MOE_BM)
        y_sorted = _experts(x_sorted, block_expert, n_valid, w_exp_gate, w_exp_up, w_exp_down, l)
        xs = _combine(y_sorted, dest, u2, xs, gates.T, w_sh_gate[l].astype(BF16), w_sh_up[l].astype(BF16),
                      w_sh_down[l].astype(BF16), g2, ln_g[l, 1], ln_b[l, 1])
    return xs.reshape(B, T, D)
```
